```python
import math
import jax
import jax.numpy as jnp
from jax import lax
import numpy as np

D_MODEL = 2048
BATCH = 4
SEQ = 2048
DEPTH = 1

D_MIX = D_MODEL
NORM_EPS = 1e-6
ATT_HEADS = 8
ATT_QK_DIM = 64
ATT_V_DIM = 2 * ATT_QK_DIM
ATT_WIDTH = ATT_HEADS * ATT_V_DIM
ATT_QK_COLS = ATT_HEADS * 2 * ATT_QK_DIM
ATT_COLS = 2 * ATT_QK_COLS + ATT_WIDTH
ROPE_THETA = 500000.0
ROPE_DIM = ATT_QK_DIM // 4
Q_BLOCK = 128
SUBLN_EPS = 1e-5
RWKV_WIDTH = D_MIX - ATT_WIDTH
RWKV_HEAD = 64
RWKV_HEADS = RWKV_WIDTH // RWKV_HEAD
DECAY_LORA = 64
ICLR_LORA = 64
GATE_LORA = 160
GN_EPS = 64e-5
SHIFT_WIDTH = 3 * RWKV_WIDTH + DECAY_LORA + ICLR_LORA + GATE_LORA
SHIFT_SPLITS = [RWKV_WIDTH, 2 * RWKV_WIDTH, 3 * RWKV_WIDTH,
                3 * RWKV_WIDTH + DECAY_LORA, 3 * RWKV_WIDTH + DECAY_LORA + ICLR_LORA]
IN_COLS = ATT_COLS + SHIFT_WIDTH
N_DIRS = 2
N_EXPERTS = 16
CAPACITY_FACTOR = 2
EXPERT_FF = 2048

kernel_name = 'hybrid_diffattn_rwkv7_ecmoe'


def rms_norm(x, w, eps=NORM_EPS):
    xf = x.astype(jnp.float32)
    y = xf * lax.rsqrt(jnp.mean(xf * xf, axis=-1, keepdims=True) + eps)
    return (y * w.astype(jnp.float32)).astype(x.dtype)


def partial_rope(x, cos, sin):
    xf = x.astype(jnp.float32)
    half = ROPE_DIM // 2
    x1 = xf[..., :half]
    x2 = xf[..., half:ROPE_DIM]
    out = jnp.concatenate([x1 * cos - x2 * sin, x2 * cos + x1 * sin, xf[..., ROPE_DIM:]], axis=-1)
    return out.astype(x.dtype)


def diff_attention(q, k, v, positions, lq1, lk1, lq2, lk2, subln_w, lambda_init):
    bsz, seq = q.shape[0], q.shape[1]
    inv_freq = ROPE_THETA ** (-jnp.arange(0, ROPE_DIM, 2, dtype=jnp.float32) / ROPE_DIM)
    ang = positions.astype(jnp.float32)[..., None] * inv_freq
    cos = jnp.cos(ang)[:, :, None, None, :]
    sin = jnp.sin(ang)[:, :, None, None, :]
    q = partial_rope(q, cos, sin) * (ATT_QK_DIM ** -0.5)
    k = partial_rope(k, cos, sin)
    f32 = jnp.float32
    lam = (jnp.exp(jnp.sum(lq1.astype(f32) * lk1.astype(f32)))
           - jnp.exp(jnp.sum(lq2.astype(f32) * lk2.astype(f32))) + lambda_init)
    n_blocks = seq // Q_BLOCK
    q_blocks = jnp.moveaxis(q.reshape(bsz, n_blocks, Q_BLOCK, ATT_HEADS, 2, ATT_QK_DIM), 1, 0)

    def block(qb):
        s = jnp.einsum('bqhcd,bkhcd->bhcqk', qb, k).astype(f32)
        p = jax.nn.softmax(s, axis=-1)
        wgt = p[:, :, 0] - lam * p[:, :, 1]
        return jnp.einsum('bhqk,bkhe->bqhe', wgt.astype(v.dtype), v)

    o = lax.map(block, q_blocks)
    o = jnp.moveaxis(o, 0, 1).reshape(bsz, seq, ATT_HEADS, ATT_V_DIM)
    o = rms_norm(o, subln_w, eps=SUBLN_EPS) * (1.0 - lambda_init)
    return o.reshape(bsz, seq, ATT_WIDTH)


def _tm_shared(t):
    s = jnp.stack([t, jnp.flip(t, axis=1)], axis=0)
    return jnp.moveaxis(s, 2, 0).astype(jnp.float32)


def _tm_dir(t):
    s = jnp.stack([t[0], jnp.flip(t[1], axis=1)], axis=0)
    return jnp.moveaxis(s, 2, 0).astype(jnp.float32)


def _rwkv_step(state, inp):
    r_t, w_t, k_t, v_t, a_t, b_t = inp
    sa = jnp.einsum('dbhvk,dbhk->dbhv', state, a_t)
    state = state * w_t[..., None, :] + sa[..., None] * b_t[..., None, :] + v_t[..., None] * k_t[..., None, :]
    y = jnp.einsum('dbhvk,dbhk->dbhv', state, r_t)
    return state, y


def rwkv7_bidir(z, w0, decay_up, a0, iclr_up, gate_up, k_k, k_a, r_k, ln_x_w, ln_x_b):
    f32 = jnp.float32
    bsz, seq = z.shape[0], z.shape[1]
    r, k, v, wd, ad, gd = jnp.split(z, SHIFT_SPLITS, axis=-1)
    w_log = -jax.nn.softplus(-(w0.astype(f32)[:, None, None, :]
                               + jnp.einsum('btr,drc->dbtc', jnp.tanh(wd), decay_up).astype(f32))) - 0.5
    decay = jnp.exp(-jnp.exp(w_log))
    a = jax.nn.sigmoid(a0.astype(f32)[:, None, None, :]
                       + jnp.einsum('btr,drc->dbtc', ad, iclr_up).astype(f32))
    g = (jax.nn.sigmoid(gd) @ gate_up).astype(f32)

    def heads(t):
        return t.reshape(t.shape[:-1] + (RWKV_HEADS, RWKV_HEAD))

    kk = heads((k * k_k).astype(f32))
    kk = kk / jnp.maximum(jnp.sqrt(jnp.sum(kk * kk, axis=-1, keepdims=True)), 1e-12)
    k_dir = heads(k.astype(f32)[None] * (1.0 + (a - 1.0) * k_a.astype(f32)))
    r_h = heads(r.astype(f32))
    v_h = heads(v.astype(f32))
    a_h = heads(a)
    xs = (_tm_shared(r_h), _tm_dir(heads(decay)), _tm_dir(k_dir), _tm_shared(v_h),
          _tm_shared(-kk), _tm_dir(kk[None] * a_h))
    state0 = jnp.zeros((N_DIRS, bsz, RWKV_HEADS, RWKV_HEAD, RWKV_HEAD), f32)
    _, ys = lax.scan(_rwkv_step, state0, xs)
    ys = jnp.moveaxis(ys, 0, 2)
    y = ys[0] + jnp.flip(ys[1], axis=1)
    mu = jnp.mean(y, axis=-1, keepdims=True)
    var = jnp.mean(jnp.square(y - mu), axis=-1, keepdims=True)
    y = ((y - mu) * lax.rsqrt(var + GN_EPS)).reshape(bsz, seq, RWKV_WIDTH)
    y = y * ln_x_w.astype(f32) + ln_x_b.astype(f32)
    bonus = jnp.sum(jnp.sum(r_h[None] * k_dir * r_k.astype(f32), axis=-1, keepdims=True), axis=0) * v_h
    out = (y + bonus.reshape(bsz, seq, RWKV_WIDTH)) * g
    return out.astype(z.dtype)


def expert_choice_moe(h, w_router, e_gate, e_up, e_down):
    bsz, seq, d = h.shape
    cap = CAPACITY_FACTOR * seq // N_EXPERTS
    aff = jax.nn.softmax((h @ w_router).astype(jnp.float32), axis=-1)
    gates, idx = lax.top_k(jnp.swapaxes(aff, 1, 2), cap)
    flat_idx = idx.reshape(bsz, N_EXPERTS * cap)
    xe = jnp.take_along_axis(h, flat_idx[..., None], axis=1).reshape(bsz, N_EXPERTS, cap, d)
    hid = jax.nn.silu(jnp.einsum('becd,edf->becf', xe, e_gate)) * jnp.einsum('becd,edf->becf', xe, e_up)
    ye = jnp.einsum('becf,efd->becd', hid, e_down) * gates[..., None].astype(h.dtype)
    seg = (jnp.arange(bsz, dtype=jnp.int32)[:, None] * seq + flat_idx).reshape(-1)
    out = jax.ops.segment_sum(ye.reshape(-1, d), seg, num_segments=bsz * seq)
    return out.reshape(bsz, seq, d)


def setup_inputs(seed: int = 0) -> dict:
    key = jax.random.key(seed)
    ks = jax.random.split(key, 32)
    f32 = jnp.float32
    L = DEPTH

    def nrm(k, shape, scale):
        return jax.random.normal(k, shape, f32) * scale

    def gain(k, shape):
        return 1.0 + 0.02 * jax.random.normal(k, shape, f32)

    return {
        'x': jax.random.normal(ks[0], (BATCH, SEQ, D_MODEL), f32),
        'positions': jnp.broadcast_to(jnp.arange(SEQ, dtype=jnp.int32), (BATCH, SEQ)),
        'norm1_w': gain(ks[1], (L, D_MODEL)),
        'w_in': nrm(ks[2], (L, D_MODEL, IN_COLS), D_MODEL ** -0.5),
        'mu_prev': jax.random.uniform(ks[3], (L, SHIFT_WIDTH), f32, 0.0, 0.5),
        'mu_next': jax.random.uniform(ks[4], (L, SHIFT_WIDTH), f32, 0.0, 0.5),
        'lambda_q1': nrm(ks[5], (L, ATT_QK_DIM), 0.1),
        'lambda_k1': nrm(ks[6], (L, ATT_QK_DIM), 0.1),
        'lambda_q2': nrm(ks[7], (L, ATT_QK_DIM), 0.1),
        'lambda_k2': nrm(ks[8], (L, ATT_QK_DIM), 0.1),
        'subln_w': gain(ks[9], (L, ATT_V_DIM)),
        'w0': jax.random.uniform(ks[10], (L, N_DIRS, RWKV_WIDTH), f32, -6.0, -1.0),
        'decay_up': nrm(ks[11], (L, N_DIRS, DECAY_LORA, RWKV_WIDTH), 0.1),
        'a0': nrm(ks[12], (L, N_DIRS, RWKV_WIDTH), 0.5),
        'iclr_up': nrm(ks[13], (L, N_DIRS, ICLR_LORA, RWKV_WIDTH), ICLR_LORA ** -0.5),
        'gate_up': nrm(ks[14], (L, GATE_LORA, RWKV_WIDTH), GATE_LORA ** -0.5),
        'k_k': 0.85 + 0.05 * jax.random.normal(ks[15], (L, RWKV_WIDTH), f32),
        'k_a': 1.0 + 0.05 * jax.random.normal(ks[16], (L, RWKV_WIDTH), f32),
        'r_k': nrm(ks[17], (L, RWKV_HEADS, RWKV_HEAD), 0.1),
        'ln_x_w': gain(ks[18], (L, RWKV_WIDTH)),
        'ln_x_b': nrm(ks[19], (L, RWKV_WIDTH), 0.02),
        'w_out': nrm(ks[20], (L, D_MIX, D_MODEL), D_MIX ** -0.5),
        'norm2_w': gain(ks[21], (L, D_MODEL)),
        'w_router': nrm(ks[22], (L, D_MODEL, N_EXPERTS), D_MODEL ** -0.5),
        'e_gate': nrm(ks[23], (L, N_EXPERTS, D_MODEL, EXPERT_FF), D_MODEL ** -0.5),
        'e_up': nrm(ks[24], (L, N_EXPERTS, D_MODEL, EXPERT_FF), D_MODEL ** -0.5),
        'e_down': nrm(ks[25], (L, N_EXPERTS, EXPERT_FF, D_MODEL), EXPERT_FF ** -0.5),
        'norm_f_w': gain(ks[26], (D_MODEL,)),
    }


def reference(x, positions, norm1_w, w_in, mu_prev, mu_next, lambda_q1, lambda_k1, lambda_q2,
              lambda_k2, subln_w, w0, decay_up, a0, iclr_up, gate_up, k_k, k_a, r_k, ln_x_w,
              ln_x_b, w_out, norm2_w, w_router, e_gate, e_up, e_down, norm_f_w):
    bsz, seq, _ = x.shape
    for l in range(DEPTH):
        lambda_init = 0.8 - 0.6 * math.exp(-0.3 * l)
        h = rms_norm(x, norm1_w[l])
        proj = h @ w_in[l]
        q = proj[..., :ATT_QK_COLS].reshape(bsz, seq, ATT_HEADS, 2, ATT_QK_DIM)
        k = proj[..., ATT_QK_COLS:2 * ATT_QK_COLS].reshape(bsz, seq, ATT_HEADS, 2, ATT_QK_DIM)
        v = proj[..., 2 * ATT_QK_COLS:ATT_COLS].reshape(bsz, seq, ATT_HEADS, ATT_V_DIM)
        att = diff_attention(q, k, v, positions, lambda_q1[l], lambda_k1[l], lambda_q2[l],
                             lambda_k2[l], subln_w[l], lambda_init)
        z = proj[..., ATT_COLS:]
        z_prev = jnp.pad(z[:, :-1], ((0, 0), (1, 0), (0, 0)))
        z_next = jnp.pad(z[:, 1:], ((0, 0), (0, 1), (0, 0)))
        z = z + mu_prev[l] * (z_prev - z) + mu_next[l] * (z_next - z)
        rw = rwkv7_bidir(z, w0[l], decay_up[l], a0[l], iclr_up[l], gate_up[l], k_k[l], k_a[l],
                         r_k[l], ln_x_w[l], ln_x_b[l])
        x = x + jnp.concatenate([att, rw], axis=-1) @ w_out[l]
        x = x + expert_choice_moe(rms_norm(x, norm2_w[l]), w_router[l], e_gate[l], e_up[l], e_down[l])
    return rms_norm(x, norm_f_w)
```

```python
import functools
import math

import jax
import jax.numpy as jnp
from jax import lax
from jax.experimental import pallas as pl
from jax.experimental.pallas import tpu as pltpu

F32 = jnp.float32
BF16 = jnp.bfloat16

LANES = 128
VMEM_LIMIT_BYTES = 56 * 1024 * 1024

D_MODEL = 2048
NORM_EPS = 1e-6
ATT_HEADS = 8
ATT_QK_DIM = 64
ATT_V_DIM = 128
ATT_WIDTH = ATT_HEADS * ATT_V_DIM
ATT_QK_COLS = ATT_HEADS * 2 * ATT_QK_DIM
ATT_COLS = 2 * ATT_QK_COLS + ATT_WIDTH
ROPE_THETA = 500000.0
ROPE_DIM = ATT_QK_DIM // 4
SUBLN_EPS = 1e-5
RWKV_WIDTH = 1024
RWKV_HEAD = 64
RWKV_HEADS = RWKV_WIDTH // RWKV_HEAD
DECAY_LORA = 64
ICLR_LORA = 64
GATE_LORA = 160
LORA_COLS = DECAY_LORA + ICLR_LORA + GATE_LORA
LORA_PAD = 384
GN_EPS = 64e-5
SHIFT_WIDTH = 3 * RWKV_WIDTH + LORA_COLS
SHIFT_PAD = 3 * RWKV_WIDTH + LORA_PAD
N_EXPERTS = 16
CAPACITY_FACTOR = 2
EXPERT_FF = 2048
CHUNK = 64


def _params(*sem):
    return pltpu.CompilerParams(dimension_semantics=sem, vmem_limit_bytes=VMEM_LIMIT_BYTES)


def _dot(a, b):
    return jnp.dot(a, b, preferred_element_type=F32)


def _dot_nt(a, b):
    return lax.dot_general(a, b, (((1,), (1,)), ((), ())), preferred_element_type=F32)


def _split2(x):
    hi = x.astype(BF16)
    lo = (x - hi.astype(F32)).astype(BF16)
    return hi, lo


def _dot_lhs2(x, m_bf16):
    hi, lo = _split2(x)
    return _dot(hi, m_bf16) + _dot(lo, m_bf16)


def _norm_mm_kernel(x_ref, nw_ref, w_ref, o_ref, h_scr):
    @pl.when(pl.program_id(1) == 0)
    def _():
        x = x_ref[...]
        ms = jnp.mean(x * x, axis=-1, keepdims=True)
        h_scr[...] = (x * lax.rsqrt(ms + NORM_EPS) * nw_ref[...]).astype(BF16)

    o_ref[...] = _dot(h_scr[...], w_ref[...]).astype(o_ref.dtype)


def norm_matmul(x, nw, w_bf16, out_dtype, tm=1024, tn=384):
    m, k = x.shape
    n = w_bf16.shape[1]
    return pl.pallas_call(
        _norm_mm_kernel,
        grid=(m // tm, n // tn),
        in_specs=[
            pl.BlockSpec((tm, k), lambda i, j: (i, 0)),
            pl.BlockSpec((1, k), lambda i, j: (0, 0)),
            pl.BlockSpec((k, tn), lambda i, j: (0, j)),
        ],
        out_specs=pl.BlockSpec((tm, tn), lambda i, j: (i, j)),
        out_shape=jax.ShapeDtypeStruct((m, n), out_dtype),
        scratch_shapes=[pltpu.VMEM((tm, k), BF16)],
        compiler_params=_params("parallel", "arbitrary"),
        name="norm_matmul",
    )(x, nw.reshape(1, k), w_bf16)


def _rope(x, pos, freq, sgn_lo, sgn_hi):
    ang = pos.astype(F32) * freq
    c = jnp.cos(ang)
    s = jnp.sin(ang)
    half = ROPE_DIM // 2
    x_dn = pltpu.roll(x, half, axis=1)
    x_up = pltpu.roll(x, LANES - half, axis=1)
    return x * c + x_dn * (s * sgn_hi) + x_up * (s * sgn_lo)


def _attn_kernel(q_ref, k_ref, v_ref, posq_ref, posk_ref, freq_ref, slo_ref, shi_ref,
                 lq1_ref, lk1_ref, lq2_ref, lk2_ref, sw_ref, o_ref, k_scr, v_scr, *, lambda_init):
    freq = freq_ref[...]
    slo = slo_ref[...]
    shi = shi_ref[...]

    @pl.when(pl.program_id(2) == 0)
    def _():
        k_scr[...] = _rope(k_ref[...], posk_ref[...], freq, slo, shi).astype(BF16)
        v_scr[...] = v_ref[...].astype(BF16)

    q = (_rope(q_ref[...], posq_ref[...], freq, slo, shi) * (ATT_QK_DIM ** -0.5)).astype(BF16)
    kr = k_scr[...]
    s1 = _dot_nt(q[:, :ATT_QK_DIM], kr[:, :ATT_QK_DIM])
    s2 = _dot_nt(q[:, ATT_QK_DIM:], kr[:, ATT_QK_DIM:])
    e1 = jnp.exp(s1 - jnp.max(s1, axis=-1, keepdims=True))
    e2 = jnp.exp(s2 - jnp.max(s2, axis=-1, keepdims=True))
    lam = (jnp.exp(jnp.sum(lq1_ref[...] * lk1_ref[...], axis=-1, keepdims=True))
           - jnp.exp(jnp.sum(lq2_ref[...] * lk2_ref[...], axis=-1, keepdims=True)) + lambda_init)
    r1 = 1.0 / jnp.sum(e1, axis=-1, keepdims=True)
    r2 = lam / jnp.sum(e2, axis=-1, keepdims=True)
    wgt = (e1 * r1 - e2 * r2).astype(BF16)
    o = _dot(wgt, v_scr[...])
    ms = jnp.mean(o * o, axis=-1, keepdims=True)
    o = o * lax.rsqrt(ms + SUBLN_EPS) * sw_ref[...] * (1.0 - lambda_init)
    o_ref[...] = o.astype(o_ref.dtype)


def attention(p_att, pos, lq1, lk1, lq2, lk2, subln_w, bsz, seq, lambda_init, tq=256):
    nq = seq // tq
    lane = jnp.arange(LANES) % ATT_QK_DIM
    half = ROPE_DIM // 2
    inv_freq = ROPE_THETA ** (-jnp.arange(0, ROPE_DIM, 2, dtype=F32) / ROPE_DIM)
    freq = jnp.where(lane < ROPE_DIM, inv_freq[lane % half], 0.0).astype(F32).reshape(1, LANES)
    sgn_lo = jnp.where(lane < half, -1.0, 0.0).astype(F32).reshape(1, LANES)
    sgn_hi = jnp.where((lane >= half) & (lane < ROPE_DIM), 1.0, 0.0).astype(F32).reshape(1, LANES)
    vec = lambda n: pl.BlockSpec((1, n), lambda b, h, i: (0, 0))
    nh = ATT_HEADS
    return pl.pallas_call(
        functools.partial(_attn_kernel, lambda_init=lambda_init),
        grid=(bsz, nh, nq),
        in_specs=[
            pl.BlockSpec((tq, LANES), lambda b, h, i: (b * nq + i, h)),
            pl.BlockSpec((seq, LANES), lambda b, h, i: (b, nh + h)),
            pl.BlockSpec((seq, LANES), lambda b, h, i: (b, 2 * nh + h)),
            pl.BlockSpec((tq, 1), lambda b, h, i: (b * nq + i, 0)),
            pl.BlockSpec((seq, 1), lambda b, h, i: (b, 0)),
            vec(LANES), vec(LANES), vec(LANES),
            vec(ATT_QK_DIM), vec(ATT_QK_DIM), vec(ATT_QK_DIM), vec(ATT_QK_DIM),
            vec(ATT_V_DIM),
        ],
        out_specs=pl.BlockSpec((tq, LANES), lambda b, h, i: (b * nq + i, h)),
        out_shape=jax.ShapeDtypeStruct((bsz * seq, ATT_WIDTH), BF16),
        scratch_shapes=[pltpu.VMEM((seq, LANES), BF16), pltpu.VMEM((seq, LANES), BF16)],
        compiler_params=_params("parallel", "parallel", "arbitrary"),
        name="diff_attention",
    )(p_att, p_att, p_att, pos, pos, freq, sgn_lo, sgn_hi,
      lq1.reshape(1, -1), lk1.reshape(1, -1), lq2.reshape(1, -1), lk2.reshape(1, -1),
      subln_w.reshape(1, -1))


def _shift_kernel(z_ref, mp_ref, mn_ref, o_ref):
    z = z_ref[...]
    n = z.shape[0]
    row = lax.broadcasted_iota(jnp.int32, z.shape, 0)
    zp = jnp.where(row == 0, 0.0, pltpu.roll(z, 1, axis=0))
    zn = jnp.where(row == n - 1, 0.0, pltpu.roll(z, n - 1, axis=0))
    o_ref[...] = z + mp_ref[...] * (zp - z) + mn_ref[...] * (zn - z)


def token_shift(pz, mu_prev, mu_next, bsz, seq):
    width = pz.shape[1]
    nt = width // LANES
    return pl.pallas_call(
        _shift_kernel,
        grid=(bsz, nt),
        in_specs=[
            pl.BlockSpec((seq, LANES), lambda b, j: (b, j)),
            pl.BlockSpec((1, LANES), lambda b, j: (0, j)),
            pl.BlockSpec((1, LANES), lambda b, j: (0, j)),
        ],
        out_specs=pl.BlockSpec((seq, LANES), lambda b, j: (b, j)),
        out_shape=jax.ShapeDtypeStruct(pz.shape, F32),
        compiler_params=_params("parallel", "parallel"),
        name="token_shift",
    )(pz, mu_prev, mu_next)


def _seg_sum(x, ones_blk):
    parts = []
    for j in range(x.shape[1] // LANES):
        parts.append(_dot_lhs2(x[:, j * LANES:(j + 1) * LANES], ones_blk))
    return jnp.concatenate(parts, axis=1)


def _prep_kernel(r_ref, k_ref, v_ref, lo_ref, w0_ref, du_ref, a0_ref, iu_ref, gu_ref, kk_ref,
                 ka_ref, rk_ref, ones_ref, tril_ref, triu_ref,
                 ab_ref, rb_ref, bt_ref, kt_ref, pl_ref, vb_ref, bonus_ref, g_ref, *, tt):
    r = r_ref[...]
    k = k_ref[...]
    v = v_ref[...]
    lo = lo_ref[...]
    ones_blk = ones_ref[...]
    vb_ref[...] = v.astype(BF16)
    g_ref[...] = _dot(jax.nn.sigmoid(lo).astype(BF16), gu_ref[...])
    kk = k * kk_ref[...]
    nrm = jnp.sqrt(_seg_sum(kk * kk, ones_blk))
    kk = kk / jnp.maximum(nrm, 1e-12)
    th = jnp.tanh(lo).astype(BF16)
    lob = lo.astype(BF16)
    nchunk = tt // CHUNK
    ksum = jnp.zeros_like(k)
    for d in range(2):
        wl = w0_ref[d] + _dot(th, du_ref[d])
        w_log = -jax.nn.softplus(-wl) - 0.5
        lw = -jnp.exp(w_log)
        a = jax.nn.sigmoid(a0_ref[d] + _dot(lob, iu_ref[d]))
        kd = k * (1.0 + (a - 1.0) * ka_ref[...])
        ksum = ksum + kd
        tri = tril_ref[...] if d == 0 else triu_ref[...]
        cum = _dot_lhs2_rhs(tri, lw)
        e_pos = jnp.exp(cum)
        e_neg = jnp.exp(-cum)
        ab_ref[d] = (-kk * jnp.exp(cum - lw)).astype(BF16)
        rb_ref[d] = (r * e_pos).astype(BF16)
        bt_ref[d] = (kk * a * e_neg).astype(BF16)
        kt_ref[d] = (kd * e_neg).astype(BF16)
        for c in range(nchunk):
            last = c * CHUNK + (CHUNK - 1 if d == 0 else 0)
            pl_ref[d, c] = e_pos[last:last + 1, :]
    bonus_ref[...] = _seg_sum(r * ksum * rk_ref[...], ones_blk) * v


def _dot_lhs2_rhs(tri_bf16, x):
    hi, lo = _split2(x)
    return _dot(tri_bf16, hi) + _dot(tri_bf16, lo)


def rwkv_prep(zs, w0, decay_up, a0, iclr_up, gate_up, k_k, k_a, r_k, bsz, seq, tt=256):
    m = bsz * seq
    c = RWKV_WIDTH
    nt = seq // tt
    ncb = tt // CHUNK
    nc = seq // CHUNK

    def pad_rows(w, start):
        out = jnp.zeros(w.shape[:-2] + (LORA_PAD, c), F32)
        return lax.dynamic_update_slice_in_dim(out, w.astype(F32), start, axis=w.ndim - 2).astype(BF16)

    du = pad_rows(decay_up, 0)
    iu = pad_rows(iclr_up, DECAY_LORA)
    gu = pad_rows(gate_up, DECAY_LORA + ICLR_LORA)
    lane = jnp.arange(LANES)
    ones_blk = (lane[:, None] // RWKV_HEAD == lane[None, :] // RWKV_HEAD).astype(BF16)
    t = jnp.arange(tt)
    same = t[:, None] // CHUNK == t[None, :] // CHUNK
    tril = (same & (t[:, None] >= t[None, :])).astype(BF16)
    triu = (same & (t[:, None] <= t[None, :])).astype(BF16)
    r_k_flat = r_k.reshape(1, c)
    row = lambda: pl.BlockSpec((1, c), lambda b, i: (0, 0))
    full3 = lambda s: pl.BlockSpec(s, lambda b, i: (0, 0, 0))
    tok = lambda j, w: pl.BlockSpec((tt, w), lambda b, i: (b * nt + i, j))
    dir_tok = pl.BlockSpec((2, tt, c), lambda b, i: (0, b * nt + i, 0))
    outs = pl.pallas_call(
        functools.partial(_prep_kernel, tt=tt),
        grid=(bsz, nt),
        in_specs=[
            tok(0, c), tok(1, c), tok(2, c),
            pl.BlockSpec((tt, LORA_PAD), lambda b, i: (b * nt + i, 3 * c // LORA_PAD)),
            full3((2, 1, c)), full3((2, LORA_PAD, c)), full3((2, 1, c)), full3((2, LORA_PAD, c)),
            pl.BlockSpec((LORA_PAD, c), lambda b, i: (0, 0)),
            row(), row(), row(),
            pl.BlockSpec((LANES, LANES), lambda b, i: (0, 0)),
            pl.BlockSpec((tt, tt), lambda b, i: (0, 0)),
            pl.BlockSpec((tt, tt), lambda b, i: (0, 0)),
        ],
        out_specs=[
            dir_tok, dir_tok, dir_tok, dir_tok,
            pl.BlockSpec((2, ncb, 1, c), lambda b, i: (0, b * nt + i, 0, 0)),
            pl.BlockSpec((tt, c), lambda b, i: (b * nt + i, 0)),
            pl.BlockSpec((tt, c), lambda b, i: (b * nt + i, 0)),
            pl.BlockSpec((tt, c), lambda b, i: (b * nt + i, 0)),
        ],
        out_shape=[
            jax.ShapeDtypeStruct((2, m, c), BF16),
            jax.ShapeDtypeStruct((2, m, c), BF16),
            jax.ShapeDtypeStruct((2, m, c), BF16),
            jax.ShapeDtypeStruct((2, m, c), BF16),
            jax.ShapeDtypeStruct((2, bsz * nc, 1, c), F32),
            jax.ShapeDtypeStruct((m, c), BF16),
            jax.ShapeDtypeStruct((m, c), F32),
            jax.ShapeDtypeStruct((m, c), F32),
        ],
        compiler_params=_params("parallel", "parallel"),
        name="rwkv_prep",
    )(zs, zs, zs, zs, w0.reshape(2, 1, c), du, a0.reshape(2, 1, c), iu, gu,
      k_k.reshape(1, c), k_a.reshape(1, c), r_k_flat, ones_blk, tril, triu)
    return outs


def _chunk_kernel(ab_ref, rb_ref, bt_ref, kt_ref, v_ref, pl_ref, q_ref, yb_ref, g_ref, h_ref):
    n = CHUNK
    row = lax.broadcasted_iota(jnp.int32, (2 * n, 2 * n), 0)
    col = lax.broadcasted_iota(jnp.int32, (2 * n, 2 * n), 1)
    top = row < n
    tr = row % n
    tc = col % n
    eye = (lax.broadcasted_iota(jnp.int32, (n, n), 0)
           == lax.broadcasted_iota(jnp.int32, (n, n), 1)).astype(F32)
    for d in range(2):
        if d == 0:
            keep = (tr > tc) | (~top & (tr == tc))
        else:
            keep = (tr < tc) | (~top & (tr == tc))
        for hh in range(2):
            sl = slice(hh * n, (hh + 1) * n)
            ab = ab_ref[d, :, sl]
            rb = rb_ref[d, :, sl]
            bt = bt_ref[d, :, sl]
            kt = kt_ref[d, :, sl]
            vv = v_ref[:, sl]
            p_last = pl_ref[d, 0, :, sl]
            g1 = _dot_nt(jnp.concatenate([ab, rb], axis=0), jnp.concatenate([bt, kt], axis=0))
            g1 = jnp.where(keep, g1, 0.0)
            a_ab = g1[:n, :n]
            a_ak = g1[:n, n:]
            a_rb = g1[n:, :n]
            low = g1[n:, :].astype(BF16)
            tm = eye + a_ab
            pw = a_ab
            span = 1
            while span * 2 < n:
                pwb = pw.astype(BF16)
                pw = _dot(pwb, pwb)
                tm = tm + _dot(tm.astype(BF16), pw.astype(BF16))
                span *= 2
            x = jnp.concatenate([ab, a_ak.astype(BF16)], axis=1)
            wc = _dot(tm.astype(BF16), x)
            w = wc[:, :n].astype(BF16)
            cu = wc[:, n:].astype(BF16)
            cuv = _dot(cu, vv).astype(BF16)
            rhs = jnp.concatenate([cuv, vv], axis=0)
            q_ref[d, 0, 0, hh] = rb.astype(F32) + _dot(a_rb.astype(BF16), w)
            yb_ref[d, 0, 0, hh] = _dot(low, rhs)
            bh = bt.astype(F32) * p_last
            kh = kt.astype(F32) * p_last
            g_ref[d, 0, 0, hh] = eye * p_last + _dot(bh.T.astype(BF16), w)
            lhs_t = jnp.concatenate([bh.T, kh.T], axis=1).astype(BF16)
            h_ref[d, 0, 0, hh] = _dot(lhs_t, rhs)


def rwkv_chunk(ab, rb, bt, kt, vb, p_last, bsz, seq):
    nc = seq // CHUNK
    nh = RWKV_HEADS
    n = CHUNK
    dir_blk = pl.BlockSpec((2, n, LANES), lambda b, c, hp: (0, b * nc + c, hp))
    out_blk = pl.BlockSpec((2, 1, 1, 2, n, n), lambda b, c, hp: (0, b, c, hp, 0, 0))
    shape = jax.ShapeDtypeStruct((2, bsz, nc, nh, n, n), F32)
    return pl.pallas_call(
        _chunk_kernel,
        grid=(bsz, nc, nh // 2),
        in_specs=[
            dir_blk, dir_blk, dir_blk, dir_blk,
            pl.BlockSpec((n, LANES), lambda b, c, hp: (b * nc + c, hp)),
            pl.BlockSpec((2, 1, 1, LANES), lambda b, c, hp: (0, b * nc + c, 0, hp)),
        ],
        out_specs=[out_blk, out_blk, out_blk, out_blk],
        out_shape=[shape, shape, shape, shape],
        compiler_params=_params("parallel", "parallel", "parallel"),
        name="rwkv_chunk",
    )(ab, rb, bt, kt, vb, p_last)


def _state_kernel(q_ref, g_ref, h_ref, yb_ref, y_ref, m_scr):
    @pl.when(pl.program_id(2) == 0)
    def _():
        m_scr[...] = jnp.zeros_like(m_scr)

    n = CHUNK
    for h in range(RWKV_HEADS):
        mb = m_scr[h].astype(BF16)
        y_ref[0, :, h * n:(h + 1) * n] = _dot(q_ref[0, 0, 0, h].astype(BF16), mb) + yb_ref[0, 0, 0, h]
        m_scr[h] = _dot(g_ref[0, 0, 0, h].astype(BF16), mb) + h_ref[0, 0, 0, h]


def rwkv_state(q, g, h, yb, bsz, seq):
    nc = seq // CHUNK
    nh = RWKV_HEADS
    n = CHUNK

    def chunk_of(d, c):
        return jnp.where(d == 0, c, nc - 1 - c)

    blk = pl.BlockSpec((1, 1, 1, nh, n, n), lambda d, b, c: (d, b, chunk_of(d, c), 0, 0, 0))
    return pl.pallas_call(
        _state_kernel,
        grid=(2, bsz, nc),
        in_specs=[blk, blk, blk, blk],
        out_specs=pl.BlockSpec((1, n, RWKV_WIDTH), lambda d, b, c: (d, b * nc + chunk_of(d, c), 0)),
        out_shape=jax.ShapeDtypeStruct((2, bsz * seq, RWKV_WIDTH), F32),
        scratch_shapes=[pltpu.VMEM((nh, n, n), F32)],
        compiler_params=_params("parallel", "parallel", "arbitrary"),
        name="rwkv_state",
    )(q, g, h, yb)


def _post_kernel(y_ref, bonus_ref, g_ref, lw_ref, lb_ref, o_ref):
    y = y_ref[0] + y_ref[1]
    n = RWKV_HEAD
    parts = []
    for h in range(RWKV_HEADS):
        yh = y[:, h * n:(h + 1) * n]
        mu = jnp.mean(yh, axis=-1, keepdims=True)
        yc = yh - mu
        var = jnp.mean(yc * yc, axis=-1, keepdims=True)
        parts.append(yc * lax.rsqrt(var + GN_EPS))
    yn = jnp.concatenate(parts, axis=1)
    o_ref[...] = ((yn * lw_ref[...] + lb_ref[...] + bonus_ref[...]) * g_ref[...]).astype(o_ref.dtype)


def rwkv_post(y, bonus, g, ln_w, ln_b, tt=256):
    m = y.shape[1]
    c = RWKV_WIDTH
    tok = pl.BlockSpec((tt, c), lambda i: (i, 0))
    row = pl.BlockSpec((1, c), lambda i: (0, 0))
    return pl.pallas_call(
        _post_kernel,
        grid=(m // tt,),
        in_specs=[pl.BlockSpec((2, tt, c), lambda i: (0, i, 0)), tok, tok, row, row],
        out_specs=tok,
        out_shape=jax.ShapeDtypeStruct((m, c), BF16),
        compiler_params=_params("parallel"),
        name="rwkv_post",
    )(y, bonus, g, ln_w.reshape(1, c), ln_b.reshape(1, c))


def _out_proj_kernel(x_ref, a_ref, r_ref, wa_ref, wr_ref, o_ref):
    o_ref[...] = x_ref[...] + _dot(a_ref[...], wa_ref[...]) + _dot(r_ref[...], wr_ref[...])


def out_proj(x, att, rw, w_att, w_rw, tm=1024, tn=512):
    m, n = x.shape
    return pl.pallas_call(
        _out_proj_kernel,
        grid=(m // tm, n // tn),
        in_specs=[
            pl.BlockSpec((tm, tn), lambda i, j: (i, j)),
            pl.BlockSpec((tm, ATT_WIDTH), lambda i, j: (i, 0)),
            pl.BlockSpec((tm, RWKV_WIDTH), lambda i, j: (i, 0)),
            pl.BlockSpec((ATT_WIDTH, tn), lambda i, j: (0, j)),
            pl.BlockSpec((RWKV_WIDTH, tn), lambda i, j: (0, j)),
        ],
        out_specs=pl.BlockSpec((tm, tn), lambda i, j: (i, j)),
        out_shape=jax.ShapeDtypeStruct((m, n), F32),
        compiler_params=_params("parallel", "parallel"),
        name="out_proj",
    )(x, att, rw, w_att, w_rw)


def _router_kernel(x_ref, nw_ref, wh_ref, wl_ref, hb_ref, aff_ref, afft_ref):
    x = x_ref[...]
    ms = jnp.mean(x * x, axis=-1, keepdims=True)
    h = x * lax.rsqrt(ms + NORM_EPS) * nw_ref[...]
    hb_ref[...] = h.astype(BF16)
    hi, lo = _split2(h)
    logits = _dot(hi, wh_ref[...]) + _dot(lo, wh_ref[...]) + _dot(hi, wl_ref[...])
    lane = lax.broadcasted_iota(jnp.int32, logits.shape, 1)
    valid = lane < N_EXPERTS
    logits = jnp.where(valid, logits, -1e30)
    e = jnp.where(valid, jnp.exp(logits - jnp.max(logits, axis=-1, keepdims=True)), 0.0)
    aff = e / jnp.sum(e, axis=-1, keepdims=True)
    aff_ref[...] = aff
    afft_ref[...] = aff.T[:N_EXPERTS, :]


def router(x1, norm_w, w_router, tm=512):
    m, k = x1.shape
    wpad = jnp.zeros((k, LANES), F32).at[:, :N_EXPERTS].set(w_router)
    wh, wl = _split2(wpad)
    return pl.pallas_call(
        _router_kernel,
        grid=(m // tm,),
        in_specs=[
            pl.BlockSpec((tm, k), lambda i: (i, 0)),
            pl.BlockSpec((1, k), lambda i: (0, 0)),
            pl.BlockSpec((k, LANES), lambda i: (0, 0)),
            pl.BlockSpec((k, LANES), lambda i: (0, 0)),
        ],
        out_specs=[
            pl.BlockSpec((tm, k), lambda i: (i, 0)),
            pl.BlockSpec((tm, LANES), lambda i: (i, 0)),
            pl.BlockSpec((N_EXPERTS, tm), lambda i: (0, i)),
        ],
        out_shape=[
            jax.ShapeDtypeStruct((m, k), BF16),
            jax.ShapeDtypeStruct((m, LANES), F32),
            jax.ShapeDtypeStruct((N_EXPERTS, m), F32),
        ],
        compiler_params=_params("parallel"),
        name="router",
    )(x1, norm_w.reshape(1, k), wh, wl)


def _topk_kernel(aff_ref, afft_ref, sp_ref, spt_ref, sel_scr, *, cap, tr):
    seq = aff_ref.shape[0]
    lane = lax.broadcasted_iota(jnp.int32, (tr, LANES), 1)
    for i in range(seq // tr):
        a_tile = aff_ref[i * tr:(i + 1) * tr, :]
        t_row = lax.broadcasted_iota(jnp.int32, (tr, seq), 0) + i * tr
        t_col = lax.broadcasted_iota(jnp.int32, (tr, seq), 1)
        earlier = t_col < t_row
        sel_tile = jnp.zeros((tr, LANES), F32)
        for e in range(N_EXPERTS):
            mine = a_tile[:, e:e + 1]
            other = afft_ref[e:e + 1, :]
            ahead = (other > mine) | ((other == mine) & earlier)
            cnt = jnp.sum(ahead.astype(F32), axis=-1, keepdims=True)
            sel_tile = jnp.where(lane == e, (cnt < cap).astype(F32), sel_tile)
        sel_scr[i * tr:(i + 1) * tr, :] = sel_tile
    sel_all = sel_scr[...].astype(BF16)
    for i in range(seq // tr):
        t_row = lax.broadcasted_iota(jnp.int32, (tr, seq), 0) + i * tr
        t_col = lax.broadcasted_iota(jnp.int32, (tr, seq), 1)
        before = (t_col < t_row).astype(BF16)
        pos = _dot(before, sel_all)
        sp = jnp.where(sel_scr[i * tr:(i + 1) * tr, :] > 0.5, pos, -1.0)
        sp_ref[i * tr:(i + 1) * tr, :] = sp
        spt_ref[:, i * tr:(i + 1) * tr] = sp.T[:N_EXPERTS, :]


def topk_select(aff, afft, bsz, seq, cap, tr=256):
    return pl.pallas_call(
        functools.partial(_topk_kernel, cap=cap, tr=tr),
        grid=(bsz,),
        in_specs=[
            pl.BlockSpec((seq, LANES), lambda b: (b, 0)),
            pl.BlockSpec((N_EXPERTS, seq), lambda b: (0, b)),
        ],
        out_specs=[
            pl.BlockSpec((seq, LANES), lambda b: (b, 0)),
            pl.BlockSpec((N_EXPERTS, seq), lambda b: (0, b)),
        ],
        out_shape=[
            jax.ShapeDtypeStruct((bsz * seq, LANES), F32),
            jax.ShapeDtypeStruct((N_EXPERTS, bsz * seq), F32),
        ],
        scratch_shapes=[pltpu.VMEM((seq, LANES), F32)],
        compiler_params=_params("parallel"),
        name="topk_select",
    )(aff, afft)


def _gather_kernel(spt_ref, h_ref, xe_ref, *, cap):
    e = pl.program_id(1)
    seq = h_ref.shape[0]
    sp = spt_ref[pl.ds(e, 1), :]
    slot = lax.broadcasted_iota(jnp.int32, (cap, seq), 0).astype(F32)
    onehot = (sp == slot).astype(BF16)
    xe_ref[0] = _dot(onehot, h_ref[...]).astype(BF16)


def moe_gather(spt, hb, bsz, seq, cap):
    d = hb.shape[1]
    return pl.pallas_call(
        functools.partial(_gather_kernel, cap=cap),
        grid=(bsz, N_EXPERTS),
        in_specs=[
            pl.BlockSpec((N_EXPERTS, seq), lambda b, e: (0, b)),
            pl.BlockSpec((seq, d), lambda b, e: (b, 0)),
        ],
        out_specs=pl.BlockSpec((1, cap, d), lambda b, e: (e, b, 0)),
        out_shape=jax.ShapeDtypeStruct((N_EXPERTS, bsz * cap, d), BF16),
        compiler_params=_params("parallel", "arbitrary"),
        name="moe_gather",
    )(spt, hb)


def _expert_kernel(xe_ref, wg_ref, wu_ref, wd_ref, ye_ref, acc_scr):
    f = pl.program_id(1)

    @pl.when(f == 0)
    def _():
        acc_scr[...] = jnp.zeros_like(acc_scr)

    xe = xe_ref[0]
    gate = _dot(xe, wg_ref[0].astype(BF16))
    up = _dot(xe, wu_ref[0].astype(BF16))
    hid = (gate * jax.nn.sigmoid(gate) * up).astype(BF16)
    acc_scr[...] += _dot(hid, wd_ref[0].astype(BF16))

    @pl.when(f == pl.num_programs(1) - 1)
    def _():
        ye_ref[0] = acc_scr[...].astype(ye_ref.dtype)


def moe_experts(xe, e_gate, e_up, e_down, tf=256):
    ne, rows, d = xe.shape
    ff = e_gate.shape[2]
    return pl.pallas_call(
        _expert_kernel,
        grid=(ne, ff // tf),
        in_specs=[
            pl.BlockSpec((1, rows, d), lambda e, f: (e, 0, 0)),
            pl.BlockSpec((1, d, tf), lambda e, f: (e, 0, f)),
            pl.BlockSpec((1, d, tf), lambda e, f: (e, 0, f)),
            pl.BlockSpec((1, tf, d), lambda e, f: (e, f, 0)),
        ],
        out_specs=pl.BlockSpec((1, rows, d), lambda e, f: (e, 0, 0)),
        out_shape=jax.ShapeDtypeStruct((ne, rows, d), BF16),
        scratch_shapes=[pltpu.VMEM((rows, d), F32)],
        compiler_params=_params("parallel", "arbitrary"),
        name="moe_experts",
    )(xe, e_gate, e_up, e_down)


def _combine_kernel(x_ref, sp_ref, aff_ref, ye_ref, nw_ref, o_ref, acc_scr, *, cap):
    e = pl.program_id(2)

    @pl.when(e == 0)
    def _():
        acc_scr[...] = x_ref[...]

    tt = x_ref.shape[0]
    lane = lax.broadcasted_iota(jnp.int32, (tt, LANES), 1)
    pick = lane == e
    sp = jnp.sum(jnp.where(pick, sp_ref[...], 0.0), axis=-1, keepdims=True)
    gate = jnp.sum(jnp.where(pick, aff_ref[...], 0.0), axis=-1, keepdims=True)
    slot = lax.broadcasted_iota(jnp.int32, (tt, cap), 1).astype(F32)
    onehot = (sp == slot).astype(BF16)
    acc_scr[...] += gate * _dot(onehot, ye_ref[0])

    @pl.when(e == pl.num_programs(2) - 1)
    def _():
        y = acc_scr[...]
        ms = jnp.mean(y * y, axis=-1, keepdims=True)
        o_ref[...] = y * lax.rsqrt(ms + NORM_EPS) * nw_ref[...]


def moe_combine(x1, sp, aff, ye, norm_w, bsz, seq, cap, tt=512):
    m, d = x1.shape
    nt = seq // tt
    return pl.pallas_call(
        functools.partial(_combine_kernel, cap=cap),
        grid=(bsz, nt, N_EXPERTS),
        in_specs=[
            pl.BlockSpec((tt, d), lambda b, i, e: (b * nt + i, 0)),
            pl.BlockSpec((tt, LANES), lambda b, i, e: (b * nt + i, 0)),
            pl.BlockSpec((tt, LANES), lambda b, i, e: (b * nt + i, 0)),
            pl.BlockSpec((1, cap, d), lambda b, i, e: (e, b, 0)),
            pl.BlockSpec((1, d), lambda b, i, e: (0, 0)),
        ],
        out_specs=pl.BlockSpec((tt, d), lambda b, i, e: (b * nt + i, 0)),
        out_shape=jax.ShapeDtypeStruct((m, d), F32),
        scratch_shapes=[pltpu.VMEM((tt, d), F32)],
        compiler_params=_params("parallel", "parallel", "arbitrary"),
        name="moe_combine",
    )(x1, sp, aff, ye, norm_w.reshape(1, d))


def _pad_cols(w, width):
    return jnp.pad(w, [(0, 0)] * (w.ndim - 1) + [(0, width - w.shape[-1])])


def rwkv_mixer(zs, w0, decay_up, a0, iclr_up, gate_up, k_k, k_a, r_k, ln_x_w, ln_x_b, bsz, seq):
    ab, rb, bt, kt, p_last, vb, bonus, g = rwkv_prep(
        zs, w0, decay_up, a0, iclr_up, gate_up, k_k, k_a, r_k, bsz, seq)
    q, yb, gm, hm = rwkv_chunk(ab, rb, bt, kt, vb, p_last, bsz, seq)
    y = rwkv_state(q, gm, hm, yb, bsz, seq)
    return rwkv_post(y, bonus, g, ln_x_w, ln_x_b)


def moe_block(x1, norm2_w, w_router, e_gate, e_up, e_down, norm_f_w, bsz, seq):
    cap = CAPACITY_FACTOR * seq // N_EXPERTS
    hb, aff, afft = router(x1, norm2_w, w_router)
    sp, spt = topk_select(aff, afft, bsz, seq, cap)
    xe = moe_gather(spt, hb, bsz, seq, cap)
    ye = moe_experts(xe, e_gate, e_up, e_down)
    return moe_combine(x1, sp, aff, ye, norm_f_w, bsz, seq, cap)


def kernel(x, positions, norm1_w, w_in, mu_prev, mu_next, lambda_q1, lambda_k1, lambda_q2, lambda_k2, subln_w, w0, decay_up, a0, iclr_up, gate_up, k_k, k_a, r_k, ln_x_w, ln_x_b, w_out, norm2_w, w_router, e_gate, e_up, e_down, norm_f_w):
    bsz, seq, d = x.shape
    m = bsz * seq
    xf = x.reshape(m, d)
    pos = positions.reshape(m, 1)
    lambda_init = 0.8 - 0.6 * math.exp(-0.3 * 0)

    w_att = w_in[0][:, :ATT_COLS].astype(BF16)
    w_z = _pad_cols(w_in[0][:, ATT_COLS:], SHIFT_PAD).astype(BF16)
    p_att = norm_matmul(xf, norm1_w[0], w_att, F32)
    p_z = norm_matmul(xf, norm1_w[0], w_z, F32)

    att = attention(p_att, pos, lambda_q1[0], lambda_k1[0], lambda_q2[0], lambda_k2[0],
                    subln_w[0], bsz, seq, lambda_init)

    zs = token_shift(p_z, _pad_cols(mu_prev, SHIFT_PAD), _pad_cols(mu_next, SHIFT_PAD), bsz, seq)
    rw = rwkv_mixer(zs, w0[0], decay_up[0], a0[0], iclr_up[0], gate_up[0], k_k[0], k_a[0], r_k[0],
                    ln_x_w[0], ln_x_b[0], bsz, seq)

    wo = w_out[0].astype(BF16)
    x1 = out_proj(xf, att, rw, wo[:ATT_WIDTH], wo[ATT_WIDTH:])

    out = moe_block(x1, norm2_w[0], w_router[0], e_gate[0], e_up[0], e_down[0], norm_f_w, bsz, seq)
    return out.reshape(bsz, seq, d)
```

```python
import functools
import math

import jax
import jax.numpy as jnp
from jax import lax
from jax.experimental import pallas as pl
from jax.experimental.pallas import tpu as pltpu

F32 = jnp.float32
BF16 = jnp.bfloat16

LANES = 128
VMEM_LIMIT_BYTES = 56 * 1024 * 1024

D_MODEL = 2048
NORM_EPS = 1e-6
ATT_HEADS = 8
ATT_QK_DIM = 64
ATT_V_DIM = 128
ATT_WIDTH = ATT_HEADS * ATT_V_DIM
ATT_QK_COLS = ATT_HEADS * 2 * ATT_QK_DIM
ATT_COLS = 2 * ATT_QK_COLS + ATT_WIDTH
ROPE_THETA = 500000.0
ROPE_DIM = ATT_QK_DIM // 4
SUBLN_EPS = 1e-5
RWKV_WIDTH = 1024
RWKV_HEAD = 64
RWKV_HEADS = RWKV_WIDTH // RWKV_HEAD
DECAY_LORA = 64
ICLR_LORA = 64
GATE_LORA = 160
LORA_COLS = DECAY_LORA + ICLR_LORA + GATE_LORA
LORA_PAD = 384
GN_EPS = 64e-5
SHIFT_WIDTH = 3 * RWKV_WIDTH + LORA_COLS
SHIFT_PAD = 3 * RWKV_WIDTH + LORA_PAD
N_EXPERTS = 16
CAPACITY_FACTOR = 2
EXPERT_FF = 2048
CHUNK = 64


def _params(*sem):
    return pltpu.CompilerParams(dimension_semantics=sem, vmem_limit_bytes=VMEM_LIMIT_BYTES)


def _dot(a, b):
    return jnp.dot(a, b, preferred_element_type=F32)


def _dot_nt(a, b):
    return lax.dot_general(a, b, (((1,), (1,)), ((), ())), preferred_element_type=F32)


def _split2(x):
    hi = x.astype(BF16)
    lo = (x - hi.astype(F32)).astype(BF16)
    return hi, lo


def _dot_lhs2(x, m_bf16):
    hi, lo = _split2(x)
    return _dot(hi, m_bf16) + _dot(lo, m_bf16)


def _norm_mm_kernel(x_ref, nw_ref, w_ref, o_ref, h_scr):
    @pl.when(pl.program_id(1) == 0)
    def _():
        x = x_ref[...]
        ms = jnp.mean(x * x, axis=-1, keepdims=True)
        h_scr[...] = (x * lax.rsqrt(ms + NORM_EPS) * nw_ref[...]).astype(BF16)

    o_ref[...] = _dot(h_scr[...], w_ref[...]).astype(o_ref.dtype)


def norm_matmul(x, nw, w_bf16, out_dtype, tm=1024, tn=384):
    m, k = x.shape
    n = w_bf16.shape[1]
    return pl.pallas_call(
        _norm_mm_kernel,
        grid=(m // tm, n // tn),
        in_specs=[
            pl.BlockSpec((tm, k), lambda i, j: (i, 0)),
            pl.BlockSpec((1, k), lambda i, j: (0, 0)),
            pl.BlockSpec((k, tn), lambda i, j: (0, j)),
        ],
        out_specs=pl.BlockSpec((tm, tn), lambda i, j: (i, j)),
        out_shape=jax.ShapeDtypeStruct((m, n), out_dtype),
        scratch_shapes=[pltpu.VMEM((tm, k), BF16)],
        compiler_params=_params("parallel", "arbitrary"),
        name="norm_matmul",
    )(x, nw.reshape(1, k), w_bf16)


def _rope(x, pos, freq, sgn_lo, sgn_hi):
    ang = pos.astype(F32) * freq
    c = jnp.cos(ang)
    s = jnp.sin(ang)
    half = ROPE_DIM // 2
    x_dn = pltpu.roll(x, half, axis=1)
    x_up = pltpu.roll(x, LANES - half, axis=1)
    return x * c + x_dn * (s * sgn_hi) + x_up * (s * sgn_lo)


def _attn_kernel(q_ref, k_ref, v_ref, posq_ref, posk_ref, freq_ref, slo_ref, shi_ref,
                 lq1_ref, lk1_ref, lq2_ref, lk2_ref, sw_ref, o_ref, k_scr, v_scr, *, lambda_init):
    freq = freq_ref[...]
    slo = slo_ref[...]
    shi = shi_ref[...]

    @pl.when(pl.program_id(2) == 0)
    def _():
        k_scr[...] = _rope(k_ref[...], posk_ref[...], freq, slo, shi).astype(BF16)
        v_scr[...] = v_ref[...].astype(BF16)

    q = (_rope(q_ref[...], posq_ref[...], freq, slo, shi) * (ATT_QK_DIM ** -0.5)).astype(BF16)
    kr = k_scr[...]
    s1 = _dot_nt(q[:, :ATT_QK_DIM], kr[:, :ATT_QK_DIM])
    s2 = _dot_nt(q[:, ATT_QK_DIM:], kr[:, ATT_QK_DIM:])
    e1 = jnp.exp(s1 - jnp.max(s1, axis=-1, keepdims=True))
    e2 = jnp.exp(s2 - jnp.max(s2, axis=-1, keepdims=True))
    lam = (jnp.exp(jnp.sum(lq1_ref[...] * lk1_ref[...], axis=-1, keepdims=True))
           - jnp.exp(jnp.sum(lq2_ref[...] * lk2_ref[...], axis=-1, keepdims=True)) + lambda_init)
    r1 = 1.0 / jnp.sum(e1, axis=-1, keepdims=True)
    r2 = lam / jnp.sum(e2, axis=-1, keepdims=True)
    wgt = (e1 * r1 - e2 * r2).astype(BF16)
    o = _dot(wgt, v_scr[...])
    ms = jnp.mean(o * o, axis=-1, keepdims=True)
    o = o * lax.rsqrt(ms + SUBLN_EPS) * sw_ref[...] * (1.0 - lambda_init)
    o_ref[...] = o.astype(o_ref.dtype)


def attention(p_att, pos, lq1, lk1, lq2, lk2, subln_w, bsz, seq, lambda_init, tq=256):
    nq = seq // tq
    lane = jnp.arange(LANES) % ATT_QK_DIM
    half = ROPE_DIM // 2
    inv_freq = ROPE_THETA ** (-jnp.arange(0, ROPE_DIM, 2, dtype=F32) / ROPE_DIM)
    freq = jnp.where(lane < ROPE_DIM, inv_freq[lane % half], 0.0).astype(F32).reshape(1, LANES)
    sgn_lo = jnp.where(lane < half, -1.0, 0.0).astype(F32).reshape(1, LANES)
    sgn_hi = jnp.where((lane >= half) & (lane < ROPE_DIM), 1.0, 0.0).astype(F32).reshape(1, LANES)
    vec = lambda n: pl.BlockSpec((1, n), lambda b, h, i: (0, 0))
    nh = ATT_HEADS
    return pl.pallas_call(
        functools.partial(_attn_kernel, lambda_init=lambda_init),
        grid=(bsz, nh, nq),
        in_specs=[
            pl.BlockSpec((tq, LANES), lambda b, h, i: (b * nq + i, h)),
            pl.BlockSpec((seq, LANES), lambda b, h, i: (b, nh + h)),
            pl.BlockSpec((seq, LANES), lambda b, h, i: (b, 2 * nh + h)),
            pl.BlockSpec((tq, 1), lambda b, h, i: (b * nq + i, 0)),
            pl.BlockSpec((seq, 1), lambda b, h, i: (b, 0)),
            vec(LANES), vec(LANES), vec(LANES),
            vec(ATT_QK_DIM), vec(ATT_QK_DIM), vec(ATT_QK_DIM), vec(ATT_QK_DIM),
            vec(ATT_V_DIM),
        ],
        out_specs=pl.BlockSpec((tq, LANES), lambda b, h, i: (b * nq + i, h)),
        out_shape=jax.ShapeDtypeStruct((bsz * seq, ATT_WIDTH), BF16),
        scratch_shapes=[pltpu.VMEM((seq, LANES), BF16), pltpu.VMEM((seq, LANES), BF16)],
        compiler_params=_params("parallel", "parallel", "arbitrary"),
        name="diff_attention",
    )(p_att, p_att, p_att, pos, pos, freq, sgn_lo, sgn_hi,
      lq1.reshape(1, -1), lk1.reshape(1, -1), lq2.reshape(1, -1), lk2.reshape(1, -1),
      subln_w.reshape(1, -1))


def _shift_kernel(z_ref, mp_ref, mn_ref, o_ref):
    z = z_ref[...]
    n = z.shape[0]
    row = lax.broadcasted_iota(jnp.int32, z.shape, 0)
    zp = jnp.where(row == 0, 0.0, pltpu.roll(z, 1, axis=0))
    zn = jnp.where(row == n - 1, 0.0, pltpu.roll(z, n - 1, axis=0))
    o_ref[...] = z + mp_ref[...] * (zp - z) + mn_ref[...] * (zn - z)


def token_shift(pz, mu_prev, mu_next, bsz, seq):
    width = pz.shape[1]
    nt = width // LANES
    return pl.pallas_call(
        _shift_kernel,
        grid=(bsz, nt),
        in_specs=[
            pl.BlockSpec((seq, LANES), lambda b, j: (b, j)),
            pl.BlockSpec((1, LANES), lambda b, j: (0, j)),
            pl.BlockSpec((1, LANES), lambda b, j: (0, j)),
        ],
        out_specs=pl.BlockSpec((seq, LANES), lambda b, j: (b, j)),
        out_shape=jax.ShapeDtypeStruct(pz.shape, F32),
        compiler_params=_params("parallel", "parallel"),
        name="token_shift",
    )(pz, mu_prev, mu_next)


def _seg_sum(x, ones_blk):
    parts = []
    for j in range(x.shape[1] // LANES):
        parts.append(_dot_lhs2(x[:, j * LANES:(j + 1) * LANES], ones_blk))
    return jnp.concatenate(parts, axis=1)


def _prep_kernel(r_ref, k_ref, v_ref, lo_ref, w0_ref, du_ref, a0_ref, iu_ref, gu_ref, kk_ref,
                 ka_ref, rk_ref, ones_ref, tril_ref, triu_ref,
                 ab_ref, rb_ref, bt_ref, kt_ref, pl_ref, vb_ref, bonus_ref, g_ref, *, tt):
    r = r_ref[...]
    k = k_ref[...]
    v = v_ref[...]
    lo = lo_ref[...]
    ones_blk = ones_ref[...]
    vb_ref[...] = v.astype(BF16)
    g_ref[...] = _dot(jax.nn.sigmoid(lo).astype(BF16), gu_ref[...])
    kk = k * kk_ref[...]
    nrm = jnp.sqrt(_seg_sum(kk * kk, ones_blk))
    kk = kk / jnp.maximum(nrm, 1e-12)
    th = jnp.tanh(lo).astype(BF16)
    lob = lo.astype(BF16)
    nchunk = tt // CHUNK
    ksum = jnp.zeros_like(k)
    for d in range(2):
        wl = w0_ref[d] + _dot(th, du_ref[d])
        w_log = -jax.nn.softplus(-wl) - 0.5
        lw = -jnp.exp(w_log)
        a = jax.nn.sigmoid(a0_ref[d] + _dot(lob, iu_ref[d]))
        kd = k * (1.0 + (a - 1.0) * ka_ref[...])
        ksum = ksum + kd
        tri = tril_ref[...] if d == 0 else triu_ref[...]
        cum = _dot_lhs2_rhs(tri, lw)
        e_pos = jnp.exp(cum)
        e_neg = jnp.exp(-cum)
        ab_ref[d] = (-kk * jnp.exp(cum - lw)).astype(BF16)
        rb_ref[d] = (r * e_pos).astype(BF16)
        bt_ref[d] = (kk * a * e_neg).astype(BF16)
        kt_ref[d] = (kd * e_neg).astype(BF16)
        for c in range(nchunk):
            last = c * CHUNK + (CHUNK - 1 if d == 0 else 0)
            pl_ref[d, c] = e_pos[last:last + 1, :]
    bonus_ref[...] = _seg_sum(r * ksum * rk_ref[...], ones_blk) * v


def _dot_lhs2_rhs(tri_bf16, x):
    hi, lo = _split2(x)
    return _dot(tri_bf16, hi) + _dot(tri_bf16, lo)


def rwkv_prep(zs, w0, decay_up, a0, iclr_up, gate_up, k_k, k_a, r_k, bsz, seq, tt=256):
    m = bsz * seq
    c = RWKV_WIDTH
    nt = seq // tt
    ncb = tt // CHUNK
    nc = seq // CHUNK

    def pad_rows(w, start):
        out = jnp.zeros(w.shape[:-2] + (LORA_PAD, c), F32)
        return lax.dynamic_update_slice_in_dim(out, w.astype(F32), start, axis=w.ndim - 2).astype(BF16)

    du = pad_rows(decay_up, 0)
    iu = pad_rows(iclr_up, DECAY_LORA)
    gu = pad_rows(gate_up, DECAY_LORA + ICLR_LORA)
    lane = jnp.arange(LANES)
    ones_blk = (lane[:, None] // RWKV_HEAD == lane[None, :] // RWKV_HEAD).astype(BF16)
    t = jnp.arange(tt)
    same = t[:, None] // CHUNK == t[None, :] // CHUNK
    tril = (same & (t[:, None] >= t[None, :])).astype(BF16)
    triu = (same & (t[:, None] <= t[None, :])).astype(BF16)
    r_k_flat = r_k.reshape(1, c)
    row = lambda: pl.BlockSpec((1, c), lambda b, i: (0, 0))
    full3 = lambda s: pl.BlockSpec(s, lambda b, i: (0, 0, 0))
    tok = lambda j, w: pl.BlockSpec((tt, w), lambda b, i: (b * nt + i, j))
    dir_tok = pl.BlockSpec((2, tt, c), lambda b, i: (0, b * nt + i, 0))
    outs = pl.pallas_call(
        functools.partial(_prep_kernel, tt=tt),
        grid=(bsz, nt),
        in_specs=[
            tok(0, c), tok(1, c), tok(2, c),
            pl.BlockSpec((tt, LORA_PAD), lambda b, i: (b * nt + i, 3 * c // LORA_PAD)),
            full3((2, 1, c)), full3((2, LORA_PAD, c)), full3((2, 1, c)), full3((2, LORA_PAD, c)),
            pl.BlockSpec((LORA_PAD, c), lambda b, i: (0, 0)),
            row(), row(), row(),
            pl.BlockSpec((LANES, LANES), lambda b, i: (0, 0)),
            pl.BlockSpec((tt, tt), lambda b, i: (0, 0)),
            pl.BlockSpec((tt, tt), lambda b, i: (0, 0)),
        ],
        out_specs=[
            dir_tok, dir_tok, dir_tok, dir_tok,
            pl.BlockSpec((2, ncb, 1, c), lambda b, i: (0, b * nt + i, 0, 0)),
            pl.BlockSpec((tt, c), lambda b, i: (b * nt + i, 0)),
            pl.BlockSpec((tt, c), lambda b, i: (b * nt + i, 0)),
            pl.BlockSpec((tt, c), lambda b, i: (b * nt + i, 0)),
        ],
        out_shape=[
            jax.ShapeDtypeStruct((2, m, c), BF16),
            jax.ShapeDtypeStruct((2, m, c), BF16),
            jax.ShapeDtypeStruct((2, m, c), BF16),
            jax.ShapeDtypeStruct((2, m, c), BF16),
            jax.ShapeDtypeStruct((2, bsz * nc, 1, c), F32),
            jax.ShapeDtypeStruct((m, c), BF16),
            jax.ShapeDtypeStruct((m, c), F32),
            jax.ShapeDtypeStruct((m, c), F32),
        ],
        compiler_params=_params("parallel", "parallel"),
        name="rwkv_prep",
    )(zs, zs, zs, zs, w0.reshape(2, 1, c), du, a0.reshape(2, 1, c), iu, gu,
      k_k.reshape(1, c), k_a.reshape(1, c), r_k_flat, ones_blk, tril, triu)
    return outs


def _chunk_kernel(ab_ref, rb_ref, bt_ref, kt_ref, v_ref, pl_ref, q_ref, yb_ref, g_ref, h_ref, *, heads):
    n = CHUNK
    row = lax.broadcasted_iota(jnp.int32, (2 * n, 2 * n), 0)
    col = lax.broadcasted_iota(jnp.int32, (2 * n, 2 * n), 1)
    top = row < n
    tr = row % n
    tc = col % n
    eye = (lax.broadcasted_iota(jnp.int32, (n, n), 0)
           == lax.broadcasted_iota(jnp.int32, (n, n), 1)).astype(F32)
    keep = ((tr > tc) | (~top & (tr == tc)), (tr < tc) | (~top & (tr == tc)))
    probs = [(d, hh) for d in range(2) for hh in range(heads)]
    sl = lambda hh: slice(hh * n, (hh + 1) * n)
    ab = [ab_ref[d, :, sl(hh)] for d, hh in probs]
    rb = [rb_ref[d, :, sl(hh)] for d, hh in probs]
    bt = [bt_ref[d, :, sl(hh)] for d, hh in probs]
    kt = [kt_ref[d, :, sl(hh)] for d, hh in probs]
    vv = [v_ref[:, sl(hh)] for d, hh in probs]
    p_last = [pl_ref[d, 0, :, sl(hh)] for d, hh in probs]
    idx = range(len(probs))
    g1 = [_dot_nt(jnp.concatenate([ab[i], rb[i]], axis=0), jnp.concatenate([bt[i], kt[i]], axis=0))
          for i in idx]
    g1 = [jnp.where(keep[probs[i][0]], g1[i], 0.0) for i in idx]
    a_ab = [g[:n, :n] for g in g1]
    low = [g[n:, :].astype(BF16) for g in g1]
    tm = [eye + a for a in a_ab]
    pw = [_dot(a.astype(BF16), a.astype(BF16)) for a in a_ab]
    span = 2
    while span < n:
        last = span * 2 >= n
        nxt = []
        for i in idx:
            pwb = pw[i].astype(BF16)
            if last:
                nxt.append((tm[i] + _dot(tm[i].astype(BF16), pwb), None))
            else:
                both = _dot(jnp.concatenate([tm[i], pw[i]], axis=0).astype(BF16), pwb)
                nxt.append((tm[i] + both[:n], both[n:]))
        tm = [t for t, _ in nxt]
        pw = [p for _, p in nxt]
        span *= 2
    wc = [_dot(tm[i].astype(BF16), jnp.concatenate([ab[i], g1[i][:n, n:].astype(BF16)], axis=1)) for i in idx]
    w = [x[:, :n].astype(BF16) for x in wc]
    cuv = [_dot(wc[i][:, n:].astype(BF16), vv[i]).astype(BF16) for i in idx]
    rhs = [jnp.concatenate([cuv[i], vv[i]], axis=0) for i in idx]
    for i, (d, hh) in enumerate(probs):
        q_ref[d, 0, 0, hh] = rb[i].astype(F32) + _dot(g1[i][n:, :n].astype(BF16), w[i])
        yb_ref[d, 0, 0, hh] = _dot(low[i], rhs[i])
    for i, (d, hh) in enumerate(probs):
        bh = bt[i].astype(F32) * p_last[i]
        kh = kt[i].astype(F32) * p_last[i]
        g_ref[d, 0, 0, hh] = eye * p_last[i] + _dot(bh.T.astype(BF16), w[i])
        lhs_t = jnp.concatenate([bh.T, kh.T], axis=1).astype(BF16)
        h_ref[d, 0, 0, hh] = _dot(lhs_t, rhs[i])


def rwkv_chunk(ab, rb, bt, kt, vb, p_last, bsz, seq, heads=16):
    nc = seq // CHUNK
    nh = RWKV_HEADS
    n = CHUNK
    width = heads * n
    dir_blk = pl.BlockSpec((2, n, width), lambda b, c, hp: (0, b * nc + c, hp))
    out_blk = pl.BlockSpec((2, 1, 1, heads, n, n), lambda b, c, hp: (0, b, c, hp, 0, 0))
    shape = jax.ShapeDtypeStruct((2, bsz, nc, nh, n, n), F32)
    return pl.pallas_call(
        functools.partial(_chunk_kernel, heads=heads),
        grid=(bsz, nc, nh // heads),
        in_specs=[
            dir_blk, dir_blk, dir_blk, dir_blk,
            pl.BlockSpec((n, width), lambda b, c, hp: (b * nc + c, hp)),
            pl.BlockSpec((2, 1, 1, width), lambda b, c, hp: (0, b * nc + c, 0, hp)),
        ],
        out_specs=[out_blk, out_blk, out_blk, out_blk],
        out_shape=[shape, shape, shape, shape],
        compiler_params=_params("parallel", "parallel", "parallel"),
        name="rwkv_chunk",
    )(ab, rb, bt, kt, vb, p_last)


def _state_kernel(q_ref, g_ref, h_ref, yb_ref, y_ref, m_scr):
    @pl.when(pl.program_id(2) == 0)
    def _():
        m_scr[...] = jnp.zeros_like(m_scr)

    n = CHUNK
    for h in range(RWKV_HEADS):
        mb = m_scr[h].astype(BF16)
        y_ref[0, :, h * n:(h + 1) * n] = _dot(q_ref[0, 0, 0, h].astype(BF16), mb) + yb_ref[0, 0, 0, h]
        m_scr[h] = _dot(g_ref[0, 0, 0, h].astype(BF16), mb) + h_ref[0, 0, 0, h]


def rwkv_state(q, g, h, yb, bsz, seq):
    nc = seq // CHUNK
    nh = RWKV_HEADS
    n = CHUNK

    def chunk_of(d, c):
        return jnp.where(d == 0, c, nc - 1 - c)

    blk = pl.BlockSpec((1, 1, 1, nh, n, n), lambda d, b, c: (d, b, chunk_of(d, c), 0, 0, 0))
    return pl.pallas_call(
        _state_kernel,
        grid=(2, bsz, nc),
        in_specs=[blk, blk, blk, blk],
        out_specs=pl.BlockSpec((1, n, RWKV_WIDTH), lambda d, b, c: (d, b * nc + chunk_of(d, c), 0)),
        out_shape=jax.ShapeDtypeStruct((2, bsz * seq, RWKV_WIDTH), F32),
        scratch_shapes=[pltpu.VMEM((nh, n, n), F32)],
        compiler_params=_params("parallel", "parallel", "arbitrary"),
        name="rwkv_state",
    )(q, g, h, yb)


def _post_kernel(y_ref, bonus_ref, g_ref, lw_ref, lb_ref, o_ref):
    y = y_ref[0] + y_ref[1]
    n = RWKV_HEAD
    parts = []
    for h in range(RWKV_HEADS):
        yh = y[:, h * n:(h + 1) * n]
        mu = jnp.mean(yh, axis=-1, keepdims=True)
        yc = yh - mu
        var = jnp.mean(yc * yc, axis=-1, keepdims=True)
        parts.append(yc * lax.rsqrt(var + GN_EPS))
    yn = jnp.concatenate(parts, axis=1)
    o_ref[...] = ((yn * lw_ref[...] + lb_ref[...] + bonus_ref[...]) * g_ref[...]).astype(o_ref.dtype)


def rwkv_post(y, bonus, g, ln_w, ln_b, tt=256):
    m = y.shape[1]
    c = RWKV_WIDTH
    tok = pl.BlockSpec((tt, c), lambda i: (i, 0))
    row = pl.BlockSpec((1, c), lambda i: (0, 0))
    return pl.pallas_call(
        _post_kernel,
        grid=(m // tt,),
        in_specs=[pl.BlockSpec((2, tt, c), lambda i: (0, i, 0)), tok, tok, row, row],
        out_specs=tok,
        out_shape=jax.ShapeDtypeStruct((m, c), BF16),
        compiler_params=_params("parallel"),
        name="rwkv_post",
    )(y, bonus, g, ln_w.reshape(1, c), ln_b.reshape(1, c))


def _out_proj_kernel(x_ref, a_ref, r_ref, wa_ref, wr_ref, o_ref):
    o_ref[...] = x_ref[...] + _dot(a_ref[...], wa_ref[...]) + _dot(r_ref[...], wr_ref[...])


def out_proj(x, att, rw, w_att, w_rw, tm=1024, tn=512):
    m, n = x.shape
    return pl.pallas_call(
        _out_proj_kernel,
        grid=(m // tm, n // tn),
        in_specs=[
            pl.BlockSpec((tm, tn), lambda i, j: (i, j)),
            pl.BlockSpec((tm, ATT_WIDTH), lambda i, j: (i, 0)),
            pl.BlockSpec((tm, RWKV_WIDTH), lambda i, j: (i, 0)),
            pl.BlockSpec((ATT_WIDTH, tn), lambda i, j: (0, j)),
            pl.BlockSpec((RWKV_WIDTH, tn), lambda i, j: (0, j)),
        ],
        out_specs=pl.BlockSpec((tm, tn), lambda i, j: (i, j)),
        out_shape=jax.ShapeDtypeStruct((m, n), F32),
        compiler_params=_params("parallel", "parallel"),
        name="out_proj",
    )(x, att, rw, w_att, w_rw)


def _router_kernel(x_ref, nw_ref, wh_ref, wl_ref, hb_ref, aff_ref, afft_ref):
    x = x_ref[...]
    ms = jnp.mean(x * x, axis=-1, keepdims=True)
    h = x * lax.rsqrt(ms + NORM_EPS) * nw_ref[...]
    hb_ref[...] = h.astype(BF16)
    hi, lo = _split2(h)
    logits = _dot(hi, wh_ref[...]) + _dot(lo, wh_ref[...]) + _dot(hi, wl_ref[...])
    lane = lax.broadcasted_iota(jnp.int32, logits.shape, 1)
    valid = lane < N_EXPERTS
    logits = jnp.where(valid, logits, -1e30)
    e = jnp.where(valid, jnp.exp(logits - jnp.max(logits, axis=-1, keepdims=True)), 0.0)
    aff = e / jnp.sum(e, axis=-1, keepdims=True)
    aff_ref[...] = aff
    afft_ref[...] = aff.T[:N_EXPERTS, :]


def router(x1, norm_w, w_router, tm=512):
    m, k = x1.shape
    wpad = jnp.zeros((k, LANES), F32).at[:, :N_EXPERTS].set(w_router)
    wh, wl = _split2(wpad)
    return pl.pallas_call(
        _router_kernel,
        grid=(m // tm,),
        in_specs=[
            pl.BlockSpec((tm, k), lambda i: (i, 0)),
            pl.BlockSpec((1, k), lambda i: (0, 0)),
            pl.BlockSpec((k, LANES), lambda i: (0, 0)),
            pl.BlockSpec((k, LANES), lambda i: (0, 0)),
        ],
        out_specs=[
            pl.BlockSpec((tm, k), lambda i: (i, 0)),
            pl.BlockSpec((tm, LANES), lambda i: (i, 0)),
            pl.BlockSpec((N_EXPERTS, tm), lambda i: (0, i)),
        ],
        out_shape=[
            jax.ShapeDtypeStruct((m, k), BF16),
            jax.ShapeDtypeStruct((m, LANES), F32),
            jax.ShapeDtypeStruct((N_EXPERTS, m), F32),
        ],
        compiler_params=_params("parallel"),
        name="router",
    )(x1, norm_w.reshape(1, k), wh, wl)


def _topk_kernel(aff_ref, afft_ref, sp_ref, spt_ref, sel_scr, *, cap, tr):
    seq = aff_ref.shape[0]
    lane = lax.broadcasted_iota(jnp.int32, (tr, LANES), 1)
    for i in range(seq // tr):
        a_tile = aff_ref[i * tr:(i + 1) * tr, :]
        t_row = lax.broadcasted_iota(jnp.int32, (tr, seq), 0) + i * tr
        t_col = lax.broadcasted_iota(jnp.int32, (tr, seq), 1)
        earlier = t_col < t_row
        sel_tile = jnp.zeros((tr, LANES), F32)
        for e in range(N_EXPERTS):
            mine = a_tile[:, e:e + 1]
            other = afft_ref[e:e + 1, :]
            ahead = (other > mine) | ((other == mine) & earlier)
            cnt = jnp.sum(ahead.astype(F32), axis=-1, keepdims=True)
            sel_tile = jnp.where(lane == e, (cnt < cap).astype(F32), sel_tile)
        sel_scr[i * tr:(i + 1) * tr, :] = sel_tile
    sel_all = sel_scr[...].astype(BF16)
    for i in range(seq // tr):
        t_row = lax.broadcasted_iota(jnp.int32, (tr, seq), 0) + i * tr
        t_col = lax.broadcasted_iota(jnp.int32, (tr, seq), 1)
        before = (t_col < t_row).astype(BF16)
        pos = _dot(before, sel_all)
        sp = jnp.where(sel_scr[i * tr:(i + 1) * tr, :] > 0.5, pos, -1.0)
        sp_ref[i * tr:(i + 1) * tr, :] = sp
        spt_ref[:, i * tr:(i + 1) * tr] = sp.T[:N_EXPERTS, :]


def topk_select(aff, afft, bsz, seq, cap, tr=256):
    return pl.pallas_call(
        functools.partial(_topk_kernel, cap=cap, tr=tr),
        grid=(bsz,),
        in_specs=[
            pl.BlockSpec((seq, LANES), lambda b: (b, 0)),
            pl.BlockSpec((N_EXPERTS, seq), lambda b: (0, b)),
        ],
        out_specs=[
            pl.BlockSpec((seq, LANES), lambda b: (b, 0)),
            pl.BlockSpec((N_EXPERTS, seq), lambda b: (0, b)),
        ],
        out_shape=[
            jax.ShapeDtypeStruct((bsz * seq, LANES), F32),
            jax.ShapeDtypeStruct((N_EXPERTS, bsz * seq), F32),
        ],
        scratch_shapes=[pltpu.VMEM((seq, LANES), F32)],
        compiler_params=_params("parallel"),
        name="topk_select",
    )(aff, afft)


def _gather_kernel(spt_ref, h_ref, xe_ref, *, cap):
    e = pl.program_id(1)
    seq = h_ref.shape[0]
    sp = spt_ref[pl.ds(e, 1), :]
    slot = lax.broadcasted_iota(jnp.int32, (cap, seq), 0).astype(F32)
    onehot = (sp == slot).astype(BF16)
    xe_ref[0] = _dot(onehot, h_ref[...]).astype(BF16)


def moe_gather(spt, hb, bsz, seq, cap):
    d = hb.shape[1]
    return pl.pallas_call(
        functools.partial(_gather_kernel, cap=cap),
        grid=(bsz, N_EXPERTS),
        in_specs=[
            pl.BlockSpec((N_EXPERTS, seq), lambda b, e: (0, b)),
            pl.BlockSpec((seq, d), lambda b, e: (b, 0)),
        ],
        out_specs=pl.BlockSpec((1, cap, d), lambda b, e: (e, b, 0)),
        out_shape=jax.ShapeDtypeStruct((N_EXPERTS, bsz * cap, d), BF16),
        compiler_params=_params("parallel", "arbitrary"),
        name="moe_gather",
    )(spt, hb)


def _expert_kernel(xe_ref, wg_ref, wu_ref, wd_ref, ye_ref, acc_scr):
    f = pl.program_id(1)

    @pl.when(f == 0)
    def _():
        acc_scr[...] = jnp.zeros_like(acc_scr)

    xe = xe_ref[0]
    gate = _dot(xe, wg_ref[0].astype(BF16))
    up = _dot(xe, wu_ref[0].astype(BF16))
    hid = (gate * jax.nn.sigmoid(gate) * up).astype(BF16)
    acc_scr[...] += _dot(hid, wd_ref[0].astype(BF16))

    @pl.when(f == pl.num_programs(1) - 1)
    def _():
        ye_ref[0] = acc_scr[...].astype(ye_ref.dtype)


def moe_experts(xe, e_gate, e_up, e_down, tf=256):
    ne, rows, d = xe.shape
    ff = e_gate.shape[2]
    return pl.pallas_call(
        _expert_kernel,
        grid=(ne, ff // tf),
        in_specs=[
            pl.BlockSpec((1, rows, d), lambda e, f: (e, 0, 0)),
            pl.BlockSpec((1, d, tf), lambda e, f: (e, 0, f)),
            pl.BlockSpec((1, d, tf), lambda e, f: (e, 0, f)),
            pl.BlockSpec((1, tf, d), lambda e, f: (e, f, 0)),
        ],
        out_specs=pl.BlockSpec((1, rows, d), lambda e, f: (e, 0, 0)),
        out_shape=jax.ShapeDtypeStruct((ne, rows, d), BF16),
        scratch_shapes=[pltpu.VMEM((rows, d), F32)],
        compiler_params=_params("parallel", "arbitrary"),
        name="moe_experts",
    )(xe, e_gate, e_up, e_down)


def _combine_kernel(x_ref, sp_ref, aff_ref, ye_ref, nw_ref, o_ref, acc_scr, *, cap):
    e = pl.program_id(2)

    @pl.when(e == 0)
    def _():
        acc_scr[...] = x_ref[...]

    tt = x_ref.shape[0]
    lane = lax.broadcasted_iota(jnp.int32, (tt, LANES), 1)
    pick = lane == e
    sp = jnp.sum(jnp.where(pick, sp_ref[...], 0.0), axis=-1, keepdims=True)
    gate = jnp.sum(jnp.where(pick, aff_ref[...], 0.0), axis=-1, keepdims=True)
    slot = lax.broadcasted_iota(jnp.int32, (tt, cap), 1).astype(F32)
    onehot = (sp == slot).astype(BF16)
    acc_scr[...] += gate * _dot(onehot, ye_ref[0])

    @pl.when(e == pl.num_programs(2) - 1)
    def _():
        y = acc_scr[...]
        ms = jnp.mean(y * y, axis=-1, keepdims=True)
        o_ref[...] = y * lax.rsqrt(ms + NORM_EPS) * nw_ref[...]


def moe_combine(x1, sp, aff, ye, norm_w, bsz, seq, cap, tt=512):
    m, d = x1.shape
    nt = seq // tt
    return pl.pallas_call(
        functools.partial(_combine_kernel, cap=cap),
        grid=(bsz, nt, N_EXPERTS),
        in_specs=[
            pl.BlockSpec((tt, d), lambda b, i, e: (b * nt + i, 0)),
            pl.BlockSpec((tt, LANES), lambda b, i, e: (b * nt + i, 0)),
            pl.BlockSpec((tt, LANES), lambda b, i, e: (b * nt + i, 0)),
            pl.BlockSpec((1, cap, d), lambda b, i, e: (e, b, 0)),
            pl.BlockSpec((1, d), lambda b, i, e: (0, 0)),
        ],
        out_specs=pl.BlockSpec((tt, d), lambda b, i, e: (b * nt + i, 0)),
        out_shape=jax.ShapeDtypeStruct((m, d), F32),
        scratch_shapes=[pltpu.VMEM((tt, d), F32)],
        compiler_params=_params("parallel", "parallel", "arbitrary"),
        name="moe_combine",
    )(x1, sp, aff, ye, norm_w.reshape(1, d))


def _pad_cols(w, width):
    return jnp.pad(w, [(0, 0)] * (w.ndim - 1) + [(0, width - w.shape[-1])])


def rwkv_mixer(zs, w0, decay_up, a0, iclr_up, gate_up, k_k, k_a, r_k, ln_x_w, ln_x_b, bsz, seq):
    ab, rb, bt, kt, p_last, vb, bonus, g = rwkv_prep(
        zs, w0, decay_up, a0, iclr_up, gate_up, k_k, k_a, r_k, bsz, seq)
    q, yb, gm, hm = rwkv_chunk(ab, rb, bt, kt, vb, p_last, bsz, seq)
    y = rwkv_state(q, gm, hm, yb, bsz, seq)
    return rwkv_post(y, bonus, g, ln_x_w, ln_x_b)


def moe_block(x1, norm2_w, w_router, e_gate, e_up, e_down, norm_f_w, bsz, seq):
    cap = CAPACITY_FACTOR * seq // N_EXPERTS
    hb, aff, afft = router(x1, norm2_w, w_router)
    sp, spt = topk_select(aff, afft, bsz, seq, cap)
    xe = moe_gather(spt, hb, bsz, seq, cap)
    ye = moe_experts(xe, e_gate, e_up, e_down)
    return moe_combine(x1, sp, aff, ye, norm_f_w, bsz, seq, cap)


def kernel(x, positions, norm1_w, w_in, mu_prev, mu_next, lambda_q1, lambda_k1, lambda_q2, lambda_k2, subln_w, w0, decay_up, a0, iclr_up, gate_up, k_k, k_a, r_k, ln_x_w, ln_x_b, w_out, norm2_w, w_router, e_gate, e_up, e_down, norm_f_w):
    bsz, seq, d = x.shape
    m = bsz * seq
    xf = x.reshape(m, d)
    pos = positions.reshape(m, 1)
    lambda_init = 0.8 - 0.6 * math.exp(-0.3 * 0)

    w_att = w_in[0][:, :ATT_COLS].astype(BF16)
    w_z = _pad_cols(w_in[0][:, ATT_COLS:], SHIFT_PAD).astype(BF16)
    p_att = norm_matmul(xf, norm1_w[0], w_att, F32)
    p_z = norm_matmul(xf, norm1_w[0], w_z, F32)

    att = attention(p_att, pos, lambda_q1[0], lambda_k1[0], lambda_q2[0], lambda_k2[0],
                    subln_w[0], bsz, seq, lambda_init)

    zs = token_shift(p_z, _pad_cols(mu_prev, SHIFT_PAD), _pad_cols(mu_next, SHIFT_PAD), bsz, seq)
    rw = rwkv_mixer(zs, w0[0], decay_up[0], a0[0], iclr_up[0], gate_up[0], k_k[0], k_a[0], r_k[0],
                    ln_x_w[0], ln_x_b[0], bsz, seq)

    wo = w_out[0].astype(BF16)
    x1 = out_proj(xf, att, rw, wo[:ATT_WIDTH], wo[ATT_WIDTH:])

    out = moe_block(x1, norm2_w[0], w_router[0], e_gate[0], e_up[0], e_down[0], norm_f_w, bsz, seq)
    return out.reshape(bsz, seq, d)
```

```python
import functools
import math

import jax
import jax.numpy as jnp
from jax import lax
from jax.experimental import pallas as pl
from jax.experimental.pallas import tpu as pltpu

F32 = jnp.float32
BF16 = jnp.bfloat16

LANES = 128
VMEM_LIMIT_BYTES = 56 * 1024 * 1024

D_MODEL = 2048
NORM_EPS = 1e-6
ATT_HEADS = 8
ATT_QK_DIM = 64
ATT_V_DIM = 128
ATT_WIDTH = ATT_HEADS * ATT_V_DIM
ATT_QK_COLS = ATT_HEADS * 2 * ATT_QK_DIM
ATT_COLS = 2 * ATT_QK_COLS + ATT_WIDTH
ROPE_THETA = 500000.0
ROPE_DIM = ATT_QK_DIM // 4
SUBLN_EPS = 1e-5
RWKV_WIDTH = 1024
RWKV_HEAD = 64
RWKV_HEADS = RWKV_WIDTH // RWKV_HEAD
DECAY_LORA = 64
ICLR_LORA = 64
GATE_LORA = 160
LORA_COLS = DECAY_LORA + ICLR_LORA + GATE_LORA
LORA_PAD = 384
GN_EPS = 64e-5
SHIFT_WIDTH = 3 * RWKV_WIDTH + LORA_COLS
SHIFT_PAD = 3 * RWKV_WIDTH + 512
N_EXPERTS = 16
CAPACITY_FACTOR = 2
EXPERT_FF = 2048
CHUNK = 64


def _params(*sem):
    return pltpu.CompilerParams(dimension_semantics=sem, vmem_limit_bytes=VMEM_LIMIT_BYTES)


def _dot(a, b):
    return jnp.dot(a, b, preferred_element_type=F32)


def _dot_nt(a, b):
    return lax.dot_general(a, b, (((1,), (1,)), ((), ())), preferred_element_type=F32)


def _split2(x):
    hi = x.astype(BF16)
    lo = (x - hi.astype(F32)).astype(BF16)
    return hi, lo


def _dot_lhs2(x, m_bf16):
    hi, lo = _split2(x)
    return _dot(hi, m_bf16) + _dot(lo, m_bf16)


def _norm_mm_kernel(x_ref, nw_ref, w_ref, o_ref, h_scr):
    @pl.when(pl.program_id(1) == 0)
    def _():
        x = x_ref[...]
        ms = jnp.mean(x * x, axis=-1, keepdims=True)
        h_scr[...] = (x * lax.rsqrt(ms + NORM_EPS) * nw_ref[...]).astype(BF16)

    o_ref[...] = _dot(h_scr[...], w_ref[...]).astype(o_ref.dtype)


def norm_matmul(x, nw, w_bf16, out_dtype, tm=1024, tn=512):
    m, k = x.shape
    n = w_bf16.shape[1]
    return pl.pallas_call(
        _norm_mm_kernel,
        grid=(m // tm, n // tn),
        in_specs=[
            pl.BlockSpec((tm, k), lambda i, j: (i, 0)),
            pl.BlockSpec((1, k), lambda i, j: (0, 0)),
            pl.BlockSpec((k, tn), lambda i, j: (0, j)),
        ],
        out_specs=pl.BlockSpec((tm, tn), lambda i, j: (i, j)),
        out_shape=jax.ShapeDtypeStruct((m, n), out_dtype),
        scratch_shapes=[pltpu.VMEM((tm, k), BF16)],
        compiler_params=_params("parallel", "arbitrary"),
        name="norm_matmul",
    )(x, nw.reshape(1, k), w_bf16)


def _norm_mm_rope_kernel(x_ref, nw_ref, w_ref, pos_ref, freq_ref, slo_ref, shi_ref, o_ref,
                         h_scr, c_scr, lo_scr, hi_scr, *, rope_tiles, q_tiles):
    j = pl.program_id(1)

    @pl.when(j == 0)
    def _():
        x = x_ref[...]
        ms = jnp.mean(x * x, axis=-1, keepdims=True)
        h_scr[...] = (x * lax.rsqrt(ms + NORM_EPS) * nw_ref[...]).astype(BF16)
        ang = pos_ref[...].astype(F32) * freq_ref[...]
        s = jnp.sin(ang)
        c_scr[...] = jnp.cos(ang)
        lo_scr[...] = s * slo_ref[...]
        hi_scr[...] = s * shi_ref[...]

    acc = _dot(h_scr[...], w_ref[...])

    @pl.when(j < rope_tiles)
    def _():
        scale = jnp.where(j < q_tiles, ATT_QK_DIM ** -0.5 * math.log2(math.e), 1.0)
        half = ROPE_DIM // 2
        for g in range(acc.shape[1] // LANES):
            x = acc[:, g * LANES:(g + 1) * LANES]
            x_dn = pltpu.roll(x, half, axis=1)
            x_up = pltpu.roll(x, LANES - half, axis=1)
            y = x * c_scr[...] + x_dn * hi_scr[...] + x_up * lo_scr[...]
            o_ref[:, g * LANES:(g + 1) * LANES] = (y * scale).astype(o_ref.dtype)

    @pl.when(j >= rope_tiles)
    def _():
        o_ref[...] = acc.astype(o_ref.dtype)


def norm_matmul_rope(x, nw, w_bf16, pos, tm=1024, tn=512):
    m, k = x.shape
    n = w_bf16.shape[1]
    lane = jnp.arange(LANES) % ATT_QK_DIM
    half = ROPE_DIM // 2
    inv_freq = ROPE_THETA ** (-jnp.arange(0, ROPE_DIM, 2, dtype=F32) / ROPE_DIM)
    freq = jnp.where(lane < ROPE_DIM, inv_freq[lane % half], 0.0).astype(F32).reshape(1, LANES)
    sgn_lo = jnp.where(lane < half, -1.0, 0.0).astype(F32).reshape(1, LANES)
    sgn_hi = jnp.where((lane >= half) & (lane < ROPE_DIM), 1.0, 0.0).astype(F32).reshape(1, LANES)
    vec = pl.BlockSpec((1, LANES), lambda i, j: (0, 0))
    return pl.pallas_call(
        functools.partial(_norm_mm_rope_kernel, rope_tiles=2 * ATT_QK_COLS // tn, q_tiles=ATT_QK_COLS // tn),
        grid=(m // tm, n // tn),
        in_specs=[
            pl.BlockSpec((tm, k), lambda i, j: (i, 0)),
            pl.BlockSpec((1, k), lambda i, j: (0, 0)),
            pl.BlockSpec((k, tn), lambda i, j: (0, j)),
            pl.BlockSpec((tm, 1), lambda i, j: (i, 0)),
            vec, vec, vec,
        ],
        out_specs=pl.BlockSpec((tm, tn), lambda i, j: (i, j)),
        out_shape=jax.ShapeDtypeStruct((m, n), BF16),
        scratch_shapes=[pltpu.VMEM((tm, k), BF16), pltpu.VMEM((tm, LANES), F32),
                        pltpu.VMEM((tm, LANES), F32), pltpu.VMEM((tm, LANES), F32)],
        compiler_params=_params("parallel", "arbitrary"),
        name="norm_matmul_rope",
    )(x, nw.reshape(1, k), w_bf16, pos, freq, sgn_lo, sgn_hi)


def _attn_kernel(q_ref, k_ref, v_ref, lq1_ref, lk1_ref, lq2_ref, lk2_ref, sw_ref, o_ref, v_scr, *,
                 lambda_init):
    dv = ATT_V_DIM

    @pl.when(pl.program_id(2) == 0)
    def _():
        lane = lax.broadcasted_iota(jnp.int32, (v_scr.shape[0], dv), 1)
        v_scr[:, :dv] = v_ref[...]
        v_scr[:, dv:] = (lane == 0).astype(BF16)

    q = q_ref[...]
    kr = k_ref[...]
    s1 = _dot_nt(q[:, :ATT_QK_DIM], kr[:, :ATT_QK_DIM])
    s2 = _dot_nt(q[:, ATT_QK_DIM:], kr[:, ATT_QK_DIM:])
    e1 = jnp.exp2(s1 - jnp.max(s1, axis=-1, keepdims=True)).astype(BF16)
    e2 = jnp.exp2(s2 - jnp.max(s2, axis=-1, keepdims=True)).astype(BF16)
    lam = (jnp.exp(jnp.sum(lq1_ref[...] * lk1_ref[...], axis=-1, keepdims=True))
           - jnp.exp(jnp.sum(lq2_ref[...] * lk2_ref[...], axis=-1, keepdims=True)) + lambda_init)
    va = v_scr[...]
    o1 = _dot(e1, va)
    o2 = _dot(e2, va)
    o = o1[:, :dv] / o1[:, dv:dv + 1] - o2[:, :dv] * (lam / o2[:, dv:dv + 1])
    ms = jnp.mean(o * o, axis=-1, keepdims=True)
    o = o * lax.rsqrt(ms + SUBLN_EPS) * sw_ref[...] * (1.0 - lambda_init)
    o_ref[...] = o.astype(o_ref.dtype)


def attention(qkv, lq1, lk1, lq2, lk2, subln_w, bsz, seq, lambda_init, tq=256):
    nq = seq // tq
    vec = lambda n: pl.BlockSpec((1, n), lambda b, h, i: (0, 0))
    nh = ATT_HEADS
    return pl.pallas_call(
        functools.partial(_attn_kernel, lambda_init=lambda_init),
        grid=(bsz, nh, nq),
        in_specs=[
            pl.BlockSpec((tq, LANES), lambda b, h, i: (b * nq + i, h)),
            pl.BlockSpec((seq, LANES), lambda b, h, i: (b, nh + h)),
            pl.BlockSpec((seq, LANES), lambda b, h, i: (b, 2 * nh + h)),
            vec(ATT_QK_DIM), vec(ATT_QK_DIM), vec(ATT_QK_DIM), vec(ATT_QK_DIM),
            vec(ATT_V_DIM),
        ],
        out_specs=pl.BlockSpec((tq, LANES), lambda b, h, i: (b * nq + i, h)),
        out_shape=jax.ShapeDtypeStruct((bsz * seq, ATT_WIDTH), BF16),
        scratch_shapes=[pltpu.VMEM((seq, 2 * ATT_V_DIM), BF16)],
        compiler_params=_params("parallel", "parallel", "arbitrary"),
        name="diff_attention",
    )(qkv, qkv, qkv,
      lq1.reshape(1, -1), lk1.reshape(1, -1), lq2.reshape(1, -1), lk2.reshape(1, -1),
      subln_w.reshape(1, -1))


def _shift_kernel(z_ref, mp_ref, mn_ref, o_ref):
    z = z_ref[...]
    n = z.shape[0]
    row = lax.broadcasted_iota(jnp.int32, z.shape, 0)
    zp = jnp.where(row == 0, 0.0, pltpu.roll(z, 1, axis=0))
    zn = jnp.where(row == n - 1, 0.0, pltpu.roll(z, n - 1, axis=0))
    o_ref[...] = z + mp_ref[...] * (zp - z) + mn_ref[...] * (zn - z)


def token_shift(pz, mu_prev, mu_next, bsz, seq):
    width = pz.shape[1]
    nt = width // LANES
    return pl.pallas_call(
        _shift_kernel,
        grid=(bsz, nt),
        in_specs=[
            pl.BlockSpec((seq, LANES), lambda b, j: (b, j)),
            pl.BlockSpec((1, LANES), lambda b, j: (0, j)),
            pl.BlockSpec((1, LANES), lambda b, j: (0, j)),
        ],
        out_specs=pl.BlockSpec((seq, LANES), lambda b, j: (b, j)),
        out_shape=jax.ShapeDtypeStruct(pz.shape, F32),
        compiler_params=_params("parallel", "parallel"),
        name="token_shift",
    )(pz, mu_prev, mu_next)


def _seg_sum(x, ones_blk):
    parts = []
    for j in range(x.shape[1] // LANES):
        parts.append(_dot_lhs2(x[:, j * LANES:(j + 1) * LANES], ones_blk))
    return jnp.concatenate(parts, axis=1)


def _prep_kernel(r_ref, k_ref, v_ref, lo_ref, w0_ref, du_ref, a0_ref, iu_ref, gu_ref, kk_ref,
                 ka_ref, rk_ref, ones_ref, tril_ref, triu_ref,
                 ab_ref, rb_ref, bt_ref, kt_ref, pl_ref, vb_ref, bonus_ref, g_ref, *, tt):
    r = r_ref[...]
    k = k_ref[...]
    v = v_ref[...]
    lo = lo_ref[...]
    ones_blk = ones_ref[...]
    vb_ref[...] = v.astype(BF16)
    g_ref[...] = _dot(jax.nn.sigmoid(lo).astype(BF16), gu_ref[...])
    kk = k * kk_ref[...]
    nrm = jnp.sqrt(_seg_sum(kk * kk, ones_blk))
    kk = kk / jnp.maximum(nrm, 1e-12)
    th = jnp.tanh(lo).astype(BF16)
    lob = lo.astype(BF16)
    nchunk = tt // CHUNK
    ksum = jnp.zeros_like(k)
    for d in range(2):
        wl = w0_ref[d] + _dot(th, du_ref[d])
        w_log = -jax.nn.softplus(-wl) - 0.5
        lw = -jnp.exp(w_log)
        a = jax.nn.sigmoid(a0_ref[d] + _dot(lob, iu_ref[d]))
        kd = k * (1.0 + (a - 1.0) * ka_ref[...])
        ksum = ksum + kd
        tri = tril_ref[...] if d == 0 else triu_ref[...]
        cum = _dot_lhs2_rhs(tri, lw)
        e_pos = jnp.exp(cum)
        e_neg = jnp.exp(-cum)
        ab_ref[d] = (-kk * jnp.exp(cum - lw)).astype(BF16)
        rb_ref[d] = (r * e_pos).astype(BF16)
        bt_ref[d] = (kk * a * e_neg).astype(BF16)
        kt_ref[d] = (kd * e_neg).astype(BF16)
        for c in range(nchunk):
            last = c * CHUNK + (CHUNK - 1 if d == 0 else 0)
            pl_ref[d, c] = e_pos[last:last + 1, :]
    bonus_ref[...] = _seg_sum(r * ksum * rk_ref[...], ones_blk) * v


def _dot_lhs2_rhs(tri_bf16, x):
    hi, lo = _split2(x)
    return _dot(tri_bf16, hi) + _dot(tri_bf16, lo)


def rwkv_prep(zs, w0, decay_up, a0, iclr_up, gate_up, k_k, k_a, r_k, bsz, seq, tt=256):
    m = bsz * seq
    c = RWKV_WIDTH
    nt = seq // tt
    ncb = tt // CHUNK
    nc = seq // CHUNK

    def pad_rows(w, start):
        out = jnp.zeros(w.shape[:-2] + (LORA_PAD, c), F32)
        return lax.dynamic_update_slice_in_dim(out, w.astype(F32), start, axis=w.ndim - 2).astype(BF16)

    du = pad_rows(decay_up, 0)
    iu = pad_rows(iclr_up, DECAY_LORA)
    gu = pad_rows(gate_up, DECAY_LORA + ICLR_LORA)
    lane = jnp.arange(LANES)
    ones_blk = (lane[:, None] // RWKV_HEAD == lane[None, :] // RWKV_HEAD).astype(BF16)
    t = jnp.arange(tt)
    same = t[:, None] // CHUNK == t[None, :] // CHUNK
    tril = (same & (t[:, None] >= t[None, :])).astype(BF16)
    triu = (same & (t[:, None] <= t[None, :])).astype(BF16)
    r_k_flat = r_k.reshape(1, c)
    row = lambda: pl.BlockSpec((1, c), lambda b, i: (0, 0))
    full3 = lambda s: pl.BlockSpec(s, lambda b, i: (0, 0, 0))
    tok = lambda j, w: pl.BlockSpec((tt, w), lambda b, i: (b * nt + i, j))
    dir_tok = pl.BlockSpec((2, tt, c), lambda b, i: (0, b * nt + i, 0))
    outs = pl.pallas_call(
        functools.partial(_prep_kernel, tt=tt),
        grid=(bsz, nt),
        in_specs=[
            tok(0, c), tok(1, c), tok(2, c),
            pl.BlockSpec((tt, LORA_PAD), lambda b, i: (b * nt + i, 3 * c // LORA_PAD)),
            full3((2, 1, c)), full3((2, LORA_PAD, c)), full3((2, 1, c)), full3((2, LORA_PAD, c)),
            pl.BlockSpec((LORA_PAD, c), lambda b, i: (0, 0)),
            row(), row(), row(),
            pl.BlockSpec((LANES, LANES), lambda b, i: (0, 0)),
            pl.BlockSpec((tt, tt), lambda b, i: (0, 0)),
            pl.BlockSpec((tt, tt), lambda b, i: (0, 0)),
        ],
        out_specs=[
            dir_tok, dir_tok, dir_tok, dir_tok,
            pl.BlockSpec((2, ncb, 1, c), lambda b, i: (0, b * nt + i, 0, 0)),
            pl.BlockSpec((tt, c), lambda b, i: (b * nt + i, 0)),
            pl.BlockSpec((tt, c), lambda b, i: (b * nt + i, 0)),
            pl.BlockSpec((tt, c), lambda b, i: (b * nt + i, 0)),
        ],
        out_shape=[
            jax.ShapeDtypeStruct((2, m, c), BF16),
            jax.ShapeDtypeStruct((2, m, c), BF16),
            jax.ShapeDtypeStruct((2, m, c), BF16),
            jax.ShapeDtypeStruct((2, m, c), BF16),
            jax.ShapeDtypeStruct((2, bsz * nc, 1, c), F32),
            jax.ShapeDtypeStruct((m, c), BF16),
            jax.ShapeDtypeStruct((m, c), F32),
            jax.ShapeDtypeStruct((m, c), F32),
        ],
        compiler_params=_params("parallel", "parallel"),
        name="rwkv_prep",
    )(zs, zs, zs, zs, w0.reshape(2, 1, c), du, a0.reshape(2, 1, c), iu, gu,
      k_k.reshape(1, c), k_a.reshape(1, c), r_k_flat, ones_blk, tril, triu)
    return outs


def _scan_kernel(abf_ref, abb_ref, rbf_ref, rbb_ref, btf_ref, btb_ref, ktf_ref, ktb_ref,
                 vf_ref, vb_ref, plf_ref, plb_ref, yf_ref, yb_ref, m_scr, *, heads):
    @pl.when(pl.program_id(1) == 0)
    def _():
        m_scr[...] = jnp.zeros_like(m_scr)

    ab_ref = (abf_ref, abb_ref)
    rb_ref = (rbf_ref, rbb_ref)
    bt_ref = (btf_ref, btb_ref)
    kt_ref = (ktf_ref, ktb_ref)
    v_ref = (vf_ref, vb_ref)
    pl_ref = (plf_ref, plb_ref)
    y_ref = (yf_ref, yb_ref)
    n = CHUNK
    row = lax.broadcasted_iota(jnp.int32, (2 * n, 2 * n), 0)
    col = lax.broadcasted_iota(jnp.int32, (2 * n, 2 * n), 1)
    top = row < n
    tr = row % n
    tc = col % n
    eye = (lax.broadcasted_iota(jnp.int32, (n, n), 0)
           == lax.broadcasted_iota(jnp.int32, (n, n), 1)).astype(F32)
    keep = ((tr > tc) | (~top & (tr == tc)), (tr < tc) | (~top & (tr == tc)))
    probs = [(d, hh) for d in range(2) for hh in range(heads)]
    sl = lambda hh: slice(hh * n, (hh + 1) * n)
    ab = [ab_ref[d][0, :, sl(hh)] for d, hh in probs]
    rb = [rb_ref[d][0, :, sl(hh)] for d, hh in probs]
    bt = [bt_ref[d][0, :, sl(hh)] for d, hh in probs]
    kt = [kt_ref[d][0, :, sl(hh)] for d, hh in probs]
    vv = [v_ref[d][:, sl(hh)] for d, hh in probs]
    p_last = [pl_ref[d][0, 0, :, sl(hh)] for d, hh in probs]
    idx = range(len(probs))
    g1 = [_dot_nt(jnp.concatenate([ab[i], rb[i]], axis=0), jnp.concatenate([bt[i], kt[i]], axis=0))
          for i in idx]
    g1 = [jnp.where(keep[probs[i][0]], g1[i], 0.0) for i in idx]
    a_ab = [g[:n, :n] for g in g1]
    low = [g[n:, :].astype(BF16) for g in g1]
    tm = [eye + a for a in a_ab]
    pw = [_dot(a.astype(BF16), a.astype(BF16)) for a in a_ab]
    span = 2
    while span < n:
        last = span * 2 >= n
        nxt = []
        for i in idx:
            pwb = pw[i].astype(BF16)
            if last:
                nxt.append((tm[i] + _dot(tm[i].astype(BF16), pwb), None))
            else:
                both = _dot(jnp.concatenate([tm[i], pw[i]], axis=0).astype(BF16), pwb)
                nxt.append((tm[i] + both[:n], both[n:]))
        tm = [t for t, _ in nxt]
        pw = [p for _, p in nxt]
        span *= 2
    wc = [_dot(tm[i].astype(BF16), jnp.concatenate([ab[i], g1[i][:n, n:].astype(BF16)], axis=1)) for i in idx]
    w = [x[:, :n].astype(BF16) for x in wc]
    cuv = [_dot(wc[i][:, n:].astype(BF16), vv[i]).astype(BF16) for i in idx]
    rhs = [jnp.concatenate([cuv[i], vv[i]], axis=0) for i in idx]
    q_m = [rb[i].astype(F32) + _dot(g1[i][n:, :n].astype(BF16), w[i]) for i in idx]
    y_0 = [_dot(low[i], rhs[i]) for i in idx]
    bh_t = [(bt[i].astype(F32) * p_last[i]).T for i in idx]
    kh_t = [(kt[i].astype(F32) * p_last[i]).T for i in idx]
    g_m = [eye * p_last[i] + _dot(bh_t[i].astype(BF16), w[i]) for i in idx]
    h_m = [_dot(jnp.concatenate([bh_t[i], kh_t[i]], axis=1).astype(BF16), rhs[i]) for i in idx]
    upd = [_dot(jnp.concatenate([q_m[i], g_m[i]], axis=0).astype(BF16), m_scr[d, hh].astype(BF16))
           for i, (d, hh) in enumerate(probs)]
    for i, (d, hh) in enumerate(probs):
        y_ref[d][:, sl(hh)] = upd[i][:n] + y_0[i]
        m_scr[d, hh] = upd[i][n:] + h_m[i]


def rwkv_scan(ab, rb, bt, kt, vb, p_last, bsz, seq):
    nc = seq // CHUNK
    nh = RWKV_HEADS
    n = CHUNK
    c_w = RWKV_WIDTH
    fwd = lambda b, c: b * nc + c
    bwd = lambda b, c: b * nc + nc - 1 - c
    dir_f = pl.BlockSpec((1, n, c_w), lambda b, c: (0, fwd(b, c), 0))
    dir_b = pl.BlockSpec((1, n, c_w), lambda b, c: (1, bwd(b, c), 0))
    tok_f = pl.BlockSpec((n, c_w), lambda b, c: (fwd(b, c), 0))
    tok_b = pl.BlockSpec((n, c_w), lambda b, c: (bwd(b, c), 0))
    pl_f = pl.BlockSpec((1, 1, 1, c_w), lambda b, c: (0, fwd(b, c), 0, 0))
    pl_b = pl.BlockSpec((1, 1, 1, c_w), lambda b, c: (1, bwd(b, c), 0, 0))
    y_shape = jax.ShapeDtypeStruct((bsz * seq, c_w), F32)
    return pl.pallas_call(
        functools.partial(_scan_kernel, heads=nh),
        grid=(bsz, nc),
        in_specs=[dir_f, dir_b, dir_f, dir_b, dir_f, dir_b, dir_f, dir_b, tok_f, tok_b, pl_f, pl_b],
        out_specs=[tok_f, tok_b],
        out_shape=[y_shape, y_shape],
        scratch_shapes=[pltpu.VMEM((2, nh, n, n), F32)],
        compiler_params=_params("parallel", "arbitrary"),
        name="rwkv_scan",
    )(ab, ab, rb, rb, bt, bt, kt, kt, vb, vb, p_last, p_last)


def _post_kernel(yf_ref, yb_ref, bonus_ref, g_ref, lw_ref, lb_ref, o_ref):
    y = yf_ref[...] + yb_ref[...]
    n = RWKV_HEAD
    parts = []
    for h in range(RWKV_HEADS):
        yh = y[:, h * n:(h + 1) * n]
        mu = jnp.mean(yh, axis=-1, keepdims=True)
        yc = yh - mu
        var = jnp.mean(yc * yc, axis=-1, keepdims=True)
        parts.append(yc * lax.rsqrt(var + GN_EPS))
    yn = jnp.concatenate(parts, axis=1)
    o_ref[...] = ((yn * lw_ref[...] + lb_ref[...] + bonus_ref[...]) * g_ref[...]).astype(o_ref.dtype)


def rwkv_post(y_fwd, y_bwd, bonus, g, ln_w, ln_b, tt=256):
    m = y_fwd.shape[0]
    c = RWKV_WIDTH
    tok = pl.BlockSpec((tt, c), lambda i: (i, 0))
    row = pl.BlockSpec((1, c), lambda i: (0, 0))
    return pl.pallas_call(
        _post_kernel,
        grid=(m // tt,),
        in_specs=[tok, tok, tok, tok, row, row],
        out_specs=tok,
        out_shape=jax.ShapeDtypeStruct((m, c), BF16),
        compiler_params=_params("parallel"),
        name="rwkv_post",
    )(y_fwd, y_bwd, bonus, g, ln_w.reshape(1, c), ln_b.reshape(1, c))


def _out_proj_kernel(x_ref, a_ref, r_ref, wa_ref, wr_ref, o_ref):
    o_ref[...] = x_ref[...] + _dot(a_ref[...], wa_ref[...]) + _dot(r_ref[...], wr_ref[...])


def out_proj(x, att, rw, w_att, w_rw, tm=1024, tn=512):
    m, n = x.shape
    return pl.pallas_call(
        _out_proj_kernel,
        grid=(m // tm, n // tn),
        in_specs=[
            pl.BlockSpec((tm, tn), lambda i, j: (i, j)),
            pl.BlockSpec((tm, ATT_WIDTH), lambda i, j: (i, 0)),
            pl.BlockSpec((tm, RWKV_WIDTH), lambda i, j: (i, 0)),
            pl.BlockSpec((ATT_WIDTH, tn), lambda i, j: (0, j)),
            pl.BlockSpec((RWKV_WIDTH, tn), lambda i, j: (0, j)),
        ],
        out_specs=pl.BlockSpec((tm, tn), lambda i, j: (i, j)),
        out_shape=jax.ShapeDtypeStruct((m, n), F32),
        compiler_params=_params("parallel", "parallel"),
        name="out_proj",
    )(x, att, rw, w_att, w_rw)


def _router_kernel(x_ref, nw_ref, wh_ref, wl_ref, hb_ref, aff_ref, afft_ref):
    x = x_ref[...]
    ms = jnp.mean(x * x, axis=-1, keepdims=True)
    h = x * lax.rsqrt(ms + NORM_EPS) * nw_ref[...]
    hb_ref[...] = h.astype(BF16)
    hi, lo = _split2(h)
    logits = _dot(hi, wh_ref[...]) + _dot(lo, wh_ref[...]) + _dot(hi, wl_ref[...])
    lane = lax.broadcasted_iota(jnp.int32, logits.shape, 1)
    valid = lane < N_EXPERTS
    logits = jnp.where(valid, logits, -1e30)
    e = jnp.where(valid, jnp.exp(logits - jnp.max(logits, axis=-1, keepdims=True)), 0.0)
    aff = e / jnp.sum(e, axis=-1, keepdims=True)
    aff_ref[...] = aff
    afft_ref[...] = aff.T[:N_EXPERTS, :]


def router(x1, norm_w, w_router, tm=512):
    m, k = x1.shape
    wpad = jnp.zeros((k, LANES), F32).at[:, :N_EXPERTS].set(w_router)
    wh, wl = _split2(wpad)
    return pl.pallas_call(
        _router_kernel,
        grid=(m // tm,),
        in_specs=[
            pl.BlockSpec((tm, k), lambda i: (i, 0)),
            pl.BlockSpec((1, k), lambda i: (0, 0)),
            pl.BlockSpec((k, LANES), lambda i: (0, 0)),
            pl.BlockSpec((k, LANES), lambda i: (0, 0)),
        ],
        out_specs=[
            pl.BlockSpec((tm, k), lambda i: (i, 0)),
            pl.BlockSpec((tm, LANES), lambda i: (i, 0)),
            pl.BlockSpec((N_EXPERTS, tm), lambda i: (0, i)),
        ],
        out_shape=[
            jax.ShapeDtypeStruct((m, k), BF16),
            jax.ShapeDtypeStruct((m, LANES), F32),
            jax.ShapeDtypeStruct((N_EXPERTS, m), F32),
        ],
        compiler_params=_params("parallel"),
        name="router",
    )(x1, norm_w.reshape(1, k), wh, wl)


def _topk_kernel(aff_ref, afft_ref, sp_ref, spt_ref, sel_scr, *, cap, tr):
    seq = aff_ref.shape[0]
    lane = lax.broadcasted_iota(jnp.int32, (tr, LANES), 1)
    for i in range(seq // tr):
        a_tile = aff_ref[i * tr:(i + 1) * tr, :]
        t_row = lax.broadcasted_iota(jnp.int32, (tr, seq), 0) + i * tr
        t_col = lax.broadcasted_iota(jnp.int32, (tr, seq), 1)
        earlier = t_col < t_row
        sel_tile = jnp.zeros((tr, LANES), F32)
        for e in range(N_EXPERTS):
            mine = a_tile[:, e:e + 1]
            other = afft_ref[e:e + 1, :]
            ahead = (other > mine) | ((other == mine) & earlier)
            cnt = jnp.sum(ahead.astype(F32), axis=-1, keepdims=True)
            sel_tile = jnp.where(lane == e, (cnt < cap).astype(F32), sel_tile)
        sel_scr[i * tr:(i + 1) * tr, :] = sel_tile
    sel_all = sel_scr[...].astype(BF16)
    for i in range(seq // tr):
        t_row = lax.broadcasted_iota(jnp.int32, (tr, seq), 0) + i * tr
        t_col = lax.broadcasted_iota(jnp.int32, (tr, seq), 1)
        before = (t_col < t_row).astype(BF16)
        pos = _dot(before, sel_all)
        sp = jnp.where(sel_scr[i * tr:(i + 1) * tr, :] > 0.5, pos, -1.0)
        sp_ref[i * tr:(i + 1) * tr, :] = sp
        spt_ref[:, i * tr:(i + 1) * tr] = sp.T[:N_EXPERTS, :]


def topk_select(aff, afft, bsz, seq, cap, tr=256):
    return pl.pallas_call(
        functools.partial(_topk_kernel, cap=cap, tr=tr),
        grid=(bsz,),
        in_specs=[
            pl.BlockSpec((seq, LANES), lambda b: (b, 0)),
            pl.BlockSpec((N_EXPERTS, seq), lambda b: (0, b)),
        ],
        out_specs=[
            pl.BlockSpec((seq, LANES), lambda b: (b, 0)),
            pl.BlockSpec((N_EXPERTS, seq), lambda b: (0, b)),
        ],
        out_shape=[
            jax.ShapeDtypeStruct((bsz * seq, LANES), F32),
            jax.ShapeDtypeStruct((N_EXPERTS, bsz * seq), F32),
        ],
        scratch_shapes=[pltpu.VMEM((seq, LANES), F32)],
        compiler_params=_params("parallel"),
        name="topk_select",
    )(aff, afft)


def _gather_kernel(spt_ref, h_ref, xe_ref, *, cap):
    e = pl.program_id(1)
    seq = h_ref.shape[0]
    sp = spt_ref[pl.ds(e, 1), :]
    slot = lax.broadcasted_iota(jnp.int32, (cap, seq), 0).astype(F32)
    onehot = (sp == slot).astype(BF16)
    xe_ref[0] = _dot(onehot, h_ref[...]).astype(BF16)


def moe_gather(spt, hb, bsz, seq, cap):
    d = hb.shape[1]
    return pl.pallas_call(
        functools.partial(_gather_kernel, cap=cap),
        grid=(bsz, N_EXPERTS),
        in_specs=[
            pl.BlockSpec((N_EXPERTS, seq), lambda b, e: (0, b)),
            pl.BlockSpec((seq, d), lambda b, e: (b, 0)),
        ],
        out_specs=pl.BlockSpec((1, cap, d), lambda b, e: (e, b, 0)),
        out_shape=jax.ShapeDtypeStruct((N_EXPERTS, bsz * cap, d), BF16),
        compiler_params=_params("parallel", "arbitrary"),
        name="moe_gather",
    )(spt, hb)


def _expert_kernel(xe_ref, wg_ref, wu_ref, wd_ref, ye_ref, acc_scr):
    f = pl.program_id(1)

    @pl.when(f == 0)
    def _():
        acc_scr[...] = jnp.zeros_like(acc_scr)

    xe = xe_ref[0]
    gate = _dot(xe, wg_ref[0].astype(BF16))
    up = _dot(xe, wu_ref[0].astype(BF16))
    hid = (gate * jax.nn.sigmoid(gate) * up).astype(BF16)
    acc_scr[...] += _dot(hid, wd_ref[0].astype(BF16))

    @pl.when(f == pl.num_programs(1) - 1)
    def _():
        ye_ref[0] = acc_scr[...].astype(ye_ref.dtype)


def moe_experts(xe, e_gate, e_up, e_down, tf=256):
    ne, rows, d = xe.shape
    ff = e_gate.shape[2]
    return pl.pallas_call(
        _expert_kernel,
        grid=(ne, ff // tf),
        in_specs=[
            pl.BlockSpec((1, rows, d), lambda e, f: (e, 0, 0)),
            pl.BlockSpec((1, d, tf), lambda e, f: (e, 0, f)),
            pl.BlockSpec((1, d, tf), lambda e, f: (e, 0, f)),
            pl.BlockSpec((1, tf, d), lambda e, f: (e, f, 0)),
        ],
        out_specs=pl.BlockSpec((1, rows, d), lambda e, f: (e, 0, 0)),
        out_shape=jax.ShapeDtypeStruct((ne, rows, d), BF16),
        scratch_shapes=[pltpu.VMEM((rows, d), F32)],
        compiler_params=_params("parallel", "arbitrary"),
        name="moe_experts",
    )(xe, e_gate, e_up, e_down)


def _combine_kernel(x_ref, sp_ref, aff_ref, ye_ref, nw_ref, o_ref, acc_scr, *, cap):
    e = pl.program_id(2)

    @pl.when(e == 0)
    def _():
        acc_scr[...] = x_ref[...]

    tt = x_ref.shape[0]
    lane = lax.broadcasted_iota(jnp.int32, (tt, LANES), 1)
    pick = lane == e
    sp = jnp.sum(jnp.where(pick, sp_ref[...], 0.0), axis=-1, keepdims=True)
    gate = jnp.sum(jnp.where(pick, aff_ref[...], 0.0), axis=-1, keepdims=True)
    slot = lax.broadcasted_iota(jnp.int32, (tt, cap), 1).astype(F32)
    onehot = (sp == slot).astype(BF16)
    acc_scr[...] += gate * _dot(onehot, ye_ref[0])

    @pl.when(e == pl.num_programs(2) - 1)
    def _():
        y = acc_scr[...]
        ms = jnp.mean(y * y, axis=-1, keepdims=True)
        o_ref[...] = y * lax.rsqrt(ms + NORM_EPS) * nw_ref[...]


def moe_combine(x1, sp, aff, ye, norm_w, bsz, seq, cap, tt=512):
    m, d = x1.shape
    nt = seq // tt
    return pl.pallas_call(
        functools.partial(_combine_kernel, cap=cap),
        grid=(bsz, nt, N_EXPERTS),
        in_specs=[
            pl.BlockSpec((tt, d), lambda b, i, e: (b * nt + i, 0)),
            pl.BlockSpec((tt, LANES), lambda b, i, e: (b * nt + i, 0)),
            pl.BlockSpec((tt, LANES), lambda b, i, e: (b * nt + i, 0)),
            pl.BlockSpec((1, cap, d), lambda b, i, e: (e, b, 0)),
            pl.BlockSpec((1, d), lambda b, i, e: (0, 0)),
        ],
        out_specs=pl.BlockSpec((tt, d), lambda b, i, e: (b * nt + i, 0)),
        out_shape=jax.ShapeDtypeStruct((m, d), F32),
        scratch_shapes=[pltpu.VMEM((tt, d), F32)],
        compiler_params=_params("parallel", "parallel", "arbitrary"),
        name="moe_combine",
    )(x1, sp, aff, ye, norm_w.reshape(1, d))


def _pad_cols(w, width):
    return jnp.pad(w, [(0, 0)] * (w.ndim - 1) + [(0, width - w.shape[-1])])


def rwkv_mixer(zs, w0, decay_up, a0, iclr_up, gate_up, k_k, k_a, r_k, ln_x_w, ln_x_b, bsz, seq):
    ab, rb, bt, kt, p_last, vb, bonus, g = rwkv_prep(
        zs, w0, decay_up, a0, iclr_up, gate_up, k_k, k_a, r_k, bsz, seq)
    y_fwd, y_bwd = rwkv_scan(ab, rb, bt, kt, vb, p_last, bsz, seq)
    return rwkv_post(y_fwd, y_bwd, bonus, g, ln_x_w, ln_x_b)


def moe_block(x1, norm2_w, w_router, e_gate, e_up, e_down, norm_f_w, bsz, seq):
    cap = CAPACITY_FACTOR * seq // N_EXPERTS
    hb, aff, afft = router(x1, norm2_w, w_router)
    sp, spt = topk_select(aff, afft, bsz, seq, cap)
    xe = moe_gather(spt, hb, bsz, seq, cap)
    ye = moe_experts(xe, e_gate, e_up, e_down)
    return moe_combine(x1, sp, aff, ye, norm_f_w, bsz, seq, cap)


def kernel(x, positions, norm1_w, w_in, mu_prev, mu_next, lambda_q1, lambda_k1, lambda_q2, lambda_k2, subln_w, w0, decay_up, a0, iclr_up, gate_up, k_k, k_a, r_k, ln_x_w, ln_x_b, w_out, norm2_w, w_router, e_gate, e_up, e_down, norm_f_w):
    bsz, seq, d = x.shape
    m = bsz * seq
    xf = x.reshape(m, d)
    pos = positions.reshape(m, 1)
    lambda_init = 0.8 - 0.6 * math.exp(-0.3 * 0)

    w_att = w_in[0][:, :ATT_COLS].astype(BF16)
    w_z = _pad_cols(w_in[0][:, ATT_COLS:], SHIFT_PAD).astype(BF16)
    qkv = norm_matmul_rope(xf, norm1_w[0], w_att, pos)
    p_z = norm_matmul(xf, norm1_w[0], w_z, F32)

    att = attention(qkv, lambda_q1[0], lambda_k1[0], lambda_q2[0], lambda_k2[0],
                    subln_w[0], bsz, seq, lambda_init)

    zs = token_shift(p_z, _pad_cols(mu_prev, SHIFT_PAD), _pad_cols(mu_next, SHIFT_PAD), bsz, seq)
    rw = rwkv_mixer(zs, w0[0], decay_up[0], a0[0], iclr_up[0], gate_up[0], k_k[0], k_a[0], r_k[0],
                    ln_x_w[0], ln_x_b[0], bsz, seq)

    wo = w_out[0].astype(BF16)
    x1 = out_proj(xf, att, rw, wo[:ATT_WIDTH], wo[ATT_WIDTH:])

    out = moe_block(x1, norm2_w[0], w_router[0], e_gate[0], e_up[0], e_down[0], norm_f_w, bsz, seq)
    return out.reshape(bsz, seq, d)
```

```python
import functools
import math

import jax
import jax.numpy as jnp
from jax import lax
from jax.experimental import pallas as pl
from jax.experimental.pallas import tpu as pltpu

F32 = jnp.float32
BF16 = jnp.bfloat16

LANES = 128
VMEM_LIMIT_BYTES = 56 * 1024 * 1024

D_MODEL = 2048
NORM_EPS = 1e-6
ATT_HEADS = 8
ATT_QK_DIM = 64
ATT_V_DIM = 128
ATT_WIDTH = ATT_HEADS * ATT_V_DIM
ATT_QK_COLS = ATT_HEADS * 2 * ATT_QK_DIM
ATT_COLS = 2 * ATT_QK_COLS + ATT_WIDTH
ROPE_THETA = 500000.0
ROPE_DIM = ATT_QK_DIM // 4
SUBLN_EPS = 1e-5
RWKV_WIDTH = 1024
RWKV_HEAD = 64
RWKV_HEADS = RWKV_WIDTH // RWKV_HEAD
DECAY_LORA = 64
ICLR_LORA = 64
GATE_LORA = 160
LORA_COLS = DECAY_LORA + ICLR_LORA + GATE_LORA
LORA_PAD = 384
GN_EPS = 64e-5
SHIFT_WIDTH = 3 * RWKV_WIDTH + LORA_COLS
SHIFT_PAD = 3 * RWKV_WIDTH + 512
N_EXPERTS = 16
CAPACITY_FACTOR = 2
EXPERT_FF = 2048
CHUNK = 64


def _params(*sem):
    return pltpu.CompilerParams(dimension_semantics=sem, vmem_limit_bytes=VMEM_LIMIT_BYTES)


def _dot(a, b):
    return jnp.dot(a, b, preferred_element_type=F32)


def _dot_nt(a, b):
    return lax.dot_general(a, b, (((1,), (1,)), ((), ())), preferred_element_type=F32)


def _split2(x):
    hi = x.astype(BF16)
    lo = (x - hi.astype(F32)).astype(BF16)
    return hi, lo


def _dot_lhs2(x, m_bf16):
    hi, lo = _split2(x)
    return _dot(hi, m_bf16) + _dot(lo, m_bf16)


def _norm_mm_kernel(x_ref, nw_ref, w_ref, o_ref, h_scr):
    @pl.when(pl.program_id(1) == 0)
    def _():
        x = x_ref[...]
        ms = jnp.mean(x * x, axis=-1, keepdims=True)
        h_scr[...] = (x * lax.rsqrt(ms + NORM_EPS) * nw_ref[...]).astype(BF16)

    o_ref[...] = _dot(h_scr[...], w_ref[...]).astype(o_ref.dtype)


def norm_matmul(x, nw, w_bf16, out_dtype, tm=1024, tn=512):
    m, k = x.shape
    n = w_bf16.shape[1]
    return pl.pallas_call(
        _norm_mm_kernel,
        grid=(m // tm, n // tn),
        in_specs=[
            pl.BlockSpec((tm, k), lambda i, j: (i, 0)),
            pl.BlockSpec((1, k), lambda i, j: (0, 0)),
            pl.BlockSpec((k, tn), lambda i, j: (0, j)),
        ],
        out_specs=pl.BlockSpec((tm, tn), lambda i, j: (i, j)),
        out_shape=jax.ShapeDtypeStruct((m, n), out_dtype),
        scratch_shapes=[pltpu.VMEM((tm, k), BF16)],
        compiler_params=_params("parallel", "arbitrary"),
        name="norm_matmul",
    )(x, nw.reshape(1, k), w_bf16)


def _norm_mm_rope_kernel(x_ref, nw_ref, w_ref, pos_ref, freq_ref, slo_ref, shi_ref, o_ref,
                         h_scr, c_scr, lo_scr, hi_scr, *, rope_tiles, q_tiles):
    j = pl.program_id(1)

    @pl.when(j == 0)
    def _():
        x = x_ref[...]
        ms = jnp.mean(x * x, axis=-1, keepdims=True)
        h_scr[...] = (x * lax.rsqrt(ms + NORM_EPS) * nw_ref[...]).astype(BF16)
        ang = pos_ref[...].astype(F32) * freq_ref[...]
        s = jnp.sin(ang)
        c_scr[...] = jnp.cos(ang)
        lo_scr[...] = s * slo_ref[...]
        hi_scr[...] = s * shi_ref[...]

    acc = _dot(h_scr[...], w_ref[...])

    @pl.when(j < rope_tiles)
    def _():
        scale = jnp.where(j < q_tiles, ATT_QK_DIM ** -0.5 * math.log2(math.e), 1.0)
        half = ROPE_DIM // 2
        for g in range(acc.shape[1] // LANES):
            x = acc[:, g * LANES:(g + 1) * LANES]
            x_dn = pltpu.roll(x, half, axis=1)
            x_up = pltpu.roll(x, LANES - half, axis=1)
            y = x * c_scr[...] + x_dn * hi_scr[...] + x_up * lo_scr[...]
            o_ref[:, g * LANES:(g + 1) * LANES] = (y * scale).astype(o_ref.dtype)

    @pl.when(j >= rope_tiles)
    def _():
        o_ref[...] = acc.astype(o_ref.dtype)


def norm_matmul_rope(x, nw, w_bf16, pos, tm=1024, tn=512):
    m, k = x.shape
    n = w_bf16.shape[1]
    lane = jnp.arange(LANES) % ATT_QK_DIM
    half = ROPE_DIM // 2
    inv_freq = ROPE_THETA ** (-jnp.arange(0, ROPE_DIM, 2, dtype=F32) / ROPE_DIM)
    freq = jnp.where(lane < ROPE_DIM, inv_freq[lane % half], 0.0).astype(F32).reshape(1, LANES)
    sgn_lo = jnp.where(lane < half, -1.0, 0.0).astype(F32).reshape(1, LANES)
    sgn_hi = jnp.where((lane >= half) & (lane < ROPE_DIM), 1.0, 0.0).astype(F32).reshape(1, LANES)
    vec = pl.BlockSpec((1, LANES), lambda i, j: (0, 0))
    return pl.pallas_call(
        functools.partial(_norm_mm_rope_kernel, rope_tiles=2 * ATT_QK_COLS // tn, q_tiles=ATT_QK_COLS // tn),
        grid=(m // tm, n // tn),
        in_specs=[
            pl.BlockSpec((tm, k), lambda i, j: (i, 0)),
            pl.BlockSpec((1, k), lambda i, j: (0, 0)),
            pl.BlockSpec((k, tn), lambda i, j: (0, j)),
            pl.BlockSpec((tm, 1), lambda i, j: (i, 0)),
            vec, vec, vec,
        ],
        out_specs=pl.BlockSpec((tm, tn), lambda i, j: (i, j)),
        out_shape=jax.ShapeDtypeStruct((m, n), BF16),
        scratch_shapes=[pltpu.VMEM((tm, k), BF16), pltpu.VMEM((tm, LANES), F32),
                        pltpu.VMEM((tm, LANES), F32), pltpu.VMEM((tm, LANES), F32)],
        compiler_params=_params("parallel", "arbitrary"),
        name="norm_matmul_rope",
    )(x, nw.reshape(1, k), w_bf16, pos, freq, sgn_lo, sgn_hi)


def _attn_kernel(q_ref, k_ref, v_ref, lq1_ref, lk1_ref, lq2_ref, lk2_ref, sw_ref, o_ref, v_scr, *,
                 lambda_init):
    dv = ATT_V_DIM

    @pl.when(pl.program_id(2) == 0)
    def _():
        lane = lax.broadcasted_iota(jnp.int32, (v_scr.shape[0], dv), 1)
        v_scr[:, :dv] = v_ref[...]
        v_scr[:, dv:] = (lane == 0).astype(BF16)

    q = q_ref[...]
    kr = k_ref[...]
    s1 = _dot_nt(q[:, :ATT_QK_DIM], kr[:, :ATT_QK_DIM])
    s2 = _dot_nt(q[:, ATT_QK_DIM:], kr[:, ATT_QK_DIM:])
    e1 = jnp.exp2(s1 - jnp.max(s1, axis=-1, keepdims=True)).astype(BF16)
    e2 = jnp.exp2(s2 - jnp.max(s2, axis=-1, keepdims=True)).astype(BF16)
    lam = (jnp.exp(jnp.sum(lq1_ref[...] * lk1_ref[...], axis=-1, keepdims=True))
           - jnp.exp(jnp.sum(lq2_ref[...] * lk2_ref[...], axis=-1, keepdims=True)) + lambda_init)
    va = v_scr[...]
    o1 = _dot(e1, va)
    o2 = _dot(e2, va)
    o = o1[:, :dv] / o1[:, dv:dv + 1] - o2[:, :dv] * (lam / o2[:, dv:dv + 1])
    ms = jnp.mean(o * o, axis=-1, keepdims=True)
    o = o * lax.rsqrt(ms + SUBLN_EPS) * sw_ref[...] * (1.0 - lambda_init)
    o_ref[...] = o.astype(o_ref.dtype)


def attention(qkv, lq1, lk1, lq2, lk2, subln_w, bsz, seq, lambda_init, tq=256):
    nq = seq // tq
    vec = lambda n: pl.BlockSpec((1, n), lambda b, h, i: (0, 0))
    nh = ATT_HEADS
    return pl.pallas_call(
        functools.partial(_attn_kernel, lambda_init=lambda_init),
        grid=(bsz, nh, nq),
        in_specs=[
            pl.BlockSpec((tq, LANES), lambda b, h, i: (b * nq + i, h)),
            pl.BlockSpec((seq, LANES), lambda b, h, i: (b, nh + h)),
            pl.BlockSpec((seq, LANES), lambda b, h, i: (b, 2 * nh + h)),
            vec(ATT_QK_DIM), vec(ATT_QK_DIM), vec(ATT_QK_DIM), vec(ATT_QK_DIM),
            vec(ATT_V_DIM),
        ],
        out_specs=pl.BlockSpec((tq, LANES), lambda b, h, i: (b * nq + i, h)),
        out_shape=jax.ShapeDtypeStruct((bsz * seq, ATT_WIDTH), BF16),
        scratch_shapes=[pltpu.VMEM((seq, 2 * ATT_V_DIM), BF16)],
        compiler_params=_params("parallel", "parallel", "arbitrary"),
        name="diff_attention",
    )(qkv, qkv, qkv,
      lq1.reshape(1, -1), lk1.reshape(1, -1), lq2.reshape(1, -1), lk2.reshape(1, -1),
      subln_w.reshape(1, -1))


HALO = 8


def _token_shift(z_ref, prev_ref, next_ref, mu_p, mu_n, first, last):
    z = z_ref[...]
    n = z.shape[0]
    row = lax.broadcasted_iota(jnp.int32, z.shape, 0)
    before = jnp.where(first, 0.0, prev_ref[HALO - 1:HALO, :])
    after = jnp.where(last, 0.0, next_ref[0:1, :])
    zp = jnp.where(row == 0, before, pltpu.roll(z, 1, axis=0))
    zn = jnp.where(row == n - 1, after, pltpu.roll(z, n - 1, axis=0))
    return z + mu_p * (zp - z) + mu_n * (zn - z)


def _seg_sum(x, ones_blk):
    parts = []
    for j in range(x.shape[1] // LANES):
        parts.append(_dot_lhs2(x[:, j * LANES:(j + 1) * LANES], ones_blk))
    return jnp.concatenate(parts, axis=1)


def _prep_kernel(r_ref, k_ref, v_ref, lo_ref, rp_ref, kp_ref, vp_ref, lp_ref, rn_ref, kn_ref, vn_ref,
                 ln_ref, mup_ref, mun_ref, w0_ref, du_ref, a0_ref, iu_ref, gu_ref, kk_ref,
                 ka_ref, rk_ref, ones_ref, tril_ref, triu_ref,
                 ab_ref, rb_ref, bt_ref, kt_ref, pl_ref, vb_ref, bonus_ref, g_ref, *, tt):
    first = pl.program_id(1) == 0
    last = pl.program_id(1) == pl.num_programs(1) - 1
    c = RWKV_WIDTH
    shift = lambda z, p, n, lo_col, hi_col: _token_shift(
        z, p, n, mup_ref[:, lo_col:hi_col], mun_ref[:, lo_col:hi_col], first, last)
    r = shift(r_ref, rp_ref, rn_ref, 0, c)
    k = shift(k_ref, kp_ref, kn_ref, c, 2 * c)
    v = shift(v_ref, vp_ref, vn_ref, 2 * c, 3 * c)
    lo = shift(lo_ref, lp_ref, ln_ref, 3 * c, 3 * c + LORA_PAD)
    ones_blk = ones_ref[...]
    vb_ref[...] = v.astype(BF16)
    g_ref[...] = _dot(jax.nn.sigmoid(lo).astype(BF16), gu_ref[...])
    kk = k * kk_ref[...]
    nrm = jnp.sqrt(_seg_sum(kk * kk, ones_blk))
    kk = kk / jnp.maximum(nrm, 1e-12)
    th = jnp.tanh(lo).astype(BF16)
    lob = lo.astype(BF16)
    nchunk = tt // CHUNK
    ksum = jnp.zeros_like(k)
    for d in range(2):
        wl = w0_ref[d] + _dot(th, du_ref[d])
        lw = -math.exp(-0.5) * jax.nn.sigmoid(wl)
        a = jax.nn.sigmoid(a0_ref[d] + _dot(lob, iu_ref[d]))
        kd = k * (1.0 + (a - 1.0) * ka_ref[...])
        ksum = ksum + kd
        tri = tril_ref[...] if d == 0 else triu_ref[...]
        cum = _dot_lhs2_rhs(tri, lw)
        e_pos = jnp.exp(cum)
        e_neg = jnp.exp(-cum)
        ab_ref[d] = (-kk * jnp.exp(cum - lw)).astype(BF16)
        rb_ref[d] = (r * e_pos).astype(BF16)
        bt_ref[d] = (kk * a * e_neg).astype(BF16)
        kt_ref[d] = (kd * e_neg).astype(BF16)
        for c in range(nchunk):
            last = c * CHUNK + (CHUNK - 1 if d == 0 else 0)
            pl_ref[d, c] = e_pos[last:last + 1, :]
    bonus_ref[...] = _seg_sum(r * ksum * rk_ref[...], ones_blk) * v


def _dot_lhs2_rhs(tri_bf16, x):
    hi, lo = _split2(x)
    return _dot(tri_bf16, hi) + _dot(tri_bf16, lo)


def rwkv_prep(pz, mu_prev, mu_next, w0, decay_up, a0, iclr_up, gate_up, k_k, k_a, r_k, bsz, seq, tt=256):
    m = bsz * seq
    c = RWKV_WIDTH
    nt = seq // tt
    ncb = tt // CHUNK
    nc = seq // CHUNK

    def pad_rows(w, start):
        out = jnp.zeros(w.shape[:-2] + (LORA_PAD, c), F32)
        return lax.dynamic_update_slice_in_dim(out, w.astype(F32), start, axis=w.ndim - 2).astype(BF16)

    du = pad_rows(decay_up, 0)
    iu = pad_rows(iclr_up, DECAY_LORA)
    gu = pad_rows(gate_up, DECAY_LORA + ICLR_LORA)
    lane = jnp.arange(LANES)
    ones_blk = (lane[:, None] // RWKV_HEAD == lane[None, :] // RWKV_HEAD).astype(BF16)
    t = jnp.arange(tt)
    same = t[:, None] // CHUNK == t[None, :] // CHUNK
    tril = (same & (t[:, None] >= t[None, :])).astype(BF16)
    triu = (same & (t[:, None] <= t[None, :])).astype(BF16)
    r_k_flat = r_k.reshape(1, c)
    row = lambda: pl.BlockSpec((1, c), lambda b, i: (0, 0))
    full3 = lambda s: pl.BlockSpec(s, lambda b, i: (0, 0, 0))
    tok = lambda j, w: pl.BlockSpec((tt, w), lambda b, i: (b * nt + i, j))
    per = tt // HALO
    prev = lambda j, w: pl.BlockSpec((HALO, w), lambda b, i: (jnp.maximum((b * nt + i) * per - 1, 0), j))
    nxt = lambda j, w: pl.BlockSpec(
        (HALO, w), lambda b, i: (jnp.minimum((b * nt + i + 1) * per, m // HALO - 1), j))
    lora_j = 3 * c // LORA_PAD
    cols = [(0, c), (1, c), (2, c), (lora_j, LORA_PAD)]
    mu_row = pl.BlockSpec((1, pz.shape[1]), lambda b, i: (0, 0))
    dir_tok = pl.BlockSpec((2, tt, c), lambda b, i: (0, b * nt + i, 0))
    outs = pl.pallas_call(
        functools.partial(_prep_kernel, tt=tt),
        grid=(bsz, nt),
        in_specs=[tok(j, w) for j, w in cols] + [prev(j, w) for j, w in cols]
        + [nxt(j, w) for j, w in cols] + [
            mu_row, mu_row,
            full3((2, 1, c)), full3((2, LORA_PAD, c)), full3((2, 1, c)), full3((2, LORA_PAD, c)),
            pl.BlockSpec((LORA_PAD, c), lambda b, i: (0, 0)),
            row(), row(), row(),
            pl.BlockSpec((LANES, LANES), lambda b, i: (0, 0)),
            pl.BlockSpec((tt, tt), lambda b, i: (0, 0)),
            pl.BlockSpec((tt, tt), lambda b, i: (0, 0)),
        ],
        out_specs=[
            dir_tok, dir_tok, dir_tok, dir_tok,
            pl.BlockSpec((2, ncb, 1, c), lambda b, i: (0, b * nt + i, 0, 0)),
            pl.BlockSpec((tt, c), lambda b, i: (b * nt + i, 0)),
            pl.BlockSpec((tt, c), lambda b, i: (b * nt + i, 0)),
            pl.BlockSpec((tt, c), lambda b, i: (b * nt + i, 0)),
        ],
        out_shape=[
            jax.ShapeDtypeStruct((2, m, c), BF16),
            jax.ShapeDtypeStruct((2, m, c), BF16),
            jax.ShapeDtypeStruct((2, m, c), BF16),
            jax.ShapeDtypeStruct((2, m, c), BF16),
            jax.ShapeDtypeStruct((2, bsz * nc, 1, c), F32),
            jax.ShapeDtypeStruct((m, c), BF16),
            jax.ShapeDtypeStruct((m, c), F32),
            jax.ShapeDtypeStruct((m, c), F32),
        ],
        compiler_params=_params("parallel", "parallel"),
        name="rwkv_prep",
    )(*([pz] * 12), mu_prev, mu_next, w0.reshape(2, 1, c), du, a0.reshape(2, 1, c), iu, gu,
      k_k.reshape(1, c), k_a.reshape(1, c), r_k_flat, ones_blk, tril, triu)
    return outs


def _scan_kernel(abf_ref, abb_ref, rbf_ref, rbb_ref, btf_ref, btb_ref, ktf_ref, ktb_ref,
                 vf_ref, vb_ref, plf_ref, plb_ref, yf_ref, yb_ref, m_scr, *, heads):
    @pl.when(pl.program_id(1) == 0)
    def _():
        m_scr[...] = jnp.zeros_like(m_scr)

    ab_ref = (abf_ref, abb_ref)
    rb_ref = (rbf_ref, rbb_ref)
    bt_ref = (btf_ref, btb_ref)
    kt_ref = (ktf_ref, ktb_ref)
    v_ref = (vf_ref, vb_ref)
    pl_ref = (plf_ref, plb_ref)
    y_ref = (yf_ref, yb_ref)
    n = CHUNK
    row = lax.broadcasted_iota(jnp.int32, (2 * n, 2 * n), 0)
    col = lax.broadcasted_iota(jnp.int32, (2 * n, 2 * n), 1)
    top = row < n
    tr = row % n
    tc = col % n
    eye = (lax.broadcasted_iota(jnp.int32, (n, n), 0)
           == lax.broadcasted_iota(jnp.int32, (n, n), 1)).astype(F32)
    keep = ((tr > tc) | (~top & (tr == tc)), (tr < tc) | (~top & (tr == tc)))
    probs = [(d, hh) for d in range(2) for hh in range(heads)]
    sl = lambda hh: slice(hh * n, (hh + 1) * n)
    ab = [ab_ref[d][0, :, sl(hh)] for d, hh in probs]
    rb = [rb_ref[d][0, :, sl(hh)] for d, hh in probs]
    bt = [bt_ref[d][0, :, sl(hh)] for d, hh in probs]
    kt = [kt_ref[d][0, :, sl(hh)] for d, hh in probs]
    vv = [v_ref[d][:, sl(hh)] for d, hh in probs]
    p_last = [pl_ref[d][0, 0, :, sl(hh)] for d, hh in probs]
    idx = range(len(probs))
    g1 = [_dot_nt(jnp.concatenate([ab[i], rb[i]], axis=0), jnp.concatenate([bt[i], kt[i]], axis=0))
          for i in idx]
    g1 = [jnp.where(keep[probs[i][0]], g1[i], 0.0) for i in idx]
    a_ab = [g[:n, :n] for g in g1]
    low = [g[n:, :].astype(BF16) for g in g1]
    tm = [eye + a for a in a_ab]
    pw = [_dot(a.astype(BF16), a.astype(BF16)) for a in a_ab]
    span = 2
    while span < n:
        last = span * 2 >= n
        nxt = []
        for i in idx:
            pwb = pw[i].astype(BF16)
            if last:
                nxt.append((tm[i] + _dot(tm[i].astype(BF16), pwb), None))
            else:
                both = _dot(jnp.concatenate([tm[i], pw[i]], axis=0).astype(BF16), pwb)
                nxt.append((tm[i] + both[:n], both[n:]))
        tm = [t for t, _ in nxt]
        pw = [p for _, p in nxt]
        span *= 2
    wc = [_dot(tm[i].astype(BF16), jnp.concatenate([ab[i], g1[i][:n, n:].astype(BF16)], axis=1)) for i in idx]
    w = [x[:, :n].astype(BF16) for x in wc]
    cuv = [_dot(wc[i][:, n:].astype(BF16), vv[i]).astype(BF16) for i in idx]
    rhs = [jnp.concatenate([cuv[i], vv[i]], axis=0) for i in idx]
    q_m = [rb[i].astype(F32) + _dot(g1[i][n:, :n].astype(BF16), w[i]) for i in idx]
    y_0 = [_dot(low[i], rhs[i]) for i in idx]
    bh_t = [(bt[i].astype(F32) * p_last[i]).T for i in idx]
    kh_t = [(kt[i].astype(F32) * p_last[i]).T for i in idx]
    g_m = [eye * p_last[i] + _dot(bh_t[i].astype(BF16), w[i]) for i in idx]
    h_m = [_dot(jnp.concatenate([bh_t[i], kh_t[i]], axis=1).astype(BF16), rhs[i]) for i in idx]
    upd = [_dot(jnp.concatenate([q_m[i], g_m[i]], axis=0).astype(BF16), m_scr[d, hh].astype(BF16))
           for i, (d, hh) in enumerate(probs)]
    for i, (d, hh) in enumerate(probs):
        y_ref[d][:, sl(hh)] = upd[i][:n] + y_0[i]
        m_scr[d, hh] = upd[i][n:] + h_m[i]


def rwkv_scan(ab, rb, bt, kt, vb, p_last, bsz, seq):
    nc = seq // CHUNK
    nh = RWKV_HEADS
    n = CHUNK
    c_w = RWKV_WIDTH
    fwd = lambda b, c: b * nc + c
    bwd = lambda b, c: b * nc + nc - 1 - c
    dir_f = pl.BlockSpec((1, n, c_w), lambda b, c: (0, fwd(b, c), 0))
    dir_b = pl.BlockSpec((1, n, c_w), lambda b, c: (1, bwd(b, c), 0))
    tok_f = pl.BlockSpec((n, c_w), lambda b, c: (fwd(b, c), 0))
    tok_b = pl.BlockSpec((n, c_w), lambda b, c: (bwd(b, c), 0))
    pl_f = pl.BlockSpec((1, 1, 1, c_w), lambda b, c: (0, fwd(b, c), 0, 0))
    pl_b = pl.BlockSpec((1, 1, 1, c_w), lambda b, c: (1, bwd(b, c), 0, 0))
    y_shape = jax.ShapeDtypeStruct((bsz * seq, c_w), F32)
    return pl.pallas_call(
        functools.partial(_scan_kernel, heads=nh),
        grid=(bsz, nc),
        in_specs=[dir_f, dir_b, dir_f, dir_b, dir_f, dir_b, dir_f, dir_b, tok_f, tok_b, pl_f, pl_b],
        out_specs=[tok_f, tok_b],
        out_shape=[y_shape, y_shape],
        scratch_shapes=[pltpu.VMEM((2, nh, n, n), F32)],
        compiler_params=_params("parallel", "arbitrary"),
        name="rwkv_scan",
    )(ab, ab, rb, rb, bt, bt, kt, kt, vb, vb, p_last, p_last)


def _post_kernel(yf_ref, yb_ref, bonus_ref, g_ref, lw_ref, lb_ref, o_ref):
    y = yf_ref[...] + yb_ref[...]
    n = RWKV_HEAD
    parts = []
    for h in range(RWKV_HEADS):
        yh = y[:, h * n:(h + 1) * n]
        mu = jnp.mean(yh, axis=-1, keepdims=True)
        yc = yh - mu
        var = jnp.mean(yc * yc, axis=-1, keepdims=True)
        parts.append(yc * lax.rsqrt(var + GN_EPS))
    yn = jnp.concatenate(parts, axis=1)
    o_ref[...] = ((yn * lw_ref[...] + lb_ref[...] + bonus_ref[...]) * g_ref[...]).astype(o_ref.dtype)


def rwkv_post(y_fwd, y_bwd, bonus, g, ln_w, ln_b, tt=256):
    m = y_fwd.shape[0]
    c = RWKV_WIDTH
    tok = pl.BlockSpec((tt, c), lambda i: (i, 0))
    row = pl.BlockSpec((1, c), lambda i: (0, 0))
    return pl.pallas_call(
        _post_kernel,
        grid=(m // tt,),
        in_specs=[tok, tok, tok, tok, row, row],
        out_specs=tok,
        out_shape=jax.ShapeDtypeStruct((m, c), BF16),
        compiler_params=_params("parallel"),
        name="rwkv_post",
    )(y_fwd, y_bwd, bonus, g, ln_w.reshape(1, c), ln_b.reshape(1, c))


def _out_proj_kernel(x_ref, a_ref, r_ref, wa_ref, wr_ref, o_ref):
    o_ref[...] = x_ref[...] + _dot(a_ref[...], wa_ref[...]) + _dot(r_ref[...], wr_ref[...])


def out_proj(x, att, rw, w_att, w_rw, tm=1024, tn=512):
    m, n = x.shape
    return pl.pallas_call(
        _out_proj_kernel,
        grid=(m // tm, n // tn),
        in_specs=[
            pl.BlockSpec((tm, tn), lambda i, j: (i, j)),
            pl.BlockSpec((tm, ATT_WIDTH), lambda i, j: (i, 0)),
            pl.BlockSpec((tm, RWKV_WIDTH), lambda i, j: (i, 0)),
            pl.BlockSpec((ATT_WIDTH, tn), lambda i, j: (0, j)),
            pl.BlockSpec((RWKV_WIDTH, tn), lambda i, j: (0, j)),
        ],
        out_specs=pl.BlockSpec((tm, tn), lambda i, j: (i, j)),
        out_shape=jax.ShapeDtypeStruct((m, n), F32),
        compiler_params=_params("parallel", "parallel"),
        name="out_proj",
    )(x, att, rw, w_att, w_rw)


def _router_kernel(x_ref, nw_ref, wh_ref, wl_ref, hb_ref, afft_ref):
    x = x_ref[...]
    ms = jnp.mean(x * x, axis=-1, keepdims=True)
    h = x * lax.rsqrt(ms + NORM_EPS) * nw_ref[...]
    hb_ref[...] = h.astype(BF16)
    hi, lo = _split2(h)
    logits = _dot(hi, wh_ref[...]) + _dot(lo, wh_ref[...]) + _dot(hi, wl_ref[...])
    lane = lax.broadcasted_iota(jnp.int32, logits.shape, 1)
    valid = lane < N_EXPERTS
    logits = jnp.where(valid, logits, -1e30)
    e = jnp.where(valid, jnp.exp(logits - jnp.max(logits, axis=-1, keepdims=True)), 0.0)
    aff = e / jnp.sum(e, axis=-1, keepdims=True)
    afft_ref[...] = aff.T[:N_EXPERTS, :]


def router(x1, norm_w, w_router, tm=512):
    m, k = x1.shape
    wpad = jnp.zeros((k, LANES), F32).at[:, :N_EXPERTS].set(w_router)
    wh, wl = _split2(wpad)
    return pl.pallas_call(
        _router_kernel,
        grid=(m // tm,),
        in_specs=[
            pl.BlockSpec((tm, k), lambda i: (i, 0)),
            pl.BlockSpec((1, k), lambda i: (0, 0)),
            pl.BlockSpec((k, LANES), lambda i: (0, 0)),
            pl.BlockSpec((k, LANES), lambda i: (0, 0)),
        ],
        out_specs=[
            pl.BlockSpec((tm, k), lambda i: (i, 0)),
            pl.BlockSpec((N_EXPERTS, tm), lambda i: (0, i)),
        ],
        out_shape=[
            jax.ShapeDtypeStruct((m, k), BF16),
            jax.ShapeDtypeStruct((N_EXPERTS, m), F32),
        ],
        compiler_params=_params("parallel"),
        name="router",
    )(x1, norm_w.reshape(1, k), wh, wl)


def _prefix_count(x, upper):
    outs = []
    carry = jnp.zeros((x.shape[0], 1), F32)
    for j in range(x.shape[1] // LANES):
        xt = x[:, j * LANES:(j + 1) * LANES]
        outs.append(_dot(xt.astype(BF16), upper) + carry)
        carry = carry + jnp.sum(xt, axis=-1, keepdims=True)
    return jnp.concatenate(outs, axis=1)


def _topk_kernel(afft_ref, sp_ref, spt_ref, *, cap):
    a = afft_ref[...]
    ne, seq = a.shape
    bits = lax.bitcast_convert_type(a, jnp.int32)
    thr = jnp.zeros((ne, 1), jnp.int32)
    for bit in range(30, -1, -1):
        cand = thr | (1 << bit)
        cnt = jnp.sum((bits >= cand).astype(F32), axis=-1, keepdims=True)
        thr = jnp.where(cnt >= cap, cand, thr)
    above = (bits > thr).astype(F32)
    tied = (bits == thr).astype(F32)
    r_i = lax.broadcasted_iota(jnp.int32, (LANES, LANES), 0)
    c_i = lax.broadcasted_iota(jnp.int32, (LANES, LANES), 1)
    upper = (r_i < c_i).astype(BF16)
    need = cap - jnp.sum(above, axis=-1, keepdims=True)
    sel = above + tied * (_prefix_count(tied, upper) < need).astype(F32)
    spt = jnp.where(sel > 0.5, _prefix_count(sel, upper), -1.0)
    spt_ref[...] = spt
    full = jnp.concatenate([spt, jnp.full((LANES - ne, seq), -1.0, F32)], axis=0)
    sp_ref[...] = full.T


def topk_select(afft, bsz, seq, cap):
    return pl.pallas_call(
        functools.partial(_topk_kernel, cap=cap),
        grid=(bsz,),
        in_specs=[pl.BlockSpec((N_EXPERTS, seq), lambda b: (0, b))],
        out_specs=[
            pl.BlockSpec((seq, LANES), lambda b: (b, 0)),
            pl.BlockSpec((N_EXPERTS, seq), lambda b: (0, b)),
        ],
        out_shape=[
            jax.ShapeDtypeStruct((bsz * seq, LANES), F32),
            jax.ShapeDtypeStruct((N_EXPERTS, bsz * seq), F32),
        ],
        compiler_params=_params("parallel"),
        name="topk_select",
    )(afft)


def _gather_kernel(spt_ref, afft_ref, h_ref, xe_ref, gate_ref, *, cap):
    e = pl.program_id(1)
    seq = h_ref.shape[0]
    sp = spt_ref[pl.ds(e, 1), :]
    slot = lax.broadcasted_iota(jnp.int32, (cap, seq), 0).astype(F32)
    hit = sp == slot
    xe_ref[0] = _dot(hit.astype(BF16), h_ref[...]).astype(BF16)
    gate_ref[0] = jnp.sum(jnp.where(hit, afft_ref[pl.ds(e, 1), :], 0.0), axis=-1, keepdims=True)


def moe_gather(spt, afft, hb, bsz, seq, cap):
    d = hb.shape[1]
    return pl.pallas_call(
        functools.partial(_gather_kernel, cap=cap),
        grid=(bsz, N_EXPERTS),
        in_specs=[
            pl.BlockSpec((N_EXPERTS, seq), lambda b, e: (0, b)),
            pl.BlockSpec((N_EXPERTS, seq), lambda b, e: (0, b)),
            pl.BlockSpec((seq, d), lambda b, e: (b, 0)),
        ],
        out_specs=[
            pl.BlockSpec((1, cap, d), lambda b, e: (e, b, 0)),
            pl.BlockSpec((1, cap, 1), lambda b, e: (e, b, 0)),
        ],
        out_shape=[
            jax.ShapeDtypeStruct((N_EXPERTS, bsz * cap, d), BF16),
            jax.ShapeDtypeStruct((N_EXPERTS, bsz * cap, 1), F32),
        ],
        compiler_params=_params("parallel", "arbitrary"),
        name="moe_gather",
    )(spt, afft, hb)


def _expert_kernel(xe_ref, gt_ref, wg_ref, wu_ref, wd_ref, ye_ref, acc_scr):
    f = pl.program_id(1)

    @pl.when(f == 0)
    def _():
        acc_scr[...] = jnp.zeros_like(acc_scr)

    xe = xe_ref[0]
    gate = _dot(xe, wg_ref[0].astype(BF16))
    up = _dot(xe, wu_ref[0].astype(BF16))
    hid = (gate * jax.nn.sigmoid(gate) * up).astype(BF16)
    acc_scr[...] += _dot(hid, wd_ref[0].astype(BF16))

    @pl.when(f == pl.num_programs(1) - 1)
    def _():
        ye_ref[0] = (acc_scr[...] * gt_ref[0]).astype(ye_ref.dtype)


def moe_experts(xe, gates, e_gate, e_up, e_down, tf=256):
    ne, rows, d = xe.shape
    ff = e_gate.shape[2]
    return pl.pallas_call(
        _expert_kernel,
        grid=(ne, ff // tf),
        in_specs=[
            pl.BlockSpec((1, rows, d), lambda e, f: (e, 0, 0)),
            pl.BlockSpec((1, rows, 1), lambda e, f: (e, 0, 0)),
            pl.BlockSpec((1, d, tf), lambda e, f: (e, 0, f)),
            pl.BlockSpec((1, d, tf), lambda e, f: (e, 0, f)),
            pl.BlockSpec((1, tf, d), lambda e, f: (e, f, 0)),
        ],
        out_specs=pl.BlockSpec((1, rows, d), lambda e, f: (e, 0, 0)),
        out_shape=jax.ShapeDtypeStruct((ne, rows, d), BF16),
        scratch_shapes=[pltpu.VMEM((rows, d), F32)],
        compiler_params=_params("parallel", "arbitrary"),
        name="moe_experts",
    )(xe, gates, e_gate, e_up, e_down)


def _combine_kernel(x_ref, sp_ref, ye_ref, nw_ref, o_ref, acc_scr, *, cap, group):
    eg = pl.program_id(2)

    @pl.when(eg == 0)
    def _():
        acc_scr[...] = x_ref[...]

    tt = x_ref.shape[0]
    lane = lax.broadcasted_iota(jnp.int32, (tt, LANES), 1)
    slot = lax.broadcasted_iota(jnp.int32, (tt, cap), 1).astype(F32)
    sp_all = sp_ref[...]
    hits = []
    for k in range(group):
        sp = jnp.sum(jnp.where(lane == eg * group + k, sp_all, 0.0), axis=-1, keepdims=True)
        hits.append((sp == slot).astype(BF16))
    onehot = jnp.concatenate(hits, axis=1)
    acc_scr[...] += _dot(onehot, ye_ref[...].reshape(group * cap, ye_ref.shape[2]))

    @pl.when(eg == pl.num_programs(2) - 1)
    def _():
        y = acc_scr[...]
        ms = jnp.mean(y * y, axis=-1, keepdims=True)
        o_ref[...] = y * lax.rsqrt(ms + NORM_EPS) * nw_ref[...]


def moe_combine(x1, sp, ye, norm_w, bsz, seq, cap, tt=512, group=4):
    m, d = x1.shape
    nt = seq // tt
    return pl.pallas_call(
        functools.partial(_combine_kernel, cap=cap, group=group),
        grid=(bsz, nt, N_EXPERTS // group),
        in_specs=[
            pl.BlockSpec((tt, d), lambda b, i, e: (b * nt + i, 0)),
            pl.BlockSpec((tt, LANES), lambda b, i, e: (b * nt + i, 0)),
            pl.BlockSpec((group, cap, d), lambda b, i, e: (e, b, 0)),
            pl.BlockSpec((1, d), lambda b, i, e: (0, 0)),
        ],
        out_specs=pl.BlockSpec((tt, d), lambda b, i, e: (b * nt + i, 0)),
        out_shape=jax.ShapeDtypeStruct((m, d), F32),
        scratch_shapes=[pltpu.VMEM((tt, d), F32)],
        compiler_params=_params("parallel", "parallel", "arbitrary"),
        name="moe_combine",
    )(x1, sp, ye, norm_w.reshape(1, d))


def _pad_cols(w, width):
    return jnp.pad(w, [(0, 0)] * (w.ndim - 1) + [(0, width - w.shape[-1])])


def rwkv_mixer(pz, mu_prev, mu_next, w0, decay_up, a0, iclr_up, gate_up, k_k, k_a, r_k, ln_x_w, ln_x_b,
               bsz, seq):
    ab, rb, bt, kt, p_last, vb, bonus, g = rwkv_prep(
        pz, mu_prev, mu_next, w0, decay_up, a0, iclr_up, gate_up, k_k, k_a, r_k, bsz, seq)
    y_fwd, y_bwd = rwkv_scan(ab, rb, bt, kt, vb, p_last, bsz, seq)
    return rwkv_post(y_fwd, y_bwd, bonus, g, ln_x_w, ln_x_b)


def moe_block(x1, norm2_w, w_router, e_gate, e_up, e_down, norm_f_w, bsz, seq):
    cap = CAPACITY_FACTOR * seq // N_EXPERTS
    hb, afft = router(x1, norm2_w, w_router)
    sp, spt = topk_select(afft, bsz, seq, cap)
    xe, gates = moe_gather(spt, afft, hb, bsz, seq, cap)
    ye = moe_experts(xe, gates, e_gate, e_up, e_down)
    return moe_combine(x1, sp, ye, norm_f_w, bsz, seq, cap)


def kernel(x, positions, norm1_w, w_in, mu_prev, mu_next, lambda_q1, lambda_k1, lambda_q2, lambda_k2, subln_w, w0, decay_up, a0, iclr_up, gate_up, k_k, k_a, r_k, ln_x_w, ln_x_b, w_out, norm2_w, w_router, e_gate, e_up, e_down, norm_f_w):
    bsz, seq, d = x.shape
    m = bsz * seq
    xf = x.reshape(m, d)
    pos = positions.reshape(m, 1)
    lambda_init = 0.8 - 0.6 * math.exp(-0.3 * 0)

    w_att = w_in[0][:, :ATT_COLS].astype(BF16)
    w_z = _pad_cols(w_in[0][:, ATT_COLS:], SHIFT_PAD).astype(BF16)
    qkv = norm_matmul_rope(xf, norm1_w[0], w_att, pos)
    p_z = norm_matmul(xf, norm1_w[0], w_z, F32)

    att = attention(qkv, lambda_q1[0], lambda_k1[0], lambda_q2[0], lambda_k2[0],
                    subln_w[0], bsz, seq, lambda_init)

    rw = rwkv_mixer(p_z, _pad_cols(mu_prev, SHIFT_PAD), _pad_cols(mu_next, SHIFT_PAD),
                    w0[0], decay_up[0], a0[0], iclr_up[0], gate_up[0], k_k[0], k_a[0], r_k[0],
                    ln_x_w[0], ln_x_b[0], bsz, seq)

    wo = w_out[0].astype(BF16)
    x1 = out_proj(xf, att, rw, wo[:ATT_WIDTH], wo[ATT_WIDTH:])

    out = moe_block(x1, norm2_w[0], w_router[0], e_gate[0], e_up[0], e_down[0], norm_f_w, bsz, seq)
    return out.reshape(bsz, seq, d)
```

```python
import functools
import math

import jax
import jax.numpy as jnp
from jax import lax
from jax.experimental import pallas as pl
from jax.experimental.pallas import tpu as pltpu

F32 = jnp.float32
BF16 = jnp.bfloat16

LANES = 128
VMEM_LIMIT_BYTES = 56 * 1024 * 1024

D_MODEL = 2048
NORM_EPS = 1e-6
ATT_HEADS = 8
ATT_QK_DIM = 64
ATT_V_DIM = 128
ATT_WIDTH = ATT_HEADS * ATT_V_DIM
ATT_QK_COLS = ATT_HEADS * 2 * ATT_QK_DIM
ATT_COLS = 2 * ATT_QK_COLS + ATT_WIDTH
ROPE_THETA = 500000.0
ROPE_DIM = ATT_QK_DIM // 4
SUBLN_EPS = 1e-5
RWKV_WIDTH = 1024
RWKV_HEAD = 64
RWKV_HEADS = RWKV_WIDTH // RWKV_HEAD
DECAY_LORA = 64
ICLR_LORA = 64
GATE_LORA = 160
LORA_COLS = DECAY_LORA + ICLR_LORA + GATE_LORA
LORA_PAD = 384
GN_EPS = 64e-5
SHIFT_WIDTH = 3 * RWKV_WIDTH + LORA_COLS
SHIFT_PAD = 3 * RWKV_WIDTH + 512
N_EXPERTS = 16
CAPACITY_FACTOR = 2
EXPERT_FF = 2048
CHUNK = 64


def _params(*sem):
    return pltpu.CompilerParams(dimension_semantics=sem, vmem_limit_bytes=VMEM_LIMIT_BYTES)


def _dot(a, b):
    return jnp.dot(a, b, preferred_element_type=F32)


def _dot_nt(a, b):
    return lax.dot_general(a, b, (((1,), (1,)), ((), ())), preferred_element_type=F32)


def _split2(x):
    hi = x.astype(BF16)
    lo = (x - hi.astype(F32)).astype(BF16)
    return hi, lo


def _dot_lhs2(x, m_bf16):
    hi, lo = _split2(x)
    return _dot(hi, m_bf16) + _dot(lo, m_bf16)


def _rms_norm_kernel(x_ref, nw_ref, o_ref):
    x = x_ref[...]
    ms = jnp.mean(x * x, axis=-1, keepdims=True)
    o_ref[...] = (x * lax.rsqrt(ms + NORM_EPS) * nw_ref[...]).astype(o_ref.dtype)


def rms_norm_bf16(x, nw, tm=512):
    m, k = x.shape
    return pl.pallas_call(
        _rms_norm_kernel,
        grid=(m // tm,),
        in_specs=[pl.BlockSpec((tm, k), lambda i: (i, 0)), pl.BlockSpec((1, k), lambda i: (0, 0))],
        out_specs=pl.BlockSpec((tm, k), lambda i: (i, 0)),
        out_shape=jax.ShapeDtypeStruct((m, k), BF16),
        compiler_params=_params("parallel"),
        name="rms_norm",
    )(x, nw.reshape(1, k))


def _mm_kernel(h_ref, w_ref, o_ref):
    o_ref[...] = _dot(h_ref[...], w_ref[...]).astype(o_ref.dtype)


def matmul(h, w_bf16, out_dtype, tm=2048, tn=512):
    m, k = h.shape
    n = w_bf16.shape[1]
    return pl.pallas_call(
        _mm_kernel,
        grid=(m // tm, n // tn),
        in_specs=[
            pl.BlockSpec((tm, k), lambda i, j: (i, 0)),
            pl.BlockSpec((k, tn), lambda i, j: (0, j)),
        ],
        out_specs=pl.BlockSpec((tm, tn), lambda i, j: (i, j)),
        out_shape=jax.ShapeDtypeStruct((m, n), out_dtype),
        compiler_params=_params("parallel", "arbitrary"),
        name="matmul",
    )(h, w_bf16)


def _mm_rope_kernel(h_ref, w_ref, pos_ref, freq_ref, slo_ref, shi_ref, o_ref,
                    c_scr, lo_scr, hi_scr, *, rope_tiles, q_tiles):
    j = pl.program_id(1)

    @pl.when(j == 0)
    def _():
        ang = pos_ref[...].astype(F32) * freq_ref[...]
        s = jnp.sin(ang)
        c_scr[...] = jnp.cos(ang)
        lo_scr[...] = s * slo_ref[...]
        hi_scr[...] = s * shi_ref[...]

    acc = _dot(h_ref[...], w_ref[...])

    @pl.when(j < rope_tiles)
    def _():
        scale = jnp.where(j < q_tiles, ATT_QK_DIM ** -0.5 * math.log2(math.e), 1.0)
        half = ROPE_DIM // 2
        for g in range(acc.shape[1] // LANES):
            x = acc[:, g * LANES:(g + 1) * LANES]
            x_dn = pltpu.roll(x, half, axis=1)
            x_up = pltpu.roll(x, LANES - half, axis=1)
            y = x * c_scr[...] + x_dn * hi_scr[...] + x_up * lo_scr[...]
            o_ref[:, g * LANES:(g + 1) * LANES] = (y * scale).astype(o_ref.dtype)

    @pl.when(j >= rope_tiles)
    def _():
        o_ref[...] = acc.astype(o_ref.dtype)


def matmul_rope(h, w_bf16, pos, tm=2048, tn=512):
    m, k = h.shape
    n = w_bf16.shape[1]
    lane = jnp.arange(LANES) % ATT_QK_DIM
    half = ROPE_DIM // 2
    inv_freq = ROPE_THETA ** (-jnp.arange(0, ROPE_DIM, 2, dtype=F32) / ROPE_DIM)
    freq = jnp.where(lane < ROPE_DIM, inv_freq[lane % half], 0.0).astype(F32).reshape(1, LANES)
    sgn_lo = jnp.where(lane < half, -1.0, 0.0).astype(F32).reshape(1, LANES)
    sgn_hi = jnp.where((lane >= half) & (lane < ROPE_DIM), 1.0, 0.0).astype(F32).reshape(1, LANES)
    vec = pl.BlockSpec((1, LANES), lambda i, j: (0, 0))
    return pl.pallas_call(
        functools.partial(_mm_rope_kernel, rope_tiles=2 * ATT_QK_COLS // tn, q_tiles=ATT_QK_COLS // tn),
        grid=(m // tm, n // tn),
        in_specs=[
            pl.BlockSpec((tm, k), lambda i, j: (i, 0)),
            pl.BlockSpec((k, tn), lambda i, j: (0, j)),
            pl.BlockSpec((tm, 1), lambda i, j: (i, 0)),
            vec, vec, vec,
        ],
        out_specs=pl.BlockSpec((tm, tn), lambda i, j: (i, j)),
        out_shape=jax.ShapeDtypeStruct((m, n), BF16),
        scratch_shapes=[pltpu.VMEM((tm, LANES), F32), pltpu.VMEM((tm, LANES), F32),
                        pltpu.VMEM((tm, LANES), F32)],
        compiler_params=_params("parallel", "arbitrary"),
        name="matmul_rope",
    )(h, w_bf16, pos, freq, sgn_lo, sgn_hi)


def _attn_kernel(q_ref, k_ref, v_ref, lq1_ref, lk1_ref, lq2_ref, lk2_ref, sw_ref, o_ref, v_scr, *,
                 lambda_init):
    dv = ATT_V_DIM

    @pl.when(pl.program_id(2) == 0)
    def _():
        lane = lax.broadcasted_iota(jnp.int32, (v_scr.shape[0], dv), 1)
        v_scr[:, :dv] = v_ref[...]
        v_scr[:, dv:] = (lane == 0).astype(BF16)

    q = q_ref[...]
    kr = k_ref[...]
    s1 = _dot_nt(q[:, :ATT_QK_DIM], kr[:, :ATT_QK_DIM])
    s2 = _dot_nt(q[:, ATT_QK_DIM:], kr[:, ATT_QK_DIM:])
    e1 = jnp.exp2(s1 - jnp.max(s1, axis=-1, keepdims=True)).astype(BF16)
    e2 = jnp.exp2(s2 - jnp.max(s2, axis=-1, keepdims=True)).astype(BF16)
    lam = (jnp.exp(jnp.sum(lq1_ref[...] * lk1_ref[...], axis=-1, keepdims=True))
           - jnp.exp(jnp.sum(lq2_ref[...] * lk2_ref[...], axis=-1, keepdims=True)) + lambda_init)
    va = v_scr[...]
    o1 = _dot(e1, va)
    o2 = _dot(e2, va)
    o = o1[:, :dv] / o1[:, dv:dv + 1] - o2[:, :dv] * (lam / o2[:, dv:dv + 1])
    ms = jnp.mean(o * o, axis=-1, keepdims=True)
    o = o * lax.rsqrt(ms + SUBLN_EPS) * sw_ref[...] * (1.0 - lambda_init)
    o_ref[...] = o.astype(o_ref.dtype)


def attention(qkv, lq1, lk1, lq2, lk2, subln_w, bsz, seq, lambda_init, tq=256):
    nq = seq // tq
    vec = lambda n: pl.BlockSpec((1, n), lambda b, h, i: (0, 0))
    nh = ATT_HEADS
    return pl.pallas_call(
        functools.partial(_attn_kernel, lambda_init=lambda_init),
        grid=(bsz, nh, nq),
        in_specs=[
            pl.BlockSpec((tq, LANES), lambda b, h, i: (b * nq + i, h)),
            pl.BlockSpec((seq, LANES), lambda b, h, i: (b, nh + h)),
            pl.BlockSpec((seq, LANES), lambda b, h, i: (b, 2 * nh + h)),
            vec(ATT_QK_DIM), vec(ATT_QK_DIM), vec(ATT_QK_DIM), vec(ATT_QK_DIM),
            vec(ATT_V_DIM),
        ],
        out_specs=pl.BlockSpec((tq, LANES), lambda b, h, i: (b * nq + i, h)),
        out_shape=jax.ShapeDtypeStruct((bsz * seq, ATT_WIDTH), BF16),
        scratch_shapes=[pltpu.VMEM((seq, 2 * ATT_V_DIM), BF16)],
        compiler_params=_params("parallel", "parallel", "arbitrary"),
        name="diff_attention",
    )(qkv, qkv, qkv,
      lq1.reshape(1, -1), lk1.reshape(1, -1), lq2.reshape(1, -1), lk2.reshape(1, -1),
      subln_w.reshape(1, -1))


HALO = 8


def _token_shift(z_ref, prev_ref, next_ref, mu_p, mu_n, first, last):
    z = z_ref[...]
    n = z.shape[0]
    row = lax.broadcasted_iota(jnp.int32, z.shape, 0)
    before = jnp.where(first, 0.0, prev_ref[HALO - 1:HALO, :])
    after = jnp.where(last, 0.0, next_ref[0:1, :])
    zp = jnp.where(row == 0, before, pltpu.roll(z, 1, axis=0))
    zn = jnp.where(row == n - 1, after, pltpu.roll(z, n - 1, axis=0))
    return z + mu_p * (zp - z) + mu_n * (zn - z)


def _seg_sum(x, ones_blk):
    parts = []
    for j in range(x.shape[1] // LANES):
        parts.append(_dot_lhs2(x[:, j * LANES:(j + 1) * LANES], ones_blk))
    return jnp.concatenate(parts, axis=1)


def _prep_kernel(r_ref, k_ref, v_ref, lo_ref, rp_ref, kp_ref, vp_ref, lp_ref, rn_ref, kn_ref, vn_ref,
                 ln_ref, mup_ref, mun_ref, w0_ref, du_ref, a0_ref, iu_ref, gu_ref, kk_ref,
                 ka_ref, rk_ref, ones_ref, tril_ref, triu_ref,
                 ab_ref, rb_ref, bt_ref, kt_ref, pl_ref, vb_ref, bonus_ref, g_ref, *, tt):
    first = pl.program_id(1) == 0
    last = pl.program_id(1) == pl.num_programs(1) - 1
    c = RWKV_WIDTH
    shift = lambda z, p, n, lo_col, hi_col: _token_shift(
        z, p, n, mup_ref[:, lo_col:hi_col], mun_ref[:, lo_col:hi_col], first, last)
    r = shift(r_ref, rp_ref, rn_ref, 0, c)
    k = shift(k_ref, kp_ref, kn_ref, c, 2 * c)
    v = shift(v_ref, vp_ref, vn_ref, 2 * c, 3 * c)
    lo = shift(lo_ref, lp_ref, ln_ref, 3 * c, 3 * c + LORA_PAD)
    ones_blk = ones_ref[...]
    vb_ref[...] = v.astype(BF16)
    g_ref[...] = _dot(jax.nn.sigmoid(lo).astype(BF16), gu_ref[...])
    kk = k * kk_ref[...]
    nrm = jnp.sqrt(_seg_sum(kk * kk, ones_blk))
    kk = kk / jnp.maximum(nrm, 1e-12)
    th = jnp.tanh(lo).astype(BF16)
    lob = lo.astype(BF16)
    nchunk = tt // CHUNK
    ksum = jnp.zeros_like(k)
    for d in range(2):
        wl = w0_ref[d] + _dot(th, du_ref[d])
        lw = -math.exp(-0.5) * jax.nn.sigmoid(wl)
        a = jax.nn.sigmoid(a0_ref[d] + _dot(lob, iu_ref[d]))
        kd = k * (1.0 + (a - 1.0) * ka_ref[...])
        ksum = ksum + kd
        tri = tril_ref[...] if d == 0 else triu_ref[...]
        cum = _dot_lhs2_rhs(tri, lw)
        e_pos = jnp.exp(cum)
        e_neg = jnp.exp(-cum)
        ab_ref[d] = (-kk * jnp.exp(cum - lw)).astype(BF16)
        rb_ref[d] = (r * e_pos).astype(BF16)
        bt_ref[d] = (kk * a * e_neg).astype(BF16)
        kt_ref[d] = (kd * e_neg).astype(BF16)
        for c in range(nchunk):
            last = c * CHUNK + (CHUNK - 1 if d == 0 else 0)
            pl_ref[d, c] = e_pos[last:last + 1, :]
    bonus_ref[...] = _seg_sum(r * ksum * rk_ref[...], ones_blk) * v


def _dot_lhs2_rhs(tri_bf16, x):
    hi, lo = _split2(x)
    return _dot(tri_bf16, hi) + _dot(tri_bf16, lo)


def rwkv_prep(pz, mu_prev, mu_next, w0, decay_up, a0, iclr_up, gate_up, k_k, k_a, r_k, bsz, seq, tt=256):
    m = bsz * seq
    c = RWKV_WIDTH
    nt = seq // tt
    ncb = tt // CHUNK
    nc = seq // CHUNK

    def pad_rows(w, start):
        out = jnp.zeros(w.shape[:-2] + (LORA_PAD, c), F32)
        return lax.dynamic_update_slice_in_dim(out, w.astype(F32), start, axis=w.ndim - 2).astype(BF16)

    du = pad_rows(decay_up, 0)
    iu = pad_rows(iclr_up, DECAY_LORA)
    gu = pad_rows(gate_up, DECAY_LORA + ICLR_LORA)
    lane = jnp.arange(LANES)
    ones_blk = (lane[:, None] // RWKV_HEAD == lane[None, :] // RWKV_HEAD).astype(BF16)
    t = jnp.arange(tt)
    same = t[:, None] // CHUNK == t[None, :] // CHUNK
    tril = (same & (t[:, None] >= t[None, :])).astype(BF16)
    triu = (same & (t[:, None] <= t[None, :])).astype(BF16)
    r_k_flat = r_k.reshape(1, c)
    row = lambda: pl.BlockSpec((1, c), lambda b, i: (0, 0))
    full3 = lambda s: pl.BlockSpec(s, lambda b, i: (0, 0, 0))
    tok = lambda j, w: pl.BlockSpec((tt, w), lambda b, i: (b * nt + i, j))
    per = tt // HALO
    prev = lambda j, w: pl.BlockSpec((HALO, w), lambda b, i: (jnp.maximum((b * nt + i) * per - 1, 0), j))
    nxt = lambda j, w: pl.BlockSpec(
        (HALO, w), lambda b, i: (jnp.minimum((b * nt + i + 1) * per, m // HALO - 1), j))
    lora_j = 3 * c // LORA_PAD
    cols = [(0, c), (1, c), (2, c), (lora_j, LORA_PAD)]
    mu_row = pl.BlockSpec((1, pz.shape[1]), lambda b, i: (0, 0))
    dir_tok = pl.BlockSpec((2, tt, c), lambda b, i: (0, b * nt + i, 0))
    outs = pl.pallas_call(
        functools.partial(_prep_kernel, tt=tt),
        grid=(bsz, nt),
        in_specs=[tok(j, w) for j, w in cols] + [prev(j, w) for j, w in cols]
        + [nxt(j, w) for j, w in cols] + [
            mu_row, mu_row,
            full3((2, 1, c)), full3((2, LORA_PAD, c)), full3((2, 1, c)), full3((2, LORA_PAD, c)),
            pl.BlockSpec((LORA_PAD, c), lambda b, i: (0, 0)),
            row(), row(), row(),
            pl.BlockSpec((LANES, LANES), lambda b, i: (0, 0)),
            pl.BlockSpec((tt, tt), lambda b, i: (0, 0)),
            pl.BlockSpec((tt, tt), lambda b, i: (0, 0)),
        ],
        out_specs=[
            dir_tok, dir_tok, dir_tok, dir_tok,
            pl.BlockSpec((2, ncb, 1, c), lambda b, i: (0, b * nt + i, 0, 0)),
            pl.BlockSpec((tt, c), lambda b, i: (b * nt + i, 0)),
            pl.BlockSpec((tt, c), lambda b, i: (b * nt + i, 0)),
            pl.BlockSpec((tt, c), lambda b, i: (b * nt + i, 0)),
        ],
        out_shape=[
            jax.ShapeDtypeStruct((2, m, c), BF16),
            jax.ShapeDtypeStruct((2, m, c), BF16),
            jax.ShapeDtypeStruct((2, m, c), BF16),
            jax.ShapeDtypeStruct((2, m, c), BF16),
            jax.ShapeDtypeStruct((2, bsz * nc, 1, c), F32),
            jax.ShapeDtypeStruct((m, c), BF16),
            jax.ShapeDtypeStruct((m, c), F32),
            jax.ShapeDtypeStruct((m, c), F32),
        ],
        compiler_params=_params("parallel", "parallel"),
        name="rwkv_prep",
    )(*([pz] * 12), mu_prev, mu_next, w0.reshape(2, 1, c), du, a0.reshape(2, 1, c), iu, gu,
      k_k.reshape(1, c), k_a.reshape(1, c), r_k_flat, ones_blk, tril, triu)
    return outs


def _scan_kernel(abf_ref, abb_ref, rbf_ref, rbb_ref, btf_ref, btb_ref, ktf_ref, ktb_ref,
                 vf_ref, vb_ref, plf_ref, plb_ref, yf_ref, yb_ref, m_scr, *, heads):
    @pl.when(pl.program_id(1) == 0)
    def _():
        m_scr[...] = jnp.zeros_like(m_scr)

    ab_ref = (abf_ref, abb_ref)
    rb_ref = (rbf_ref, rbb_ref)
    bt_ref = (btf_ref, btb_ref)
    kt_ref = (ktf_ref, ktb_ref)
    v_ref = (vf_ref, vb_ref)
    pl_ref = (plf_ref, plb_ref)
    y_ref = (yf_ref, yb_ref)
    n = CHUNK
    row = lax.broadcasted_iota(jnp.int32, (2 * n, 2 * n), 0)
    col = lax.broadcasted_iota(jnp.int32, (2 * n, 2 * n), 1)
    top = row < n
    tr = row % n
    tc = col % n
    eye = (lax.broadcasted_iota(jnp.int32, (n, n), 0)
           == lax.broadcasted_iota(jnp.int32, (n, n), 1)).astype(F32)
    keep = ((tr > tc) | (~top & (tr == tc)), (tr < tc) | (~top & (tr == tc)))
    probs = [(d, hh) for d in range(2) for hh in range(heads)]
    sl = lambda hh: slice(hh * n, (hh + 1) * n)
    ab = [ab_ref[d][0, :, sl(hh)] for d, hh in probs]
    rb = [rb_ref[d][0, :, sl(hh)] for d, hh in probs]
    bt = [bt_ref[d][0, :, sl(hh)] for d, hh in probs]
    kt = [kt_ref[d][0, :, sl(hh)] for d, hh in probs]
    vv = [v_ref[d][:, sl(hh)] for d, hh in probs]
    p_last = [pl_ref[d][0, 0, :, sl(hh)] for d, hh in probs]
    idx = range(len(probs))
    g1 = [_dot_nt(jnp.concatenate([ab[i], rb[i]], axis=0), jnp.concatenate([bt[i], kt[i]], axis=0))
          for i in idx]
    g1 = [jnp.where(keep[probs[i][0]], g1[i], 0.0) for i in idx]
    a_ab = [g[:n, :n] for g in g1]
    low = [g[n:, :].astype(BF16) for g in g1]
    tm = [eye + a for a in a_ab]
    pw = [_dot(a.astype(BF16), a.astype(BF16)) for a in a_ab]
    span = 2
    while span < n:
        last = span * 2 >= n
        nxt = []
        for i in idx:
            pwb = pw[i].astype(BF16)
            if last:
                nxt.append((tm[i] + _dot(tm[i].astype(BF16), pwb), None))
            else:
                both = _dot(jnp.concatenate([tm[i], pw[i]], axis=0).astype(BF16), pwb)
                nxt.append((tm[i] + both[:n], both[n:]))
        tm = [t for t, _ in nxt]
        pw = [p for _, p in nxt]
        span *= 2
    wc = [_dot(tm[i].astype(BF16), jnp.concatenate([ab[i], g1[i][:n, n:].astype(BF16)], axis=1)) for i in idx]
    w = [x[:, :n].astype(BF16) for x in wc]
    cuv = [_dot(wc[i][:, n:].astype(BF16), vv[i]).astype(BF16) for i in idx]
    rhs = [jnp.concatenate([cuv[i], vv[i]], axis=0) for i in idx]
    q_m = [rb[i].astype(F32) + _dot(g1[i][n:, :n].astype(BF16), w[i]) for i in idx]
    y_0 = [_dot(low[i], rhs[i]) for i in idx]
    bh_t = [(bt[i].astype(F32) * p_last[i]).T for i in idx]
    kh_t = [(kt[i].astype(F32) * p_last[i]).T for i in idx]
    g_m = [eye * p_last[i] + _dot(bh_t[i].astype(BF16), w[i]) for i in idx]
    h_m = [_dot(jnp.concatenate([bh_t[i], kh_t[i]], axis=1).astype(BF16), rhs[i]) for i in idx]
    upd = [_dot(jnp.concatenate([q_m[i], g_m[i]], axis=0).astype(BF16), m_scr[d, hh].astype(BF16))
           for i, (d, hh) in enumerate(probs)]
    for i, (d, hh) in enumerate(probs):
        y_ref[d][:, sl(hh)] = upd[i][:n] + y_0[i]
        m_scr[d, hh] = upd[i][n:] + h_m[i]


def rwkv_scan(ab, rb, bt, kt, vb, p_last, bsz, seq):
    nc = seq // CHUNK
    nh = RWKV_HEADS
    n = CHUNK
    c_w = RWKV_WIDTH
    fwd = lambda b, c: b * nc + c
    bwd = lambda b, c: b * nc + nc - 1 - c
    dir_f = pl.BlockSpec((1, n, c_w), lambda b, c: (0, fwd(b, c), 0))
    dir_b = pl.BlockSpec((1, n, c_w), lambda b, c: (1, bwd(b, c), 0))
    tok_f = pl.BlockSpec((n, c_w), lambda b, c: (fwd(b, c), 0))
    tok_b = pl.BlockSpec((n, c_w), lambda b, c: (bwd(b, c), 0))
    pl_f = pl.BlockSpec((1, 1, 1, c_w), lambda b, c: (0, fwd(b, c), 0, 0))
    pl_b = pl.BlockSpec((1, 1, 1, c_w), lambda b, c: (1, bwd(b, c), 0, 0))
    y_shape = jax.ShapeDtypeStruct((bsz * seq, c_w), F32)
    return pl.pallas_call(
        functools.partial(_scan_kernel, heads=nh),
        grid=(bsz, nc),
        in_specs=[dir_f, dir_b, dir_f, dir_b, dir_f, dir_b, dir_f, dir_b, tok_f, tok_b, pl_f, pl_b],
        out_specs=[tok_f, tok_b],
        out_shape=[y_shape, y_shape],
        scratch_shapes=[pltpu.VMEM((2, nh, n, n), F32)],
        compiler_params=_params("parallel", "arbitrary"),
        name="rwkv_scan",
    )(ab, ab, rb, rb, bt, bt, kt, kt, vb, vb, p_last, p_last)


def _post_kernel(yf_ref, yb_ref, bonus_ref, g_ref, lw_ref, lb_ref, o_ref):
    y = yf_ref[...] + yb_ref[...]
    n = RWKV_HEAD
    parts = []
    for h in range(RWKV_HEADS):
        yh = y[:, h * n:(h + 1) * n]
        mu = jnp.mean(yh, axis=-1, keepdims=True)
        yc = yh - mu
        var = jnp.mean(yc * yc, axis=-1, keepdims=True)
        parts.append(yc * lax.rsqrt(var + GN_EPS))
    yn = jnp.concatenate(parts, axis=1)
    o_ref[...] = ((yn * lw_ref[...] + lb_ref[...] + bonus_ref[...]) * g_ref[...]).astype(o_ref.dtype)


def rwkv_post(y_fwd, y_bwd, bonus, g, ln_w, ln_b, tt=256):
    m = y_fwd.shape[0]
    c = RWKV_WIDTH
    tok = pl.BlockSpec((tt, c), lambda i: (i, 0))
    row = pl.BlockSpec((1, c), lambda i: (0, 0))
    return pl.pallas_call(
        _post_kernel,
        grid=(m // tt,),
        in_specs=[tok, tok, tok, tok, row, row],
        out_specs=tok,
        out_shape=jax.ShapeDtypeStruct((m, c), BF16),
        compiler_params=_params("parallel"),
        name="rwkv_post",
    )(y_fwd, y_bwd, bonus, g, ln_w.reshape(1, c), ln_b.reshape(1, c))


def _out_proj_kernel(x_ref, a_ref, r_ref, wa_ref, wr_ref, o_ref):
    o_ref[...] = x_ref[...] + _dot(a_ref[...], wa_ref[...]) + _dot(r_ref[...], wr_ref[...])


def out_proj(x, att, rw, w_att, w_rw, tm=2048, tn=512):
    m, n = x.shape
    return pl.pallas_call(
        _out_proj_kernel,
        grid=(m // tm, n // tn),
        in_specs=[
            pl.BlockSpec((tm, tn), lambda i, j: (i, j)),
            pl.BlockSpec((tm, ATT_WIDTH), lambda i, j: (i, 0)),
            pl.BlockSpec((tm, RWKV_WIDTH), lambda i, j: (i, 0)),
            pl.BlockSpec((ATT_WIDTH, tn), lambda i, j: (0, j)),
            pl.BlockSpec((RWKV_WIDTH, tn), lambda i, j: (0, j)),
        ],
        out_specs=pl.BlockSpec((tm, tn), lambda i, j: (i, j)),
        out_shape=jax.ShapeDtypeStruct((m, n), F32),
        compiler_params=_params("parallel", "parallel"),
        name="out_proj",
    )(x, att, rw, w_att, w_rw)


def _router_kernel(x_ref, nw_ref, wh_ref, wl_ref, hb_ref, afft_ref):
    x = x_ref[...]
    ms = jnp.mean(x * x, axis=-1, keepdims=True)
    h = x * lax.rsqrt(ms + NORM_EPS) * nw_ref[...]
    hb_ref[...] = h.astype(BF16)
    hi, lo = _split2(h)
    logits = _dot(hi, wh_ref[...]) + _dot(lo, wh_ref[...]) + _dot(hi, wl_ref[...])
    lane = lax.broadcasted_iota(jnp.int32, logits.shape, 1)
    valid = lane < N_EXPERTS
    logits = jnp.where(valid, logits, -1e30)
    e = jnp.where(valid, jnp.exp(logits - jnp.max(logits, axis=-1, keepdims=True)), 0.0)
    aff = e / jnp.sum(e, axis=-1, keepdims=True)
    afft_ref[...] = aff.T[:N_EXPERTS, :]


def router(x1, norm_w, w_router, tm=512):
    m, k = x1.shape
    wpad = jnp.zeros((k, LANES), F32).at[:, :N_EXPERTS].set(w_router)
    wh, wl = _split2(wpad)
    return pl.pallas_call(
        _router_kernel,
        grid=(m // tm,),
        in_specs=[
            pl.BlockSpec((tm, k), lambda i: (i, 0)),
            pl.BlockSpec((1, k), lambda i: (0, 0)),
            pl.BlockSpec((k, LANES), lambda i: (0, 0)),
            pl.BlockSpec((k, LANES), lambda i: (0, 0)),
        ],
        out_specs=[
            pl.BlockSpec((tm, k), lambda i: (i, 0)),
            pl.BlockSpec((N_EXPERTS, tm), lambda i: (0, i)),
        ],
        out_shape=[
            jax.ShapeDtypeStruct((m, k), BF16),
            jax.ShapeDtypeStruct((N_EXPERTS, m), F32),
        ],
        compiler_params=_params("parallel"),
        name="router",
    )(x1, norm_w.reshape(1, k), wh, wl)


def _prefix_count(x, upper):
    outs = []
    carry = jnp.zeros((x.shape[0], 1), F32)
    for j in range(x.shape[1] // LANES):
        xt = x[:, j * LANES:(j + 1) * LANES]
        outs.append(_dot(xt.astype(BF16), upper) + carry)
        carry = carry + jnp.sum(xt, axis=-1, keepdims=True)
    return jnp.concatenate(outs, axis=1)


BISECT_STEPS = 152


def _topk_kernel(afft_ref, sp_ref, spt_ref, *, cap, bsz):
    ne = afft_ref.shape[0]
    seq = afft_ref.shape[1] // bsz
    a = [afft_ref[:, b * seq:(b + 1) * seq] for b in range(bsz)]

    def halve(_, bracket):
        out = []
        for b in range(bsz):
            lo, hi = bracket[b]
            mid = (lo + hi) * 0.5
            cnt = jnp.sum((a[b] >= mid).astype(F32), axis=-1, keepdims=True)
            enough = cnt >= cap
            out.append((jnp.where(enough, mid, lo), jnp.where(enough, hi, mid)))
        return tuple(out)

    start = tuple((jnp.zeros((ne, 1), F32), jnp.full((ne, 1), 2.0, F32)) for _ in range(bsz))
    bracket = lax.fori_loop(0, BISECT_STEPS, halve, start)
    r_i = lax.broadcasted_iota(jnp.int32, (LANES, LANES), 0)
    c_i = lax.broadcasted_iota(jnp.int32, (LANES, LANES), 1)
    upper = (r_i < c_i).astype(BF16)
    for b in range(bsz):
        lo, hi = bracket[b]
        above = (a[b] >= hi).astype(F32)
        tied = ((a[b] >= lo) & (a[b] < hi)).astype(F32)
        need = cap - jnp.sum(above, axis=-1, keepdims=True)
        sel = above + tied * (_prefix_count(tied, upper) < need).astype(F32)
        spt = jnp.where(sel > 0.5, _prefix_count(sel, upper), -1.0)
        spt_ref[:, b * seq:(b + 1) * seq] = spt
        full = jnp.concatenate([spt, jnp.full((LANES - ne, seq), -1.0, F32)], axis=0)
        sp_ref[b * seq:(b + 1) * seq, :] = full.T


def topk_select(afft, bsz, seq, cap):
    return pl.pallas_call(
        functools.partial(_topk_kernel, cap=cap, bsz=bsz),
        grid=(1,),
        in_specs=[pl.BlockSpec((N_EXPERTS, bsz * seq), lambda i: (0, 0))],
        out_specs=[
            pl.BlockSpec((bsz * seq, LANES), lambda i: (0, 0)),
            pl.BlockSpec((N_EXPERTS, bsz * seq), lambda i: (0, 0)),
        ],
        out_shape=[
            jax.ShapeDtypeStruct((bsz * seq, LANES), F32),
            jax.ShapeDtypeStruct((N_EXPERTS, bsz * seq), F32),
        ],
        compiler_params=_params("arbitrary"),
        name="topk_select",
    )(afft)


def _gather_kernel(spt_ref, afft_ref, h_ref, xe_ref, gate_ref, *, cap):
    e = pl.program_id(1)
    seq = h_ref.shape[0]
    sp = spt_ref[pl.ds(e, 1), :]
    slot = lax.broadcasted_iota(jnp.int32, (cap, seq), 0).astype(F32)
    hit = sp == slot
    xe_ref[0] = _dot(hit.astype(BF16), h_ref[...]).astype(BF16)
    gate_ref[0] = jnp.sum(jnp.where(hit, afft_ref[pl.ds(e, 1), :], 0.0), axis=-1, keepdims=True)


def moe_gather(spt, afft, hb, bsz, seq, cap):
    d = hb.shape[1]
    return pl.pallas_call(
        functools.partial(_gather_kernel, cap=cap),
        grid=(bsz, N_EXPERTS),
        in_specs=[
            pl.BlockSpec((N_EXPERTS, seq), lambda b, e: (0, b)),
            pl.BlockSpec((N_EXPERTS, seq), lambda b, e: (0, b)),
            pl.BlockSpec((seq, d), lambda b, e: (b, 0)),
        ],
        out_specs=[
            pl.BlockSpec((1, cap, d), lambda b, e: (e, b, 0)),
            pl.BlockSpec((1, cap, 1), lambda b, e: (e, b, 0)),
        ],
        out_shape=[
            jax.ShapeDtypeStruct((N_EXPERTS, bsz * cap, d), BF16),
            jax.ShapeDtypeStruct((N_EXPERTS, bsz * cap, 1), F32),
        ],
        compiler_params=_params("parallel", "arbitrary"),
        name="moe_gather",
    )(spt, afft, hb)


def _expert_kernel(xe_ref, gt_ref, wg_ref, wu_ref, wd_ref, ye_ref, acc_scr):
    f = pl.program_id(1)

    @pl.when(f == 0)
    def _():
        acc_scr[...] = jnp.zeros_like(acc_scr)

    xe = xe_ref[0]
    gate = _dot(xe, wg_ref[0].astype(BF16))
    up = _dot(xe, wu_ref[0].astype(BF16))
    hid = (gate * jax.nn.sigmoid(gate) * up).astype(BF16)
    acc_scr[...] += _dot(hid, wd_ref[0].astype(BF16))

    @pl.when(f == pl.num_programs(1) - 1)
    def _():
        ye_ref[0] = (acc_scr[...] * gt_ref[0]).astype(ye_ref.dtype)


def moe_experts(xe, gates, e_gate, e_up, e_down, tf=256):
    ne, rows, d = xe.shape
    ff = e_gate.shape[2]
    return pl.pallas_call(
        _expert_kernel,
        grid=(ne, ff // tf),
        in_specs=[
            pl.BlockSpec((1, rows, d), lambda e, f: (e, 0, 0)),
            pl.BlockSpec((1, rows, 1), lambda e, f: (e, 0, 0)),
            pl.BlockSpec((1, d, tf), lambda e, f: (e, 0, f)),
            pl.BlockSpec((1, d, tf), lambda e, f: (e, 0, f)),
            pl.BlockSpec((1, tf, d), lambda e, f: (e, f, 0)),
        ],
        out_specs=pl.BlockSpec((1, rows, d), lambda e, f: (e, 0, 0)),
        out_shape=jax.ShapeDtypeStruct((ne, rows, d), BF16),
        scratch_shapes=[pltpu.VMEM((rows, d), F32)],
        compiler_params=_params("parallel", "arbitrary"),
        name="moe_experts",
    )(xe, gates, e_gate, e_up, e_down)


def _combine_kernel(x_ref, sp_ref, ye_ref, nw_ref, o_ref, acc_scr, *, cap, group):
    eg = pl.program_id(2)

    @pl.when(eg == 0)
    def _():
        acc_scr[...] = x_ref[...]

    tt = x_ref.shape[0]
    lane = lax.broadcasted_iota(jnp.int32, (tt, LANES), 1)
    slot = lax.broadcasted_iota(jnp.int32, (tt, cap), 1).astype(F32)
    sp_all = sp_ref[...]
    hits = []
    for k in range(group):
        sp = jnp.sum(jnp.where(lane == eg * group + k, sp_all, 0.0), axis=-1, keepdims=True)
        hits.append((sp == slot).astype(BF16))
    onehot = jnp.concatenate(hits, axis=1)
    acc_scr[...] += _dot(onehot, ye_ref[...].reshape(group * cap, ye_ref.shape[2]))

    @pl.when(eg == pl.num_programs(2) - 1)
    def _():
        y = acc_scr[...]
        ms = jnp.mean(y * y, axis=-1, keepdims=True)
        o_ref[...] = y * lax.rsqrt(ms + NORM_EPS) * nw_ref[...]


def moe_combine(x1, sp, ye, norm_w, bsz, seq, cap, tt=512, group=4):
    m, d = x1.shape
    nt = seq // tt
    return pl.pallas_call(
        functools.partial(_combine_kernel, cap=cap, group=group),
        grid=(bsz, nt, N_EXPERTS // group),
        in_specs=[
            pl.BlockSpec((tt, d), lambda b, i, e: (b * nt + i, 0)),
            pl.BlockSpec((tt, LANES), lambda b, i, e: (b * nt + i, 0)),
            pl.BlockSpec((group, cap, d), lambda b, i, e: (e, b, 0)),
            pl.BlockSpec((1, d), lambda b, i, e: (0, 0)),
        ],
        out_specs=pl.BlockSpec((tt, d), lambda b, i, e: (b * nt + i, 0)),
        out_shape=jax.ShapeDtypeStruct((m, d), F32),
        scratch_shapes=[pltpu.VMEM((tt, d), F32)],
        compiler_params=_params("parallel", "parallel", "arbitrary"),
        name="moe_combine",
    )(x1, sp, ye, norm_w.reshape(1, d))


def _pad_cols(w, width):
    return jnp.pad(w, [(0, 0)] * (w.ndim - 1) + [(0, width - w.shape[-1])])


def rwkv_mixer(pz, mu_prev, mu_next, w0, decay_up, a0, iclr_up, gate_up, k_k, k_a, r_k, ln_x_w, ln_x_b,
               bsz, seq):
    ab, rb, bt, kt, p_last, vb, bonus, g = rwkv_prep(
        pz, mu_prev, mu_next, w0, decay_up, a0, iclr_up, gate_up, k_k, k_a, r_k, bsz, seq)
    y_fwd, y_bwd = rwkv_scan(ab, rb, bt, kt, vb, p_last, bsz, seq)
    return rwkv_post(y_fwd, y_bwd, bonus, g, ln_x_w, ln_x_b)


def moe_block(x1, norm2_w, w_router, e_gate, e_up, e_down, norm_f_w, bsz, seq):
    cap = CAPACITY_FACTOR * seq // N_EXPERTS
    hb, afft = router(x1, norm2_w, w_router)
    sp, spt = topk_select(afft, bsz, seq, cap)
    xe, gates = moe_gather(spt, afft, hb, bsz, seq, cap)
    ye = moe_experts(xe, gates, e_gate, e_up, e_down)
    return moe_combine(x1, sp, ye, norm_f_w, bsz, seq, cap)


def kernel(x, positions, norm1_w, w_in, mu_prev, mu_next, lambda_q1, lambda_k1, lambda_q2, lambda_k2, subln_w, w0, decay_up, a0, iclr_up, gate_up, k_k, k_a, r_k, ln_x_w, ln_x_b, w_out, norm2_w, w_router, e_gate, e_up, e_down, norm_f_w):
    bsz, seq, d = x.shape
    m = bsz * seq
    xf = x.reshape(m, d)
    pos = positions.reshape(m, 1)
    lambda_init = 0.8 - 0.6 * math.exp(-0.3 * 0)

    w_att = w_in[0][:, :ATT_COLS].astype(BF16)
    w_z = _pad_cols(w_in[0][:, ATT_COLS:], SHIFT_PAD).astype(BF16)
    h = rms_norm_bf16(xf, norm1_w[0])
    qkv = matmul_rope(h, w_att, pos)
    p_z = matmul(h, w_z, F32)

    att = attention(qkv, lambda_q1[0], lambda_k1[0], lambda_q2[0], lambda_k2[0],
                    subln_w[0], bsz, seq, lambda_init)

    rw = rwkv_mixer(p_z, _pad_cols(mu_prev, SHIFT_PAD), _pad_cols(mu_next, SHIFT_PAD),
                    w0[0], decay_up[0], a0[0], iclr_up[0], gate_up[0], k_k[0], k_a[0], r_k[0],
                    ln_x_w[0], ln_x_b[0], bsz, seq)

    wo = w_out[0].astype(BF16)
    x1 = out_proj(xf, att, rw, wo[:ATT_WIDTH], wo[ATT_WIDTH:])

    out = moe_block(x1, norm2_w[0], w_router[0], e_gate[0], e_up[0], e_down[0], norm_f_w, bsz, seq)
    return out.reshape(bsz, seq, d)
```

```python
import functools
import math

import jax
import jax.numpy as jnp
from jax import lax
from jax.experimental import pallas as pl
from jax.experimental.pallas import tpu as pltpu

F32 = jnp.float32
BF16 = jnp.bfloat16

LANES = 128
VMEM_LIMIT_BYTES = 56 * 1024 * 1024

D_MODEL = 2048
NORM_EPS = 1e-6
ATT_HEADS = 8
ATT_QK_DIM = 64
ATT_V_DIM = 128
ATT_WIDTH = ATT_HEADS * ATT_V_DIM
ATT_QK_COLS = ATT_HEADS * 2 * ATT_QK_DIM
ATT_COLS = 2 * ATT_QK_COLS + ATT_WIDTH
ROPE_THETA = 500000.0
ROPE_DIM = ATT_QK_DIM // 4
SUBLN_EPS = 1e-5
RWKV_WIDTH = 1024
RWKV_HEAD = 64
RWKV_HEADS = RWKV_WIDTH // RWKV_HEAD
DECAY_LORA = 64
ICLR_LORA = 64
GATE_LORA = 160
LORA_COLS = DECAY_LORA + ICLR_LORA + GATE_LORA
LORA_PAD = 384
GN_EPS = 64e-5
SHIFT_WIDTH = 3 * RWKV_WIDTH + LORA_COLS
SHIFT_PAD = 3 * RWKV_WIDTH + 512
N_EXPERTS = 16
CAPACITY_FACTOR = 2
EXPERT_FF = 2048
CHUNK = 64


def _params(*sem):
    return pltpu.CompilerParams(dimension_semantics=sem, vmem_limit_bytes=VMEM_LIMIT_BYTES)


def _dot(a, b):
    return jnp.dot(a, b, preferred_element_type=F32)


def _dot_nt(a, b):
    return lax.dot_general(a, b, (((1,), (1,)), ((), ())), preferred_element_type=F32)


def _split2(x):
    hi = x.astype(BF16)
    lo = (x - hi.astype(F32)).astype(BF16)
    return hi, lo


def _dot_lhs2(x, m_bf16):
    hi, lo = _split2(x)
    return _dot(hi, m_bf16) + _dot(lo, m_bf16)


def _rms_norm_kernel(x_ref, nw_ref, o_ref):
    x = x_ref[...]
    ms = jnp.mean(x * x, axis=-1, keepdims=True)
    o_ref[...] = (x * lax.rsqrt(ms + NORM_EPS) * nw_ref[...]).astype(o_ref.dtype)


def rms_norm_bf16(x, nw, tm=512):
    m, k = x.shape
    return pl.pallas_call(
        _rms_norm_kernel,
        grid=(m // tm,),
        in_specs=[pl.BlockSpec((tm, k), lambda i: (i, 0)), pl.BlockSpec((1, k), lambda i: (0, 0))],
        out_specs=pl.BlockSpec((tm, k), lambda i: (i, 0)),
        out_shape=jax.ShapeDtypeStruct((m, k), BF16),
        compiler_params=_params("parallel"),
        name="rms_norm",
    )(x, nw.reshape(1, k))


def _mm_kernel(h_ref, w_ref, o_ref):
    o_ref[...] = _dot(h_ref[...], w_ref[...]).astype(o_ref.dtype)


def matmul(h, w_bf16, col0, n, out_dtype, tm=2048, tn=512):
    m, k = h.shape
    j0 = col0 // tn
    return pl.pallas_call(
        _mm_kernel,
        grid=(m // tm, n // tn),
        in_specs=[
            pl.BlockSpec((tm, k), lambda i, j: (i, 0)),
            pl.BlockSpec((k, tn), lambda i, j: (0, j0 + j)),
        ],
        out_specs=pl.BlockSpec((tm, tn), lambda i, j: (i, j)),
        out_shape=jax.ShapeDtypeStruct((m, n), out_dtype),
        compiler_params=_params("parallel", "arbitrary"),
        name="matmul",
    )(h, w_bf16)


def _mm_rope_kernel(h_ref, w_ref, pos_ref, freq_ref, slo_ref, shi_ref, o_ref,
                    c_scr, lo_scr, hi_scr, *, rope_tiles, q_tiles):
    j = pl.program_id(1)

    @pl.when(j == 0)
    def _():
        ang = pos_ref[...].astype(F32) * freq_ref[...]
        s = jnp.sin(ang)
        c_scr[...] = jnp.cos(ang)
        lo_scr[...] = s * slo_ref[...]
        hi_scr[...] = s * shi_ref[...]

    acc = _dot(h_ref[...], w_ref[...])

    @pl.when(j < rope_tiles)
    def _():
        scale = jnp.where(j < q_tiles, ATT_QK_DIM ** -0.5 * math.log2(math.e), 1.0)
        half = ROPE_DIM // 2
        for g in range(acc.shape[1] // LANES):
            x = acc[:, g * LANES:(g + 1) * LANES]
            x_dn = pltpu.roll(x, half, axis=1)
            x_up = pltpu.roll(x, LANES - half, axis=1)
            y = x * c_scr[...] + x_dn * hi_scr[...] + x_up * lo_scr[...]
            o_ref[:, g * LANES:(g + 1) * LANES] = (y * scale).astype(o_ref.dtype)

    @pl.when(j >= rope_tiles)
    def _():
        o_ref[...] = acc.astype(o_ref.dtype)


def matmul_rope(h, w_bf16, pos, tm=2048, tn=512):
    m, k = h.shape
    n = ATT_COLS
    lane = jnp.arange(LANES) % ATT_QK_DIM
    half = ROPE_DIM // 2
    inv_freq = ROPE_THETA ** (-jnp.arange(0, ROPE_DIM, 2, dtype=F32) / ROPE_DIM)
    freq = jnp.where(lane < ROPE_DIM, inv_freq[lane % half], 0.0).astype(F32).reshape(1, LANES)
    sgn_lo = jnp.where(lane < half, -1.0, 0.0).astype(F32).reshape(1, LANES)
    sgn_hi = jnp.where((lane >= half) & (lane < ROPE_DIM), 1.0, 0.0).astype(F32).reshape(1, LANES)
    vec = pl.BlockSpec((1, LANES), lambda i, j: (0, 0))
    return pl.pallas_call(
        functools.partial(_mm_rope_kernel, rope_tiles=2 * ATT_QK_COLS // tn, q_tiles=ATT_QK_COLS // tn),
        grid=(m // tm, n // tn),
        in_specs=[
            pl.BlockSpec((tm, k), lambda i, j: (i, 0)),
            pl.BlockSpec((k, tn), lambda i, j: (0, j)),
            pl.BlockSpec((tm, 1), lambda i, j: (i, 0)),
            vec, vec, vec,
        ],
        out_specs=pl.BlockSpec((tm, tn), lambda i, j: (i, j)),
        out_shape=jax.ShapeDtypeStruct((m, n), BF16),
        scratch_shapes=[pltpu.VMEM((tm, LANES), F32), pltpu.VMEM((tm, LANES), F32),
                        pltpu.VMEM((tm, LANES), F32)],
        compiler_params=_params("parallel", "arbitrary"),
        name="matmul_rope",
    )(h, w_bf16, pos, freq, sgn_lo, sgn_hi)


def _attn_kernel(q_ref, k_ref, v_ref, lq1_ref, lk1_ref, lq2_ref, lk2_ref, sw_ref, o_ref, v_scr, *,
                 lambda_init):
    dv = ATT_V_DIM

    @pl.when(pl.program_id(2) == 0)
    def _():
        lane = lax.broadcasted_iota(jnp.int32, (v_scr.shape[0], dv), 1)
        v_scr[:, :dv] = v_ref[...]
        v_scr[:, dv:] = (lane == 0).astype(BF16)

    q = q_ref[...]
    kr = k_ref[...]
    s1 = _dot_nt(q[:, :ATT_QK_DIM], kr[:, :ATT_QK_DIM])
    s2 = _dot_nt(q[:, ATT_QK_DIM:], kr[:, ATT_QK_DIM:])
    e1 = jnp.exp2(s1 - jnp.max(s1, axis=-1, keepdims=True)).astype(BF16)
    e2 = jnp.exp2(s2 - jnp.max(s2, axis=-1, keepdims=True)).astype(BF16)
    lam = (jnp.exp(jnp.sum(lq1_ref[...] * lk1_ref[...], axis=-1, keepdims=True))
           - jnp.exp(jnp.sum(lq2_ref[...] * lk2_ref[...], axis=-1, keepdims=True)) + lambda_init)
    va = v_scr[...]
    o1 = _dot(e1, va)
    o2 = _dot(e2, va)
    o = o1[:, :dv] / o1[:, dv:dv + 1] - o2[:, :dv] * (lam / o2[:, dv:dv + 1])
    ms = jnp.mean(o * o, axis=-1, keepdims=True)
    o = o * lax.rsqrt(ms + SUBLN_EPS) * sw_ref[...] * (1.0 - lambda_init)
    o_ref[...] = o.astype(o_ref.dtype)


def attention(qkv, lq1, lk1, lq2, lk2, subln_w, bsz, seq, lambda_init, tq=256):
    nq = seq // tq
    vec = lambda n: pl.BlockSpec((1, n), lambda b, h, i: (0, 0))
    nh = ATT_HEADS
    return pl.pallas_call(
        functools.partial(_attn_kernel, lambda_init=lambda_init),
        grid=(bsz, nh, nq),
        in_specs=[
            pl.BlockSpec((tq, LANES), lambda b, h, i: (b * nq + i, h)),
            pl.BlockSpec((seq, LANES), lambda b, h, i: (b, nh + h)),
            pl.BlockSpec((seq, LANES), lambda b, h, i: (b, 2 * nh + h)),
            vec(ATT_QK_DIM), vec(ATT_QK_DIM), vec(ATT_QK_DIM), vec(ATT_QK_DIM),
            vec(ATT_V_DIM),
        ],
        out_specs=pl.BlockSpec((tq, LANES), lambda b, h, i: (b * nq + i, h)),
        out_shape=jax.ShapeDtypeStruct((bsz * seq, ATT_WIDTH), BF16),
        scratch_shapes=[pltpu.VMEM((seq, 2 * ATT_V_DIM), BF16)],
        compiler_params=_params("parallel", "parallel", "arbitrary"),
        name="diff_attention",
    )(qkv, qkv, qkv,
      lq1.reshape(1, -1), lk1.reshape(1, -1), lq2.reshape(1, -1), lk2.reshape(1, -1),
      subln_w.reshape(1, -1))


HALO = 8


def _token_shift(z_ref, prev_ref, next_ref, mu_p, mu_n, first, last):
    z = z_ref[...]
    n = z.shape[0]
    row = lax.broadcasted_iota(jnp.int32, z.shape, 0)
    before = jnp.where(first, 0.0, prev_ref[HALO - 1:HALO, :])
    after = jnp.where(last, 0.0, next_ref[0:1, :])
    zp = jnp.where(row == 0, before, pltpu.roll(z, 1, axis=0))
    zn = jnp.where(row == n - 1, after, pltpu.roll(z, n - 1, axis=0))
    return z + mu_p * (zp - z) + mu_n * (zn - z)


def _seg_sum(x, ones_blk):
    parts = []
    for j in range(x.shape[1] // LANES):
        parts.append(_dot_lhs2(x[:, j * LANES:(j + 1) * LANES], ones_blk))
    return jnp.concatenate(parts, axis=1)


def _prep_kernel(r_ref, k_ref, v_ref, lo_ref, rp_ref, kp_ref, vp_ref, lp_ref, rn_ref, kn_ref, vn_ref,
                 ln_ref, mup_ref, mun_ref, w0_ref, du_ref, a0_ref, iu_ref, gu_ref, kk_ref,
                 ka_ref, rk_ref, ones_ref, tril_ref, triu_ref,
                 ab_ref, rb_ref, bt_ref, kt_ref, pl_ref, vb_ref, bonus_ref, g_ref, *, tt):
    first = pl.program_id(1) == 0
    last = pl.program_id(1) == pl.num_programs(1) - 1
    c = RWKV_WIDTH
    shift = lambda z, p, n, lo_col, hi_col: _token_shift(
        z, p, n, mup_ref[:, lo_col:hi_col], mun_ref[:, lo_col:hi_col], first, last)
    r = shift(r_ref, rp_ref, rn_ref, 0, c)
    k = shift(k_ref, kp_ref, kn_ref, c, 2 * c)
    v = shift(v_ref, vp_ref, vn_ref, 2 * c, 3 * c)
    lo = shift(lo_ref, lp_ref, ln_ref, 3 * c, 3 * c + LORA_PAD)
    ones_blk = ones_ref[...]
    vb_ref[...] = v.astype(BF16)
    g_ref[...] = _dot(jax.nn.sigmoid(lo).astype(BF16), gu_ref[...])
    kk = k * kk_ref[...]
    nrm = jnp.sqrt(_seg_sum(kk * kk, ones_blk))
    kk = kk / jnp.maximum(nrm, 1e-12)
    th = jnp.tanh(lo).astype(BF16)
    lob = lo.astype(BF16)
    nchunk = tt // CHUNK
    ksum = jnp.zeros_like(k)
    for d in range(2):
        wl = w0_ref[d] + _dot(th, du_ref[d])
        lw = -math.exp(-0.5) * jax.nn.sigmoid(wl)
        a = jax.nn.sigmoid(a0_ref[d] + _dot(lob, iu_ref[d]))
        kd = k * (1.0 + (a - 1.0) * ka_ref[...])
        ksum = ksum + kd
        tri = tril_ref[...] if d == 0 else triu_ref[...]
        cum = _dot_lhs2_rhs(tri, lw)
        e_pos = jnp.exp(cum)
        e_neg = jnp.exp(-cum)
        ab_ref[d] = (-kk * jnp.exp(cum - lw)).astype(BF16)
        rb_ref[d] = (r * e_pos).astype(BF16)
        bt_ref[d] = (kk * a * e_neg).astype(BF16)
        kt_ref[d] = (kd * e_neg).astype(BF16)
        for c in range(nchunk):
            last = c * CHUNK + (CHUNK - 1 if d == 0 else 0)
            pl_ref[d, c] = e_pos[last:last + 1, :]
    bonus_ref[...] = _seg_sum(r * ksum * rk_ref[...], ones_blk) * v


def _dot_lhs2_rhs(tri_bf16, x):
    hi, lo = _split2(x)
    return _dot(tri_bf16, hi) + _dot(tri_bf16, lo)


def rwkv_prep(pz, mu_prev, mu_next, w0, decay_up, a0, iclr_up, gate_up, k_k, k_a, r_k, bsz, seq, tt=256):
    m = bsz * seq
    c = RWKV_WIDTH
    nt = seq // tt
    ncb = tt // CHUNK
    nc = seq // CHUNK

    def pad_rows(w, start):
        out = jnp.zeros(w.shape[:-2] + (LORA_PAD, c), F32)
        return lax.dynamic_update_slice_in_dim(out, w.astype(F32), start, axis=w.ndim - 2).astype(BF16)

    du = pad_rows(decay_up, 0)
    iu = pad_rows(iclr_up, DECAY_LORA)
    gu = pad_rows(gate_up, DECAY_LORA + ICLR_LORA)
    lane = jnp.arange(LANES)
    ones_blk = (lane[:, None] // RWKV_HEAD == lane[None, :] // RWKV_HEAD).astype(BF16)
    t = jnp.arange(tt)
    same = t[:, None] // CHUNK == t[None, :] // CHUNK
    tril = (same & (t[:, None] >= t[None, :])).astype(BF16)
    triu = (same & (t[:, None] <= t[None, :])).astype(BF16)
    r_k_flat = r_k.reshape(1, c)
    row = lambda: pl.BlockSpec((1, c), lambda b, i: (0, 0))
    full3 = lambda s: pl.BlockSpec(s, lambda b, i: (0, 0, 0))
    tok = lambda j, w: pl.BlockSpec((tt, w), lambda b, i: (b * nt + i, j))
    per = tt // HALO
    prev = lambda j, w: pl.BlockSpec((HALO, w), lambda b, i: (jnp.maximum((b * nt + i) * per - 1, 0), j))
    nxt = lambda j, w: pl.BlockSpec(
        (HALO, w), lambda b, i: (jnp.minimum((b * nt + i + 1) * per, m // HALO - 1), j))
    lora_j = 3 * c // LORA_PAD
    cols = [(0, c), (1, c), (2, c), (lora_j, LORA_PAD)]
    mu_row = pl.BlockSpec((1, pz.shape[1]), lambda b, i: (0, 0))
    dir_tok = pl.BlockSpec((2, tt, c), lambda b, i: (0, b * nt + i, 0))
    outs = pl.pallas_call(
        functools.partial(_prep_kernel, tt=tt),
        grid=(bsz, nt),
        in_specs=[tok(j, w) for j, w in cols] + [prev(j, w) for j, w in cols]
        + [nxt(j, w) for j, w in cols] + [
            mu_row, mu_row,
            full3((2, 1, c)), full3((2, LORA_PAD, c)), full3((2, 1, c)), full3((2, LORA_PAD, c)),
            pl.BlockSpec((LORA_PAD, c), lambda b, i: (0, 0)),
            row(), row(), row(),
            pl.BlockSpec((LANES, LANES), lambda b, i: (0, 0)),
            pl.BlockSpec((tt, tt), lambda b, i: (0, 0)),
            pl.BlockSpec((tt, tt), lambda b, i: (0, 0)),
        ],
        out_specs=[
            dir_tok, dir_tok, dir_tok, dir_tok,
            pl.BlockSpec((2, ncb, 1, c), lambda b, i: (0, b * nt + i, 0, 0)),
            pl.BlockSpec((tt, c), lambda b, i: (b * nt + i, 0)),
            pl.BlockSpec((tt, c), lambda b, i: (b * nt + i, 0)),
            pl.BlockSpec((tt, c), lambda b, i: (b * nt + i, 0)),
        ],
        out_shape=[
            jax.ShapeDtypeStruct((2, m, c), BF16),
            jax.ShapeDtypeStruct((2, m, c), BF16),
            jax.ShapeDtypeStruct((2, m, c), BF16),
            jax.ShapeDtypeStruct((2, m, c), BF16),
            jax.ShapeDtypeStruct((2, bsz * nc, 1, c), F32),
            jax.ShapeDtypeStruct((m, c), BF16),
            jax.ShapeDtypeStruct((m, c), F32),
            jax.ShapeDtypeStruct((m, c), F32),
        ],
        compiler_params=_params("parallel", "parallel"),
        name="rwkv_prep",
    )(*([pz] * 12), mu_prev, mu_next, w0.reshape(2, 1, c), du, a0.reshape(2, 1, c), iu, gu,
      k_k.reshape(1, c), k_a.reshape(1, c), r_k_flat, ones_blk, tril, triu)
    return outs


def _scan_kernel(abf_ref, abb_ref, rbf_ref, rbb_ref, btf_ref, btb_ref, ktf_ref, ktb_ref,
                 vf_ref, vb_ref, plf_ref, plb_ref, yf_ref, yb_ref, m_scr, *, heads):
    @pl.when(pl.program_id(1) == 0)
    def _():
        m_scr[...] = jnp.zeros_like(m_scr)

    ab_ref = (abf_ref, abb_ref)
    rb_ref = (rbf_ref, rbb_ref)
    bt_ref = (btf_ref, btb_ref)
    kt_ref = (ktf_ref, ktb_ref)
    v_ref = (vf_ref, vb_ref)
    pl_ref = (plf_ref, plb_ref)
    y_ref = (yf_ref, yb_ref)
    n = CHUNK
    row = lax.broadcasted_iota(jnp.int32, (2 * n, 2 * n), 0)
    col = lax.broadcasted_iota(jnp.int32, (2 * n, 2 * n), 1)
    top = row < n
    tr = row % n
    tc = col % n
    eye = (lax.broadcasted_iota(jnp.int32, (n, n), 0)
           == lax.broadcasted_iota(jnp.int32, (n, n), 1)).astype(F32)
    keep = ((tr > tc) | (~top & (tr == tc)), (tr < tc) | (~top & (tr == tc)))
    probs = [(d, hh) for d in range(2) for hh in range(heads)]
    sl = lambda hh: slice(hh * n, (hh + 1) * n)
    ab = [ab_ref[d][0, :, sl(hh)] for d, hh in probs]
    rb = [rb_ref[d][0, :, sl(hh)] for d, hh in probs]
    bt = [bt_ref[d][0, :, sl(hh)] for d, hh in probs]
    kt = [kt_ref[d][0, :, sl(hh)] for d, hh in probs]
    vv = [v_ref[d][:, sl(hh)] for d, hh in probs]
    p_last = [pl_ref[d][0, 0, :, sl(hh)] for d, hh in probs]
    idx = range(len(probs))
    g1 = [_dot_nt(jnp.concatenate([ab[i], rb[i]], axis=0), jnp.concatenate([bt[i], kt[i]], axis=0))
          for i in idx]
    g1 = [jnp.where(keep[probs[i][0]], g1[i], 0.0) for i in idx]
    a_ab = [g[:n, :n] for g in g1]
    low = [g[n:, :].astype(BF16) for g in g1]
    tm = [eye + a for a in a_ab]
    pw = [_dot(a.astype(BF16), a.astype(BF16)) for a in a_ab]
    span = 2
    while span < n:
        last = span * 2 >= n
        nxt = []
        for i in idx:
            pwb = pw[i].astype(BF16)
            if last:
                nxt.append((tm[i] + _dot(tm[i].astype(BF16), pwb), None))
            else:
                both = _dot(jnp.concatenate([tm[i], pw[i]], axis=0).astype(BF16), pwb)
                nxt.append((tm[i] + both[:n], both[n:]))
        tm = [t for t, _ in nxt]
        pw = [p for _, p in nxt]
        span *= 2
    wc = [_dot(tm[i].astype(BF16), jnp.concatenate([ab[i], g1[i][:n, n:].astype(BF16)], axis=1)) for i in idx]
    w = [x[:, :n].astype(BF16) for x in wc]
    cuv = [_dot(wc[i][:, n:].astype(BF16), vv[i]).astype(BF16) for i in idx]
    zero = jnp.zeros((n, n), BF16)
    rhs = [jnp.concatenate([jnp.concatenate([w[i], cuv[i]], axis=1),
                            jnp.concatenate([zero, vv[i]], axis=1)], axis=0) for i in idx]
    bh_t = [(bt[i].astype(F32) * p_last[i]).T for i in idx]
    kh_t = [(kt[i].astype(F32) * p_last[i]).T for i in idx]
    lhs = [jnp.concatenate([low[i], jnp.concatenate([bh_t[i], kh_t[i]], axis=1).astype(BF16)], axis=0)
           for i in idx]
    out = [_dot(lhs[i], rhs[i]) for i in idx]
    q_m = [rb[i].astype(F32) + out[i][:n, :n] for i in idx]
    g_m = [eye * p_last[i] + out[i][n:, :n] for i in idx]
    upd = [_dot(jnp.concatenate([q_m[i], g_m[i]], axis=0).astype(BF16), m_scr[d, hh].astype(BF16))
           for i, (d, hh) in enumerate(probs)]
    for i, (d, hh) in enumerate(probs):
        y_ref[d][:, sl(hh)] = upd[i][:n] + out[i][:n, n:]
        m_scr[d, hh] = upd[i][n:] + out[i][n:, n:]


def rwkv_scan(ab, rb, bt, kt, vb, p_last, bsz, seq):
    nc = seq // CHUNK
    nh = RWKV_HEADS
    n = CHUNK
    c_w = RWKV_WIDTH
    fwd = lambda b, c: b * nc + c
    bwd = lambda b, c: b * nc + nc - 1 - c
    dir_f = pl.BlockSpec((1, n, c_w), lambda b, c: (0, fwd(b, c), 0))
    dir_b = pl.BlockSpec((1, n, c_w), lambda b, c: (1, bwd(b, c), 0))
    tok_f = pl.BlockSpec((n, c_w), lambda b, c: (fwd(b, c), 0))
    tok_b = pl.BlockSpec((n, c_w), lambda b, c: (bwd(b, c), 0))
    pl_f = pl.BlockSpec((1, 1, 1, c_w), lambda b, c: (0, fwd(b, c), 0, 0))
    pl_b = pl.BlockSpec((1, 1, 1, c_w), lambda b, c: (1, bwd(b, c), 0, 0))
    y_shape = jax.ShapeDtypeStruct((bsz * seq, c_w), F32)
    return pl.pallas_call(
        functools.partial(_scan_kernel, heads=nh),
        grid=(bsz, nc),
        in_specs=[dir_f, dir_b, dir_f, dir_b, dir_f, dir_b, dir_f, dir_b, tok_f, tok_b, pl_f, pl_b],
        out_specs=[tok_f, tok_b],
        out_shape=[y_shape, y_shape],
        scratch_shapes=[pltpu.VMEM((2, nh, n, n), F32)],
        compiler_params=_params("parallel", "arbitrary"),
        name="rwkv_scan",
    )(ab, ab, rb, rb, bt, bt, kt, kt, vb, vb, p_last, p_last)


def _post_kernel(yf_ref, yb_ref, bonus_ref, g_ref, lw_ref, lb_ref, o_ref):
    y = yf_ref[...] + yb_ref[...]
    n = RWKV_HEAD
    parts = []
    for h in range(RWKV_HEADS):
        yh = y[:, h * n:(h + 1) * n]
        mu = jnp.mean(yh, axis=-1, keepdims=True)
        yc = yh - mu
        var = jnp.mean(yc * yc, axis=-1, keepdims=True)
        parts.append(yc * lax.rsqrt(var + GN_EPS))
    yn = jnp.concatenate(parts, axis=1)
    o_ref[...] = ((yn * lw_ref[...] + lb_ref[...] + bonus_ref[...]) * g_ref[...]).astype(o_ref.dtype)


def rwkv_post(y_fwd, y_bwd, bonus, g, ln_w, ln_b, tt=256):
    m = y_fwd.shape[0]
    c = RWKV_WIDTH
    tok = pl.BlockSpec((tt, c), lambda i: (i, 0))
    row = pl.BlockSpec((1, c), lambda i: (0, 0))
    return pl.pallas_call(
        _post_kernel,
        grid=(m // tt,),
        in_specs=[tok, tok, tok, tok, row, row],
        out_specs=tok,
        out_shape=jax.ShapeDtypeStruct((m, c), BF16),
        compiler_params=_params("parallel"),
        name="rwkv_post",
    )(y_fwd, y_bwd, bonus, g, ln_w.reshape(1, c), ln_b.reshape(1, c))


def _out_proj_kernel(x_ref, a_ref, r_ref, wa_ref, wr_ref, o_ref):
    o_ref[...] = x_ref[...] + _dot(a_ref[...], wa_ref[...]) + _dot(r_ref[...], wr_ref[...])


def out_proj(x, att, rw, w_att, w_rw, tm=2048, tn=512):
    m, n = x.shape
    return pl.pallas_call(
        _out_proj_kernel,
        grid=(m // tm, n // tn),
        in_specs=[
            pl.BlockSpec((tm, tn), lambda i, j: (i, j)),
            pl.BlockSpec((tm, ATT_WIDTH), lambda i, j: (i, 0)),
            pl.BlockSpec((tm, RWKV_WIDTH), lambda i, j: (i, 0)),
            pl.BlockSpec((ATT_WIDTH, tn), lambda i, j: (0, j)),
            pl.BlockSpec((RWKV_WIDTH, tn), lambda i, j: (0, j)),
        ],
        out_specs=pl.BlockSpec((tm, tn), lambda i, j: (i, j)),
        out_shape=jax.ShapeDtypeStruct((m, n), F32),
        compiler_params=_params("parallel", "parallel"),
        name="out_proj",
    )(x, att, rw, w_att, w_rw)


def _router_kernel(x_ref, nw_ref, wh_ref, wl_ref, hb_ref, afft_ref):
    x = x_ref[...]
    ms = jnp.mean(x * x, axis=-1, keepdims=True)
    h = x * lax.rsqrt(ms + NORM_EPS) * nw_ref[...]
    hb_ref[...] = h.astype(BF16)
    hi, lo = _split2(h)
    logits = _dot(hi, wh_ref[...]) + _dot(lo, wh_ref[...]) + _dot(hi, wl_ref[...])
    lane = lax.broadcasted_iota(jnp.int32, logits.shape, 1)
    valid = lane < N_EXPERTS
    logits = jnp.where(valid, logits, -1e30)
    e = jnp.where(valid, jnp.exp(logits - jnp.max(logits, axis=-1, keepdims=True)), 0.0)
    aff = e / jnp.sum(e, axis=-1, keepdims=True)
    afft_ref[...] = aff.T[:N_EXPERTS, :]


def router(x1, norm_w, w_router, tm=512):
    m, k = x1.shape
    wpad = jnp.zeros((k, LANES), F32).at[:, :N_EXPERTS].set(w_router)
    wh, wl = _split2(wpad)
    return pl.pallas_call(
        _router_kernel,
        grid=(m // tm,),
        in_specs=[
            pl.BlockSpec((tm, k), lambda i: (i, 0)),
            pl.BlockSpec((1, k), lambda i: (0, 0)),
            pl.BlockSpec((k, LANES), lambda i: (0, 0)),
            pl.BlockSpec((k, LANES), lambda i: (0, 0)),
        ],
        out_specs=[
            pl.BlockSpec((tm, k), lambda i: (i, 0)),
            pl.BlockSpec((N_EXPERTS, tm), lambda i: (0, i)),
        ],
        out_shape=[
            jax.ShapeDtypeStruct((m, k), BF16),
            jax.ShapeDtypeStruct((N_EXPERTS, m), F32),
        ],
        compiler_params=_params("parallel"),
        name="router",
    )(x1, norm_w.reshape(1, k), wh, wl)


def _prefix_count(x, upper):
    outs = []
    carry = jnp.zeros((x.shape[0], 1), F32)
    for j in range(x.shape[1] // LANES):
        xt = x[:, j * LANES:(j + 1) * LANES]
        outs.append(_dot(xt.astype(BF16), upper) + carry)
        carry = carry + jnp.sum(xt, axis=-1, keepdims=True)
    return jnp.concatenate(outs, axis=1)


BISECT_STEPS = 152


def _topk_kernel(afft_ref, sp_ref, spt_ref, *, cap, bsz):
    ne = afft_ref.shape[0]
    seq = afft_ref.shape[1] // bsz
    a = [afft_ref[:, b * seq:(b + 1) * seq] for b in range(bsz)]

    def halve(_, bracket):
        out = []
        for b in range(bsz):
            lo, hi = bracket[b]
            mid = (lo + hi) * 0.5
            cnt = jnp.sum((a[b] >= mid).astype(F32), axis=-1, keepdims=True)
            enough = cnt >= cap
            out.append((jnp.where(enough, mid, lo), jnp.where(enough, hi, mid)))
        return tuple(out)

    start = tuple((jnp.zeros((ne, 1), F32), jnp.full((ne, 1), 2.0, F32)) for _ in range(bsz))
    bracket = lax.fori_loop(0, BISECT_STEPS, halve, start)
    r_i = lax.broadcasted_iota(jnp.int32, (LANES, LANES), 0)
    c_i = lax.broadcasted_iota(jnp.int32, (LANES, LANES), 1)
    upper = (r_i < c_i).astype(BF16)
    for b in range(bsz):
        lo, hi = bracket[b]
        above = (a[b] >= hi).astype(F32)
        tied = ((a[b] >= lo) & (a[b] < hi)).astype(F32)
        need = cap - jnp.sum(above, axis=-1, keepdims=True)
        sel = above + tied * (_prefix_count(tied, upper) < need).astype(F32)
        spt = jnp.where(sel > 0.5, _prefix_count(sel, upper), -1.0)
        spt_ref[:, b * seq:(b + 1) * seq] = spt
        full = jnp.concatenate([spt, jnp.full((LANES - ne, seq), -1.0, F32)], axis=0)
        sp_ref[b * seq:(b + 1) * seq, :] = full.T


def topk_select(afft, bsz, seq, cap):
    return pl.pallas_call(
        functools.partial(_topk_kernel, cap=cap, bsz=bsz),
        grid=(1,),
        in_specs=[pl.BlockSpec((N_EXPERTS, bsz * seq), lambda i: (0, 0))],
        out_specs=[
            pl.BlockSpec((bsz * seq, LANES), lambda i: (0, 0)),
            pl.BlockSpec((N_EXPERTS, bsz * seq), lambda i: (0, 0)),
        ],
        out_shape=[
            jax.ShapeDtypeStruct((bsz * seq, LANES), F32),
            jax.ShapeDtypeStruct((N_EXPERTS, bsz * seq), F32),
        ],
        compiler_params=_params("arbitrary"),
        name="topk_select",
    )(afft)


def _gather_kernel(spt_ref, afft_ref, h_ref, xe_ref, gate_ref, *, cap):
    e = pl.program_id(1)
    seq = h_ref.shape[0]
    sp = spt_ref[pl.ds(e, 1), :]
    slot = lax.broadcasted_iota(jnp.int32, (cap, seq), 0).astype(F32)
    hit = sp == slot
    xe_ref[0] = _dot(hit.astype(BF16), h_ref[...]).astype(BF16)
    gate_ref[0] = jnp.sum(jnp.where(hit, afft_ref[pl.ds(e, 1), :], 0.0), axis=-1, keepdims=True)


def moe_gather(spt, afft, hb, bsz, seq, cap):
    d = hb.shape[1]
    return pl.pallas_call(
        functools.partial(_gather_kernel, cap=cap),
        grid=(bsz, N_EXPERTS),
        in_specs=[
            pl.BlockSpec((N_EXPERTS, seq), lambda b, e: (0, b)),
            pl.BlockSpec((N_EXPERTS, seq), lambda b, e: (0, b)),
            pl.BlockSpec((seq, d), lambda b, e: (b, 0)),
        ],
        out_specs=[
            pl.BlockSpec((1, cap, d), lambda b, e: (e, b, 0)),
            pl.BlockSpec((1, cap, 1), lambda b, e: (e, b, 0)),
        ],
        out_shape=[
            jax.ShapeDtypeStruct((N_EXPERTS, bsz * cap, d), BF16),
            jax.ShapeDtypeStruct((N_EXPERTS, bsz * cap, 1), F32),
        ],
        compiler_params=_params("parallel", "arbitrary"),
        name="moe_gather",
    )(spt, afft, hb)


def _expert_kernel(xe_ref, gt_ref, wg_ref, wu_ref, wd_ref, ye_ref, acc_scr):
    f = pl.program_id(1)

    @pl.when(f == 0)
    def _():
        acc_scr[...] = jnp.zeros_like(acc_scr)

    xe = xe_ref[0]
    gate = _dot(xe, wg_ref[0].astype(BF16))
    up = _dot(xe, wu_ref[0].astype(BF16))
    hid = (gate * jax.nn.sigmoid(gate) * up).astype(BF16)
    acc_scr[...] += _dot(hid, wd_ref[0].astype(BF16))

    @pl.when(f == pl.num_programs(1) - 1)
    def _():
        ye_ref[0] = (acc_scr[...] * gt_ref[0]).astype(ye_ref.dtype)


def moe_experts(xe, gates, e_gate, e_up, e_down, tf=512):
    ne, rows, d = xe.shape
    ff = e_gate.shape[2]
    return pl.pallas_call(
        _expert_kernel,
        grid=(ne, ff // tf),
        in_specs=[
            pl.BlockSpec((1, rows, d), lambda e, f: (e, 0, 0)),
            pl.BlockSpec((1, rows, 1), lambda e, f: (e, 0, 0)),
            pl.BlockSpec((1, d, tf), lambda e, f: (e, 0, f)),
            pl.BlockSpec((1, d, tf), lambda e, f: (e, 0, f)),
            pl.BlockSpec((1, tf, d), lambda e, f: (e, f, 0)),
        ],
        out_specs=pl.BlockSpec((1, rows, d), lambda e, f: (e, 0, 0)),
        out_shape=jax.ShapeDtypeStruct((ne, rows, d), BF16),
        scratch_shapes=[pltpu.VMEM((rows, d), F32)],
        compiler_params=_params("parallel", "arbitrary"),
        name="moe_experts",
    )(xe, gates, e_gate, e_up, e_down)


def _combine_kernel(x_ref, sp_ref, ye_ref, nw_ref, o_ref, acc_scr, *, cap, group):
    eg = pl.program_id(2)

    @pl.when(eg == 0)
    def _():
        acc_scr[...] = x_ref[...]

    tt = x_ref.shape[0]
    lane = lax.broadcasted_iota(jnp.int32, (tt, LANES), 1)
    slot = lax.broadcasted_iota(jnp.int32, (tt, cap), 1).astype(F32)
    sp_all = sp_ref[...]
    hits = []
    for k in range(group):
        sp = jnp.sum(jnp.where(lane == eg * group + k, sp_all, 0.0), axis=-1, keepdims=True)
        hits.append((sp == slot).astype(BF16))
    onehot = jnp.concatenate(hits, axis=1)
    acc_scr[...] += _dot(onehot, ye_ref[...].reshape(group * cap, ye_ref.shape[2]))

    @pl.when(eg == pl.num_programs(2) - 1)
    def _():
        y = acc_scr[...]
        ms = jnp.mean(y * y, axis=-1, keepdims=True)
        o_ref[...] = y * lax.rsqrt(ms + NORM_EPS) * nw_ref[...]


def moe_combine(x1, sp, ye, norm_w, bsz, seq, cap, tt=512, group=4):
    m, d = x1.shape
    nt = seq // tt
    return pl.pallas_call(
        functools.partial(_combine_kernel, cap=cap, group=group),
        grid=(bsz, nt, N_EXPERTS // group),
        in_specs=[
            pl.BlockSpec((tt, d), lambda b, i, e: (b * nt + i, 0)),
            pl.BlockSpec((tt, LANES), lambda b, i, e: (b * nt + i, 0)),
            pl.BlockSpec((group, cap, d), lambda b, i, e: (e, b, 0)),
            pl.BlockSpec((1, d), lambda b, i, e: (0, 0)),
        ],
        out_specs=pl.BlockSpec((tt, d), lambda b, i, e: (b * nt + i, 0)),
        out_shape=jax.ShapeDtypeStruct((m, d), F32),
        scratch_shapes=[pltpu.VMEM((tt, d), F32)],
        compiler_params=_params("parallel", "parallel", "arbitrary"),
        name="moe_combine",
    )(x1, sp, ye, norm_w.reshape(1, d))


def _pad_cols(w, width):
    return jnp.pad(w, [(0, 0)] * (w.ndim - 1) + [(0, width - w.shape[-1])])


def rwkv_mixer(pz, mu_prev, mu_next, w0, decay_up, a0, iclr_up, gate_up, k_k, k_a, r_k, ln_x_w, ln_x_b,
               bsz, seq):
    ab, rb, bt, kt, p_last, vb, bonus, g = rwkv_prep(
        pz, mu_prev, mu_next, w0, decay_up, a0, iclr_up, gate_up, k_k, k_a, r_k, bsz, seq)
    y_fwd, y_bwd = rwkv_scan(ab, rb, bt, kt, vb, p_last, bsz, seq)
    return rwkv_post(y_fwd, y_bwd, bonus, g, ln_x_w, ln_x_b)


def moe_block(x1, norm2_w, w_router, e_gate, e_up, e_down, norm_f_w, bsz, seq):
    cap = CAPACITY_FACTOR * seq // N_EXPERTS
    hb, afft = router(x1, norm2_w, w_router)
    sp, spt = topk_select(afft, bsz, seq, cap)
    xe, gates = moe_gather(spt, afft, hb, bsz, seq, cap)
    ye = moe_experts(xe, gates, e_gate, e_up, e_down)
    return moe_combine(x1, sp, ye, norm_f_w, bsz, seq, cap)


def kernel(x, positions, norm1_w, w_in, mu_prev, mu_next, lambda_q1, lambda_k1, lambda_q2, lambda_k2, subln_w, w0, decay_up, a0, iclr_up, gate_up, k_k, k_a, r_k, ln_x_w, ln_x_b, w_out, norm2_w, w_router, e_gate, e_up, e_down, norm_f_w):
    bsz, seq, d = x.shape
    m = bsz * seq
    xf = x.reshape(m, d)
    pos = positions.reshape(m, 1)
    lambda_init = 0.8 - 0.6 * math.exp(-0.3 * 0)

    w_all = _pad_cols(w_in[0].astype(BF16), ATT_COLS + SHIFT_PAD)
    h = rms_norm_bf16(xf, norm1_w[0])
    qkv = matmul_rope(h, w_all, pos)
    p_z = matmul(h, w_all, ATT_COLS, SHIFT_PAD, F32)

    att = attention(qkv, lambda_q1[0], lambda_k1[0], lambda_q2[0], lambda_k2[0],
                    subln_w[0], bsz, seq, lambda_init)

    rw = rwkv_mixer(p_z, _pad_cols(mu_prev, SHIFT_PAD), _pad_cols(mu_next, SHIFT_PAD),
                    w0[0], decay_up[0], a0[0], iclr_up[0], gate_up[0], k_k[0], k_a[0], r_k[0],
                    ln_x_w[0], ln_x_b[0], bsz, seq)

    wo = w_out[0].astype(BF16)
    x1 = out_proj(xf, att, rw, wo[:ATT_WIDTH], wo[ATT_WIDTH:])

    out = moe_block(x1, norm2_w[0], w_router[0], e_gate[0], e_up[0], e_down[0], norm_f_w, bsz, seq)
    return out.reshape(bsz, seq, d)
```

```python
import functools
import math

import jax
import jax.numpy as jnp
from jax import lax
from jax.experimental import pallas as pl
from jax.experimental.pallas import tpu as pltpu

F32 = jnp.float32
BF16 = jnp.bfloat16

LANES = 128
VMEM_LIMIT_BYTES = 56 * 1024 * 1024

D_MODEL = 2048
NORM_EPS = 1e-6
ATT_HEADS = 8
ATT_QK_DIM = 64
ATT_V_DIM = 128
ATT_WIDTH = ATT_HEADS * ATT_V_DIM
ATT_QK_COLS = ATT_HEADS * 2 * ATT_QK_DIM
ATT_COLS = 2 * ATT_QK_COLS + ATT_WIDTH
ROPE_THETA = 500000.0
ROPE_DIM = ATT_QK_DIM // 4
SUBLN_EPS = 1e-5
RWKV_WIDTH = 1024
RWKV_HEAD = 64
RWKV_HEADS = RWKV_WIDTH // RWKV_HEAD
DECAY_LORA = 64
ICLR_LORA = 64
GATE_LORA = 160
LORA_COLS = DECAY_LORA + ICLR_LORA + GATE_LORA
LORA_PAD = 384
GN_EPS = 64e-5
SHIFT_WIDTH = 3 * RWKV_WIDTH + LORA_COLS
SHIFT_PAD = 3 * RWKV_WIDTH + 512
N_EXPERTS = 16
CAPACITY_FACTOR = 2
EXPERT_FF = 2048
CHUNK = 64


def _params(*sem):
    return pltpu.CompilerParams(dimension_semantics=sem, vmem_limit_bytes=VMEM_LIMIT_BYTES)


def _dot(a, b):
    return jnp.dot(a, b, preferred_element_type=F32)


def _dot_nt(a, b):
    return lax.dot_general(a, b, (((1,), (1,)), ((), ())), preferred_element_type=F32)


def _split2(x):
    hi = x.astype(BF16)
    lo = (x - hi.astype(F32)).astype(BF16)
    return hi, lo


def _dot_lhs2(x, m_bf16):
    hi, lo = _split2(x)
    return _dot(hi, m_bf16) + _dot(lo, m_bf16)


def _rms_norm_kernel(x_ref, nw_ref, o_ref):
    x = x_ref[...]
    ms = jnp.mean(x * x, axis=-1, keepdims=True)
    o_ref[...] = (x * lax.rsqrt(ms + NORM_EPS) * nw_ref[...]).astype(o_ref.dtype)


def rms_norm_bf16(x, nw, tm=512):
    m, k = x.shape
    return pl.pallas_call(
        _rms_norm_kernel,
        grid=(m // tm,),
        in_specs=[pl.BlockSpec((tm, k), lambda i: (i, 0)), pl.BlockSpec((1, k), lambda i: (0, 0))],
        out_specs=pl.BlockSpec((tm, k), lambda i: (i, 0)),
        out_shape=jax.ShapeDtypeStruct((m, k), BF16),
        compiler_params=_params("parallel"),
        name="rms_norm",
    )(x, nw.reshape(1, k))


def _mm_kernel(h_ref, w_ref, o_ref):
    o_ref[...] = _dot(h_ref[...], w_ref[...]).astype(o_ref.dtype)


def matmul(h, w_bf16, col0, n, out_dtype, tm=2048, tn=512):
    m, k = h.shape
    j0 = col0 // tn
    return pl.pallas_call(
        _mm_kernel,
        grid=(m // tm, n // tn),
        in_specs=[
            pl.BlockSpec((tm, k), lambda i, j: (i, 0)),
            pl.BlockSpec((k, tn), lambda i, j: (0, j0 + j)),
        ],
        out_specs=pl.BlockSpec((tm, tn), lambda i, j: (i, j)),
        out_shape=jax.ShapeDtypeStruct((m, n), out_dtype),
        compiler_params=_params("parallel", "arbitrary"),
        name="matmul",
    )(h, w_bf16)


def _mm_rope_kernel(h_ref, w_ref, pos_ref, freq_ref, slo_ref, shi_ref, o_ref,
                    c_scr, lo_scr, hi_scr, *, rope_tiles, q_tiles):
    j = pl.program_id(1)

    @pl.when(j == 0)
    def _():
        ang = pos_ref[...].astype(F32) * freq_ref[...]
        s = jnp.sin(ang)
        c_scr[...] = jnp.cos(ang)
        lo_scr[...] = s * slo_ref[...]
        hi_scr[...] = s * shi_ref[...]

    acc = _dot(h_ref[...], w_ref[...])

    @pl.when(j < rope_tiles)
    def _():
        scale = jnp.where(j < q_tiles, ATT_QK_DIM ** -0.5 * math.log2(math.e), 1.0)
        half = ROPE_DIM // 2
        for g in range(acc.shape[1] // LANES):
            x = acc[:, g * LANES:(g + 1) * LANES]
            x_dn = pltpu.roll(x, half, axis=1)
            x_up = pltpu.roll(x, LANES - half, axis=1)
            y = x * c_scr[...] + x_dn * hi_scr[...] + x_up * lo_scr[...]
            o_ref[:, g * LANES:(g + 1) * LANES] = (y * scale).astype(o_ref.dtype)

    @pl.when(j >= rope_tiles)
    def _():
        o_ref[...] = acc.astype(o_ref.dtype)


def matmul_rope(h, w_bf16, pos, tm=2048, tn=512):
    m, k = h.shape
    n = ATT_COLS
    lane = jnp.arange(LANES) % ATT_QK_DIM
    half = ROPE_DIM // 2
    inv_freq = ROPE_THETA ** (-jnp.arange(0, ROPE_DIM, 2, dtype=F32) / ROPE_DIM)
    freq = jnp.where(lane < ROPE_DIM, inv_freq[lane % half], 0.0).astype(F32).reshape(1, LANES)
    sgn_lo = jnp.where(lane < half, -1.0, 0.0).astype(F32).reshape(1, LANES)
    sgn_hi = jnp.where((lane >= half) & (lane < ROPE_DIM), 1.0, 0.0).astype(F32).reshape(1, LANES)
    vec = pl.BlockSpec((1, LANES), lambda i, j: (0, 0))
    return pl.pallas_call(
        functools.partial(_mm_rope_kernel, rope_tiles=2 * ATT_QK_COLS // tn, q_tiles=ATT_QK_COLS // tn),
        grid=(m // tm, n // tn),
        in_specs=[
            pl.BlockSpec((tm, k), lambda i, j: (i, 0)),
            pl.BlockSpec((k, tn), lambda i, j: (0, j)),
            pl.BlockSpec((tm, 1), lambda i, j: (i, 0)),
            vec, vec, vec,
        ],
        out_specs=pl.BlockSpec((tm, tn), lambda i, j: (i, j)),
        out_shape=jax.ShapeDtypeStruct((m, n), BF16),
        scratch_shapes=[pltpu.VMEM((tm, LANES), F32), pltpu.VMEM((tm, LANES), F32),
                        pltpu.VMEM((tm, LANES), F32)],
        compiler_params=_params("parallel", "arbitrary"),
        name="matmul_rope",
    )(h, w_bf16, pos, freq, sgn_lo, sgn_hi)


def _attn_kernel(q_ref, qn_ref, k_ref, v_ref, lq1_ref, lk1_ref, lq2_ref, lk2_ref, sw_ref, o_ref,
                 v_scr, sa_scr, ma_scr, sb_scr, mb_scr, *, lambda_init):
    dv = ATT_V_DIM
    i = pl.program_id(2)

    def scores(q, s_scr, m_scr):
        for c in range(2):
            cols = slice(c * ATT_QK_DIM, (c + 1) * ATT_QK_DIM)
            s = _dot_nt(q[:, cols], k_ref[:, cols])
            s_scr[c] = s
            m_scr[c] = jnp.max(s, axis=-1, keepdims=True)

    @pl.when(i == 0)
    def _():
        lane = lax.broadcasted_iota(jnp.int32, (v_scr.shape[0], dv), 1)
        v_scr[:, :dv] = v_ref[...]
        v_scr[:, dv:] = (lane == 0).astype(BF16)
        scores(q_ref[...], sa_scr, ma_scr)

    def step(cur_s, cur_m, nxt_s, nxt_m):
        scores(qn_ref[...], nxt_s, nxt_m)
        va = v_scr[...]
        o1 = _dot(jnp.exp2(cur_s[0] - cur_m[0]).astype(BF16), va)
        o2 = _dot(jnp.exp2(cur_s[1] - cur_m[1]).astype(BF16), va)
        lam = (jnp.exp(jnp.sum(lq1_ref[...] * lk1_ref[...], axis=-1, keepdims=True))
               - jnp.exp(jnp.sum(lq2_ref[...] * lk2_ref[...], axis=-1, keepdims=True)) + lambda_init)
        o = o1[:, :dv] / o1[:, dv:dv + 1] - o2[:, :dv] * (lam / o2[:, dv:dv + 1])
        ms = jnp.mean(o * o, axis=-1, keepdims=True)
        o = o * lax.rsqrt(ms + SUBLN_EPS) * sw_ref[...] * (1.0 - lambda_init)
        o_ref[...] = o.astype(o_ref.dtype)

    @pl.when(i % 2 == 0)
    def _():
        step(sa_scr, ma_scr, sb_scr, mb_scr)

    @pl.when(i % 2 == 1)
    def _():
        step(sb_scr, mb_scr, sa_scr, ma_scr)


def attention(qkv, lq1, lk1, lq2, lk2, subln_w, bsz, seq, lambda_init, tq=256):
    nq = seq // tq
    vec = lambda n: pl.BlockSpec((1, n), lambda b, h, i: (0, 0))
    nh = ATT_HEADS
    s_buf = pltpu.VMEM((2, tq, seq), F32)
    m_buf = pltpu.VMEM((2, tq, 1), F32)
    return pl.pallas_call(
        functools.partial(_attn_kernel, lambda_init=lambda_init),
        grid=(bsz, nh, nq),
        in_specs=[
            pl.BlockSpec((tq, LANES), lambda b, h, i: (b * nq + i, h)),
            pl.BlockSpec((tq, LANES), lambda b, h, i: (b * nq + jnp.minimum(i + 1, nq - 1), h)),
            pl.BlockSpec((seq, LANES), lambda b, h, i: (b, nh + h)),
            pl.BlockSpec((seq, LANES), lambda b, h, i: (b, 2 * nh + h)),
            vec(ATT_QK_DIM), vec(ATT_QK_DIM), vec(ATT_QK_DIM), vec(ATT_QK_DIM),
            vec(ATT_V_DIM),
        ],
        out_specs=pl.BlockSpec((tq, LANES), lambda b, h, i: (b * nq + i, h)),
        out_shape=jax.ShapeDtypeStruct((bsz * seq, ATT_WIDTH), BF16),
        scratch_shapes=[pltpu.VMEM((seq, 2 * ATT_V_DIM), BF16), s_buf, m_buf, s_buf, m_buf],
        compiler_params=_params("parallel", "parallel", "arbitrary"),
        name="diff_attention",
    )(qkv, qkv, qkv, qkv,
      lq1.reshape(1, -1), lk1.reshape(1, -1), lq2.reshape(1, -1), lk2.reshape(1, -1),
      subln_w.reshape(1, -1))


HALO = 8


def _token_shift(z_ref, prev_ref, next_ref, mu_p, mu_n, first, last):
    z = z_ref[...]
    n = z.shape[0]
    row = lax.broadcasted_iota(jnp.int32, z.shape, 0)
    before = jnp.where(first, 0.0, prev_ref[HALO - 1:HALO, :])
    after = jnp.where(last, 0.0, next_ref[0:1, :])
    zp = jnp.where(row == 0, before, pltpu.roll(z, 1, axis=0))
    zn = jnp.where(row == n - 1, after, pltpu.roll(z, n - 1, axis=0))
    return z + mu_p * (zp - z) + mu_n * (zn - z)


def _seg_sum(x, ones_blk):
    parts = []
    for j in range(x.shape[1] // LANES):
        parts.append(_dot_lhs2(x[:, j * LANES:(j + 1) * LANES], ones_blk))
    return jnp.concatenate(parts, axis=1)


def _prep_kernel(r_ref, k_ref, v_ref, lo_ref, rp_ref, kp_ref, vp_ref, lp_ref, rn_ref, kn_ref, vn_ref,
                 ln_ref, mup_ref, mun_ref, w0_ref, du_ref, a0_ref, iu_ref, gu_ref, kk_ref,
                 ka_ref, rk_ref, ones_ref, tril_ref, triu_ref,
                 ab_ref, rb_ref, bt_ref, kt_ref, pl_ref, vb_ref, bonus_ref, g_ref, *, tt):
    first = pl.program_id(1) == 0
    last = pl.program_id(1) == pl.num_programs(1) - 1
    c = RWKV_WIDTH
    shift = lambda z, p, n, lo_col, hi_col: _token_shift(
        z, p, n, mup_ref[:, lo_col:hi_col], mun_ref[:, lo_col:hi_col], first, last)
    r = shift(r_ref, rp_ref, rn_ref, 0, c)
    k = shift(k_ref, kp_ref, kn_ref, c, 2 * c)
    v = shift(v_ref, vp_ref, vn_ref, 2 * c, 3 * c)
    lo = shift(lo_ref, lp_ref, ln_ref, 3 * c, 3 * c + LORA_PAD)
    ones_blk = ones_ref[...]
    vb_ref[...] = v.astype(BF16)
    g_ref[...] = _dot(jax.nn.sigmoid(lo).astype(BF16), gu_ref[...])
    kk = k * kk_ref[...]
    kk = kk * jnp.minimum(lax.rsqrt(_seg_sum(kk * kk, ones_blk)), 1e12)
    th = jnp.tanh(lo).astype(BF16)
    lob = lo.astype(BF16)
    nchunk = tt // CHUNK
    ksum = jnp.zeros_like(k)
    for d in range(2):
        wl = w0_ref[d] + _dot(th, du_ref[d])
        lw = -math.exp(-0.5) * jax.nn.sigmoid(wl)
        a = jax.nn.sigmoid(a0_ref[d] + _dot(lob, iu_ref[d]))
        kd = k * (1.0 + (a - 1.0) * ka_ref[...])
        ksum = ksum + kd
        tri = tril_ref[...] if d == 0 else triu_ref[...]
        cum = _dot_lhs2_rhs(tri, lw)
        e_pos = jnp.exp(cum)
        e_neg = jnp.exp(-cum)
        ab_ref[d] = (-kk * jnp.exp(cum - lw)).astype(BF16)
        rb_ref[d] = (r * e_pos).astype(BF16)
        bt_ref[d] = (kk * a * e_neg).astype(BF16)
        kt_ref[d] = (kd * e_neg).astype(BF16)
        for c in range(nchunk):
            last = c * CHUNK + (CHUNK - 1 if d == 0 else 0)
            pl_ref[d, c] = e_pos[last:last + 1, :]
    bonus_ref[...] = _seg_sum(r * ksum * rk_ref[...], ones_blk) * v


def _dot_lhs2_rhs(tri_bf16, x):
    hi, lo = _split2(x)
    return _dot(tri_bf16, hi) + _dot(tri_bf16, lo)


def rwkv_prep(pz, mu_prev, mu_next, w0, decay_up, a0, iclr_up, gate_up, k_k, k_a, r_k, bsz, seq, tt=256):
    m = bsz * seq
    c = RWKV_WIDTH
    nt = seq // tt
    ncb = tt // CHUNK
    nc = seq // CHUNK

    def pad_rows(w, start):
        out = jnp.zeros(w.shape[:-2] + (LORA_PAD, c), F32)
        return lax.dynamic_update_slice_in_dim(out, w.astype(F32), start, axis=w.ndim - 2).astype(BF16)

    du = pad_rows(decay_up, 0)
    iu = pad_rows(iclr_up, DECAY_LORA)
    gu = pad_rows(gate_up, DECAY_LORA + ICLR_LORA)
    lane = jnp.arange(LANES)
    ones_blk = (lane[:, None] // RWKV_HEAD == lane[None, :] // RWKV_HEAD).astype(BF16)
    t = jnp.arange(tt)
    same = t[:, None] // CHUNK == t[None, :] // CHUNK
    tril = (same & (t[:, None] >= t[None, :])).astype(BF16)
    triu = (same & (t[:, None] <= t[None, :])).astype(BF16)
    r_k_flat = r_k.reshape(1, c)
    row = lambda: pl.BlockSpec((1, c), lambda b, i: (0, 0))
    full3 = lambda s: pl.BlockSpec(s, lambda b, i: (0, 0, 0))
    tok = lambda j, w: pl.BlockSpec((tt, w), lambda b, i: (b * nt + i, j))
    per = tt // HALO
    prev = lambda j, w: pl.BlockSpec((HALO, w), lambda b, i: (jnp.maximum((b * nt + i) * per - 1, 0), j))
    nxt = lambda j, w: pl.BlockSpec(
        (HALO, w), lambda b, i: (jnp.minimum((b * nt + i + 1) * per, m // HALO - 1), j))
    lora_j = 3 * c // LORA_PAD
    cols = [(0, c), (1, c), (2, c), (lora_j, LORA_PAD)]
    mu_row = pl.BlockSpec((1, pz.shape[1]), lambda b, i: (0, 0))
    dir_tok = pl.BlockSpec((2, tt, c), lambda b, i: (0, b * nt + i, 0))
    outs = pl.pallas_call(
        functools.partial(_prep_kernel, tt=tt),
        grid=(bsz, nt),
        in_specs=[tok(j, w) for j, w in cols] + [prev(j, w) for j, w in cols]
        + [nxt(j, w) for j, w in cols] + [
            mu_row, mu_row,
            full3((2, 1, c)), full3((2, LORA_PAD, c)), full3((2, 1, c)), full3((2, LORA_PAD, c)),
            pl.BlockSpec((LORA_PAD, c), lambda b, i: (0, 0)),
            row(), row(), row(),
            pl.BlockSpec((LANES, LANES), lambda b, i: (0, 0)),
            pl.BlockSpec((tt, tt), lambda b, i: (0, 0)),
            pl.BlockSpec((tt, tt), lambda b, i: (0, 0)),
        ],
        out_specs=[
            dir_tok, dir_tok, dir_tok, dir_tok,
            pl.BlockSpec((2, ncb, 1, c), lambda b, i: (0, b * nt + i, 0, 0)),
            pl.BlockSpec((tt, c), lambda b, i: (b * nt + i, 0)),
            pl.BlockSpec((tt, c), lambda b, i: (b * nt + i, 0)),
            pl.BlockSpec((tt, c), lambda b, i: (b * nt + i, 0)),
        ],
        out_shape=[
            jax.ShapeDtypeStruct((2, m, c), BF16),
            jax.ShapeDtypeStruct((2, m, c), BF16),
            jax.ShapeDtypeStruct((2, m, c), BF16),
            jax.ShapeDtypeStruct((2, m, c), BF16),
            jax.ShapeDtypeStruct((2, bsz * nc, 1, c), F32),
            jax.ShapeDtypeStruct((m, c), BF16),
            jax.ShapeDtypeStruct((m, c), F32),
            jax.ShapeDtypeStruct((m, c), F32),
        ],
        compiler_params=_params("parallel", "parallel"),
        name="rwkv_prep",
    )(*([pz] * 12), mu_prev, mu_next, w0.reshape(2, 1, c), du, a0.reshape(2, 1, c), iu, gu,
      k_k.reshape(1, c), k_a.reshape(1, c), r_k_flat, ones_blk, tril, triu)
    return outs


def _scan_kernel(abf_ref, abb_ref, rbf_ref, rbb_ref, btf_ref, btb_ref, ktf_ref, ktb_ref,
                 vf_ref, vb_ref, plf_ref, plb_ref, yf_ref, yb_ref, m_scr, *, heads):
    @pl.when(pl.program_id(1) == 0)
    def _():
        m_scr[...] = jnp.zeros_like(m_scr)

    ab_ref = (abf_ref, abb_ref)
    rb_ref = (rbf_ref, rbb_ref)
    bt_ref = (btf_ref, btb_ref)
    kt_ref = (ktf_ref, ktb_ref)
    v_ref = (vf_ref, vb_ref)
    pl_ref = (plf_ref, plb_ref)
    y_ref = (yf_ref, yb_ref)
    n = CHUNK
    row = lax.broadcasted_iota(jnp.int32, (2 * n, 2 * n), 0)
    col = lax.broadcasted_iota(jnp.int32, (2 * n, 2 * n), 1)
    top = row < n
    tr = row % n
    tc = col % n
    eye = (lax.broadcasted_iota(jnp.int32, (n, n), 0)
           == lax.broadcasted_iota(jnp.int32, (n, n), 1)).astype(F32)
    keep = ((tr > tc) | (~top & (tr == tc)), (tr < tc) | (~top & (tr == tc)))
    probs = [(d, hh) for d in range(2) for hh in range(heads)]
    sl = lambda hh: slice(hh * n, (hh + 1) * n)
    ab = [ab_ref[d][0, :, sl(hh)] for d, hh in probs]
    rb = [rb_ref[d][0, :, sl(hh)] for d, hh in probs]
    bt = [bt_ref[d][0, :, sl(hh)] for d, hh in probs]
    kt = [kt_ref[d][0, :, sl(hh)] for d, hh in probs]
    vv = [v_ref[d][:, sl(hh)] for d, hh in probs]
    p_last = [pl_ref[d][0, 0, :, sl(hh)] for d, hh in probs]
    idx = range(len(probs))
    g1 = [_dot_nt(jnp.concatenate([ab[i], rb[i]], axis=0), jnp.concatenate([bt[i], kt[i]], axis=0))
          for i in idx]
    g1 = [jnp.where(keep[probs[i][0]], g1[i], 0.0) for i in idx]
    a_ab = [g[:n, :n] for g in g1]
    low = [g[n:, :].astype(BF16) for g in g1]
    tm = [eye + a for a in a_ab]
    pw = [_dot(a.astype(BF16), a.astype(BF16)) for a in a_ab]
    span = 2
    while span < n:
        last = span * 2 >= n
        nxt = []
        for i in idx:
            pwb = pw[i].astype(BF16)
            if last:
                nxt.append((tm[i] + _dot(tm[i].astype(BF16), pwb), None))
            else:
                both = _dot(jnp.concatenate([tm[i], pw[i]], axis=0).astype(BF16), pwb)
                nxt.append((tm[i] + both[:n], both[n:]))
        tm = [t for t, _ in nxt]
        pw = [p for _, p in nxt]
        span *= 2
    wc = [_dot(tm[i].astype(BF16), jnp.concatenate([ab[i], g1[i][:n, n:].astype(BF16)], axis=1)) for i in idx]
    w = [x[:, :n].astype(BF16) for x in wc]
    cuv = [_dot(wc[i][:, n:].astype(BF16), vv[i]).astype(BF16) for i in idx]
    zero = jnp.zeros((n, n), BF16)
    rhs = [jnp.concatenate([jnp.concatenate([w[i], cuv[i]], axis=1),
                            jnp.concatenate([zero, vv[i]], axis=1)], axis=0) for i in idx]
    bh_t = [(bt[i].astype(F32) * p_last[i]).T for i in idx]
    kh_t = [(kt[i].astype(F32) * p_last[i]).T for i in idx]
    lhs = [jnp.concatenate([low[i], jnp.concatenate([bh_t[i], kh_t[i]], axis=1).astype(BF16)], axis=0)
           for i in idx]
    out = [_dot(lhs[i], rhs[i]) for i in idx]
    q_m = [rb[i].astype(F32) + out[i][:n, :n] for i in idx]
    g_m = [eye * p_last[i] + out[i][n:, :n] for i in idx]
    upd = [_dot(jnp.concatenate([q_m[i], g_m[i]], axis=0).astype(BF16), m_scr[d, hh].astype(BF16))
           for i, (d, hh) in enumerate(probs)]
    for i, (d, hh) in enumerate(probs):
        y_ref[d][:, sl(hh)] = upd[i][:n] + out[i][:n, n:]
        m_scr[d, hh] = upd[i][n:] + out[i][n:, n:]


def rwkv_scan(ab, rb, bt, kt, vb, p_last, bsz, seq):
    nc = seq // CHUNK
    nh = RWKV_HEADS
    n = CHUNK
    c_w = RWKV_WIDTH
    fwd = lambda b, c: b * nc + c
    bwd = lambda b, c: b * nc + nc - 1 - c
    dir_f = pl.BlockSpec((1, n, c_w), lambda b, c: (0, fwd(b, c), 0))
    dir_b = pl.BlockSpec((1, n, c_w), lambda b, c: (1, bwd(b, c), 0))
    tok_f = pl.BlockSpec((n, c_w), lambda b, c: (fwd(b, c), 0))
    tok_b = pl.BlockSpec((n, c_w), lambda b, c: (bwd(b, c), 0))
    pl_f = pl.BlockSpec((1, 1, 1, c_w), lambda b, c: (0, fwd(b, c), 0, 0))
    pl_b = pl.BlockSpec((1, 1, 1, c_w), lambda b, c: (1, bwd(b, c), 0, 0))
    y_shape = jax.ShapeDtypeStruct((bsz * seq, c_w), F32)
    return pl.pallas_call(
        functools.partial(_scan_kernel, heads=nh),
        grid=(bsz, nc),
        in_specs=[dir_f, dir_b, dir_f, dir_b, dir_f, dir_b, dir_f, dir_b, tok_f, tok_b, pl_f, pl_b],
        out_specs=[tok_f, tok_b],
        out_shape=[y_shape, y_shape],
        scratch_shapes=[pltpu.VMEM((2, nh, n, n), F32)],
        compiler_params=_params("parallel", "arbitrary"),
        name="rwkv_scan",
    )(ab, ab, rb, rb, bt, bt, kt, kt, vb, vb, p_last, p_last)


def _post_kernel(yf_ref, yb_ref, bonus_ref, g_ref, lw_ref, lb_ref, o_ref):
    y = yf_ref[...] + yb_ref[...]
    n = RWKV_HEAD
    parts = []
    for h in range(RWKV_HEADS):
        yh = y[:, h * n:(h + 1) * n]
        mu = jnp.mean(yh, axis=-1, keepdims=True)
        yc = yh - mu
        var = jnp.mean(yc * yc, axis=-1, keepdims=True)
        parts.append(yc * lax.rsqrt(var + GN_EPS))
    yn = jnp.concatenate(parts, axis=1)
    o_ref[...] = ((yn * lw_ref[...] + lb_ref[...] + bonus_ref[...]) * g_ref[...]).astype(o_ref.dtype)


def rwkv_post(y_fwd, y_bwd, bonus, g, ln_w, ln_b, tt=256):
    m = y_fwd.shape[0]
    c = RWKV_WIDTH
    tok = pl.BlockSpec((tt, c), lambda i: (i, 0))
    row = pl.BlockSpec((1, c), lambda i: (0, 0))
    return pl.pallas_call(
        _post_kernel,
        grid=(m // tt,),
        in_specs=[tok, tok, tok, tok, row, row],
        out_specs=tok,
        out_shape=jax.ShapeDtypeStruct((m, c), BF16),
        compiler_params=_params("parallel"),
        name="rwkv_post",
    )(y_fwd, y_bwd, bonus, g, ln_w.reshape(1, c), ln_b.reshape(1, c))


def _out_proj_kernel(x_ref, a_ref, r_ref, wa_ref, wr_ref, o_ref):
    o_ref[...] = x_ref[...] + _dot(a_ref[...], wa_ref[...]) + _dot(r_ref[...], wr_ref[...])


def out_proj(x, att, rw, w_att, w_rw, tm=2048, tn=512):
    m, n = x.shape
    return pl.pallas_call(
        _out_proj_kernel,
        grid=(m // tm, n // tn),
        in_specs=[
            pl.BlockSpec((tm, tn), lambda i, j: (i, j)),
            pl.BlockSpec((tm, ATT_WIDTH), lambda i, j: (i, 0)),
            pl.BlockSpec((tm, RWKV_WIDTH), lambda i, j: (i, 0)),
            pl.BlockSpec((ATT_WIDTH, tn), lambda i, j: (0, j)),
            pl.BlockSpec((RWKV_WIDTH, tn), lambda i, j: (0, j)),
        ],
        out_specs=pl.BlockSpec((tm, tn), lambda i, j: (i, j)),
        out_shape=jax.ShapeDtypeStruct((m, n), F32),
        compiler_params=_params("parallel", "parallel"),
        name="out_proj",
    )(x, att, rw, w_att, w_rw)


def _router_kernel(x_ref, nw_ref, wh_ref, wl_ref, hb_ref, afft_ref):
    x = x_ref[...]
    ms = jnp.mean(x * x, axis=-1, keepdims=True)
    h = x * lax.rsqrt(ms + NORM_EPS) * nw_ref[...]
    hb_ref[...] = h.astype(BF16)
    hi, lo = _split2(h)
    logits = _dot(hi, wh_ref[...]) + _dot(lo, wh_ref[...]) + _dot(hi, wl_ref[...])
    lane = lax.broadcasted_iota(jnp.int32, logits.shape, 1)
    valid = lane < N_EXPERTS
    logits = jnp.where(valid, logits, -1e30)
    e = jnp.where(valid, jnp.exp(logits - jnp.max(logits, axis=-1, keepdims=True)), 0.0)
    aff = e / jnp.sum(e, axis=-1, keepdims=True)
    afft_ref[...] = aff.T[:N_EXPERTS, :]


def router(x1, norm_w, w_router, tm=512):
    m, k = x1.shape
    wpad = jnp.zeros((k, LANES), F32).at[:, :N_EXPERTS].set(w_router)
    wh, wl = _split2(wpad)
    return pl.pallas_call(
        _router_kernel,
        grid=(m // tm,),
        in_specs=[
            pl.BlockSpec((tm, k), lambda i: (i, 0)),
            pl.BlockSpec((1, k), lambda i: (0, 0)),
            pl.BlockSpec((k, LANES), lambda i: (0, 0)),
            pl.BlockSpec((k, LANES), lambda i: (0, 0)),
        ],
        out_specs=[
            pl.BlockSpec((tm, k), lambda i: (i, 0)),
            pl.BlockSpec((N_EXPERTS, tm), lambda i: (0, i)),
        ],
        out_shape=[
            jax.ShapeDtypeStruct((m, k), BF16),
            jax.ShapeDtypeStruct((N_EXPERTS, m), F32),
        ],
        compiler_params=_params("parallel"),
        name="router",
    )(x1, norm_w.reshape(1, k), wh, wl)


def _prefix_count(x, upper):
    outs = []
    carry = jnp.zeros((x.shape[0], 1), F32)
    for j in range(x.shape[1] // LANES):
        xt = x[:, j * LANES:(j + 1) * LANES]
        outs.append(_dot(xt.astype(BF16), upper) + carry)
        carry = carry + jnp.sum(xt, axis=-1, keepdims=True)
    return jnp.concatenate(outs, axis=1)


BISECT_STEPS = 152


def _topk_kernel(afft_ref, sp_ref, spt_ref, *, cap, bsz):
    ne = afft_ref.shape[0]
    seq = afft_ref.shape[1] // bsz
    a = [afft_ref[:, b * seq:(b + 1) * seq] for b in range(bsz)]

    def halve(_, bracket):
        out = []
        for b in range(bsz):
            lo, hi = bracket[b]
            mid = (lo + hi) * 0.5
            cnt = jnp.sum((a[b] >= mid).astype(F32), axis=-1, keepdims=True)
            enough = cnt >= cap
            out.append((jnp.where(enough, mid, lo), jnp.where(enough, hi, mid)))
        return tuple(out)

    start = tuple((jnp.zeros((ne, 1), F32), jnp.full((ne, 1), 2.0, F32)) for _ in range(bsz))
    bracket = lax.fori_loop(0, BISECT_STEPS, halve, start)
    r_i = lax.broadcasted_iota(jnp.int32, (LANES, LANES), 0)
    c_i = lax.broadcasted_iota(jnp.int32, (LANES, LANES), 1)
    upper = (r_i < c_i).astype(BF16)
    for b in range(bsz):
        lo, hi = bracket[b]
        above = (a[b] >= hi).astype(F32)
        tied = ((a[b] >= lo) & (a[b] < hi)).astype(F32)
        need = cap - jnp.sum(above, axis=-1, keepdims=True)
        sel = above + tied * (_prefix_count(tied, upper) < need).astype(F32)
        spt = jnp.where(sel > 0.5, _prefix_count(sel, upper), -1.0)
        spt_ref[:, b * seq:(b + 1) * seq] = spt
        full = jnp.concatenate([spt, jnp.full((LANES - ne, seq), -1.0, F32)], axis=0)
        sp_ref[b * seq:(b + 1) * seq, :] = full.T


def topk_select(afft, bsz, seq, cap):
    return pl.pallas_call(
        functools.partial(_topk_kernel, cap=cap, bsz=bsz),
        grid=(1,),
        in_specs=[pl.BlockSpec((N_EXPERTS, bsz * seq), lambda i: (0, 0))],
        out_specs=[
            pl.BlockSpec((bsz * seq, LANES), lambda i: (0, 0)),
            pl.BlockSpec((N_EXPERTS, bsz * seq), lambda i: (0, 0)),
        ],
        out_shape=[
            jax.ShapeDtypeStruct((bsz * seq, LANES), F32),
            jax.ShapeDtypeStruct((N_EXPERTS, bsz * seq), F32),
        ],
        compiler_params=_params("arbitrary"),
        name="topk_select",
    )(afft)


def _gather_kernel(spt_ref, afft_ref, h_ref, xe_ref, gate_ref, *, cap):
    e = pl.program_id(1)
    seq = h_ref.shape[0]
    sp = spt_ref[pl.ds(e, 1), :]
    slot = lax.broadcasted_iota(jnp.int32, (cap, seq), 0).astype(F32)
    hit = sp == slot
    xe_ref[0] = _dot(hit.astype(BF16), h_ref[...]).astype(BF16)
    gate_ref[0] = jnp.sum(jnp.where(hit, afft_ref[pl.ds(e, 1), :], 0.0), axis=-1, keepdims=True)


def moe_gather(spt, afft, hb, bsz, seq, cap):
    d = hb.shape[1]
    return pl.pallas_call(
        functools.partial(_gather_kernel, cap=cap),
        grid=(bsz, N_EXPERTS),
        in_specs=[
            pl.BlockSpec((N_EXPERTS, seq), lambda b, e: (0, b)),
            pl.BlockSpec((N_EXPERTS, seq), lambda b, e: (0, b)),
            pl.BlockSpec((seq, d), lambda b, e: (b, 0)),
        ],
        out_specs=[
            pl.BlockSpec((1, cap, d), lambda b, e: (e, b, 0)),
            pl.BlockSpec((1, cap, 1), lambda b, e: (e, b, 0)),
        ],
        out_shape=[
            jax.ShapeDtypeStruct((N_EXPERTS, bsz * cap, d), BF16),
            jax.ShapeDtypeStruct((N_EXPERTS, bsz * cap, 1), F32),
        ],
        compiler_params=_params("parallel", "arbitrary"),
        name="moe_gather",
    )(spt, afft, hb)


def _expert_kernel(xe_ref, gt_ref, wg_ref, wu_ref, wd_ref, ye_ref, acc_scr):
    f = pl.program_id(1)

    @pl.when(f == 0)
    def _():
        acc_scr[...] = jnp.zeros_like(acc_scr)

    xe = xe_ref[0]
    gate = _dot(xe, wg_ref[0].astype(BF16))
    up = _dot(xe, wu_ref[0].astype(BF16))
    hid = (gate * jax.nn.sigmoid(gate) * up).astype(BF16)
    acc_scr[...] += _dot(hid, wd_ref[0].astype(BF16))

    @pl.when(f == pl.num_programs(1) - 1)
    def _():
        ye_ref[0] = (acc_scr[...] * gt_ref[0]).astype(ye_ref.dtype)


def moe_experts(xe, gates, e_gate, e_up, e_down, tf=512):
    ne, rows, d = xe.shape
    ff = e_gate.shape[2]
    return pl.pallas_call(
        _expert_kernel,
        grid=(ne, ff // tf),
        in_specs=[
            pl.BlockSpec((1, rows, d), lambda e, f: (e, 0, 0)),
            pl.BlockSpec((1, rows, 1), lambda e, f: (e, 0, 0)),
            pl.BlockSpec((1, d, tf), lambda e, f: (e, 0, f)),
            pl.BlockSpec((1, d, tf), lambda e, f: (e, 0, f)),
            pl.BlockSpec((1, tf, d), lambda e, f: (e, f, 0)),
        ],
        out_specs=pl.BlockSpec((1, rows, d), lambda e, f: (e, 0, 0)),
        out_shape=jax.ShapeDtypeStruct((ne, rows, d), BF16),
        scratch_shapes=[pltpu.VMEM((rows, d), F32)],
        compiler_params=_params("parallel", "arbitrary"),
        name="moe_experts",
    )(xe, gates, e_gate, e_up, e_down)


def _combine_kernel(x_ref, sp_ref, ye_ref, nw_ref, o_ref, acc_scr, *, cap, group):
    eg = pl.program_id(2)

    @pl.when(eg == 0)
    def _():
        acc_scr[...] = x_ref[...]

    tt = x_ref.shape[0]
    lane = lax.broadcasted_iota(jnp.int32, (tt, LANES), 1)
    slot = lax.broadcasted_iota(jnp.int32, (tt, cap), 1).astype(F32)
    sp_all = sp_ref[...]
    hits = []
    for k in range(group):
        sp = jnp.sum(jnp.where(lane == eg * group + k, sp_all, 0.0), axis=-1, keepdims=True)
        hits.append((sp == slot).astype(BF16))
    onehot = jnp.concatenate(hits, axis=1)
    acc_scr[...] += _dot(onehot, ye_ref[...].reshape(group * cap, ye_ref.shape[2]))

    @pl.when(eg == pl.num_programs(2) - 1)
    def _():
        y = acc_scr[...]
        ms = jnp.mean(y * y, axis=-1, keepdims=True)
        o_ref[...] = y * lax.rsqrt(ms + NORM_EPS) * nw_ref[...]


def moe_combine(x1, sp, ye, norm_w, bsz, seq, cap, tt=512, group=4):
    m, d = x1.shape
    nt = seq // tt
    return pl.pallas_call(
        functools.partial(_combine_kernel, cap=cap, group=group),
        grid=(bsz, nt, N_EXPERTS // group),
        in_specs=[
            pl.BlockSpec((tt, d), lambda b, i, e: (b * nt + i, 0)),
            pl.BlockSpec((tt, LANES), lambda b, i, e: (b * nt + i, 0)),
            pl.BlockSpec((group, cap, d), lambda b, i, e: (e, b, 0)),
            pl.BlockSpec((1, d), lambda b, i, e: (0, 0)),
        ],
        out_specs=pl.BlockSpec((tt, d), lambda b, i, e: (b * nt + i, 0)),
        out_shape=jax.ShapeDtypeStruct((m, d), F32),
        scratch_shapes=[pltpu.VMEM((tt, d), F32)],
        compiler_params=_params("parallel", "parallel", "arbitrary"),
        name="moe_combine",
    )(x1, sp, ye, norm_w.reshape(1, d))


def _pad_cols(w, width):
    return jnp.pad(w, [(0, 0)] * (w.ndim - 1) + [(0, width - w.shape[-1])])


def rwkv_mixer(pz, mu_prev, mu_next, w0, decay_up, a0, iclr_up, gate_up, k_k, k_a, r_k, ln_x_w, ln_x_b,
               bsz, seq):
    ab, rb, bt, kt, p_last, vb, bonus, g = rwkv_prep(
        pz, mu_prev, mu_next, w0, decay_up, a0, iclr_up, gate_up, k_k, k_a, r_k, bsz, seq)
    y_fwd, y_bwd = rwkv_scan(ab, rb, bt, kt, vb, p_last, bsz, seq)
    return rwkv_post(y_fwd, y_bwd, bonus, g, ln_x_w, ln_x_b)


def moe_block(x1, norm2_w, w_router, e_gate, e_up, e_down, norm_f_w, bsz, seq):
    cap = CAPACITY_FACTOR * seq // N_EXPERTS
    hb, afft = router(x1, norm2_w, w_router)
    sp, spt = topk_select(afft, bsz, seq, cap)
    xe, gates = moe_gather(spt, afft, hb, bsz, seq, cap)
    ye = moe_experts(xe, gates, e_gate, e_up, e_down)
    return moe_combine(x1, sp, ye, norm_f_w, bsz, seq, cap)


def kernel(x, positions, norm1_w, w_in, mu_prev, mu_next, lambda_q1, lambda_k1, lambda_q2, lambda_k2, subln_w, w0, decay_up, a0, iclr_up, gate_up, k_k, k_a, r_k, ln_x_w, ln_x_b, w_out, norm2_w, w_router, e_gate, e_up, e_down, norm_f_w):
    bsz, seq, d = x.shape
    m = bsz * seq
    xf = x.reshape(m, d)
    pos = positions.reshape(m, 1)
    lambda_init = 0.8 - 0.6 * math.exp(-0.3 * 0)

    w_all = _pad_cols(w_in[0].astype(BF16), ATT_COLS + SHIFT_PAD)
    h = rms_norm_bf16(xf, norm1_w[0])
    qkv = matmul_rope(h, w_all, pos)
    p_z = matmul(h, w_all, ATT_COLS, SHIFT_PAD, F32)

    att = attention(qkv, lambda_q1[0], lambda_k1[0], lambda_q2[0], lambda_k2[0],
                    subln_w[0], bsz, seq, lambda_init)

    rw = rwkv_mixer(p_z, _pad_cols(mu_prev, SHIFT_PAD), _pad_cols(mu_next, SHIFT_PAD),
                    w0[0], decay_up[0], a0[0], iclr_up[0], gate_up[0], k_k[0], k_a[0], r_k[0],
                    ln_x_w[0], ln_x_b[0], bsz, seq)

    wo = w_out[0].astype(BF16)
    x1 = out_proj(xf, att, rw, wo[:ATT_WIDTH], wo[ATT_WIDTH:])

    out = moe_block(x1, norm2_w[0], w_router[0], e_gate[0], e_up[0], e_down[0], norm_f_w, bsz, seq)
    return out.reshape(bsz, seq, d)
```

```python
import functools
import math

import jax
import jax.numpy as jnp
from jax import lax
from jax.experimental import pallas as pl
from jax.experimental.pallas import tpu as pltpu

F32 = jnp.float32
BF16 = jnp.bfloat16

LANES = 128
VMEM_LIMIT_BYTES = 56 * 1024 * 1024

D_MODEL = 2048
NORM_EPS = 1e-6
ATT_HEADS = 8
ATT_QK_DIM = 64
ATT_V_DIM = 128
ATT_WIDTH = ATT_HEADS * ATT_V_DIM
ATT_QK_COLS = ATT_HEADS * 2 * ATT_QK_DIM
ATT_COLS = 2 * ATT_QK_COLS + ATT_WIDTH
ROPE_THETA = 500000.0
ROPE_DIM = ATT_QK_DIM // 4
SUBLN_EPS = 1e-5
RWKV_WIDTH = 1024
RWKV_HEAD = 64
RWKV_HEADS = RWKV_WIDTH // RWKV_HEAD
DECAY_LORA = 64
ICLR_LORA = 64
GATE_LORA = 160
LORA_COLS = DECAY_LORA + ICLR_LORA + GATE_LORA
LORA_PAD = 384
GN_EPS = 64e-5
SHIFT_WIDTH = 3 * RWKV_WIDTH + LORA_COLS
SHIFT_PAD = 3 * RWKV_WIDTH + 512
N_EXPERTS = 16
CAPACITY_FACTOR = 2
EXPERT_FF = 2048
CHUNK = 64


def _params(*sem):
    return pltpu.CompilerParams(dimension_semantics=sem, vmem_limit_bytes=VMEM_LIMIT_BYTES)


def _dot(a, b):
    return jnp.dot(a, b, preferred_element_type=F32)


def _dot_nt(a, b):
    return lax.dot_general(a, b, (((1,), (1,)), ((), ())), preferred_element_type=F32)


def _split2(x):
    hi = x.astype(BF16)
    lo = (x - hi.astype(F32)).astype(BF16)
    return hi, lo


def _dot_lhs2(x, m_bf16):
    hi, lo = _split2(x)
    return _dot(hi, m_bf16) + _dot(lo, m_bf16)


def _rms_norm_kernel(x_ref, nw_ref, o_ref):
    x = x_ref[...]
    ms = jnp.mean(x * x, axis=-1, keepdims=True)
    o_ref[...] = (x * lax.rsqrt(ms + NORM_EPS) * nw_ref[...]).astype(o_ref.dtype)


def rms_norm_bf16(x, nw, tm=512):
    m, k = x.shape
    return pl.pallas_call(
        _rms_norm_kernel,
        grid=(m // tm,),
        in_specs=[pl.BlockSpec((tm, k), lambda i: (i, 0)), pl.BlockSpec((1, k), lambda i: (0, 0))],
        out_specs=pl.BlockSpec((tm, k), lambda i: (i, 0)),
        out_shape=jax.ShapeDtypeStruct((m, k), BF16),
        compiler_params=_params("parallel"),
        name="rms_norm",
    )(x, nw.reshape(1, k))


def _cast_pad_kernel(w_ref, o_ref, *, valid):
    col = pl.program_id(0) * o_ref.shape[1] + lax.broadcasted_iota(jnp.int32, o_ref.shape, 1)
    o_ref[...] = jnp.where(col < valid, w_ref[...], 0.0).astype(o_ref.dtype)


def cast_pad_bf16(w, width, tn=512):
    k, n = w.shape
    return pl.pallas_call(
        functools.partial(_cast_pad_kernel, valid=n),
        grid=(width // tn,),
        in_specs=[pl.BlockSpec((k, tn), lambda j: (0, j))],
        out_specs=pl.BlockSpec((k, tn), lambda j: (0, j)),
        out_shape=jax.ShapeDtypeStruct((k, width), BF16),
        compiler_params=_params("parallel"),
        name="cast_pad",
    )(w)


def _mm_kernel(h_ref, w_ref, o_ref):
    o_ref[...] = _dot(h_ref[...], w_ref[...]).astype(o_ref.dtype)


def matmul(h, w_bf16, col0, n, out_dtype, tm=2048, tn=512):
    m, k = h.shape
    j0 = col0 // tn
    return pl.pallas_call(
        _mm_kernel,
        grid=(m // tm, n // tn),
        in_specs=[
            pl.BlockSpec((tm, k), lambda i, j: (i, 0)),
            pl.BlockSpec((k, tn), lambda i, j: (0, j0 + j)),
        ],
        out_specs=pl.BlockSpec((tm, tn), lambda i, j: (i, j)),
        out_shape=jax.ShapeDtypeStruct((m, n), out_dtype),
        compiler_params=_params("parallel", "arbitrary"),
        name="matmul",
    )(h, w_bf16)


def _mm_rope_kernel(h_ref, w_ref, pos_ref, freq_ref, slo_ref, shi_ref, o_ref,
                    c_scr, lo_scr, hi_scr, *, rope_tiles, q_tiles):
    j = pl.program_id(1)

    @pl.when(j == 0)
    def _():
        ang = pos_ref[...].astype(F32) * freq_ref[...]
        s = jnp.sin(ang)
        c_scr[...] = jnp.cos(ang)
        lo_scr[...] = s * slo_ref[...]
        hi_scr[...] = s * shi_ref[...]

    acc = _dot(h_ref[...], w_ref[...])

    @pl.when(j < rope_tiles)
    def _():
        scale = jnp.where(j < q_tiles, ATT_QK_DIM ** -0.5 * math.log2(math.e), 1.0)
        half = ROPE_DIM // 2
        for g in range(acc.shape[1] // LANES):
            x = acc[:, g * LANES:(g + 1) * LANES]
            x_dn = pltpu.roll(x, half, axis=1)
            x_up = pltpu.roll(x, LANES - half, axis=1)
            y = x * c_scr[...] + x_dn * hi_scr[...] + x_up * lo_scr[...]
            o_ref[:, g * LANES:(g + 1) * LANES] = (y * scale).astype(o_ref.dtype)

    @pl.when(j >= rope_tiles)
    def _():
        o_ref[...] = acc.astype(o_ref.dtype)


def matmul_rope(h, w_bf16, pos, tm=2048, tn=512):
    m, k = h.shape
    n = ATT_COLS
    lane = jnp.arange(LANES) % ATT_QK_DIM
    half = ROPE_DIM // 2
    inv_freq = ROPE_THETA ** (-jnp.arange(0, ROPE_DIM, 2, dtype=F32) / ROPE_DIM)
    freq = jnp.where(lane < ROPE_DIM, inv_freq[lane % half], 0.0).astype(F32).reshape(1, LANES)
    sgn_lo = jnp.where(lane < half, -1.0, 0.0).astype(F32).reshape(1, LANES)
    sgn_hi = jnp.where((lane >= half) & (lane < ROPE_DIM), 1.0, 0.0).astype(F32).reshape(1, LANES)
    vec = pl.BlockSpec((1, LANES), lambda i, j: (0, 0))
    return pl.pallas_call(
        functools.partial(_mm_rope_kernel, rope_tiles=2 * ATT_QK_COLS // tn, q_tiles=ATT_QK_COLS // tn),
        grid=(m // tm, n // tn),
        in_specs=[
            pl.BlockSpec((tm, k), lambda i, j: (i, 0)),
            pl.BlockSpec((k, tn), lambda i, j: (0, j)),
            pl.BlockSpec((tm, 1), lambda i, j: (i, 0)),
            vec, vec, vec,
        ],
        out_specs=pl.BlockSpec((tm, tn), lambda i, j: (i, j)),
        out_shape=jax.ShapeDtypeStruct((m, n), BF16),
        scratch_shapes=[pltpu.VMEM((tm, LANES), F32), pltpu.VMEM((tm, LANES), F32),
                        pltpu.VMEM((tm, LANES), F32)],
        compiler_params=_params("parallel", "arbitrary"),
        name="matmul_rope",
    )(h, w_bf16, pos, freq, sgn_lo, sgn_hi)


def _attn_kernel(q_ref, qn_ref, k_ref, v_ref, lq1_ref, lk1_ref, lq2_ref, lk2_ref, sw_ref, o_ref,
                 v_scr, sa_scr, ma_scr, sb_scr, mb_scr, *, lambda_init):
    dv = ATT_V_DIM
    i = pl.program_id(2)

    def scores(q, s_scr, m_scr):
        for c in range(2):
            cols = slice(c * ATT_QK_DIM, (c + 1) * ATT_QK_DIM)
            s = _dot_nt(q[:, cols], k_ref[:, cols])
            s_scr[c] = s
            m_scr[c] = jnp.max(s, axis=-1, keepdims=True)

    @pl.when(i == 0)
    def _():
        lane = lax.broadcasted_iota(jnp.int32, (v_scr.shape[0], dv), 1)
        v_scr[:, :dv] = v_ref[...]
        v_scr[:, dv:] = (lane == 0).astype(BF16)
        scores(q_ref[...], sa_scr, ma_scr)

    def step(cur_s, cur_m, nxt_s, nxt_m):
        scores(qn_ref[...], nxt_s, nxt_m)
        va = v_scr[...]
        o1 = _dot(jnp.exp2(cur_s[0] - cur_m[0]).astype(BF16), va)
        o2 = _dot(jnp.exp2(cur_s[1] - cur_m[1]).astype(BF16), va)
        lam = (jnp.exp(jnp.sum(lq1_ref[...] * lk1_ref[...], axis=-1, keepdims=True))
               - jnp.exp(jnp.sum(lq2_ref[...] * lk2_ref[...], axis=-1, keepdims=True)) + lambda_init)
        o = o1[:, :dv] / o1[:, dv:dv + 1] - o2[:, :dv] * (lam / o2[:, dv:dv + 1])
        ms = jnp.mean(o * o, axis=-1, keepdims=True)
        o = o * lax.rsqrt(ms + SUBLN_EPS) * sw_ref[...] * (1.0 - lambda_init)
        o_ref[...] = o.astype(o_ref.dtype)

    @pl.when(i % 2 == 0)
    def _():
        step(sa_scr, ma_scr, sb_scr, mb_scr)

    @pl.when(i % 2 == 1)
    def _():
        step(sb_scr, mb_scr, sa_scr, ma_scr)


def attention(qkv, lq1, lk1, lq2, lk2, subln_w, bsz, seq, lambda_init, tq=512):
    nq = seq // tq
    vec = lambda n: pl.BlockSpec((1, n), lambda b, h, i: (0, 0))
    nh = ATT_HEADS
    s_buf = pltpu.VMEM((2, tq, seq), F32)
    m_buf = pltpu.VMEM((2, tq, 1), F32)
    return pl.pallas_call(
        functools.partial(_attn_kernel, lambda_init=lambda_init),
        grid=(bsz, nh, nq),
        in_specs=[
            pl.BlockSpec((tq, LANES), lambda b, h, i: (b * nq + i, h)),
            pl.BlockSpec((tq, LANES), lambda b, h, i: (b * nq + jnp.minimum(i + 1, nq - 1), h)),
            pl.BlockSpec((seq, LANES), lambda b, h, i: (b, nh + h)),
            pl.BlockSpec((seq, LANES), lambda b, h, i: (b, 2 * nh + h)),
            vec(ATT_QK_DIM), vec(ATT_QK_DIM), vec(ATT_QK_DIM), vec(ATT_QK_DIM),
            vec(ATT_V_DIM),
        ],
        out_specs=pl.BlockSpec((tq, LANES), lambda b, h, i: (b * nq + i, h)),
        out_shape=jax.ShapeDtypeStruct((bsz * seq, ATT_WIDTH), BF16),
        scratch_shapes=[pltpu.VMEM((seq, 2 * ATT_V_DIM), BF16), s_buf, m_buf, s_buf, m_buf],
        compiler_params=_params("parallel", "parallel", "arbitrary"),
        name="diff_attention",
    )(qkv, qkv, qkv, qkv,
      lq1.reshape(1, -1), lk1.reshape(1, -1), lq2.reshape(1, -1), lk2.reshape(1, -1),
      subln_w.reshape(1, -1))


HALO = 8


def _token_shift(z_ref, prev_ref, next_ref, mu_p, mu_n, first, last):
    z = z_ref[...]
    n = z.shape[0]
    row = lax.broadcasted_iota(jnp.int32, z.shape, 0)
    before = jnp.where(first, 0.0, prev_ref[HALO - 1:HALO, :])
    after = jnp.where(last, 0.0, next_ref[0:1, :])
    zp = jnp.where(row == 0, before, pltpu.roll(z, 1, axis=0))
    zn = jnp.where(row == n - 1, after, pltpu.roll(z, n - 1, axis=0))
    return z + mu_p * (zp - z) + mu_n * (zn - z)


def _seg_sum(x, ones_blk):
    parts = []
    for j in range(x.shape[1] // LANES):
        parts.append(_dot_lhs2(x[:, j * LANES:(j + 1) * LANES], ones_blk))
    return jnp.concatenate(parts, axis=1)


def _prep_kernel(r_ref, k_ref, v_ref, lo_ref, rp_ref, kp_ref, vp_ref, lp_ref, rn_ref, kn_ref, vn_ref,
                 ln_ref, mup_ref, mun_ref, w0_ref, du_ref, a0_ref, iu_ref, gu_ref, kk_ref,
                 ka_ref, rk_ref, ones_ref, tril_ref, triu_ref,
                 ab_ref, rb_ref, bt_ref, kt_ref, pl_ref, vb_ref, bonus_ref, g_ref, *, tt):
    first = pl.program_id(1) == 0
    last = pl.program_id(1) == pl.num_programs(1) - 1
    c = RWKV_WIDTH
    shift = lambda z, p, n, lo_col, hi_col: _token_shift(
        z, p, n, mup_ref[:, lo_col:hi_col], mun_ref[:, lo_col:hi_col], first, last)
    r = shift(r_ref, rp_ref, rn_ref, 0, c)
    k = shift(k_ref, kp_ref, kn_ref, c, 2 * c)
    v = shift(v_ref, vp_ref, vn_ref, 2 * c, 3 * c)
    lo = shift(lo_ref, lp_ref, ln_ref, 3 * c, 3 * c + LORA_PAD)
    ones_blk = ones_ref[...]
    vb_ref[...] = v.astype(BF16)
    g_ref[...] = _dot(jax.nn.sigmoid(lo).astype(BF16), gu_ref[...])
    kk = k * kk_ref[...]
    kk = kk * jnp.minimum(lax.rsqrt(_seg_sum(kk * kk, ones_blk)), 1e12)
    th = jnp.tanh(lo).astype(BF16)
    lob = lo.astype(BF16)
    nchunk = tt // CHUNK
    ksum = jnp.zeros_like(k)
    for d in range(2):
        wl = w0_ref[d] + _dot(th, du_ref[d])
        lw = -math.exp(-0.5) * jax.nn.sigmoid(wl)
        a = jax.nn.sigmoid(a0_ref[d] + _dot(lob, iu_ref[d]))
        kd = k * (1.0 + (a - 1.0) * ka_ref[...])
        ksum = ksum + kd
        tri = tril_ref[...] if d == 0 else triu_ref[...]
        cum = _dot_lhs2_rhs(tri, lw)
        e_pos = jnp.exp(cum)
        e_neg = jnp.exp(-cum)
        ab_ref[d] = (-kk * jnp.exp(cum - lw)).astype(BF16)
        rb_ref[d] = (r * e_pos).astype(BF16)
        bt_ref[d] = (kk * a * e_neg).astype(BF16)
        kt_ref[d] = (kd * e_neg).astype(BF16)
        for c in range(nchunk):
            last = c * CHUNK + (CHUNK - 1 if d == 0 else 0)
            pl_ref[d, c] = e_pos[last:last + 1, :]
    bonus_ref[...] = _seg_sum(r * ksum * rk_ref[...], ones_blk) * v


def _dot_lhs2_rhs(tri_bf16, x):
    hi, lo = _split2(x)
    return _dot(tri_bf16, hi) + _dot(tri_bf16, lo)


def rwkv_prep(pz, mu_prev, mu_next, w0, decay_up, a0, iclr_up, gate_up, k_k, k_a, r_k, bsz, seq, tt=256):
    m = bsz * seq
    c = RWKV_WIDTH
    nt = seq // tt
    ncb = tt // CHUNK
    nc = seq // CHUNK

    def pad_rows(w, start):
        out = jnp.zeros(w.shape[:-2] + (LORA_PAD, c), F32)
        return lax.dynamic_update_slice_in_dim(out, w.astype(F32), start, axis=w.ndim - 2).astype(BF16)

    du = pad_rows(decay_up, 0)
    iu = pad_rows(iclr_up, DECAY_LORA)
    gu = pad_rows(gate_up, DECAY_LORA + ICLR_LORA)
    lane = jnp.arange(LANES)
    ones_blk = (lane[:, None] // RWKV_HEAD == lane[None, :] // RWKV_HEAD).astype(BF16)
    t = jnp.arange(tt)
    same = t[:, None] // CHUNK == t[None, :] // CHUNK
    tril = (same & (t[:, None] >= t[None, :])).astype(BF16)
    triu = (same & (t[:, None] <= t[None, :])).astype(BF16)
    r_k_flat = r_k.reshape(1, c)
    row = lambda: pl.BlockSpec((1, c), lambda b, i: (0, 0))
    full3 = lambda s: pl.BlockSpec(s, lambda b, i: (0, 0, 0))
    tok = lambda j, w: pl.BlockSpec((tt, w), lambda b, i: (b * nt + i, j))
    per = tt // HALO
    prev = lambda j, w: pl.BlockSpec((HALO, w), lambda b, i: (jnp.maximum((b * nt + i) * per - 1, 0), j))
    nxt = lambda j, w: pl.BlockSpec(
        (HALO, w), lambda b, i: (jnp.minimum((b * nt + i + 1) * per, m // HALO - 1), j))
    lora_j = 3 * c // LORA_PAD
    cols = [(0, c), (1, c), (2, c), (lora_j, LORA_PAD)]
    mu_row = pl.BlockSpec((1, pz.shape[1]), lambda b, i: (0, 0))
    dir_tok = pl.BlockSpec((2, tt, c), lambda b, i: (0, b * nt + i, 0))
    outs = pl.pallas_call(
        functools.partial(_prep_kernel, tt=tt),
        grid=(bsz, nt),
        in_specs=[tok(j, w) for j, w in cols] + [prev(j, w) for j, w in cols]
        + [nxt(j, w) for j, w in cols] + [
            mu_row, mu_row,
            full3((2, 1, c)), full3((2, LORA_PAD, c)), full3((2, 1, c)), full3((2, LORA_PAD, c)),
            pl.BlockSpec((LORA_PAD, c), lambda b, i: (0, 0)),
            row(), row(), row(),
            pl.BlockSpec((LANES, LANES), lambda b, i: (0, 0)),
            pl.BlockSpec((tt, tt), lambda b, i: (0, 0)),
            pl.BlockSpec((tt, tt), lambda b, i: (0, 0)),
        ],
        out_specs=[
            dir_tok, dir_tok, dir_tok, dir_tok,
            pl.BlockSpec((2, ncb, 1, c), lambda b, i: (0, b * nt + i, 0, 0)),
            pl.BlockSpec((tt, c), lambda b, i: (b * nt + i, 0)),
            pl.BlockSpec((tt, c), lambda b, i: (b * nt + i, 0)),
            pl.BlockSpec((tt, c), lambda b, i: (b * nt + i, 0)),
        ],
        out_shape=[
            jax.ShapeDtypeStruct((2, m, c), BF16),
            jax.ShapeDtypeStruct((2, m, c), BF16),
            jax.ShapeDtypeStruct((2, m, c), BF16),
            jax.ShapeDtypeStruct((2, m, c), BF16),
            jax.ShapeDtypeStruct((2, bsz * nc, 1, c), F32),
            jax.ShapeDtypeStruct((m, c), BF16),
            jax.ShapeDtypeStruct((m, c), F32),
            jax.ShapeDtypeStruct((m, c), F32),
        ],
        compiler_params=_params("parallel", "parallel"),
        name="rwkv_prep",
    )(*([pz] * 12), mu_prev, mu_next, w0.reshape(2, 1, c), du, a0.reshape(2, 1, c), iu, gu,
      k_k.reshape(1, c), k_a.reshape(1, c), r_k_flat, ones_blk, tril, triu)
    return outs


def _scan_kernel(abf_ref, abb_ref, rbf_ref, rbb_ref, btf_ref, btb_ref, ktf_ref, ktb_ref,
                 vf_ref, vb_ref, plf_ref, plb_ref, yf_ref, yb_ref, m_scr, *, heads):
    @pl.when(pl.program_id(1) == 0)
    def _():
        m_scr[...] = jnp.zeros_like(m_scr)

    ab_ref = (abf_ref, abb_ref)
    rb_ref = (rbf_ref, rbb_ref)
    bt_ref = (btf_ref, btb_ref)
    kt_ref = (ktf_ref, ktb_ref)
    v_ref = (vf_ref, vb_ref)
    pl_ref = (plf_ref, plb_ref)
    y_ref = (yf_ref, yb_ref)
    n = CHUNK
    row = lax.broadcasted_iota(jnp.int32, (2 * n, 2 * n), 0)
    col = lax.broadcasted_iota(jnp.int32, (2 * n, 2 * n), 1)
    top = row < n
    tr = row % n
    tc = col % n
    eye = (lax.broadcasted_iota(jnp.int32, (n, n), 0)
           == lax.broadcasted_iota(jnp.int32, (n, n), 1)).astype(F32)
    keep = ((tr > tc) | (~top & (tr == tc)), (tr < tc) | (~top & (tr == tc)))
    probs = [(d, hh) for d in range(2) for hh in range(heads)]
    sl = lambda hh: slice(hh * n, (hh + 1) * n)
    ab = [ab_ref[d][0, :, sl(hh)] for d, hh in probs]
    rb = [rb_ref[d][0, :, sl(hh)] for d, hh in probs]
    bt = [bt_ref[d][0, :, sl(hh)] for d, hh in probs]
    kt = [kt_ref[d][0, :, sl(hh)] for d, hh in probs]
    vv = [v_ref[d][:, sl(hh)] for d, hh in probs]
    p_last = [pl_ref[d][0, 0, :, sl(hh)] for d, hh in probs]
    idx = range(len(probs))
    g1 = [_dot_nt(jnp.concatenate([ab[i], rb[i]], axis=0), jnp.concatenate([bt[i], kt[i]], axis=0))
          for i in idx]
    g1 = [jnp.where(keep[probs[i][0]], g1[i], 0.0) for i in idx]
    a_ab = [g[:n, :n] for g in g1]
    low = [g[n:, :].astype(BF16) for g in g1]
    tm = [eye + a for a in a_ab]
    pw = [_dot(a.astype(BF16), a.astype(BF16)) for a in a_ab]
    span = 2
    while span < n:
        last = span * 2 >= n
        nxt = []
        for i in idx:
            pwb = pw[i].astype(BF16)
            if last:
                nxt.append((tm[i] + _dot(tm[i].astype(BF16), pwb), None))
            else:
                both = _dot(jnp.concatenate([tm[i], pw[i]], axis=0).astype(BF16), pwb)
                nxt.append((tm[i] + both[:n], both[n:]))
        tm = [t for t, _ in nxt]
        pw = [p for _, p in nxt]
        span *= 2
    wc = [_dot(tm[i].astype(BF16), jnp.concatenate([ab[i], g1[i][:n, n:].astype(BF16)], axis=1)) for i in idx]
    w = [x[:, :n].astype(BF16) for x in wc]
    cuv = [_dot(wc[i][:, n:].astype(BF16), vv[i]).astype(BF16) for i in idx]
    zero = jnp.zeros((n, n), BF16)
    rhs = [jnp.concatenate([jnp.concatenate([w[i], cuv[i]], axis=1),
                            jnp.concatenate([zero, vv[i]], axis=1)], axis=0) for i in idx]
    bh_t = [(bt[i].astype(F32) * p_last[i]).T for i in idx]
    kh_t = [(kt[i].astype(F32) * p_last[i]).T for i in idx]
    lhs = [jnp.concatenate([low[i], jnp.concatenate([bh_t[i], kh_t[i]], axis=1).astype(BF16)], axis=0)
           for i in idx]
    out = [_dot(lhs[i], rhs[i]) for i in idx]
    q_m = [rb[i].astype(F32) + out[i][:n, :n] for i in idx]
    g_m = [eye * p_last[i] + out[i][n:, :n] for i in idx]
    upd = [_dot(jnp.concatenate([q_m[i], g_m[i]], axis=0).astype(BF16), m_scr[d, hh].astype(BF16))
           for i, (d, hh) in enumerate(probs)]
    for i, (d, hh) in enumerate(probs):
        y_ref[d][:, sl(hh)] = upd[i][:n] + out[i][:n, n:]
        m_scr[d, hh] = upd[i][n:] + out[i][n:, n:]


def rwkv_scan(ab, rb, bt, kt, vb, p_last, bsz, seq):
    nc = seq // CHUNK
    nh = RWKV_HEADS
    n = CHUNK
    c_w = RWKV_WIDTH
    fwd = lambda b, c: b * nc + c
    bwd = lambda b, c: b * nc + nc - 1 - c
    dir_f = pl.BlockSpec((1, n, c_w), lambda b, c: (0, fwd(b, c), 0))
    dir_b = pl.BlockSpec((1, n, c_w), lambda b, c: (1, bwd(b, c), 0))
    tok_f = pl.BlockSpec((n, c_w), lambda b, c: (fwd(b, c), 0))
    tok_b = pl.BlockSpec((n, c_w), lambda b, c: (bwd(b, c), 0))
    pl_f = pl.BlockSpec((1, 1, 1, c_w), lambda b, c: (0, fwd(b, c), 0, 0))
    pl_b = pl.BlockSpec((1, 1, 1, c_w), lambda b, c: (1, bwd(b, c), 0, 0))
    y_shape = jax.ShapeDtypeStruct((bsz * seq, c_w), F32)
    return pl.pallas_call(
        functools.partial(_scan_kernel, heads=nh),
        grid=(bsz, nc),
        in_specs=[dir_f, dir_b, dir_f, dir_b, dir_f, dir_b, dir_f, dir_b, tok_f, tok_b, pl_f, pl_b],
        out_specs=[tok_f, tok_b],
        out_shape=[y_shape, y_shape],
        scratch_shapes=[pltpu.VMEM((2, nh, n, n), F32)],
        compiler_params=_params("parallel", "arbitrary"),
        name="rwkv_scan",
    )(ab, ab, rb, rb, bt, bt, kt, kt, vb, vb, p_last, p_last)


def _post_kernel(yf_ref, yb_ref, bonus_ref, g_ref, lw_ref, lb_ref, o_ref):
    y = yf_ref[...] + yb_ref[...]
    n = RWKV_HEAD
    parts = []
    for h in range(RWKV_HEADS):
        yh = y[:, h * n:(h + 1) * n]
        mu = jnp.mean(yh, axis=-1, keepdims=True)
        yc = yh - mu
        var = jnp.mean(yc * yc, axis=-1, keepdims=True)
        parts.append(yc * lax.rsqrt(var + GN_EPS))
    yn = jnp.concatenate(parts, axis=1)
    o_ref[...] = ((yn * lw_ref[...] + lb_ref[...] + bonus_ref[...]) * g_ref[...]).astype(o_ref.dtype)


def rwkv_post(y_fwd, y_bwd, bonus, g, ln_w, ln_b, tt=256):
    m = y_fwd.shape[0]
    c = RWKV_WIDTH
    tok = pl.BlockSpec((tt, c), lambda i: (i, 0))
    row = pl.BlockSpec((1, c), lambda i: (0, 0))
    return pl.pallas_call(
        _post_kernel,
        grid=(m // tt,),
        in_specs=[tok, tok, tok, tok, row, row],
        out_specs=tok,
        out_shape=jax.ShapeDtypeStruct((m, c), BF16),
        compiler_params=_params("parallel"),
        name="rwkv_post",
    )(y_fwd, y_bwd, bonus, g, ln_w.reshape(1, c), ln_b.reshape(1, c))


def _out_router_kernel(x_ref, a_ref, r_ref, wa_ref, wr_ref, nw_ref, wh_ref, wl_ref,
                       x1_ref, hb_ref, afft_ref):
    x = x_ref[...] + _dot(a_ref[...], wa_ref[...]) + _dot(r_ref[...], wr_ref[...])
    x1_ref[...] = x
    ms = jnp.mean(x * x, axis=-1, keepdims=True)
    h = x * lax.rsqrt(ms + NORM_EPS) * nw_ref[...]
    hb_ref[...] = h.astype(BF16)
    hi, lo = _split2(h)
    logits = _dot(hi, wh_ref[...]) + _dot(lo, wh_ref[...]) + _dot(hi, wl_ref[...])
    lane = lax.broadcasted_iota(jnp.int32, logits.shape, 1)
    valid = lane < N_EXPERTS
    logits = jnp.where(valid, logits, -1e30)
    e = jnp.where(valid, jnp.exp(logits - jnp.max(logits, axis=-1, keepdims=True)), 0.0)
    aff = e / jnp.sum(e, axis=-1, keepdims=True)
    afft_ref[...] = aff.T[:N_EXPERTS, :]


def out_proj_router(x, att, rw, w_att, w_rw, norm_w, w_router, tm=512):
    m, k = x.shape
    wpad = jnp.zeros((k, LANES), F32).at[:, :N_EXPERTS].set(w_router)
    wh, wl = _split2(wpad)
    full = lambda r, c: pl.BlockSpec((r, c), lambda i: (0, 0))
    return pl.pallas_call(
        _out_router_kernel,
        grid=(m // tm,),
        in_specs=[
            pl.BlockSpec((tm, k), lambda i: (i, 0)),
            pl.BlockSpec((tm, ATT_WIDTH), lambda i: (i, 0)),
            pl.BlockSpec((tm, RWKV_WIDTH), lambda i: (i, 0)),
            full(ATT_WIDTH, k), full(RWKV_WIDTH, k), full(1, k), full(k, LANES), full(k, LANES),
        ],
        out_specs=[
            pl.BlockSpec((tm, k), lambda i: (i, 0)),
            pl.BlockSpec((tm, k), lambda i: (i, 0)),
            pl.BlockSpec((N_EXPERTS, tm), lambda i: (0, i)),
        ],
        out_shape=[
            jax.ShapeDtypeStruct((m, k), F32),
            jax.ShapeDtypeStruct((m, k), BF16),
            jax.ShapeDtypeStruct((N_EXPERTS, m), F32),
        ],
        compiler_params=_params("parallel"),
        name="out_proj_router",
    )(x, att, rw, w_att, w_rw, norm_w.reshape(1, k), wh, wl)


def _prefix_count(x, upper):
    outs = []
    carry = jnp.zeros((x.shape[0], 1), F32)
    for j in range(x.shape[1] // LANES):
        xt = x[:, j * LANES:(j + 1) * LANES]
        outs.append(_dot(xt.astype(BF16), upper) + carry)
        carry = carry + jnp.sum(xt, axis=-1, keepdims=True)
    return jnp.concatenate(outs, axis=1)


BISECT_STEPS = 152


def _topk_kernel(afft_ref, sp_ref, spt_ref, *, cap, bsz):
    ne = afft_ref.shape[0]
    seq = afft_ref.shape[1] // bsz
    a = [afft_ref[:, b * seq:(b + 1) * seq] for b in range(bsz)]

    def halve(_, bracket):
        out = []
        for b in range(bsz):
            lo, hi = bracket[b]
            mid = (lo + hi) * 0.5
            cnt = jnp.sum((a[b] >= mid).astype(F32), axis=-1, keepdims=True)
            enough = cnt >= cap
            out.append((jnp.where(enough, mid, lo), jnp.where(enough, hi, mid)))
        return tuple(out)

    start = tuple((jnp.zeros((ne, 1), F32), jnp.full((ne, 1), 2.0, F32)) for _ in range(bsz))
    bracket = lax.fori_loop(0, BISECT_STEPS, halve, start)
    r_i = lax.broadcasted_iota(jnp.int32, (LANES, LANES), 0)
    c_i = lax.broadcasted_iota(jnp.int32, (LANES, LANES), 1)
    upper = (r_i < c_i).astype(BF16)
    for b in range(bsz):
        lo, hi = bracket[b]
        above = (a[b] >= hi).astype(F32)
        tied = ((a[b] >= lo) & (a[b] < hi)).astype(F32)
        need = cap - jnp.sum(above, axis=-1, keepdims=True)
        sel = above + tied * (_prefix_count(tied, upper) < need).astype(F32)
        spt = jnp.where(sel > 0.5, _prefix_count(sel, upper), -1.0)
        spt_ref[:, b * seq:(b + 1) * seq] = spt
        full = jnp.concatenate([spt, jnp.full((LANES - ne, seq), -1.0, F32)], axis=0)
        sp_ref[b * seq:(b + 1) * seq, :] = full.T


def topk_select(afft, bsz, seq, cap):
    return pl.pallas_call(
        functools.partial(_topk_kernel, cap=cap, bsz=bsz),
        grid=(1,),
        in_specs=[pl.BlockSpec((N_EXPERTS, bsz * seq), lambda i: (0, 0))],
        out_specs=[
            pl.BlockSpec((bsz * seq, LANES), lambda i: (0, 0)),
            pl.BlockSpec((N_EXPERTS, bsz * seq), lambda i: (0, 0)),
        ],
        out_shape=[
            jax.ShapeDtypeStruct((bsz * seq, LANES), F32),
            jax.ShapeDtypeStruct((N_EXPERTS, bsz * seq), F32),
        ],
        compiler_params=_params("arbitrary"),
        name="topk_select",
    )(afft)


def _gather_kernel(spt_ref, afft_ref, h_ref, xe_ref, gate_ref, *, cap):
    e = pl.program_id(1)
    seq = h_ref.shape[0]
    sp = spt_ref[pl.ds(e, 1), :]
    slot = lax.broadcasted_iota(jnp.int32, (cap, seq), 0).astype(F32)
    hit = sp == slot
    xe_ref[0] = _dot(hit.astype(BF16), h_ref[...]).astype(BF16)
    gate_ref[0] = jnp.sum(jnp.where(hit, afft_ref[pl.ds(e, 1), :], 0.0), axis=-1, keepdims=True)


def moe_gather(spt, afft, hb, bsz, seq, cap):
    d = hb.shape[1]
    return pl.pallas_call(
        functools.partial(_gather_kernel, cap=cap),
        grid=(bsz, N_EXPERTS),
        in_specs=[
            pl.BlockSpec((N_EXPERTS, seq), lambda b, e: (0, b)),
            pl.BlockSpec((N_EXPERTS, seq), lambda b, e: (0, b)),
            pl.BlockSpec((seq, d), lambda b, e: (b, 0)),
        ],
        out_specs=[
            pl.BlockSpec((1, cap, d), lambda b, e: (e, b, 0)),
            pl.BlockSpec((1, cap, 1), lambda b, e: (e, b, 0)),
        ],
        out_shape=[
            jax.ShapeDtypeStruct((N_EXPERTS, bsz * cap, d), BF16),
            jax.ShapeDtypeStruct((N_EXPERTS, bsz * cap, 1), F32),
        ],
        compiler_params=_params("parallel", "arbitrary"),
        name="moe_gather",
    )(spt, afft, hb)


def _expert_kernel(xe_ref, gt_ref, wg_ref, wu_ref, wd_ref, ye_ref, acc_scr):
    f = pl.program_id(1)

    @pl.when(f == 0)
    def _():
        acc_scr[...] = jnp.zeros_like(acc_scr)

    xe = xe_ref[0]
    gate = _dot(xe, wg_ref[0].astype(BF16))
    up = _dot(xe, wu_ref[0].astype(BF16))
    hid = (gate * jax.nn.sigmoid(gate) * up).astype(BF16)
    acc_scr[...] += _dot(hid, wd_ref[0].astype(BF16))

    @pl.when(f == pl.num_programs(1) - 1)
    def _():
        ye_ref[0] = (acc_scr[...] * gt_ref[0]).astype(ye_ref.dtype)


def moe_experts(xe, gates, e_gate, e_up, e_down, tf=512):
    ne, rows, d = xe.shape
    ff = e_gate.shape[2]
    return pl.pallas_call(
        _expert_kernel,
        grid=(ne, ff // tf),
        in_specs=[
            pl.BlockSpec((1, rows, d), lambda e, f: (e, 0, 0)),
            pl.BlockSpec((1, rows, 1), lambda e, f: (e, 0, 0)),
            pl.BlockSpec((1, d, tf), lambda e, f: (e, 0, f)),
            pl.BlockSpec((1, d, tf), lambda e, f: (e, 0, f)),
            pl.BlockSpec((1, tf, d), lambda e, f: (e, f, 0)),
        ],
        out_specs=pl.BlockSpec((1, rows, d), lambda e, f: (e, 0, 0)),
        out_shape=jax.ShapeDtypeStruct((ne, rows, d), BF16),
        scratch_shapes=[pltpu.VMEM((rows, d), F32)],
        compiler_params=_params("parallel", "arbitrary"),
        name="moe_experts",
    )(xe, gates, e_gate, e_up, e_down)


def _combine_kernel(x_ref, sp_ref, ye_ref, nw_ref, o_ref, acc_scr, *, cap, group):
    eg = pl.program_id(2)

    @pl.when(eg == 0)
    def _():
        acc_scr[...] = x_ref[...]

    tt = x_ref.shape[0]
    lane = lax.broadcasted_iota(jnp.int32, (tt, LANES), 1)
    slot = lax.broadcasted_iota(jnp.int32, (tt, cap), 1).astype(F32)
    sp_all = sp_ref[...]
    hits = []
    for k in range(group):
        sp = jnp.sum(jnp.where(lane == eg * group + k, sp_all, 0.0), axis=-1, keepdims=True)
        hits.append((sp == slot).astype(BF16))
    onehot = jnp.concatenate(hits, axis=1)
    acc_scr[...] += _dot(onehot, ye_ref[...].reshape(group * cap, ye_ref.shape[2]))

    @pl.when(eg == pl.num_programs(2) - 1)
    def _():
        y = acc_scr[...]
        ms = jnp.mean(y * y, axis=-1, keepdims=True)
        o_ref[...] = y * lax.rsqrt(ms + NORM_EPS) * nw_ref[...]


def moe_combine(x1, sp, ye, norm_w, bsz, seq, cap, tt=512, group=4):
    m, d = x1.shape
    nt = seq // tt
    return pl.pallas_call(
        functools.partial(_combine_kernel, cap=cap, group=group),
        grid=(bsz, nt, N_EXPERTS // group),
        in_specs=[
            pl.BlockSpec((tt, d), lambda b, i, e: (b * nt + i, 0)),
            pl.BlockSpec((tt, LANES), lambda b, i, e: (b * nt + i, 0)),
            pl.BlockSpec((group, cap, d), lambda b, i, e: (e, b, 0)),
            pl.BlockSpec((1, d), lambda b, i, e: (0, 0)),
        ],
        out_specs=pl.BlockSpec((tt, d), lambda b, i, e: (b * nt + i, 0)),
        out_shape=jax.ShapeDtypeStruct((m, d), F32),
        scratch_shapes=[pltpu.VMEM((tt, d), F32)],
        compiler_params=_params("parallel", "parallel", "arbitrary"),
        name="moe_combine",
    )(x1, sp, ye, norm_w.reshape(1, d))


def _pad_cols(w, width):
    return jnp.pad(w, [(0, 0)] * (w.ndim - 1) + [(0, width - w.shape[-1])])


def rwkv_mixer(pz, mu_prev, mu_next, w0, decay_up, a0, iclr_up, gate_up, k_k, k_a, r_k, ln_x_w, ln_x_b,
               bsz, seq):
    ab, rb, bt, kt, p_last, vb, bonus, g = rwkv_prep(
        pz, mu_prev, mu_next, w0, decay_up, a0, iclr_up, gate_up, k_k, k_a, r_k, bsz, seq)
    y_fwd, y_bwd = rwkv_scan(ab, rb, bt, kt, vb, p_last, bsz, seq)
    return rwkv_post(y_fwd, y_bwd, bonus, g, ln_x_w, ln_x_b)


def moe_block(x1, hb, afft, e_gate, e_up, e_down, norm_f_w, bsz, seq):
    cap = CAPACITY_FACTOR * seq // N_EXPERTS
    sp, spt = topk_select(afft, bsz, seq, cap)
    xe, gates = moe_gather(spt, afft, hb, bsz, seq, cap)
    ye = moe_experts(xe, gates, e_gate, e_up, e_down)
    return moe_combine(x1, sp, ye, norm_f_w, bsz, seq, cap)


def kernel(x, positions, norm1_w, w_in, mu_prev, mu_next, lambda_q1, lambda_k1, lambda_q2, lambda_k2, subln_w, w0, decay_up, a0, iclr_up, gate_up, k_k, k_a, r_k, ln_x_w, ln_x_b, w_out, norm2_w, w_router, e_gate, e_up, e_down, norm_f_w):
    bsz, seq, d = x.shape
    m = bsz * seq
    xf = x.reshape(m, d)
    pos = positions.reshape(m, 1)
    lambda_init = 0.8 - 0.6 * math.exp(-0.3 * 0)

    w_all = cast_pad_bf16(w_in[0], ATT_COLS + SHIFT_PAD)
    h = rms_norm_bf16(xf, norm1_w[0])
    qkv = matmul_rope(h, w_all, pos)
    p_z = matmul(h, w_all, ATT_COLS, SHIFT_PAD, F32)

    att = attention(qkv, lambda_q1[0], lambda_k1[0], lambda_q2[0], lambda_k2[0],
                    subln_w[0], bsz, seq, lambda_init)

    rw = rwkv_mixer(p_z, _pad_cols(mu_prev, SHIFT_PAD), _pad_cols(mu_next, SHIFT_PAD),
                    w0[0], decay_up[0], a0[0], iclr_up[0], gate_up[0], k_k[0], k_a[0], r_k[0],
                    ln_x_w[0], ln_x_b[0], bsz, seq)

    wo = w_out[0].astype(BF16)
    x1, hb, afft = out_proj_router(xf, att, rw, wo[:ATT_WIDTH], wo[ATT_WIDTH:], norm2_w[0], w_router[0])

    out = moe_block(x1, hb, afft, e_gate[0], e_up[0], e_down[0], norm_f_w, bsz, seq)
    return out.reshape(bsz, seq, d)
```

```python
import functools
import math

import jax
import jax.numpy as jnp
from jax import lax
from jax.experimental import pallas as pl
from jax.experimental.pallas import tpu as pltpu

F32 = jnp.float32
BF16 = jnp.bfloat16

LANES = 128
VMEM_LIMIT_BYTES = 56 * 1024 * 1024

D_MODEL = 2048
NORM_EPS = 1e-6
ATT_HEADS = 8
ATT_QK_DIM = 64
ATT_V_DIM = 128
ATT_WIDTH = ATT_HEADS * ATT_V_DIM
ATT_QK_COLS = ATT_HEADS * 2 * ATT_QK_DIM
ATT_COLS = 2 * ATT_QK_COLS + ATT_WIDTH
ROPE_THETA = 500000.0
ROPE_DIM = ATT_QK_DIM // 4
SUBLN_EPS = 1e-5
RWKV_WIDTH = 1024
RWKV_HEAD = 64
RWKV_HEADS = RWKV_WIDTH // RWKV_HEAD
DECAY_LORA = 64
ICLR_LORA = 64
GATE_LORA = 160
LORA_COLS = DECAY_LORA + ICLR_LORA + GATE_LORA
LORA_PAD = 384
GN_EPS = 64e-5
SHIFT_WIDTH = 3 * RWKV_WIDTH + LORA_COLS
SHIFT_PAD = 3 * RWKV_WIDTH + 512
N_EXPERTS = 16
CAPACITY_FACTOR = 2
EXPERT_FF = 2048
CHUNK = 64


def _params(*sem):
    return pltpu.CompilerParams(dimension_semantics=sem, vmem_limit_bytes=VMEM_LIMIT_BYTES)


def _dot(a, b):
    return jnp.dot(a, b, preferred_element_type=F32)


def _dot_nt(a, b):
    return lax.dot_general(a, b, (((1,), (1,)), ((), ())), preferred_element_type=F32)


def _split2(x):
    hi = x.astype(BF16)
    lo = (x - hi.astype(F32)).astype(BF16)
    return hi, lo


def _dot_lhs2(x, m_bf16):
    hi, lo = _split2(x)
    return _dot(hi, m_bf16) + _dot(lo, m_bf16)


def _rms_norm_kernel(x_ref, nw_ref, o_ref):
    x = x_ref[...]
    ms = jnp.mean(x * x, axis=-1, keepdims=True)
    o_ref[...] = (x * lax.rsqrt(ms + NORM_EPS) * nw_ref[...]).astype(o_ref.dtype)


def rms_norm_bf16(x, nw, tm=512):
    m, k = x.shape
    return pl.pallas_call(
        _rms_norm_kernel,
        grid=(m // tm,),
        in_specs=[pl.BlockSpec((tm, k), lambda i: (i, 0)), pl.BlockSpec((1, k), lambda i: (0, 0))],
        out_specs=pl.BlockSpec((tm, k), lambda i: (i, 0)),
        out_shape=jax.ShapeDtypeStruct((m, k), BF16),
        compiler_params=_params("parallel"),
        name="rms_norm",
    )(x, nw.reshape(1, k))


def _mm_kernel(h_ref, wt_ref, o_ref, *, j0, valid):
    tn = o_ref.shape[1]
    col = (j0 + pl.program_id(1)) * tn + lax.broadcasted_iota(jnp.int32, wt_ref.shape[1:], 0)
    wt = jnp.where(col < valid, wt_ref[0], 0.0).astype(BF16)
    o_ref[...] = _dot_nt(h_ref[...], wt).astype(o_ref.dtype)


def matmul(h, wt, col0, n, out_dtype, tm=2048, tn=512):
    m, k = h.shape
    j0 = col0 // tn
    return pl.pallas_call(
        functools.partial(_mm_kernel, j0=j0, valid=wt.shape[1]),
        grid=(m // tm, n // tn),
        in_specs=[
            pl.BlockSpec((tm, k), lambda i, j: (i, 0)),
            pl.BlockSpec((1, tn, k), lambda i, j: (0, j0 + j, 0)),
        ],
        out_specs=pl.BlockSpec((tm, tn), lambda i, j: (i, j)),
        out_shape=jax.ShapeDtypeStruct((m, n), out_dtype),
        compiler_params=_params("parallel", "arbitrary"),
        name="matmul",
    )(h, wt)


def _mm_rope_kernel(h_ref, wt_ref, pos_ref, freq_ref, slo_ref, shi_ref, o_ref,
                    c_scr, lo_scr, hi_scr, *, rope_tiles, q_tiles):
    j = pl.program_id(1)

    @pl.when(j == 0)
    def _():
        ang = pos_ref[...].astype(F32) * freq_ref[...]
        s = jnp.sin(ang)
        c_scr[...] = jnp.cos(ang)
        lo_scr[...] = s * slo_ref[...]
        hi_scr[...] = s * shi_ref[...]

    acc = _dot_nt(h_ref[...], wt_ref[0].astype(BF16))

    @pl.when(j < rope_tiles)
    def _():
        scale = jnp.where(j < q_tiles, ATT_QK_DIM ** -0.5 * math.log2(math.e), 1.0)
        half = ROPE_DIM // 2
        for g in range(acc.shape[1] // LANES):
            x = acc[:, g * LANES:(g + 1) * LANES]
            x_dn = pltpu.roll(x, half, axis=1)
            x_up = pltpu.roll(x, LANES - half, axis=1)
            y = x * c_scr[...] + x_dn * hi_scr[...] + x_up * lo_scr[...]
            o_ref[:, g * LANES:(g + 1) * LANES] = (y * scale).astype(o_ref.dtype)

    @pl.when(j >= rope_tiles)
    def _():
        o_ref[...] = acc.astype(o_ref.dtype)


def matmul_rope(h, wt, pos, tm=2048, tn=512):
    m, k = h.shape
    n = ATT_COLS
    lane = jnp.arange(LANES) % ATT_QK_DIM
    half = ROPE_DIM // 2
    inv_freq = ROPE_THETA ** (-jnp.arange(0, ROPE_DIM, 2, dtype=F32) / ROPE_DIM)
    freq = jnp.where(lane < ROPE_DIM, inv_freq[lane % half], 0.0).astype(F32).reshape(1, LANES)
    sgn_lo = jnp.where(lane < half, -1.0, 0.0).astype(F32).reshape(1, LANES)
    sgn_hi = jnp.where((lane >= half) & (lane < ROPE_DIM), 1.0, 0.0).astype(F32).reshape(1, LANES)
    vec = pl.BlockSpec((1, LANES), lambda i, j: (0, 0))
    return pl.pallas_call(
        functools.partial(_mm_rope_kernel, rope_tiles=2 * ATT_QK_COLS // tn, q_tiles=ATT_QK_COLS // tn),
        grid=(m // tm, n // tn),
        in_specs=[
            pl.BlockSpec((tm, k), lambda i, j: (i, 0)),
            pl.BlockSpec((1, tn, k), lambda i, j: (0, j, 0)),
            pl.BlockSpec((tm, 1), lambda i, j: (i, 0)),
            vec, vec, vec,
        ],
        out_specs=pl.BlockSpec((tm, tn), lambda i, j: (i, j)),
        out_shape=jax.ShapeDtypeStruct((m, n), BF16),
        scratch_shapes=[pltpu.VMEM((tm, LANES), F32), pltpu.VMEM((tm, LANES), F32),
                        pltpu.VMEM((tm, LANES), F32)],
        compiler_params=_params("parallel", "arbitrary"),
        name="matmul_rope",
    )(h, wt, pos, freq, sgn_lo, sgn_hi)


def _attn_kernel(q_ref, qn_ref, k_ref, v_ref, lq1_ref, lk1_ref, lq2_ref, lk2_ref, sw_ref, o_ref,
                 v_scr, sa_scr, ma_scr, sb_scr, mb_scr, *, lambda_init):
    dv = ATT_V_DIM
    i = pl.program_id(2)

    def scores(q, s_scr, m_scr):
        for c in range(2):
            cols = slice(c * ATT_QK_DIM, (c + 1) * ATT_QK_DIM)
            s = _dot_nt(q[:, cols], k_ref[:, cols])
            s_scr[c] = s
            m_scr[c] = jnp.max(s, axis=-1, keepdims=True)

    @pl.when(i == 0)
    def _():
        lane = lax.broadcasted_iota(jnp.int32, (v_scr.shape[0], dv), 1)
        v_scr[:, :dv] = v_ref[...]
        v_scr[:, dv:] = (lane == 0).astype(BF16)
        scores(q_ref[...], sa_scr, ma_scr)

    def step(cur_s, cur_m, nxt_s, nxt_m):
        scores(qn_ref[...], nxt_s, nxt_m)
        va = v_scr[...]
        o1 = _dot(jnp.exp2(cur_s[0] - cur_m[0]).astype(BF16), va)
        o2 = _dot(jnp.exp2(cur_s[1] - cur_m[1]).astype(BF16), va)
        lam = (jnp.exp(jnp.sum(lq1_ref[...] * lk1_ref[...], axis=-1, keepdims=True))
               - jnp.exp(jnp.sum(lq2_ref[...] * lk2_ref[...], axis=-1, keepdims=True)) + lambda_init)
        o = o1[:, :dv] / o1[:, dv:dv + 1] - o2[:, :dv] * (lam / o2[:, dv:dv + 1])
        ms = jnp.mean(o * o, axis=-1, keepdims=True)
        o = o * lax.rsqrt(ms + SUBLN_EPS) * sw_ref[...] * (1.0 - lambda_init)
        o_ref[...] = o.astype(o_ref.dtype)

    @pl.when(i % 2 == 0)
    def _():
        step(sa_scr, ma_scr, sb_scr, mb_scr)

    @pl.when(i % 2 == 1)
    def _():
        step(sb_scr, mb_scr, sa_scr, ma_scr)


def attention(qkv, lq1, lk1, lq2, lk2, subln_w, bsz, seq, lambda_init, tq=256):
    nq = seq // tq
    vec = lambda n: pl.BlockSpec((1, n), lambda b, h, i: (0, 0))
    nh = ATT_HEADS
    s_buf = pltpu.VMEM((2, tq, seq), F32)
    m_buf = pltpu.VMEM((2, tq, 1), F32)
    return pl.pallas_call(
        functools.partial(_attn_kernel, lambda_init=lambda_init),
        grid=(bsz, nh, nq),
        in_specs=[
            pl.BlockSpec((tq, LANES), lambda b, h, i: (b * nq + i, h)),
            pl.BlockSpec((tq, LANES), lambda b, h, i: (b * nq + jnp.minimum(i + 1, nq - 1), h)),
            pl.BlockSpec((seq, LANES), lambda b, h, i: (b, nh + h)),
            pl.BlockSpec((seq, LANES), lambda b, h, i: (b, 2 * nh + h)),
            vec(ATT_QK_DIM), vec(ATT_QK_DIM), vec(ATT_QK_DIM), vec(ATT_QK_DIM),
            vec(ATT_V_DIM),
        ],
        out_specs=pl.BlockSpec((tq, LANES), lambda b, h, i: (b * nq + i, h)),
        out_shape=jax.ShapeDtypeStruct((bsz * seq, ATT_WIDTH), BF16),
        scratch_shapes=[pltpu.VMEM((seq, 2 * ATT_V_DIM), BF16), s_buf, m_buf, s_buf, m_buf],
        compiler_params=_params("parallel", "parallel", "arbitrary"),
        name="diff_attention",
    )(qkv, qkv, qkv, qkv,
      lq1.reshape(1, -1), lk1.reshape(1, -1), lq2.reshape(1, -1), lk2.reshape(1, -1),
      subln_w.reshape(1, -1))


HALO = 8


def _token_shift(z_ref, prev_ref, next_ref, mu_p, mu_n, first, last):
    z = z_ref[...]
    n = z.shape[0]
    row = lax.broadcasted_iota(jnp.int32, z.shape, 0)
    before = jnp.where(first, 0.0, prev_ref[HALO - 1:HALO, :])
    after = jnp.where(last, 0.0, next_ref[0:1, :])
    zp = jnp.where(row == 0, before, pltpu.roll(z, 1, axis=0))
    zn = jnp.where(row == n - 1, after, pltpu.roll(z, n - 1, axis=0))
    return z + mu_p * (zp - z) + mu_n * (zn - z)


def _seg_sum(x, ones_blk):
    parts = []
    for j in range(x.shape[1] // LANES):
        parts.append(_dot_lhs2(x[:, j * LANES:(j + 1) * LANES], ones_blk))
    return jnp.concatenate(parts, axis=1)


def _prep_kernel(r_ref, k_ref, v_ref, lo_ref, rp_ref, kp_ref, vp_ref, lp_ref, rn_ref, kn_ref, vn_ref,
                 ln_ref, mup_ref, mun_ref, w0_ref, du_ref, a0_ref, iu_ref, gu_ref, kk_ref,
                 ka_ref, rk_ref, ones_ref, tril_ref, triu_ref,
                 ab_ref, rb_ref, bt_ref, kt_ref, pl_ref, vb_ref, bonus_ref, g_ref, *, tt):
    first = pl.program_id(1) == 0
    last = pl.program_id(1) == pl.num_programs(1) - 1
    c = RWKV_WIDTH
    shift = lambda z, p, n, lo_col, hi_col: _token_shift(
        z, p, n, mup_ref[:, lo_col:hi_col], mun_ref[:, lo_col:hi_col], first, last)
    r = shift(r_ref, rp_ref, rn_ref, 0, c)
    k = shift(k_ref, kp_ref, kn_ref, c, 2 * c)
    v = shift(v_ref, vp_ref, vn_ref, 2 * c, 3 * c)
    lo = shift(lo_ref, lp_ref, ln_ref, 3 * c, 3 * c + LORA_PAD)
    ones_blk = ones_ref[...]
    vb_ref[...] = v.astype(BF16)
    g_ref[...] = _dot(jax.nn.sigmoid(lo).astype(BF16), gu_ref[...])
    kk = k * kk_ref[...]
    kk = kk * jnp.minimum(lax.rsqrt(_seg_sum(kk * kk, ones_blk)), 1e12)
    th = jnp.tanh(lo).astype(BF16)
    lob = lo.astype(BF16)
    nchunk = tt // CHUNK
    ksum = jnp.zeros_like(k)
    for d in range(2):
        wl = w0_ref[d] + _dot(th, du_ref[d])
        lw = -math.exp(-0.5) * jax.nn.sigmoid(wl)
        a = jax.nn.sigmoid(a0_ref[d] + _dot(lob, iu_ref[d]))
        kd = k * (1.0 + (a - 1.0) * ka_ref[...])
        ksum = ksum + kd
        tri = tril_ref[...] if d == 0 else triu_ref[...]
        cum = _dot_lhs2_rhs(tri, lw)
        e_pos = jnp.exp(cum)
        e_neg = jnp.exp(-cum)
        ab_ref[d] = (-kk * jnp.exp(cum - lw)).astype(BF16)
        rb_ref[d] = (r * e_pos).astype(BF16)
        bt_ref[d] = (kk * a * e_neg).astype(BF16)
        kt_ref[d] = (kd * e_neg).astype(BF16)
        for c in range(nchunk):
            last = c * CHUNK + (CHUNK - 1 if d == 0 else 0)
            pl_ref[d, c] = e_pos[last:last + 1, :]
    bonus_ref[...] = _seg_sum(r * ksum * rk_ref[...], ones_blk) * v


def _dot_lhs2_rhs(tri_bf16, x):
    hi, lo = _split2(x)
    return _dot(tri_bf16, hi) + _dot(tri_bf16, lo)


def rwkv_prep(pz, mu_prev, mu_next, w0, decay_up, a0, iclr_up, gate_up, k_k, k_a, r_k, bsz, seq, tt=256):
    m = bsz * seq
    c = RWKV_WIDTH
    nt = seq // tt
    ncb = tt // CHUNK
    nc = seq // CHUNK

    def pad_rows(w, start):
        out = jnp.zeros(w.shape[:-2] + (LORA_PAD, c), F32)
        return lax.dynamic_update_slice_in_dim(out, w.astype(F32), start, axis=w.ndim - 2).astype(BF16)

    du = pad_rows(decay_up, 0)
    iu = pad_rows(iclr_up, DECAY_LORA)
    gu = pad_rows(gate_up, DECAY_LORA + ICLR_LORA)
    lane = jnp.arange(LANES)
    ones_blk = (lane[:, None] // RWKV_HEAD == lane[None, :] // RWKV_HEAD).astype(BF16)
    t = jnp.arange(tt)
    same = t[:, None] // CHUNK == t[None, :] // CHUNK
    tril = (same & (t[:, None] >= t[None, :])).astype(BF16)
    triu = (same & (t[:, None] <= t[None, :])).astype(BF16)
    r_k_flat = r_k.reshape(1, c)
    row = lambda: pl.BlockSpec((1, c), lambda b, i: (0, 0))
    full3 = lambda s: pl.BlockSpec(s, lambda b, i: (0, 0, 0))
    tok = lambda j, w: pl.BlockSpec((tt, w), lambda b, i: (b * nt + i, j))
    per = tt // HALO
    prev = lambda j, w: pl.BlockSpec((HALO, w), lambda b, i: (jnp.maximum((b * nt + i) * per - 1, 0), j))
    nxt = lambda j, w: pl.BlockSpec(
        (HALO, w), lambda b, i: (jnp.minimum((b * nt + i + 1) * per, m // HALO - 1), j))
    lora_j = 3 * c // LORA_PAD
    cols = [(0, c), (1, c), (2, c), (lora_j, LORA_PAD)]
    mu_row = pl.BlockSpec((1, pz.shape[1]), lambda b, i: (0, 0))
    dir_tok = pl.BlockSpec((2, tt, c), lambda b, i: (0, b * nt + i, 0))
    outs = pl.pallas_call(
        functools.partial(_prep_kernel, tt=tt),
        grid=(bsz, nt),
        in_specs=[tok(j, w) for j, w in cols] + [prev(j, w) for j, w in cols]
        + [nxt(j, w) for j, w in cols] + [
            mu_row, mu_row,
            full3((2, 1, c)), full3((2, LORA_PAD, c)), full3((2, 1, c)), full3((2, LORA_PAD, c)),
            pl.BlockSpec((LORA_PAD, c), lambda b, i: (0, 0)),
            row(), row(), row(),
            pl.BlockSpec((LANES, LANES), lambda b, i: (0, 0)),
            pl.BlockSpec((tt, tt), lambda b, i: (0, 0)),
            pl.BlockSpec((tt, tt), lambda b, i: (0, 0)),
        ],
        out_specs=[
            dir_tok, dir_tok, dir_tok, dir_tok,
            pl.BlockSpec((2, ncb, 1, c), lambda b, i: (0, b * nt + i, 0, 0)),
            pl.BlockSpec((tt, c), lambda b, i: (b * nt + i, 0)),
            pl.BlockSpec((tt, c), lambda b, i: (b * nt + i, 0)),
            pl.BlockSpec((tt, c), lambda b, i: (b * nt + i, 0)),
        ],
        out_shape=[
            jax.ShapeDtypeStruct((2, m, c), BF16),
            jax.ShapeDtypeStruct((2, m, c), BF16),
            jax.ShapeDtypeStruct((2, m, c), BF16),
            jax.ShapeDtypeStruct((2, m, c), BF16),
            jax.ShapeDtypeStruct((2, bsz * nc, 1, c), F32),
            jax.ShapeDtypeStruct((m, c), BF16),
            jax.ShapeDtypeStruct((m, c), F32),
            jax.ShapeDtypeStruct((m, c), F32),
        ],
        compiler_params=_params("parallel", "parallel"),
        name="rwkv_prep",
    )(*([pz] * 12), mu_prev, mu_next, w0.reshape(2, 1, c), du, a0.reshape(2, 1, c), iu, gu,
      k_k.reshape(1, c), k_a.reshape(1, c), r_k_flat, ones_blk, tril, triu)
    return outs


def _scan_kernel(abf_ref, abb_ref, rbf_ref, rbb_ref, btf_ref, btb_ref, ktf_ref, ktb_ref,
                 vf_ref, vb_ref, plf_ref, plb_ref, yf_ref, yb_ref, m_scr, *, heads):
    @pl.when(pl.program_id(1) == 0)
    def _():
        m_scr[...] = jnp.zeros_like(m_scr)

    ab_ref = (abf_ref, abb_ref)
    rb_ref = (rbf_ref, rbb_ref)
    bt_ref = (btf_ref, btb_ref)
    kt_ref = (ktf_ref, ktb_ref)
    v_ref = (vf_ref, vb_ref)
    pl_ref = (plf_ref, plb_ref)
    y_ref = (yf_ref, yb_ref)
    n = CHUNK
    row = lax.broadcasted_iota(jnp.int32, (2 * n, 2 * n), 0)
    col = lax.broadcasted_iota(jnp.int32, (2 * n, 2 * n), 1)
    top = row < n
    tr = row % n
    tc = col % n
    eye = (lax.broadcasted_iota(jnp.int32, (n, n), 0)
           == lax.broadcasted_iota(jnp.int32, (n, n), 1)).astype(F32)
    keep = ((tr > tc) | (~top & (tr == tc)), (tr < tc) | (~top & (tr == tc)))
    probs = [(d, hh) for d in range(2) for hh in range(heads)]
    sl = lambda hh: slice(hh * n, (hh + 1) * n)
    ab = [ab_ref[d][0, :, sl(hh)] for d, hh in probs]
    rb = [rb_ref[d][0, :, sl(hh)] for d, hh in probs]
    bt = [bt_ref[d][0, :, sl(hh)] for d, hh in probs]
    kt = [kt_ref[d][0, :, sl(hh)] for d, hh in probs]
    vv = [v_ref[d][:, sl(hh)] for d, hh in probs]
    p_last = [pl_ref[d][0, 0, :, sl(hh)] for d, hh in probs]
    idx = range(len(probs))
    g1 = [_dot_nt(jnp.concatenate([ab[i], rb[i]], axis=0), jnp.concatenate([bt[i], kt[i]], axis=0))
          for i in idx]
    g1 = [jnp.where(keep[probs[i][0]], g1[i], 0.0) for i in idx]
    a_ab = [g[:n, :n] for g in g1]
    low = [g[n:, :].astype(BF16) for g in g1]
    tm = [eye + a for a in a_ab]
    pw = [_dot(a.astype(BF16), a.astype(BF16)) for a in a_ab]
    span = 2
    while span < n:
        last = span * 2 >= n
        nxt = []
        for i in idx:
            pwb = pw[i].astype(BF16)
            if last:
                nxt.append((tm[i] + _dot(tm[i].astype(BF16), pwb), None))
            else:
                both = _dot(jnp.concatenate([tm[i], pw[i]], axis=0).astype(BF16), pwb)
                nxt.append((tm[i] + both[:n], both[n:]))
        tm = [t for t, _ in nxt]
        pw = [p for _, p in nxt]
        span *= 2
    wc = [_dot(tm[i].astype(BF16), jnp.concatenate([ab[i], g1[i][:n, n:].astype(BF16)], axis=1)) for i in idx]
    w = [x[:, :n].astype(BF16) for x in wc]
    cuv = [_dot(wc[i][:, n:].astype(BF16), vv[i]).astype(BF16) for i in idx]
    zero = jnp.zeros((n, n), BF16)
    rhs = [jnp.concatenate([jnp.concatenate([w[i], cuv[i]], axis=1),
                            jnp.concatenate([zero, vv[i]], axis=1)], axis=0) for i in idx]
    bh_t = [(bt[i].astype(F32) * p_last[i]).T for i in idx]
    kh_t = [(kt[i].astype(F32) * p_last[i]).T for i in idx]
    lhs = [jnp.concatenate([low[i], jnp.concatenate([bh_t[i], kh_t[i]], axis=1).astype(BF16)], axis=0)
           for i in idx]
    out = [_dot(lhs[i], rhs[i]) for i in idx]
    q_m = [rb[i].astype(F32) + out[i][:n, :n] for i in idx]
    g_m = [eye * p_last[i] + out[i][n:, :n] for i in idx]
    upd = [_dot(jnp.concatenate([q_m[i], g_m[i]], axis=0).astype(BF16), m_scr[d, hh].astype(BF16))
           for i, (d, hh) in enumerate(probs)]
    for i, (d, hh) in enumerate(probs):
        y_ref[d][:, sl(hh)] = upd[i][:n] + out[i][:n, n:]
        m_scr[d, hh] = upd[i][n:] + out[i][n:, n:]


def rwkv_scan(ab, rb, bt, kt, vb, p_last, bsz, seq):
    nc = seq // CHUNK
    nh = RWKV_HEADS
    n = CHUNK
    c_w = RWKV_WIDTH
    fwd = lambda b, c: b * nc + c
    bwd = lambda b, c: b * nc + nc - 1 - c
    dir_f = pl.BlockSpec((1, n, c_w), lambda b, c: (0, fwd(b, c), 0))
    dir_b = pl.BlockSpec((1, n, c_w), lambda b, c: (1, bwd(b, c), 0))
    tok_f = pl.BlockSpec((n, c_w), lambda b, c: (fwd(b, c), 0))
    tok_b = pl.BlockSpec((n, c_w), lambda b, c: (bwd(b, c), 0))
    pl_f = pl.BlockSpec((1, 1, 1, c_w), lambda b, c: (0, fwd(b, c), 0, 0))
    pl_b = pl.BlockSpec((1, 1, 1, c_w), lambda b, c: (1, bwd(b, c), 0, 0))
    y_shape = jax.ShapeDtypeStruct((bsz * seq, c_w), F32)
    return pl.pallas_call(
        functools.partial(_scan_kernel, heads=nh),
        grid=(bsz, nc),
        in_specs=[dir_f, dir_b, dir_f, dir_b, dir_f, dir_b, dir_f, dir_b, tok_f, tok_b, pl_f, pl_b],
        out_specs=[tok_f, tok_b],
        out_shape=[y_shape, y_shape],
        scratch_shapes=[pltpu.VMEM((2, nh, n, n), F32)],
        compiler_params=_params("parallel", "arbitrary"),
        name="rwkv_scan",
    )(ab, ab, rb, rb, bt, bt, kt, kt, vb, vb, p_last, p_last)


def _post_kernel(yf_ref, yb_ref, bonus_ref, g_ref, lw_ref, lb_ref, o_ref):
    y = yf_ref[...] + yb_ref[...]
    n = RWKV_HEAD
    parts = []
    for h in range(RWKV_HEADS):
        yh = y[:, h * n:(h + 1) * n]
        mu = jnp.mean(yh, axis=-1, keepdims=True)
        yc = yh - mu
        var = jnp.mean(yc * yc, axis=-1, keepdims=True)
        parts.append(yc * lax.rsqrt(var + GN_EPS))
    yn = jnp.concatenate(parts, axis=1)
    o_ref[...] = ((yn * lw_ref[...] + lb_ref[...] + bonus_ref[...]) * g_ref[...]).astype(o_ref.dtype)


def rwkv_post(y_fwd, y_bwd, bonus, g, ln_w, ln_b, tt=256):
    m = y_fwd.shape[0]
    c = RWKV_WIDTH
    tok = pl.BlockSpec((tt, c), lambda i: (i, 0))
    row = pl.BlockSpec((1, c), lambda i: (0, 0))
    return pl.pallas_call(
        _post_kernel,
        grid=(m // tt,),
        in_specs=[tok, tok, tok, tok, row, row],
        out_specs=tok,
        out_shape=jax.ShapeDtypeStruct((m, c), BF16),
        compiler_params=_params("parallel"),
        name="rwkv_post",
    )(y_fwd, y_bwd, bonus, g, ln_w.reshape(1, c), ln_b.reshape(1, c))


def _out_proj_kernel(x_ref, a_ref, r_ref, wa_ref, wr_ref, o_ref):
    o_ref[...] = x_ref[...] + _dot(a_ref[...], wa_ref[...]) + _dot(r_ref[...], wr_ref[...])


def out_proj(x, att, rw, w_att, w_rw, tm=2048, tn=512):
    m, n = x.shape
    return pl.pallas_call(
        _out_proj_kernel,
        grid=(m // tm, n // tn),
        in_specs=[
            pl.BlockSpec((tm, tn), lambda i, j: (i, j)),
            pl.BlockSpec((tm, ATT_WIDTH), lambda i, j: (i, 0)),
            pl.BlockSpec((tm, RWKV_WIDTH), lambda i, j: (i, 0)),
            pl.BlockSpec((ATT_WIDTH, tn), lambda i, j: (0, j)),
            pl.BlockSpec((RWKV_WIDTH, tn), lambda i, j: (0, j)),
        ],
        out_specs=pl.BlockSpec((tm, tn), lambda i, j: (i, j)),
        out_shape=jax.ShapeDtypeStruct((m, n), F32),
        compiler_params=_params("parallel", "parallel"),
        name="out_proj",
    )(x, att, rw, w_att, w_rw)


def _router_kernel(x_ref, nw_ref, wh_ref, wl_ref, hb_ref, afft_ref):
    x = x_ref[...]
    ms = jnp.mean(x * x, axis=-1, keepdims=True)
    h = x * lax.rsqrt(ms + NORM_EPS) * nw_ref[...]
    hb_ref[...] = h.astype(BF16)
    hi, lo = _split2(h)
    logits = _dot(hi, wh_ref[...]) + _dot(lo, wh_ref[...]) + _dot(hi, wl_ref[...])
    lane = lax.broadcasted_iota(jnp.int32, logits.shape, 1)
    valid = lane < N_EXPERTS
    logits = jnp.where(valid, logits, -1e30)
    e = jnp.where(valid, jnp.exp(logits - jnp.max(logits, axis=-1, keepdims=True)), 0.0)
    aff = e / jnp.sum(e, axis=-1, keepdims=True)
    afft_ref[...] = aff.T[:N_EXPERTS, :]


def router(x1, norm_w, w_router, tm=512):
    m, k = x1.shape
    wpad = jnp.zeros((k, LANES), F32).at[:, :N_EXPERTS].set(w_router)
    wh, wl = _split2(wpad)
    return pl.pallas_call(
        _router_kernel,
        grid=(m // tm,),
        in_specs=[
            pl.BlockSpec((tm, k), lambda i: (i, 0)),
            pl.BlockSpec((1, k), lambda i: (0, 0)),
            pl.BlockSpec((k, LANES), lambda i: (0, 0)),
            pl.BlockSpec((k, LANES), lambda i: (0, 0)),
        ],
        out_specs=[
            pl.BlockSpec((tm, k), lambda i: (i, 0)),
            pl.BlockSpec((N_EXPERTS, tm), lambda i: (0, i)),
        ],
        out_shape=[
            jax.ShapeDtypeStruct((m, k), BF16),
            jax.ShapeDtypeStruct((N_EXPERTS, m), F32),
        ],
        compiler_params=_params("parallel"),
        name="router",
    )(x1, norm_w.reshape(1, k), wh, wl)


def _prefix_count(x, upper):
    outs = []
    carry = jnp.zeros((x.shape[0], 1), F32)
    for j in range(x.shape[1] // LANES):
        xt = x[:, j * LANES:(j + 1) * LANES]
        outs.append(_dot(xt.astype(BF16), upper) + carry)
        carry = carry + jnp.sum(xt, axis=-1, keepdims=True)
    return jnp.concatenate(outs, axis=1)


BISECT_STEPS = 152


def _topk_kernel(afft_ref, sp_ref, spt_ref, *, cap, bsz):
    ne = afft_ref.shape[0]
    seq = afft_ref.shape[1] // bsz
    a = [afft_ref[:, b * seq:(b + 1) * seq] for b in range(bsz)]

    def halve(_, bracket):
        out = []
        for b in range(bsz):
            lo, hi = bracket[b]
            mid = (lo + hi) * 0.5
            cnt = jnp.sum((a[b] >= mid).astype(F32), axis=-1, keepdims=True)
            enough = cnt >= cap
            out.append((jnp.where(enough, mid, lo), jnp.where(enough, hi, mid)))
        return tuple(out)

    start = tuple((jnp.zeros((ne, 1), F32), jnp.full((ne, 1), 2.0, F32)) for _ in range(bsz))
    bracket = lax.fori_loop(0, BISECT_STEPS, halve, start)
    r_i = lax.broadcasted_iota(jnp.int32, (LANES, LANES), 0)
    c_i = lax.broadcasted_iota(jnp.int32, (LANES, LANES), 1)
    upper = (r_i < c_i).astype(BF16)
    for b in range(bsz):
        lo, hi = bracket[b]
        above = (a[b] >= hi).astype(F32)
        tied = ((a[b] >= lo) & (a[b] < hi)).astype(F32)
        need = cap - jnp.sum(above, axis=-1, keepdims=True)
        sel = above + tied * (_prefix_count(tied, upper) < need).astype(F32)
        spt = jnp.where(sel > 0.5, _prefix_count(sel, upper), -1.0)
        spt_ref[:, b * seq:(b + 1) * seq] = spt
        full = jnp.concatenate([spt, jnp.full((LANES - ne, seq), -1.0, F32)], axis=0)
        sp_ref[b * seq:(b + 1) * seq, :] = full.T


def topk_select(afft, bsz, seq, cap):
    return pl.pallas_call(
        functools.partial(_topk_kernel, cap=cap, bsz=bsz),
        grid=(1,),
        in_specs=[pl.BlockSpec((N_EXPERTS, bsz * seq), lambda i: (0, 0))],
        out_specs=[
            pl.BlockSpec((bsz * seq, LANES), lambda i: (0, 0)),
            pl.BlockSpec((N_EXPERTS, bsz * seq), lambda i: (0, 0)),
        ],
        out_shape=[
            jax.ShapeDtypeStruct((bsz * seq, LANES), F32),
            jax.ShapeDtypeStruct((N_EXPERTS, bsz * seq), F32),
        ],
        compiler_params=_params("arbitrary"),
        name="topk_select",
    )(afft)


def _gather_kernel(spt_ref, afft_ref, h_ref, xe_ref, gate_ref, *, cap):
    e = pl.program_id(1)
    seq = h_ref.shape[0]
    sp = spt_ref[pl.ds(e, 1), :]
    slot = lax.broadcasted_iota(jnp.int32, (cap, seq), 0).astype(F32)
    hit = sp == slot
    xe_ref[0] = _dot(hit.astype(BF16), h_ref[...]).astype(BF16)
    gate_ref[0] = jnp.sum(jnp.where(hit, afft_ref[pl.ds(e, 1), :], 0.0), axis=-1, keepdims=True)


def moe_gather(spt, afft, hb, bsz, seq, cap):
    d = hb.shape[1]
    return pl.pallas_call(
        functools.partial(_gather_kernel, cap=cap),
        grid=(bsz, N_EXPERTS),
        in_specs=[
            pl.BlockSpec((N_EXPERTS, seq), lambda b, e: (0, b)),
            pl.BlockSpec((N_EXPERTS, seq), lambda b, e: (0, b)),
            pl.BlockSpec((seq, d), lambda b, e: (b, 0)),
        ],
        out_specs=[
            pl.BlockSpec((1, cap, d), lambda b, e: (e, b, 0)),
            pl.BlockSpec((1, cap, 1), lambda b, e: (e, b, 0)),
        ],
        out_shape=[
            jax.ShapeDtypeStruct((N_EXPERTS, bsz * cap, d), BF16),
            jax.ShapeDtypeStruct((N_EXPERTS, bsz * cap, 1), F32),
        ],
        compiler_params=_params("parallel", "arbitrary"),
        name="moe_gather",
    )(spt, afft, hb)


def _expert_kernel(xe_ref, gt_ref, wg_ref, wu_ref, wd_ref, ye_ref, acc_scr):
    f = pl.program_id(1)

    @pl.when(f == 0)
    def _():
        acc_scr[...] = jnp.zeros_like(acc_scr)

    xe = xe_ref[0]
    gate = _dot(xe, wg_ref[0].astype(BF16))
    up = _dot(xe, wu_ref[0].astype(BF16))
    hid = (gate * jax.nn.sigmoid(gate) * up).astype(BF16)
    acc_scr[...] += _dot(hid, wd_ref[0].astype(BF16))

    @pl.when(f == pl.num_programs(1) - 1)
    def _():
        ye_ref[0] = (acc_scr[...] * gt_ref[0]).astype(ye_ref.dtype)


def moe_experts(xe, gates, e_gate, e_up, e_down, tf=512):
    ne, rows, d = xe.shape
    ff = e_gate.shape[2]
    return pl.pallas_call(
        _expert_kernel,
        grid=(ne, ff // tf),
        in_specs=[
            pl.BlockSpec((1, rows, d), lambda e, f: (e, 0, 0)),
            pl.BlockSpec((1, rows, 1), lambda e, f: (e, 0, 0)),
            pl.BlockSpec((1, d, tf), lambda e, f: (e, 0, f)),
            pl.BlockSpec((1, d, tf), lambda e, f: (e, 0, f)),
            pl.BlockSpec((1, tf, d), lambda e, f: (e, f, 0)),
        ],
        out_specs=pl.BlockSpec((1, rows, d), lambda e, f: (e, 0, 0)),
        out_shape=jax.ShapeDtypeStruct((ne, rows, d), BF16),
        scratch_shapes=[pltpu.VMEM((rows, d), F32)],
        compiler_params=_params("parallel", "arbitrary"),
        name="moe_experts",
    )(xe, gates, e_gate, e_up, e_down)


def _combine_kernel(x_ref, sp_ref, ye_ref, nw_ref, o_ref, acc_scr, *, cap, group):
    eg = pl.program_id(2)

    @pl.when(eg == 0)
    def _():
        acc_scr[...] = x_ref[...]

    tt = x_ref.shape[0]
    lane = lax.broadcasted_iota(jnp.int32, (tt, LANES), 1)
    slot = lax.broadcasted_iota(jnp.int32, (tt, cap), 1).astype(F32)
    sp_all = sp_ref[...]
    hits = []
    for k in range(group):
        sp = jnp.sum(jnp.where(lane == eg * group + k, sp_all, 0.0), axis=-1, keepdims=True)
        hits.append((sp == slot).astype(BF16))
    onehot = jnp.concatenate(hits, axis=1)
    acc_scr[...] += _dot(onehot, ye_ref[...].reshape(group * cap, ye_ref.shape[2]))

    @pl.when(eg == pl.num_programs(2) - 1)
    def _():
        y = acc_scr[...]
        ms = jnp.mean(y * y, axis=-1, keepdims=True)
        o_ref[...] = y * lax.rsqrt(ms + NORM_EPS) * nw_ref[...]


def moe_combine(x1, sp, ye, norm_w, bsz, seq, cap, tt=512, group=4):
    m, d = x1.shape
    nt = seq // tt
    return pl.pallas_call(
        functools.partial(_combine_kernel, cap=cap, group=group),
        grid=(bsz, nt, N_EXPERTS // group),
        in_specs=[
            pl.BlockSpec((tt, d), lambda b, i, e: (b * nt + i, 0)),
            pl.BlockSpec((tt, LANES), lambda b, i, e: (b * nt + i, 0)),
            pl.BlockSpec((group, cap, d), lambda b, i, e: (e, b, 0)),
            pl.BlockSpec((1, d), lambda b, i, e: (0, 0)),
        ],
        out_specs=pl.BlockSpec((tt, d), lambda b, i, e: (b * nt + i, 0)),
        out_shape=jax.ShapeDtypeStruct((m, d), F32),
        scratch_shapes=[pltpu.VMEM((tt, d), F32)],
        compiler_params=_params("parallel", "parallel", "arbitrary"),
        name="moe_combine",
    )(x1, sp, ye, norm_w.reshape(1, d))


def _pad_cols(w, width):
    return jnp.pad(w, [(0, 0)] * (w.ndim - 1) + [(0, width - w.shape[-1])])


def rwkv_mixer(pz, mu_prev, mu_next, w0, decay_up, a0, iclr_up, gate_up, k_k, k_a, r_k, ln_x_w, ln_x_b,
               bsz, seq):
    ab, rb, bt, kt, p_last, vb, bonus, g = rwkv_prep(
        pz, mu_prev, mu_next, w0, decay_up, a0, iclr_up, gate_up, k_k, k_a, r_k, bsz, seq)
    y_fwd, y_bwd = rwkv_scan(ab, rb, bt, kt, vb, p_last, bsz, seq)
    return rwkv_post(y_fwd, y_bwd, bonus, g, ln_x_w, ln_x_b)


def moe_block(x1, norm2_w, w_router, e_gate, e_up, e_down, norm_f_w, bsz, seq):
    cap = CAPACITY_FACTOR * seq // N_EXPERTS
    hb, afft = router(x1, norm2_w, w_router)
    sp, spt = topk_select(afft, bsz, seq, cap)
    xe, gates = moe_gather(spt, afft, hb, bsz, seq, cap)
    ye = moe_experts(xe, gates, e_gate, e_up, e_down)
    return moe_combine(x1, sp, ye, norm_f_w, bsz, seq, cap)


def kernel(x, positions, norm1_w, w_in, mu_prev, mu_next, lambda_q1, lambda_k1, lambda_q2, lambda_k2, subln_w, w0, decay_up, a0, iclr_up, gate_up, k_k, k_a, r_k, ln_x_w, ln_x_b, w_out, norm2_w, w_router, e_gate, e_up, e_down, norm_f_w):
    bsz, seq, d = x.shape
    m = bsz * seq
    xf = x.reshape(m, d)
    pos = positions.reshape(m, 1)
    lambda_init = 0.8 - 0.6 * math.exp(-0.3 * 0)

    h = rms_norm_bf16(xf, norm1_w[0])
    w_in_t = jnp.swapaxes(w_in, 1, 2)
    qkv = matmul_rope(h, w_in_t, pos)
    p_z = matmul(h, w_in_t, ATT_COLS, SHIFT_PAD, F32)

    att = attention(qkv, lambda_q1[0], lambda_k1[0], lambda_q2[0], lambda_k2[0],
                    subln_w[0], bsz, seq, lambda_init)

    rw = rwkv_mixer(p_z, _pad_cols(mu_prev, SHIFT_PAD), _pad_cols(mu_next, SHIFT_PAD),
                    w0[0], decay_up[0], a0[0], iclr_up[0], gate_up[0], k_k[0], k_a[0], r_k[0],
                    ln_x_w[0], ln_x_b[0], bsz, seq)

    wo = w_out[0].astype(BF16)
    x1 = out_proj(xf, att, rw, wo[:ATT_WIDTH], wo[ATT_WIDTH:])

    out = moe_block(x1, norm2_w[0], w_router[0], e_gate[0], e_up[0], e_down[0], norm_f_w, bsz, seq)
    return out.reshape(bsz, seq, d)
```

```python
import functools
import math

import jax
import jax.numpy as jnp
from jax import lax
from jax.experimental import pallas as pl
from jax.experimental.pallas import tpu as pltpu

F32 = jnp.float32
BF16 = jnp.bfloat16

LANES = 128
VMEM_LIMIT_BYTES = 56 * 1024 * 1024

D_MODEL = 2048
NORM_EPS = 1e-6
ATT_HEADS = 8
ATT_QK_DIM = 64
ATT_V_DIM = 128
ATT_WIDTH = ATT_HEADS * ATT_V_DIM
ATT_QK_COLS = ATT_HEADS * 2 * ATT_QK_DIM
ATT_COLS = 2 * ATT_QK_COLS + ATT_WIDTH
ROPE_THETA = 500000.0
ROPE_DIM = ATT_QK_DIM // 4
SUBLN_EPS = 1e-5
RWKV_WIDTH = 1024
RWKV_HEAD = 64
RWKV_HEADS = RWKV_WIDTH // RWKV_HEAD
DECAY_LORA = 64
ICLR_LORA = 64
GATE_LORA = 160
LORA_COLS = DECAY_LORA + ICLR_LORA + GATE_LORA
LORA_PAD = 384
GN_EPS = 64e-5
SHIFT_WIDTH = 3 * RWKV_WIDTH + LORA_COLS
SHIFT_PAD = 3 * RWKV_WIDTH + 512
N_EXPERTS = 16
CAPACITY_FACTOR = 2
EXPERT_FF = 2048
CHUNK = 64


def _params(*sem):
    return pltpu.CompilerParams(dimension_semantics=sem, vmem_limit_bytes=VMEM_LIMIT_BYTES)


def _dot(a, b):
    return jnp.dot(a, b, preferred_element_type=F32)


def _dot_nt(a, b):
    return lax.dot_general(a, b, (((1,), (1,)), ((), ())), preferred_element_type=F32)


def _split2(x):
    hi = x.astype(BF16)
    lo = (x - hi.astype(F32)).astype(BF16)
    return hi, lo


def _dot_lhs2(x, m_bf16):
    hi, lo = _split2(x)
    return _dot(hi, m_bf16) + _dot(lo, m_bf16)


def _rms_norm_kernel(x_ref, nw_ref, o_ref):
    x = x_ref[...]
    ms = jnp.mean(x * x, axis=-1, keepdims=True)
    o_ref[...] = (x * lax.rsqrt(ms + NORM_EPS) * nw_ref[...]).astype(o_ref.dtype)


def rms_norm_bf16(x, nw, tm=512):
    m, k = x.shape
    return pl.pallas_call(
        _rms_norm_kernel,
        grid=(m // tm,),
        in_specs=[pl.BlockSpec((tm, k), lambda i: (i, 0)), pl.BlockSpec((1, k), lambda i: (0, 0))],
        out_specs=pl.BlockSpec((tm, k), lambda i: (i, 0)),
        out_shape=jax.ShapeDtypeStruct((m, k), BF16),
        compiler_params=_params("parallel"),
        name="rms_norm",
    )(x, nw.reshape(1, k))


def _mm_kernel(h_ref, wt_ref, o_ref, *, j0, valid):
    tn = o_ref.shape[1]
    col = (j0 + pl.program_id(1)) * tn + lax.broadcasted_iota(jnp.int32, wt_ref.shape[1:], 0)
    wt = jnp.where(col < valid, wt_ref[0], 0.0).astype(BF16)
    o_ref[...] = _dot_nt(h_ref[...], wt).astype(o_ref.dtype)


def matmul(h, wt, col0, n, out_dtype, tm=2048, tn=512):
    m, k = h.shape
    j0 = col0 // tn
    return pl.pallas_call(
        functools.partial(_mm_kernel, j0=j0, valid=wt.shape[1]),
        grid=(m // tm, n // tn),
        in_specs=[
            pl.BlockSpec((tm, k), lambda i, j: (i, 0)),
            pl.BlockSpec((1, tn, k), lambda i, j: (0, j0 + j, 0)),
        ],
        out_specs=pl.BlockSpec((tm, tn), lambda i, j: (i, j)),
        out_shape=jax.ShapeDtypeStruct((m, n), out_dtype),
        compiler_params=_params("parallel", "arbitrary"),
        name="matmul",
    )(h, wt)


def _mm_rope_kernel(h_ref, wt_ref, pos_ref, freq_ref, slo_ref, shi_ref, o_ref,
                    c_scr, lo_scr, hi_scr, *, rope_tiles, q_tiles):
    j = pl.program_id(1)

    @pl.when(j == 0)
    def _():
        ang = pos_ref[...].astype(F32) * freq_ref[...]
        s = jnp.sin(ang)
        c_scr[...] = jnp.cos(ang)
        lo_scr[...] = s * slo_ref[...]
        hi_scr[...] = s * shi_ref[...]

    acc = _dot_nt(h_ref[...], wt_ref[0].astype(BF16))

    @pl.when(j < rope_tiles)
    def _():
        scale = jnp.where(j < q_tiles, ATT_QK_DIM ** -0.5 * math.log2(math.e), 1.0)
        half = ROPE_DIM // 2
        for g in range(acc.shape[1] // LANES):
            x = acc[:, g * LANES:(g + 1) * LANES]
            x_dn = pltpu.roll(x, half, axis=1)
            x_up = pltpu.roll(x, LANES - half, axis=1)
            y = x * c_scr[...] + x_dn * hi_scr[...] + x_up * lo_scr[...]
            o_ref[:, g * LANES:(g + 1) * LANES] = (y * scale).astype(o_ref.dtype)

    @pl.when(j >= rope_tiles)
    def _():
        o_ref[...] = acc.astype(o_ref.dtype)


def matmul_rope(h, wt, pos, tm=2048, tn=512):
    m, k = h.shape
    n = ATT_COLS
    lane = jnp.arange(LANES) % ATT_QK_DIM
    half = ROPE_DIM // 2
    inv_freq = ROPE_THETA ** (-jnp.arange(0, ROPE_DIM, 2, dtype=F32) / ROPE_DIM)
    freq = jnp.where(lane < ROPE_DIM, inv_freq[lane % half], 0.0).astype(F32).reshape(1, LANES)
    sgn_lo = jnp.where(lane < half, -1.0, 0.0).astype(F32).reshape(1, LANES)
    sgn_hi = jnp.where((lane >= half) & (lane < ROPE_DIM), 1.0, 0.0).astype(F32).reshape(1, LANES)
    vec = pl.BlockSpec((1, LANES), lambda i, j: (0, 0))
    return pl.pallas_call(
        functools.partial(_mm_rope_kernel, rope_tiles=2 * ATT_QK_COLS // tn, q_tiles=ATT_QK_COLS // tn),
        grid=(m // tm, n // tn),
        in_specs=[
            pl.BlockSpec((tm, k), lambda i, j: (i, 0)),
            pl.BlockSpec((1, tn, k), lambda i, j: (0, j, 0)),
            pl.BlockSpec((tm, 1), lambda i, j: (i, 0)),
            vec, vec, vec,
        ],
        out_specs=pl.BlockSpec((tm, tn), lambda i, j: (i, j)),
        out_shape=jax.ShapeDtypeStruct((m, n), BF16),
        scratch_shapes=[pltpu.VMEM((tm, LANES), F32), pltpu.VMEM((tm, LANES), F32),
                        pltpu.VMEM((tm, LANES), F32)],
        compiler_params=_params("parallel", "arbitrary"),
        name="matmul_rope",
    )(h, wt, pos, freq, sgn_lo, sgn_hi)


def _attn_kernel(q_ref, qn_ref, k_ref, v_ref, lq1_ref, lk1_ref, lq2_ref, lk2_ref, sw_ref, o_ref,
                 v_scr, sa_scr, ma_scr, sb_scr, mb_scr, *, lambda_init, heads):
    dv = ATT_V_DIM
    i = pl.program_id(2)

    def scores(q, s_scr, m_scr):
        for hc in range(2 * heads):
            cols = slice(hc * ATT_QK_DIM, (hc + 1) * ATT_QK_DIM)
            s = _dot_nt(q[:, cols], k_ref[:, cols])
            s_scr[hc] = s
            m_scr[hc] = jnp.max(s, axis=-1, keepdims=True)

    @pl.when(i == 0)
    def _():
        lane = lax.broadcasted_iota(jnp.int32, (v_scr.shape[1], dv), 1)
        for h in range(heads):
            v_scr[h, :, :dv] = v_ref[:, h * dv:(h + 1) * dv]
            v_scr[h, :, dv:] = (lane == 0).astype(BF16)
        scores(q_ref[...], sa_scr, ma_scr)

    def step(cur_s, cur_m, nxt_s, nxt_m):
        scores(qn_ref[...], nxt_s, nxt_m)
        lam = (jnp.exp(jnp.sum(lq1_ref[...] * lk1_ref[...], axis=-1, keepdims=True))
               - jnp.exp(jnp.sum(lq2_ref[...] * lk2_ref[...], axis=-1, keepdims=True)) + lambda_init)
        for h in range(heads):
            va = v_scr[h]
            o1 = _dot(jnp.exp2(cur_s[2 * h] - cur_m[2 * h]).astype(BF16), va)
            o2 = _dot(jnp.exp2(cur_s[2 * h + 1] - cur_m[2 * h + 1]).astype(BF16), va)
            o = o1[:, :dv] / o1[:, dv:dv + 1] - o2[:, :dv] * (lam / o2[:, dv:dv + 1])
            ms = jnp.mean(o * o, axis=-1, keepdims=True)
            o = o * lax.rsqrt(ms + SUBLN_EPS) * sw_ref[...] * (1.0 - lambda_init)
            o_ref[:, h * dv:(h + 1) * dv] = o.astype(o_ref.dtype)

    @pl.when(i % 2 == 0)
    def _():
        step(sa_scr, ma_scr, sb_scr, mb_scr)

    @pl.when(i % 2 == 1)
    def _():
        step(sb_scr, mb_scr, sa_scr, ma_scr)


def attention(qkv, lq1, lk1, lq2, lk2, subln_w, bsz, seq, lambda_init, tq=256, heads=4):
    nq = seq // tq
    vec = lambda n: pl.BlockSpec((1, n), lambda b, h, i: (0, 0))
    ng = ATT_HEADS // heads
    width = heads * LANES
    s_buf = pltpu.VMEM((2 * heads, tq, seq), F32)
    m_buf = pltpu.VMEM((2 * heads, tq, 1), F32)
    return pl.pallas_call(
        functools.partial(_attn_kernel, lambda_init=lambda_init, heads=heads),
        grid=(bsz, ng, nq),
        in_specs=[
            pl.BlockSpec((tq, width), lambda b, h, i: (b * nq + i, h)),
            pl.BlockSpec((tq, width), lambda b, h, i: (b * nq + jnp.minimum(i + 1, nq - 1), h)),
            pl.BlockSpec((seq, width), lambda b, h, i: (b, ng + h)),
            pl.BlockSpec((seq, width), lambda b, h, i: (b, 2 * ng + h)),
            vec(ATT_QK_DIM), vec(ATT_QK_DIM), vec(ATT_QK_DIM), vec(ATT_QK_DIM),
            vec(ATT_V_DIM),
        ],
        out_specs=pl.BlockSpec((tq, width), lambda b, h, i: (b * nq + i, h)),
        out_shape=jax.ShapeDtypeStruct((bsz * seq, ATT_WIDTH), BF16),
        scratch_shapes=[pltpu.VMEM((heads, seq, 2 * ATT_V_DIM), BF16), s_buf, m_buf, s_buf, m_buf],
        compiler_params=_params("parallel", "parallel", "arbitrary"),
        name="diff_attention",
    )(qkv, qkv, qkv, qkv,
      lq1.reshape(1, -1), lk1.reshape(1, -1), lq2.reshape(1, -1), lk2.reshape(1, -1),
      subln_w.reshape(1, -1))


HALO = 8


def _token_shift(z_ref, prev_ref, next_ref, mu_p, mu_n, first, last):
    z = z_ref[...]
    n = z.shape[0]
    row = lax.broadcasted_iota(jnp.int32, z.shape, 0)
    before = jnp.where(first, 0.0, prev_ref[HALO - 1:HALO, :])
    after = jnp.where(last, 0.0, next_ref[0:1, :])
    zp = jnp.where(row == 0, before, pltpu.roll(z, 1, axis=0))
    zn = jnp.where(row == n - 1, after, pltpu.roll(z, n - 1, axis=0))
    return z + mu_p * (zp - z) + mu_n * (zn - z)


def _seg_sum(x, ones_blk):
    parts = []
    for j in range(x.shape[1] // LANES):
        parts.append(_dot_lhs2(x[:, j * LANES:(j + 1) * LANES], ones_blk))
    return jnp.concatenate(parts, axis=1)


def _prep_kernel(r_ref, k_ref, v_ref, lo_ref, rp_ref, kp_ref, vp_ref, lp_ref, rn_ref, kn_ref, vn_ref,
                 ln_ref, mup_ref, mun_ref, w0_ref, du_ref, a0_ref, iu_ref, gu_ref, kk_ref,
                 ka_ref, rk_ref, ones_ref, tril_ref, triu_ref,
                 ab_ref, rb_ref, bt_ref, kt_ref, pl_ref, vb_ref, bonus_ref, g_ref, *, tt):
    first = pl.program_id(1) == 0
    last = pl.program_id(1) == pl.num_programs(1) - 1
    c = RWKV_WIDTH
    shift = lambda z, p, n, lo_col, hi_col: _token_shift(
        z, p, n, mup_ref[:, lo_col:hi_col], mun_ref[:, lo_col:hi_col], first, last)
    r = shift(r_ref, rp_ref, rn_ref, 0, c)
    k = shift(k_ref, kp_ref, kn_ref, c, 2 * c)
    v = shift(v_ref, vp_ref, vn_ref, 2 * c, 3 * c)
    lo = shift(lo_ref, lp_ref, ln_ref, 3 * c, 3 * c + LORA_PAD)
    ones_blk = ones_ref[...]
    vb_ref[...] = v.astype(BF16)
    g_ref[...] = _dot(jax.nn.sigmoid(lo).astype(BF16), gu_ref[...])
    kk = k * kk_ref[...]
    kk = kk * jnp.minimum(lax.rsqrt(_seg_sum(kk * kk, ones_blk)), 1e12)
    th = jnp.tanh(lo).astype(BF16)
    lob = lo.astype(BF16)
    nchunk = tt // CHUNK
    ksum = jnp.zeros_like(k)
    for d in range(2):
        wl = w0_ref[d] + _dot(th, du_ref[d])
        lw = -math.exp(-0.5) * jax.nn.sigmoid(wl)
        a = jax.nn.sigmoid(a0_ref[d] + _dot(lob, iu_ref[d]))
        kd = k * (1.0 + (a - 1.0) * ka_ref[...])
        ksum = ksum + kd
        tri = tril_ref[...] if d == 0 else triu_ref[...]
        cum = _dot_lhs2_rhs(tri, lw)
        e_pos = jnp.exp(cum)
        e_neg = jnp.exp(-cum)
        ab_ref[d] = (-kk * jnp.exp(cum - lw)).astype(BF16)
        rb_ref[d] = (r * e_pos).astype(BF16)
        bt_ref[d] = (kk * a * e_neg).astype(BF16)
        kt_ref[d] = (kd * e_neg).astype(BF16)
        for c in range(nchunk):
            last = c * CHUNK + (CHUNK - 1 if d == 0 else 0)
            pl_ref[d, c] = e_pos[last:last + 1, :]
    bonus_ref[...] = _seg_sum(r * ksum * rk_ref[...], ones_blk) * v


def _dot_lhs2_rhs(tri_bf16, x):
    hi, lo = _split2(x)
    return _dot(tri_bf16, hi) + _dot(tri_bf16, lo)


def rwkv_prep(pz, mu_prev, mu_next, w0, decay_up, a0, iclr_up, gate_up, k_k, k_a, r_k, bsz, seq, tt=256):
    m = bsz * seq
    c = RWKV_WIDTH
    nt = seq // tt
    ncb = tt // CHUNK
    nc = seq // CHUNK

    def pad_rows(w, start):
        out = jnp.zeros(w.shape[:-2] + (LORA_PAD, c), F32)
        return lax.dynamic_update_slice_in_dim(out, w.astype(F32), start, axis=w.ndim - 2).astype(BF16)

    du = pad_rows(decay_up, 0)
    iu = pad_rows(iclr_up, DECAY_LORA)
    gu = pad_rows(gate_up, DECAY_LORA + ICLR_LORA)
    lane = jnp.arange(LANES)
    ones_blk = (lane[:, None] // RWKV_HEAD == lane[None, :] // RWKV_HEAD).astype(BF16)
    t = jnp.arange(tt)
    same = t[:, None] // CHUNK == t[None, :] // CHUNK
    tril = (same & (t[:, None] >= t[None, :])).astype(BF16)
    triu = (same & (t[:, None] <= t[None, :])).astype(BF16)
    r_k_flat = r_k.reshape(1, c)
    row = lambda: pl.BlockSpec((1, c), lambda b, i: (0, 0))
    full3 = lambda s: pl.BlockSpec(s, lambda b, i: (0, 0, 0))
    tok = lambda j, w: pl.BlockSpec((tt, w), lambda b, i: (b * nt + i, j))
    per = tt // HALO
    prev = lambda j, w: pl.BlockSpec((HALO, w), lambda b, i: (jnp.maximum((b * nt + i) * per - 1, 0), j))
    nxt = lambda j, w: pl.BlockSpec(
        (HALO, w), lambda b, i: (jnp.minimum((b * nt + i + 1) * per, m // HALO - 1), j))
    lora_j = 3 * c // LORA_PAD
    cols = [(0, c), (1, c), (2, c), (lora_j, LORA_PAD)]
    mu_row = pl.BlockSpec((1, pz.shape[1]), lambda b, i: (0, 0))
    dir_tok = pl.BlockSpec((2, tt, c), lambda b, i: (0, b * nt + i, 0))
    outs = pl.pallas_call(
        functools.partial(_prep_kernel, tt=tt),
        grid=(bsz, nt),
        in_specs=[tok(j, w) for j, w in cols] + [prev(j, w) for j, w in cols]
        + [nxt(j, w) for j, w in cols] + [
            mu_row, mu_row,
            full3((2, 1, c)), full3((2, LORA_PAD, c)), full3((2, 1, c)), full3((2, LORA_PAD, c)),
            pl.BlockSpec((LORA_PAD, c), lambda b, i: (0, 0)),
            row(), row(), row(),
            pl.BlockSpec((LANES, LANES), lambda b, i: (0, 0)),
            pl.BlockSpec((tt, tt), lambda b, i: (0, 0)),
            pl.BlockSpec((tt, tt), lambda b, i: (0, 0)),
        ],
        out_specs=[
            dir_tok, dir_tok, dir_tok, dir_tok,
            pl.BlockSpec((2, ncb, 1, c), lambda b, i: (0, b * nt + i, 0, 0)),
            pl.BlockSpec((tt, c), lambda b, i: (b * nt + i, 0)),
            pl.BlockSpec((tt, c), lambda b, i: (b * nt + i, 0)),
            pl.BlockSpec((tt, c), lambda b, i: (b * nt + i, 0)),
        ],
        out_shape=[
            jax.ShapeDtypeStruct((2, m, c), BF16),
            jax.ShapeDtypeStruct((2, m, c), BF16),
            jax.ShapeDtypeStruct((2, m, c), BF16),
            jax.ShapeDtypeStruct((2, m, c), BF16),
            jax.ShapeDtypeStruct((2, bsz * nc, 1, c), F32),
            jax.ShapeDtypeStruct((m, c), BF16),
            jax.ShapeDtypeStruct((m, c), F32),
            jax.ShapeDtypeStruct((m, c), F32),
        ],
        compiler_params=_params("parallel", "parallel"),
        name="rwkv_prep",
    )(*([pz] * 12), mu_prev, mu_next, w0.reshape(2, 1, c), du, a0.reshape(2, 1, c), iu, gu,
      k_k.reshape(1, c), k_a.reshape(1, c), r_k_flat, ones_blk, tril, triu)
    return outs


def _scan_kernel(abf_ref, abb_ref, rbf_ref, rbb_ref, btf_ref, btb_ref, ktf_ref, ktb_ref,
                 vf_ref, vb_ref, plf_ref, plb_ref, yf_ref, yb_ref, m_scr, *, heads):
    @pl.when(pl.program_id(1) == 0)
    def _():
        m_scr[...] = jnp.zeros_like(m_scr)

    ab_ref = (abf_ref, abb_ref)
    rb_ref = (rbf_ref, rbb_ref)
    bt_ref = (btf_ref, btb_ref)
    kt_ref = (ktf_ref, ktb_ref)
    v_ref = (vf_ref, vb_ref)
    pl_ref = (plf_ref, plb_ref)
    y_ref = (yf_ref, yb_ref)
    n = CHUNK
    row = lax.broadcasted_iota(jnp.int32, (2 * n, 2 * n), 0)
    col = lax.broadcasted_iota(jnp.int32, (2 * n, 2 * n), 1)
    top = row < n
    tr = row % n
    tc = col % n
    eye = (lax.broadcasted_iota(jnp.int32, (n, n), 0)
           == lax.broadcasted_iota(jnp.int32, (n, n), 1)).astype(F32)
    keep = ((tr > tc) | (~top & (tr == tc)), (tr < tc) | (~top & (tr == tc)))
    probs = [(d, hh) for d in range(2) for hh in range(heads)]
    sl = lambda hh: slice(hh * n, (hh + 1) * n)
    ab = [ab_ref[d][0, :, sl(hh)] for d, hh in probs]
    rb = [rb_ref[d][0, :, sl(hh)] for d, hh in probs]
    bt = [bt_ref[d][0, :, sl(hh)] for d, hh in probs]
    kt = [kt_ref[d][0, :, sl(hh)] for d, hh in probs]
    vv = [v_ref[d][:, sl(hh)] for d, hh in probs]
    p_last = [pl_ref[d][0, 0, :, sl(hh)] for d, hh in probs]
    idx = range(len(probs))
    g1 = [_dot_nt(jnp.concatenate([ab[i], rb[i]], axis=0), jnp.concatenate([bt[i], kt[i]], axis=0))
          for i in idx]
    g1 = [jnp.where(keep[probs[i][0]], g1[i], 0.0) for i in idx]
    a_ab = [g[:n, :n] for g in g1]
    low = [g[n:, :].astype(BF16) for g in g1]
    tm = [eye + a for a in a_ab]
    pw = [_dot(a.astype(BF16), a.astype(BF16)) for a in a_ab]
    span = 2
    while span < n:
        last = span * 2 >= n
        nxt = []
        for i in idx:
            pwb = pw[i].astype(BF16)
            if last:
                nxt.append((tm[i] + _dot(tm[i].astype(BF16), pwb), None))
            else:
                both = _dot(jnp.concatenate([tm[i], pw[i]], axis=0).astype(BF16), pwb)
                nxt.append((tm[i] + both[:n], both[n:]))
        tm = [t for t, _ in nxt]
        pw = [p for _, p in nxt]
        span *= 2
    wc = [_dot(tm[i].astype(BF16), jnp.concatenate([ab[i], g1[i][:n, n:].astype(BF16)], axis=1)) for i in idx]
    w = [x[:, :n].astype(BF16) for x in wc]
    cuv = [_dot(wc[i][:, n:].astype(BF16), vv[i]).astype(BF16) for i in idx]
    zero = jnp.zeros((n, n), BF16)
    rhs = [jnp.concatenate([jnp.concatenate([w[i], cuv[i]], axis=1),
                            jnp.concatenate([zero, vv[i]], axis=1)], axis=0) for i in idx]
    bh_t = [(bt[i].astype(F32) * p_last[i]).T for i in idx]
    kh_t = [(kt[i].astype(F32) * p_last[i]).T for i in idx]
    lhs = [jnp.concatenate([low[i], jnp.concatenate([bh_t[i], kh_t[i]], axis=1).astype(BF16)], axis=0)
           for i in idx]
    out = [_dot(lhs[i], rhs[i]) for i in idx]
    q_m = [rb[i].astype(F32) + out[i][:n, :n] for i in idx]
    g_m = [eye * p_last[i] + out[i][n:, :n] for i in idx]
    upd = [_dot(jnp.concatenate([q_m[i], g_m[i]], axis=0).astype(BF16), m_scr[d, hh].astype(BF16))
           for i, (d, hh) in enumerate(probs)]
    for i, (d, hh) in enumerate(probs):
        y_ref[d][:, sl(hh)] = upd[i][:n] + out[i][:n, n:]
        m_scr[d, hh] = upd[i][n:] + out[i][n:, n:]


def rwkv_scan(ab, rb, bt, kt, vb, p_last, bsz, seq):
    nc = seq // CHUNK
    nh = RWKV_HEADS
    n = CHUNK
    c_w = RWKV_WIDTH
    fwd = lambda b, c: b * nc + c
    bwd = lambda b, c: b * nc + nc - 1 - c
    dir_f = pl.BlockSpec((1, n, c_w), lambda b, c: (0, fwd(b, c), 0))
    dir_b = pl.BlockSpec((1, n, c_w), lambda b, c: (1, bwd(b, c), 0))
    tok_f = pl.BlockSpec((n, c_w), lambda b, c: (fwd(b, c), 0))
    tok_b = pl.BlockSpec((n, c_w), lambda b, c: (bwd(b, c), 0))
    pl_f = pl.BlockSpec((1, 1, 1, c_w), lambda b, c: (0, fwd(b, c), 0, 0))
    pl_b = pl.BlockSpec((1, 1, 1, c_w), lambda b, c: (1, bwd(b, c), 0, 0))
    y_shape = jax.ShapeDtypeStruct((bsz * seq, c_w), F32)
    return pl.pallas_call(
        functools.partial(_scan_kernel, heads=nh),
        grid=(bsz, nc),
        in_specs=[dir_f, dir_b, dir_f, dir_b, dir_f, dir_b, dir_f, dir_b, tok_f, tok_b, pl_f, pl_b],
        out_specs=[tok_f, tok_b],
        out_shape=[y_shape, y_shape],
        scratch_shapes=[pltpu.VMEM((2, nh, n, n), F32)],
        compiler_params=_params("parallel", "arbitrary"),
        name="rwkv_scan",
    )(ab, ab, rb, rb, bt, bt, kt, kt, vb, vb, p_last, p_last)


def _post_kernel(yf_ref, yb_ref, bonus_ref, g_ref, lw_ref, lb_ref, o_ref):
    y = yf_ref[...] + yb_ref[...]
    n = RWKV_HEAD
    parts = []
    for h in range(RWKV_HEADS):
        yh = y[:, h * n:(h + 1) * n]
        mu = jnp.mean(yh, axis=-1, keepdims=True)
        yc = yh - mu
        var = jnp.mean(yc * yc, axis=-1, keepdims=True)
        parts.append(yc * lax.rsqrt(var + GN_EPS))
    yn = jnp.concatenate(parts, axis=1)
    o_ref[...] = ((yn * lw_ref[...] + lb_ref[...] + bonus_ref[...]) * g_ref[...]).astype(o_ref.dtype)


def rwkv_post(y_fwd, y_bwd, bonus, g, ln_w, ln_b, tt=256):
    m = y_fwd.shape[0]
    c = RWKV_WIDTH
    tok = pl.BlockSpec((tt, c), lambda i: (i, 0))
    row = pl.BlockSpec((1, c), lambda i: (0, 0))
    return pl.pallas_call(
        _post_kernel,
        grid=(m // tt,),
        in_specs=[tok, tok, tok, tok, row, row],
        out_specs=tok,
        out_shape=jax.ShapeDtypeStruct((m, c), BF16),
        compiler_params=_params("parallel"),
        name="rwkv_post",
    )(y_fwd, y_bwd, bonus, g, ln_w.reshape(1, c), ln_b.reshape(1, c))


def _out_proj_kernel(x_ref, a_ref, r_ref, wa_ref, wr_ref, o_ref):
    o_ref[...] = x_ref[...] + _dot(a_ref[...], wa_ref[...]) + _dot(r_ref[...], wr_ref[...])


def out_proj(x, att, rw, w_att, w_rw, tm=2048, tn=512):
    m, n = x.shape
    return pl.pallas_call(
        _out_proj_kernel,
        grid=(m // tm, n // tn),
        in_specs=[
            pl.BlockSpec((tm, tn), lambda i, j: (i, j)),
            pl.BlockSpec((tm, ATT_WIDTH), lambda i, j: (i, 0)),
            pl.BlockSpec((tm, RWKV_WIDTH), lambda i, j: (i, 0)),
            pl.BlockSpec((ATT_WIDTH, tn), lambda i, j: (0, j)),
            pl.BlockSpec((RWKV_WIDTH, tn), lambda i, j: (0, j)),
        ],
        out_specs=pl.BlockSpec((tm, tn), lambda i, j: (i, j)),
        out_shape=jax.ShapeDtypeStruct((m, n), F32),
        compiler_params=_params("parallel", "parallel"),
        name="out_proj",
    )(x, att, rw, w_att, w_rw)


def _router_kernel(x_ref, nw_ref, wh_ref, wl_ref, hb_ref, afft_ref):
    x = x_ref[...]
    ms = jnp.mean(x * x, axis=-1, keepdims=True)
    h = x * lax.rsqrt(ms + NORM_EPS) * nw_ref[...]
    hb_ref[...] = h.astype(BF16)
    hi, lo = _split2(h)
    logits = _dot(hi, wh_ref[...]) + _dot(lo, wh_ref[...]) + _dot(hi, wl_ref[...])
    lane = lax.broadcasted_iota(jnp.int32, logits.shape, 1)
    valid = lane < N_EXPERTS
    logits = jnp.where(valid, logits, -1e30)
    e = jnp.where(valid, jnp.exp(logits - jnp.max(logits, axis=-1, keepdims=True)), 0.0)
    aff = e / jnp.sum(e, axis=-1, keepdims=True)
    afft_ref[...] = aff.T[:N_EXPERTS, :]


def router(x1, norm_w, w_router, tm=512):
    m, k = x1.shape
    wpad = jnp.zeros((k, LANES), F32).at[:, :N_EXPERTS].set(w_router)
    wh, wl = _split2(wpad)
    return pl.pallas_call(
        _router_kernel,
        grid=(m // tm,),
        in_specs=[
            pl.BlockSpec((tm, k), lambda i: (i, 0)),
            pl.BlockSpec((1, k), lambda i: (0, 0)),
            pl.BlockSpec((k, LANES), lambda i: (0, 0)),
            pl.BlockSpec((k, LANES), lambda i: (0, 0)),
        ],
        out_specs=[
            pl.BlockSpec((tm, k), lambda i: (i, 0)),
            pl.BlockSpec((N_EXPERTS, tm), lambda i: (0, i)),
        ],
        out_shape=[
            jax.ShapeDtypeStruct((m, k), BF16),
            jax.ShapeDtypeStruct((N_EXPERTS, m), F32),
        ],
        compiler_params=_params("parallel"),
        name="router",
    )(x1, norm_w.reshape(1, k), wh, wl)


def _prefix_count(x, upper):
    outs = []
    carry = jnp.zeros((x.shape[0], 1), F32)
    for j in range(x.shape[1] // LANES):
        xt = x[:, j * LANES:(j + 1) * LANES]
        outs.append(_dot(xt.astype(BF16), upper) + carry)
        carry = carry + jnp.sum(xt, axis=-1, keepdims=True)
    return jnp.concatenate(outs, axis=1)


BISECT_STEPS = 152


def _topk_kernel(afft_ref, sp_ref, spt_ref, *, cap, bsz):
    ne = afft_ref.shape[0]
    seq = afft_ref.shape[1] // bsz
    a = [afft_ref[:, b * seq:(b + 1) * seq] for b in range(bsz)]

    def halve(_, bracket):
        out = []
        for b in range(bsz):
            lo, hi = bracket[b]
            mid = (lo + hi) * 0.5
            cnt = jnp.sum((a[b] >= mid).astype(F32), axis=-1, keepdims=True)
            enough = cnt >= cap
            out.append((jnp.where(enough, mid, lo), jnp.where(enough, hi, mid)))
        return tuple(out)

    start = tuple((jnp.zeros((ne, 1), F32), jnp.full((ne, 1), 2.0, F32)) for _ in range(bsz))
    bracket = lax.fori_loop(0, BISECT_STEPS, halve, start)
    r_i = lax.broadcasted_iota(jnp.int32, (LANES, LANES), 0)
    c_i = lax.broadcasted_iota(jnp.int32, (LANES, LANES), 1)
    upper = (r_i < c_i).astype(BF16)
    for b in range(bsz):
        lo, hi = bracket[b]
        above = (a[b] >= hi).astype(F32)
        tied = ((a[b] >= lo) & (a[b] < hi)).astype(F32)
        need = cap - jnp.sum(above, axis=-1, keepdims=True)
        sel = above + tied * (_prefix_count(tied, upper) < need).astype(F32)
        spt = jnp.where(sel > 0.5, _prefix_count(sel, upper), -1.0)
        spt_ref[:, b * seq:(b + 1) * seq] = spt
        full = jnp.concatenate([spt, jnp.full((LANES - ne, seq), -1.0, F32)], axis=0)
        sp_ref[b * seq:(b + 1) * seq, :] = full.T


def topk_select(afft, bsz, seq, cap):
    return pl.pallas_call(
        functools.partial(_topk_kernel, cap=cap, bsz=bsz),
        grid=(1,),
        in_specs=[pl.BlockSpec((N_EXPERTS, bsz * seq), lambda i: (0, 0))],
        out_specs=[
            pl.BlockSpec((bsz * seq, LANES), lambda i: (0, 0)),
            pl.BlockSpec((N_EXPERTS, bsz * seq), lambda i: (0, 0)),
        ],
        out_shape=[
            jax.ShapeDtypeStruct((bsz * seq, LANES), F32),
            jax.ShapeDtypeStruct((N_EXPERTS, bsz * seq), F32),
        ],
        compiler_params=_params("arbitrary"),
        name="topk_select",
    )(afft)


def _gather_kernel(spt_ref, afft_ref, h_ref, xe_ref, gate_ref, *, cap):
    e = pl.program_id(1)
    seq = h_ref.shape[0]
    sp = spt_ref[pl.ds(e, 1), :]
    slot = lax.broadcasted_iota(jnp.int32, (cap, seq), 0).astype(F32)
    hit = sp == slot
    xe_ref[0] = _dot(hit.astype(BF16), h_ref[...]).astype(BF16)
    gate_ref[0] = jnp.sum(jnp.where(hit, afft_ref[pl.ds(e, 1), :], 0.0), axis=-1, keepdims=True)


def moe_gather(spt, afft, hb, bsz, seq, cap):
    d = hb.shape[1]
    return pl.pallas_call(
        functools.partial(_gather_kernel, cap=cap),
        grid=(bsz, N_EXPERTS),
        in_specs=[
            pl.BlockSpec((N_EXPERTS, seq), lambda b, e: (0, b)),
            pl.BlockSpec((N_EXPERTS, seq), lambda b, e: (0, b)),
            pl.BlockSpec((seq, d), lambda b, e: (b, 0)),
        ],
        out_specs=[
            pl.BlockSpec((1, cap, d), lambda b, e: (e, b, 0)),
            pl.BlockSpec((1, cap, 1), lambda b, e: (e, b, 0)),
        ],
        out_shape=[
            jax.ShapeDtypeStruct((N_EXPERTS, bsz * cap, d), BF16),
            jax.ShapeDtypeStruct((N_EXPERTS, bsz * cap, 1), F32),
        ],
        compiler_params=_params("parallel", "arbitrary"),
        name="moe_gather",
    )(spt, afft, hb)


def _expert_kernel(xe_ref, gt_ref, wg_ref, wu_ref, wd_ref, ye_ref, acc_scr):
    f = pl.program_id(1)

    @pl.when(f == 0)
    def _():
        acc_scr[...] = jnp.zeros_like(acc_scr)

    xe = xe_ref[0]
    gate = _dot(xe, wg_ref[0].astype(BF16))
    up = _dot(xe, wu_ref[0].astype(BF16))
    hid = (gate * jax.nn.sigmoid(gate) * up).astype(BF16)
    acc_scr[...] += _dot(hid, wd_ref[0].astype(BF16))

    @pl.when(f == pl.num_programs(1) - 1)
    def _():
        ye_ref[0] = (acc_scr[...] * gt_ref[0]).astype(ye_ref.dtype)


def moe_experts(xe, gates, e_gate, e_up, e_down, tf=512):
    ne, rows, d = xe.shape
    ff = e_gate.shape[2]
    return pl.pallas_call(
        _expert_kernel,
        grid=(ne, ff // tf),
        in_specs=[
            pl.BlockSpec((1, rows, d), lambda e, f: (e, 0, 0)),
            pl.BlockSpec((1, rows, 1), lambda e, f: (e, 0, 0)),
            pl.BlockSpec((1, d, tf), lambda e, f: (e, 0, f)),
            pl.BlockSpec((1, d, tf), lambda e, f: (e, 0, f)),
            pl.BlockSpec((1, tf, d), lambda e, f: (e, f, 0)),
        ],
        out_specs=pl.BlockSpec((1, rows, d), lambda e, f: (e, 0, 0)),
        out_shape=jax.ShapeDtypeStruct((ne, rows, d), BF16),
        scratch_shapes=[pltpu.VMEM((rows, d), F32)],
        compiler_params=_params("parallel", "arbitrary"),
        name="moe_experts",
    )(xe, gates, e_gate, e_up, e_down)


def _combine_kernel(x_ref, sp_ref, ye_ref, nw_ref, o_ref, acc_scr, *, cap, group):
    eg = pl.program_id(2)

    @pl.when(eg == 0)
    def _():
        acc_scr[...] = x_ref[...]

    tt = x_ref.shape[0]
    lane = lax.broadcasted_iota(jnp.int32, (tt, LANES), 1)
    slot = lax.broadcasted_iota(jnp.int32, (tt, cap), 1).astype(F32)
    sp_all = sp_ref[...]
    hits = []
    for k in range(group):
        sp = jnp.sum(jnp.where(lane == eg * group + k, sp_all, 0.0), axis=-1, keepdims=True)
        hits.append((sp == slot).astype(BF16))
    onehot = jnp.concatenate(hits, axis=1)
    acc_scr[...] += _dot(onehot, ye_ref[...].reshape(group * cap, ye_ref.shape[2]))

    @pl.when(eg == pl.num_programs(2) - 1)
    def _():
        y = acc_scr[...]
        ms = jnp.mean(y * y, axis=-1, keepdims=True)
        o_ref[...] = y * lax.rsqrt(ms + NORM_EPS) * nw_ref[...]


def moe_combine(x1, sp, ye, norm_w, bsz, seq, cap, tt=512, group=4):
    m, d = x1.shape
    nt = seq // tt
    return pl.pallas_call(
        functools.partial(_combine_kernel, cap=cap, group=group),
        grid=(bsz, nt, N_EXPERTS // group),
        in_specs=[
            pl.BlockSpec((tt, d), lambda b, i, e: (b * nt + i, 0)),
            pl.BlockSpec((tt, LANES), lambda b, i, e: (b * nt + i, 0)),
            pl.BlockSpec((group, cap, d), lambda b, i, e: (e, b, 0)),
            pl.BlockSpec((1, d), lambda b, i, e: (0, 0)),
        ],
        out_specs=pl.BlockSpec((tt, d), lambda b, i, e: (b * nt + i, 0)),
        out_shape=jax.ShapeDtypeStruct((m, d), F32),
        scratch_shapes=[pltpu.VMEM((tt, d), F32)],
        compiler_params=_params("parallel", "parallel", "arbitrary"),
        name="moe_combine",
    )(x1, sp, ye, norm_w.reshape(1, d))


def _pad_cols(w, width):
    return jnp.pad(w, [(0, 0)] * (w.ndim - 1) + [(0, width - w.shape[-1])])


def rwkv_mixer(pz, mu_prev, mu_next, w0, decay_up, a0, iclr_up, gate_up, k_k, k_a, r_k, ln_x_w, ln_x_b,
               bsz, seq):
    ab, rb, bt, kt, p_last, vb, bonus, g = rwkv_prep(
        pz, mu_prev, mu_next, w0, decay_up, a0, iclr_up, gate_up, k_k, k_a, r_k, bsz, seq)
    y_fwd, y_bwd = rwkv_scan(ab, rb, bt, kt, vb, p_last, bsz, seq)
    return rwkv_post(y_fwd, y_bwd, bonus, g, ln_x_w, ln_x_b)


def moe_block(x1, norm2_w, w_router, e_gate, e_up, e_down, norm_f_w, bsz, seq):
    cap = CAPACITY_FACTOR * seq // N_EXPERTS
    hb, afft = router(x1, norm2_w, w_router)
    sp, spt = topk_select(afft, bsz, seq, cap)
    xe, gates = moe_gather(spt, afft, hb, bsz, seq, cap)
    ye = moe_experts(xe, gates, e_gate, e_up, e_down)
    return moe_combine(x1, sp, ye, norm_f_w, bsz, seq, cap)


def kernel(x, positions, norm1_w, w_in, mu_prev, mu_next, lambda_q1, lambda_k1, lambda_q2, lambda_k2, subln_w, w0, decay_up, a0, iclr_up, gate_up, k_k, k_a, r_k, ln_x_w, ln_x_b, w_out, norm2_w, w_router, e_gate, e_up, e_down, norm_f_w):
    bsz, seq, d = x.shape
    m = bsz * seq
    xf = x.reshape(m, d)
    pos = positions.reshape(m, 1)
    lambda_init = 0.8 - 0.6 * math.exp(-0.3 * 0)

    h = rms_norm_bf16(xf, norm1_w[0])
    w_in_t = jnp.swapaxes(w_in, 1, 2)
    qkv = matmul_rope(h, w_in_t, pos)
    p_z = matmul(h, w_in_t, ATT_COLS, SHIFT_PAD, F32)

    att = attention(qkv, lambda_q1[0], lambda_k1[0], lambda_q2[0], lambda_k2[0],
                    subln_w[0], bsz, seq, lambda_init)

    rw = rwkv_mixer(p_z, _pad_cols(mu_prev, SHIFT_PAD), _pad_cols(mu_next, SHIFT_PAD),
                    w0[0], decay_up[0], a0[0], iclr_up[0], gate_up[0], k_k[0], k_a[0], r_k[0],
                    ln_x_w[0], ln_x_b[0], bsz, seq)

    wo = w_out[0].astype(BF16)
    x1 = out_proj(xf, att, rw, wo[:ATT_WIDTH], wo[ATT_WIDTH:])

    out = moe_block(x1, norm2_w[0], w_router[0], e_gate[0], e_up[0], e_down[0], norm_f_w, bsz, seq)
    return out.reshape(bsz, seq, d)
```

```python
import functools
import math

import jax
import jax.numpy as jnp
from jax import lax
from jax.experimental import pallas as pl
from jax.experimental.pallas import tpu as pltpu

F32 = jnp.float32
BF16 = jnp.bfloat16

LANES = 128
VMEM_LIMIT_BYTES = 56 * 1024 * 1024

D_MODEL = 2048
NORM_EPS = 1e-6
ATT_HEADS = 8
ATT_QK_DIM = 64
ATT_V_DIM = 128
ATT_WIDTH = ATT_HEADS * ATT_V_DIM
ATT_QK_COLS = ATT_HEADS * 2 * ATT_QK_DIM
ATT_COLS = 2 * ATT_QK_COLS + ATT_WIDTH
ROPE_THETA = 500000.0
ROPE_DIM = ATT_QK_DIM // 4
SUBLN_EPS = 1e-5
RWKV_WIDTH = 1024
RWKV_HEAD = 64
RWKV_HEADS = RWKV_WIDTH // RWKV_HEAD
DECAY_LORA = 64
ICLR_LORA = 64
GATE_LORA = 160
LORA_COLS = DECAY_LORA + ICLR_LORA + GATE_LORA
LORA_PAD = 384
GN_EPS = 64e-5
SHIFT_WIDTH = 3 * RWKV_WIDTH + LORA_COLS
SHIFT_PAD = 3 * RWKV_WIDTH + 512
N_EXPERTS = 16
CAPACITY_FACTOR = 2
EXPERT_FF = 2048
CHUNK = 64


def _params(*sem):
    return pltpu.CompilerParams(dimension_semantics=sem, vmem_limit_bytes=VMEM_LIMIT_BYTES)


def _dot(a, b):
    return jnp.dot(a, b, preferred_element_type=F32)


def _dot_nt(a, b):
    return lax.dot_general(a, b, (((1,), (1,)), ((), ())), preferred_element_type=F32)


def _split2(x):
    hi = x.astype(BF16)
    lo = (x - hi.astype(F32)).astype(BF16)
    return hi, lo


def _dot_lhs2(x, m_bf16):
    hi, lo = _split2(x)
    return _dot(hi, m_bf16) + _dot(lo, m_bf16)


def _rms_norm_kernel(x_ref, nw_ref, o_ref):
    x = x_ref[...]
    ms = jnp.mean(x * x, axis=-1, keepdims=True)
    o_ref[...] = (x * lax.rsqrt(ms + NORM_EPS) * nw_ref[...]).astype(o_ref.dtype)


def rms_norm_bf16(x, nw, tm=512):
    m, k = x.shape
    return pl.pallas_call(
        _rms_norm_kernel,
        grid=(m // tm,),
        in_specs=[pl.BlockSpec((tm, k), lambda i: (i, 0)), pl.BlockSpec((1, k), lambda i: (0, 0))],
        out_specs=pl.BlockSpec((tm, k), lambda i: (i, 0)),
        out_shape=jax.ShapeDtypeStruct((m, k), BF16),
        compiler_params=_params("parallel"),
        name="rms_norm",
    )(x, nw.reshape(1, k))


def _mm_kernel(h_ref, wt_ref, o_ref, *, j0, valid):
    tn = o_ref.shape[1]
    col = (j0 + pl.program_id(1)) * tn + lax.broadcasted_iota(jnp.int32, wt_ref.shape[1:], 0)
    wt = jnp.where(col < valid, wt_ref[0], 0.0).astype(BF16)
    o_ref[...] = _dot_nt(h_ref[...], wt).astype(o_ref.dtype)


def matmul(h, wt, col0, n, out_dtype, tm=2048, tn=512):
    m, k = h.shape
    j0 = col0 // tn
    return pl.pallas_call(
        functools.partial(_mm_kernel, j0=j0, valid=wt.shape[1]),
        grid=(m // tm, n // tn),
        in_specs=[
            pl.BlockSpec((tm, k), lambda i, j: (i, 0)),
            pl.BlockSpec((1, tn, k), lambda i, j: (0, j0 + j, 0)),
        ],
        out_specs=pl.BlockSpec((tm, tn), lambda i, j: (i, j)),
        out_shape=jax.ShapeDtypeStruct((m, n), out_dtype),
        compiler_params=_params("parallel", "arbitrary"),
        name="matmul",
    )(h, wt)


def _mm_rope_kernel(h_ref, wt_ref, pos_ref, freq_ref, slo_ref, shi_ref, o_ref,
                    c_scr, lo_scr, hi_scr, *, rope_tiles, q_tiles):
    j = pl.program_id(1)

    @pl.when(j == 0)
    def _():
        ang = pos_ref[...].astype(F32) * freq_ref[...]
        s = jnp.sin(ang)
        c_scr[...] = jnp.cos(ang)
        lo_scr[...] = s * slo_ref[...]
        hi_scr[...] = s * shi_ref[...]

    acc = _dot_nt(h_ref[...], wt_ref[0].astype(BF16))

    @pl.when(j < rope_tiles)
    def _():
        scale = jnp.where(j < q_tiles, ATT_QK_DIM ** -0.5 * math.log2(math.e), 1.0)
        half = ROPE_DIM // 2
        for g in range(acc.shape[1] // LANES):
            x = acc[:, g * LANES:(g + 1) * LANES]
            x_dn = pltpu.roll(x, half, axis=1)
            x_up = pltpu.roll(x, LANES - half, axis=1)
            y = x * c_scr[...] + x_dn * hi_scr[...] + x_up * lo_scr[...]
            o_ref[:, g * LANES:(g + 1) * LANES] = (y * scale).astype(o_ref.dtype)

    @pl.when(j >= rope_tiles)
    def _():
        o_ref[...] = acc.astype(o_ref.dtype)


def matmul_rope(h, wt, pos, tm=2048, tn=512):
    m, k = h.shape
    n = ATT_COLS
    lane = jnp.arange(LANES) % ATT_QK_DIM
    half = ROPE_DIM // 2
    inv_freq = ROPE_THETA ** (-jnp.arange(0, ROPE_DIM, 2, dtype=F32) / ROPE_DIM)
    freq = jnp.where(lane < ROPE_DIM, inv_freq[lane % half], 0.0).astype(F32).reshape(1, LANES)
    sgn_lo = jnp.where(lane < half, -1.0, 0.0).astype(F32).reshape(1, LANES)
    sgn_hi = jnp.where((lane >= half) & (lane < ROPE_DIM), 1.0, 0.0).astype(F32).reshape(1, LANES)
    vec = pl.BlockSpec((1, LANES), lambda i, j: (0, 0))
    return pl.pallas_call(
        functools.partial(_mm_rope_kernel, rope_tiles=2 * ATT_QK_COLS // tn, q_tiles=ATT_QK_COLS // tn),
        grid=(m // tm, n // tn),
        in_specs=[
            pl.BlockSpec((tm, k), lambda i, j: (i, 0)),
            pl.BlockSpec((1, tn, k), lambda i, j: (0, j, 0)),
            pl.BlockSpec((tm, 1), lambda i, j: (i, 0)),
            vec, vec, vec,
        ],
        out_specs=pl.BlockSpec((tm, tn), lambda i, j: (i, j)),
        out_shape=jax.ShapeDtypeStruct((m, n), BF16),
        scratch_shapes=[pltpu.VMEM((tm, LANES), F32), pltpu.VMEM((tm, LANES), F32),
                        pltpu.VMEM((tm, LANES), F32)],
        compiler_params=_params("parallel", "arbitrary"),
        name="matmul_rope",
    )(h, wt, pos, freq, sgn_lo, sgn_hi)


def _attn_kernel(q_ref, qn_ref, k_ref, v_ref, lq1_ref, lk1_ref, lq2_ref, lk2_ref, sw_ref, o_ref,
                 v_scr, sa_scr, ma_scr, sb_scr, mb_scr, *, lambda_init, heads):
    dv = ATT_V_DIM
    i = pl.program_id(2)

    def scores(q, s_scr, m_scr):
        for hc in range(2 * heads):
            cols = slice(hc * ATT_QK_DIM, (hc + 1) * ATT_QK_DIM)
            s = _dot_nt(q[:, cols], k_ref[:, cols])
            s_scr[hc] = s
            m_scr[hc] = jnp.max(s, axis=-1, keepdims=True)

    @pl.when(i == 0)
    def _():
        lane = lax.broadcasted_iota(jnp.int32, (v_scr.shape[1], dv), 1)
        for h in range(heads):
            v_scr[h, :, :dv] = v_ref[:, h * dv:(h + 1) * dv]
            v_scr[h, :, dv:] = (lane == 0).astype(BF16)
        scores(q_ref[...], sa_scr, ma_scr)

    def step(cur_s, cur_m, nxt_s, nxt_m):
        scores(qn_ref[...], nxt_s, nxt_m)
        lam = (jnp.exp(jnp.sum(lq1_ref[...] * lk1_ref[...], axis=-1, keepdims=True))
               - jnp.exp(jnp.sum(lq2_ref[...] * lk2_ref[...], axis=-1, keepdims=True)) + lambda_init)
        for h in range(heads):
            va = v_scr[h]
            o1 = _dot(jnp.exp2(cur_s[2 * h] - cur_m[2 * h]).astype(BF16), va)
            o2 = _dot(jnp.exp2(cur_s[2 * h + 1] - cur_m[2 * h + 1]).astype(BF16), va)
            o = o1[:, :dv] / o1[:, dv:dv + 1] - o2[:, :dv] * (lam / o2[:, dv:dv + 1])
            ms = jnp.mean(o * o, axis=-1, keepdims=True)
            o = o * lax.rsqrt(ms + SUBLN_EPS) * sw_ref[...] * (1.0 - lambda_init)
            o_ref[:, h * dv:(h + 1) * dv] = o.astype(o_ref.dtype)

    @pl.when(i % 2 == 0)
    def _():
        step(sa_scr, ma_scr, sb_scr, mb_scr)

    @pl.when(i % 2 == 1)
    def _():
        step(sb_scr, mb_scr, sa_scr, ma_scr)


def attention(qkv, lq1, lk1, lq2, lk2, subln_w, bsz, seq, lambda_init, tq=256, heads=4):
    nq = seq // tq
    vec = lambda n: pl.BlockSpec((1, n), lambda b, h, i: (0, 0))
    ng = ATT_HEADS // heads
    width = heads * LANES
    s_buf = pltpu.VMEM((2 * heads, tq, seq), F32)
    m_buf = pltpu.VMEM((2 * heads, tq, 1), F32)
    return pl.pallas_call(
        functools.partial(_attn_kernel, lambda_init=lambda_init, heads=heads),
        grid=(bsz, ng, nq),
        in_specs=[
            pl.BlockSpec((tq, width), lambda b, h, i: (b * nq + i, h)),
            pl.BlockSpec((tq, width), lambda b, h, i: (b * nq + jnp.minimum(i + 1, nq - 1), h)),
            pl.BlockSpec((seq, width), lambda b, h, i: (b, ng + h)),
            pl.BlockSpec((seq, width), lambda b, h, i: (b, 2 * ng + h)),
            vec(ATT_QK_DIM), vec(ATT_QK_DIM), vec(ATT_QK_DIM), vec(ATT_QK_DIM),
            vec(ATT_V_DIM),
        ],
        out_specs=pl.BlockSpec((tq, width), lambda b, h, i: (b * nq + i, h)),
        out_shape=jax.ShapeDtypeStruct((bsz * seq, ATT_WIDTH), BF16),
        scratch_shapes=[pltpu.VMEM((heads, seq, 2 * ATT_V_DIM), BF16), s_buf, m_buf, s_buf, m_buf],
        compiler_params=_params("parallel", "parallel", "arbitrary"),
        name="diff_attention",
    )(qkv, qkv, qkv, qkv,
      lq1.reshape(1, -1), lk1.reshape(1, -1), lq2.reshape(1, -1), lk2.reshape(1, -1),
      subln_w.reshape(1, -1))


HALO = 8


def _token_shift(z_ref, prev_ref, next_ref, mu_p, mu_n, first, last):
    z = z_ref[...]
    n = z.shape[0]
    row = lax.broadcasted_iota(jnp.int32, z.shape, 0)
    before = jnp.where(first, 0.0, prev_ref[HALO - 1:HALO, :])
    after = jnp.where(last, 0.0, next_ref[0:1, :])
    zp = jnp.where(row == 0, before, pltpu.roll(z, 1, axis=0))
    zn = jnp.where(row == n - 1, after, pltpu.roll(z, n - 1, axis=0))
    return z + mu_p * (zp - z) + mu_n * (zn - z)


def _seg_sum(x, ones_blk):
    parts = []
    for j in range(x.shape[1] // LANES):
        parts.append(_dot_lhs2(x[:, j * LANES:(j + 1) * LANES], ones_blk))
    return jnp.concatenate(parts, axis=1)


def _prep_kernel(r_ref, k_ref, v_ref, lo_ref, rp_ref, kp_ref, vp_ref, lp_ref, rn_ref, kn_ref, vn_ref,
                 ln_ref, mup_ref, mun_ref, w0_ref, du_ref, a0_ref, iu_ref, gu_ref, kk_ref,
                 ka_ref, rk_ref, ones_ref, tril_ref, triu_ref,
                 ab_ref, rb_ref, bt_ref, kt_ref, pl_ref, vb_ref, bonus_ref, g_ref, *, tt):
    first = pl.program_id(1) == 0
    last = pl.program_id(1) == pl.num_programs(1) - 1
    c = RWKV_WIDTH
    shift = lambda z, p, n, lo_col, hi_col: _token_shift(
        z, p, n, mup_ref[:, lo_col:hi_col], mun_ref[:, lo_col:hi_col], first, last)
    r = shift(r_ref, rp_ref, rn_ref, 0, c)
    k = shift(k_ref, kp_ref, kn_ref, c, 2 * c)
    v = shift(v_ref, vp_ref, vn_ref, 2 * c, 3 * c)
    lo = shift(lo_ref, lp_ref, ln_ref, 3 * c, 3 * c + LORA_PAD)
    ones_blk = ones_ref[...]
    vb_ref[...] = v.astype(BF16)
    g_ref[...] = _dot(jax.nn.sigmoid(lo).astype(BF16), gu_ref[...])
    kk = k * kk_ref[...]
    kk = kk * jnp.minimum(lax.rsqrt(_seg_sum(kk * kk, ones_blk)), 1e12)
    th = jnp.tanh(lo).astype(BF16)
    lob = lo.astype(BF16)
    nchunk = tt // CHUNK
    ksum = jnp.zeros_like(k)
    for d in range(2):
        wl = w0_ref[d] + _dot(th, du_ref[d])
        lw = -math.exp(-0.5) * jax.nn.sigmoid(wl)
        a = jax.nn.sigmoid(a0_ref[d] + _dot(lob, iu_ref[d]))
        kd = k * (1.0 + (a - 1.0) * ka_ref[...])
        ksum = ksum + kd
        tri = tril_ref[...] if d == 0 else triu_ref[...]
        cum = _dot_lhs2_rhs(tri, lw)
        e_pos = jnp.exp(cum)
        e_neg = jnp.exp(-cum)
        ab_ref[d] = (-kk * jnp.exp(cum - lw)).astype(BF16)
        rb_ref[d] = (r * e_pos).astype(BF16)
        bt_ref[d] = (kk * a * e_neg).astype(BF16)
        kt_ref[d] = (kd * e_neg).astype(BF16)
        for c in range(nchunk):
            last = c * CHUNK + (CHUNK - 1 if d == 0 else 0)
            pl_ref[d, c] = e_pos[last:last + 1, :]
    bonus_ref[...] = _seg_sum(r * ksum * rk_ref[...], ones_blk) * v


def _dot_lhs2_rhs(tri_bf16, x):
    hi, lo = _split2(x)
    return _dot(tri_bf16, hi) + _dot(tri_bf16, lo)


def rwkv_prep(pz, mu_prev, mu_next, w0, decay_up, a0, iclr_up, gate_up, k_k, k_a, r_k, bsz, seq, tt=256):
    m = bsz * seq
    c = RWKV_WIDTH
    nt = seq // tt
    ncb = tt // CHUNK
    nc = seq // CHUNK

    def pad_rows(w, start):
        out = jnp.zeros(w.shape[:-2] + (LORA_PAD, c), F32)
        return lax.dynamic_update_slice_in_dim(out, w.astype(F32), start, axis=w.ndim - 2).astype(BF16)

    du = pad_rows(decay_up, 0)
    iu = pad_rows(iclr_up, DECAY_LORA)
    gu = pad_rows(gate_up, DECAY_LORA + ICLR_LORA)
    lane = jnp.arange(LANES)
    ones_blk = (lane[:, None] // RWKV_HEAD == lane[None, :] // RWKV_HEAD).astype(BF16)
    t = jnp.arange(tt)
    same = t[:, None] // CHUNK == t[None, :] // CHUNK
    tril = (same & (t[:, None] >= t[None, :])).astype(BF16)
    triu = (same & (t[:, None] <= t[None, :])).astype(BF16)
    r_k_flat = r_k.reshape(1, c)
    row = lambda: pl.BlockSpec((1, c), lambda b, i: (0, 0))
    full3 = lambda s: pl.BlockSpec(s, lambda b, i: (0, 0, 0))
    tok = lambda j, w: pl.BlockSpec((tt, w), lambda b, i: (b * nt + i, j))
    per = tt // HALO
    prev = lambda j, w: pl.BlockSpec((HALO, w), lambda b, i: (jnp.maximum((b * nt + i) * per - 1, 0), j))
    nxt = lambda j, w: pl.BlockSpec(
        (HALO, w), lambda b, i: (jnp.minimum((b * nt + i + 1) * per, m // HALO - 1), j))
    lora_j = 3 * c // LORA_PAD
    cols = [(0, c), (1, c), (2, c), (lora_j, LORA_PAD)]
    mu_row = pl.BlockSpec((1, pz.shape[1]), lambda b, i: (0, 0))
    dir_tok = pl.BlockSpec((2, tt, c), lambda b, i: (0, b * nt + i, 0))
    outs = pl.pallas_call(
        functools.partial(_prep_kernel, tt=tt),
        grid=(bsz, nt),
        in_specs=[tok(j, w) for j, w in cols] + [prev(j, w) for j, w in cols]
        + [nxt(j, w) for j, w in cols] + [
            mu_row, mu_row,
            full3((2, 1, c)), full3((2, LORA_PAD, c)), full3((2, 1, c)), full3((2, LORA_PAD, c)),
            pl.BlockSpec((LORA_PAD, c), lambda b, i: (0, 0)),
            row(), row(), row(),
            pl.BlockSpec((LANES, LANES), lambda b, i: (0, 0)),
            pl.BlockSpec((tt, tt), lambda b, i: (0, 0)),
            pl.BlockSpec((tt, tt), lambda b, i: (0, 0)),
        ],
        out_specs=[
            dir_tok, dir_tok, dir_tok, dir_tok,
            pl.BlockSpec((2, ncb, 1, c), lambda b, i: (0, b * nt + i, 0, 0)),
            pl.BlockSpec((tt, c), lambda b, i: (b * nt + i, 0)),
            pl.BlockSpec((tt, c), lambda b, i: (b * nt + i, 0)),
            pl.BlockSpec((tt, c), lambda b, i: (b * nt + i, 0)),
        ],
        out_shape=[
            jax.ShapeDtypeStruct((2, m, c), BF16),
            jax.ShapeDtypeStruct((2, m, c), BF16),
            jax.ShapeDtypeStruct((2, m, c), BF16),
            jax.ShapeDtypeStruct((2, m, c), BF16),
            jax.ShapeDtypeStruct((2, bsz * nc, 1, c), F32),
            jax.ShapeDtypeStruct((m, c), BF16),
            jax.ShapeDtypeStruct((m, c), F32),
            jax.ShapeDtypeStruct((m, c), F32),
        ],
        compiler_params=_params("parallel", "parallel"),
        name="rwkv_prep",
    )(*([pz] * 12), mu_prev, mu_next, w0.reshape(2, 1, c), du, a0.reshape(2, 1, c), iu, gu,
      k_k.reshape(1, c), k_a.reshape(1, c), r_k_flat, ones_blk, tril, triu)
    return outs


def _scan_kernel(abf_ref, abb_ref, rbf_ref, rbb_ref, btf_ref, btb_ref, ktf_ref, ktb_ref,
                 vf_ref, vb_ref, plf_ref, plb_ref, yf_ref, yb_ref, m_scr, *, heads, cpb):
    @pl.when(pl.program_id(1) == 0)
    def _():
        m_scr[...] = jnp.zeros_like(m_scr)

    ab_ref = (abf_ref, abb_ref)
    rb_ref = (rbf_ref, rbb_ref)
    bt_ref = (btf_ref, btb_ref)
    kt_ref = (ktf_ref, ktb_ref)
    v_ref = (vf_ref, vb_ref)
    pl_ref = (plf_ref, plb_ref)
    y_ref = (yf_ref, yb_ref)
    n = CHUNK
    row = lax.broadcasted_iota(jnp.int32, (2 * n, 2 * n), 0)
    col = lax.broadcasted_iota(jnp.int32, (2 * n, 2 * n), 1)
    top = row < n
    tr = row % n
    tc = col % n
    eye = (lax.broadcasted_iota(jnp.int32, (n, n), 0)
           == lax.broadcasted_iota(jnp.int32, (n, n), 1)).astype(F32)
    keep = ((tr > tc) | (~top & (tr == tc)), (tr < tc) | (~top & (tr == tc)))
    probs = [(sub, d, hh) for sub in range(cpb) for d in range(2) for hh in range(heads)]
    sl = lambda hh: slice(hh * n, (hh + 1) * n)
    chunk_of = lambda sub, d: sub if d == 0 else cpb - 1 - sub
    rows = lambda sub, d: slice(chunk_of(sub, d) * n, (chunk_of(sub, d) + 1) * n)
    ab = [ab_ref[d][0, rows(sub, d), sl(hh)] for sub, d, hh in probs]
    rb = [rb_ref[d][0, rows(sub, d), sl(hh)] for sub, d, hh in probs]
    bt = [bt_ref[d][0, rows(sub, d), sl(hh)] for sub, d, hh in probs]
    kt = [kt_ref[d][0, rows(sub, d), sl(hh)] for sub, d, hh in probs]
    vv = [v_ref[d][rows(sub, d), sl(hh)] for sub, d, hh in probs]
    p_last = [pl_ref[d][0, chunk_of(sub, d), :, sl(hh)] for sub, d, hh in probs]
    idx = range(len(probs))
    g1 = [_dot_nt(jnp.concatenate([ab[i], rb[i]], axis=0), jnp.concatenate([bt[i], kt[i]], axis=0))
          for i in idx]
    g1 = [jnp.where(keep[probs[i][1]], g1[i], 0.0) for i in idx]
    a_ab = [g[:n, :n] for g in g1]
    low = [g[n:, :].astype(BF16) for g in g1]
    tm = [eye + a for a in a_ab]
    pw = [_dot(a.astype(BF16), a.astype(BF16)) for a in a_ab]
    span = 2
    while span < n:
        last = span * 2 >= n
        nxt = []
        for i in idx:
            pwb = pw[i].astype(BF16)
            if last:
                nxt.append((tm[i] + _dot(tm[i].astype(BF16), pwb), None))
            else:
                both = _dot(jnp.concatenate([tm[i], pw[i]], axis=0).astype(BF16), pwb)
                nxt.append((tm[i] + both[:n], both[n:]))
        tm = [t for t, _ in nxt]
        pw = [p for _, p in nxt]
        span *= 2
    wc = [_dot(tm[i].astype(BF16), jnp.concatenate([ab[i], g1[i][:n, n:].astype(BF16)], axis=1)) for i in idx]
    w = [x[:, :n].astype(BF16) for x in wc]
    cuv = [_dot(wc[i][:, n:].astype(BF16), vv[i]).astype(BF16) for i in idx]
    zero = jnp.zeros((n, n), BF16)
    rhs = [jnp.concatenate([jnp.concatenate([w[i], cuv[i]], axis=1),
                            jnp.concatenate([zero, vv[i]], axis=1)], axis=0) for i in idx]
    bh_t = [(bt[i].astype(F32) * p_last[i]).T for i in idx]
    kh_t = [(kt[i].astype(F32) * p_last[i]).T for i in idx]
    lhs = [jnp.concatenate([low[i], jnp.concatenate([bh_t[i], kh_t[i]], axis=1).astype(BF16)], axis=0)
           for i in idx]
    out = [_dot(lhs[i], rhs[i]) for i in idx]
    q_m = [rb[i].astype(F32) + out[i][:n, :n] for i in idx]
    g_m = [eye * p_last[i] + out[i][n:, :n] for i in idx]
    state = {(d, hh): m_scr[d, hh] for d in range(2) for hh in range(heads)}
    for i, (sub, d, hh) in enumerate(probs):
        upd = _dot(jnp.concatenate([q_m[i], g_m[i]], axis=0).astype(BF16), state[d, hh].astype(BF16))
        y_ref[d][rows(sub, d), sl(hh)] = upd[:n] + out[i][:n, n:]
        state[d, hh] = upd[n:] + out[i][n:, n:]
    for (d, hh), m_new in state.items():
        m_scr[d, hh] = m_new


def rwkv_scan(ab, rb, bt, kt, vb, p_last, bsz, seq, cpb=2):
    n = CHUNK
    nb = seq // (CHUNK * cpb)
    nh = RWKV_HEADS
    c_w = RWKV_WIDTH
    fwd = lambda b, c: b * nb + c
    bwd = lambda b, c: b * nb + nb - 1 - c
    dir_f = pl.BlockSpec((1, cpb * n, c_w), lambda b, c: (0, fwd(b, c), 0))
    dir_b = pl.BlockSpec((1, cpb * n, c_w), lambda b, c: (1, bwd(b, c), 0))
    tok_f = pl.BlockSpec((cpb * n, c_w), lambda b, c: (fwd(b, c), 0))
    tok_b = pl.BlockSpec((cpb * n, c_w), lambda b, c: (bwd(b, c), 0))
    pl_f = pl.BlockSpec((1, cpb, 1, c_w), lambda b, c: (0, fwd(b, c), 0, 0))
    pl_b = pl.BlockSpec((1, cpb, 1, c_w), lambda b, c: (1, bwd(b, c), 0, 0))
    y_shape = jax.ShapeDtypeStruct((bsz * seq, c_w), F32)
    return pl.pallas_call(
        functools.partial(_scan_kernel, heads=nh, cpb=cpb),
        grid=(bsz, nb),
        in_specs=[dir_f, dir_b, dir_f, dir_b, dir_f, dir_b, dir_f, dir_b, tok_f, tok_b, pl_f, pl_b],
        out_specs=[tok_f, tok_b],
        out_shape=[y_shape, y_shape],
        scratch_shapes=[pltpu.VMEM((2, nh, n, n), F32)],
        compiler_params=_params("parallel", "arbitrary"),
        name="rwkv_scan",
    )(ab, ab, rb, rb, bt, bt, kt, kt, vb, vb, p_last, p_last)


def _post_kernel(yf_ref, yb_ref, bonus_ref, g_ref, lw_ref, lb_ref, o_ref):
    y = yf_ref[...] + yb_ref[...]
    n = RWKV_HEAD
    parts = []
    for h in range(RWKV_HEADS):
        yh = y[:, h * n:(h + 1) * n]
        mu = jnp.mean(yh, axis=-1, keepdims=True)
        yc = yh - mu
        var = jnp.mean(yc * yc, axis=-1, keepdims=True)
        parts.append(yc * lax.rsqrt(var + GN_EPS))
    yn = jnp.concatenate(parts, axis=1)
    o_ref[...] = ((yn * lw_ref[...] + lb_ref[...] + bonus_ref[...]) * g_ref[...]).astype(o_ref.dtype)


def rwkv_post(y_fwd, y_bwd, bonus, g, ln_w, ln_b, tt=256):
    m = y_fwd.shape[0]
    c = RWKV_WIDTH
    tok = pl.BlockSpec((tt, c), lambda i: (i, 0))
    row = pl.BlockSpec((1, c), lambda i: (0, 0))
    return pl.pallas_call(
        _post_kernel,
        grid=(m // tt,),
        in_specs=[tok, tok, tok, tok, row, row],
        out_specs=tok,
        out_shape=jax.ShapeDtypeStruct((m, c), BF16),
        compiler_params=_params("parallel"),
        name="rwkv_post",
    )(y_fwd, y_bwd, bonus, g, ln_w.reshape(1, c), ln_b.reshape(1, c))


def _out_proj_kernel(x_ref, a_ref, r_ref, wa_ref, wr_ref, o_ref):
    o_ref[...] = x_ref[...] + _dot(a_ref[...], wa_ref[...]) + _dot(r_ref[...], wr_ref[...])


def out_proj(x, att, rw, w_att, w_rw, tm=2048, tn=512):
    m, n = x.shape
    return pl.pallas_call(
        _out_proj_kernel,
        grid=(m // tm, n // tn),
        in_specs=[
            pl.BlockSpec((tm, tn), lambda i, j: (i, j)),
            pl.BlockSpec((tm, ATT_WIDTH), lambda i, j: (i, 0)),
            pl.BlockSpec((tm, RWKV_WIDTH), lambda i, j: (i, 0)),
            pl.BlockSpec((ATT_WIDTH, tn), lambda i, j: (0, j)),
            pl.BlockSpec((RWKV_WIDTH, tn), lambda i, j: (0, j)),
        ],
        out_specs=pl.BlockSpec((tm, tn), lambda i, j: (i, j)),
        out_shape=jax.ShapeDtypeStruct((m, n), F32),
        compiler_params=_params("parallel", "parallel"),
        name="out_proj",
    )(x, att, rw, w_att, w_rw)


def _router_kernel(x_ref, nw_ref, wh_ref, wl_ref, hb_ref, afft_ref):
    x = x_ref[...]
    ms = jnp.mean(x * x, axis=-1, keepdims=True)
    h = x * lax.rsqrt(ms + NORM_EPS) * nw_ref[...]
    hb_ref[...] = h.astype(BF16)
    hi, lo = _split2(h)
    logits = _dot(hi, wh_ref[...]) + _dot(lo, wh_ref[...]) + _dot(hi, wl_ref[...])
    lane = lax.broadcasted_iota(jnp.int32, logits.shape, 1)
    valid = lane < N_EXPERTS
    logits = jnp.where(valid, logits, -1e30)
    e = jnp.where(valid, jnp.exp(logits - jnp.max(logits, axis=-1, keepdims=True)), 0.0)
    aff = e / jnp.sum(e, axis=-1, keepdims=True)
    afft_ref[...] = aff.T[:N_EXPERTS, :]


def router(x1, norm_w, w_router, tm=512):
    m, k = x1.shape
    wpad = jnp.zeros((k, LANES), F32).at[:, :N_EXPERTS].set(w_router)
    wh, wl = _split2(wpad)
    return pl.pallas_call(
        _router_kernel,
        grid=(m // tm,),
        in_specs=[
            pl.BlockSpec((tm, k), lambda i: (i, 0)),
            pl.BlockSpec((1, k), lambda i: (0, 0)),
            pl.BlockSpec((k, LANES), lambda i: (0, 0)),
            pl.BlockSpec((k, LANES), lambda i: (0, 0)),
        ],
        out_specs=[
            pl.BlockSpec((tm, k), lambda i: (i, 0)),
            pl.BlockSpec((N_EXPERTS, tm), lambda i: (0, i)),
        ],
        out_shape=[
            jax.ShapeDtypeStruct((m, k), BF16),
            jax.ShapeDtypeStruct((N_EXPERTS, m), F32),
        ],
        compiler_params=_params("parallel"),
        name="router",
    )(x1, norm_w.reshape(1, k), wh, wl)


def _prefix_count(x, upper):
    outs = []
    carry = jnp.zeros((x.shape[0], 1), F32)
    for j in range(x.shape[1] // LANES):
        xt = x[:, j * LANES:(j + 1) * LANES]
        outs.append(_dot(xt.astype(BF16), upper) + carry)
        carry = carry + jnp.sum(xt, axis=-1, keepdims=True)
    return jnp.concatenate(outs, axis=1)


BISECT_STEPS = 152


def _topk_kernel(afft_ref, sp_ref, spt_ref, *, cap, bsz):
    ne = afft_ref.shape[0]
    seq = afft_ref.shape[1] // bsz
    a = [afft_ref[:, b * seq:(b + 1) * seq] for b in range(bsz)]

    def halve(_, bracket):
        out = []
        for b in range(bsz):
            lo, hi = bracket[b]
            mid = (lo + hi) * 0.5
            cnt = jnp.sum((a[b] >= mid).astype(F32), axis=-1, keepdims=True)
            enough = cnt >= cap
            out.append((jnp.where(enough, mid, lo), jnp.where(enough, hi, mid)))
        return tuple(out)

    start = tuple((jnp.zeros((ne, 1), F32), jnp.full((ne, 1), 2.0, F32)) for _ in range(bsz))
    bracket = lax.fori_loop(0, BISECT_STEPS, halve, start)
    r_i = lax.broadcasted_iota(jnp.int32, (LANES, LANES), 0)
    c_i = lax.broadcasted_iota(jnp.int32, (LANES, LANES), 1)
    upper = (r_i < c_i).astype(BF16)
    for b in range(bsz):
        lo, hi = bracket[b]
        above = (a[b] >= hi).astype(F32)
        tied = ((a[b] >= lo) & (a[b] < hi)).astype(F32)
        need = cap - jnp.sum(above, axis=-1, keepdims=True)
        sel = above + tied * (_prefix_count(tied, upper) < need).astype(F32)
        spt = jnp.where(sel > 0.5, _prefix_count(sel, upper), -1.0)
        spt_ref[:, b * seq:(b + 1) * seq] = spt
        full = jnp.concatenate([spt, jnp.full((LANES - ne, seq), -1.0, F32)], axis=0)
        sp_ref[b * seq:(b + 1) * seq, :] = full.T


def topk_select(afft, bsz, seq, cap):
    return pl.pallas_call(
        functools.partial(_topk_kernel, cap=cap, bsz=bsz),
        grid=(1,),
        in_specs=[pl.BlockSpec((N_EXPERTS, bsz * seq), lambda i: (0, 0))],
        out_specs=[
            pl.BlockSpec((bsz * seq, LANES), lambda i: (0, 0)),
            pl.BlockSpec((N_EXPERTS, bsz * seq), lambda i: (0, 0)),
        ],
        out_shape=[
            jax.ShapeDtypeStruct((bsz * seq, LANES), F32),
            jax.ShapeDtypeStruct((N_EXPERTS, bsz * seq), F32),
        ],
        compiler_params=_params("arbitrary"),
        name="topk_select",
    )(afft)


def _gather_kernel(spt_ref, afft_ref, h_ref, xe_ref, gate_ref, *, cap, group):
    eg = pl.program_id(1)
    seq = h_ref.shape[0]
    slot = lax.broadcasted_iota(jnp.int32, (cap, seq), 0).astype(F32)
    hits = []
    for k in range(group):
        e = eg * group + k
        hit = spt_ref[pl.ds(e, 1), :] == slot
        gate_ref[k] = jnp.sum(jnp.where(hit, afft_ref[pl.ds(e, 1), :], 0.0), axis=-1, keepdims=True)
        hits.append(hit.astype(BF16))
    xe = _dot(jnp.concatenate(hits, axis=0), h_ref[...]).astype(BF16)
    xe_ref[...] = xe.reshape(group, cap, xe.shape[1])


def moe_gather(spt, afft, hb, bsz, seq, cap, group=4):
    d = hb.shape[1]
    return pl.pallas_call(
        functools.partial(_gather_kernel, cap=cap, group=group),
        grid=(bsz, N_EXPERTS // group),
        in_specs=[
            pl.BlockSpec((N_EXPERTS, seq), lambda b, e: (0, b)),
            pl.BlockSpec((N_EXPERTS, seq), lambda b, e: (0, b)),
            pl.BlockSpec((seq, d), lambda b, e: (b, 0)),
        ],
        out_specs=[
            pl.BlockSpec((group, cap, d), lambda b, e: (e, b, 0)),
            pl.BlockSpec((group, cap, 1), lambda b, e: (e, b, 0)),
        ],
        out_shape=[
            jax.ShapeDtypeStruct((N_EXPERTS, bsz * cap, d), BF16),
            jax.ShapeDtypeStruct((N_EXPERTS, bsz * cap, 1), F32),
        ],
        compiler_params=_params("parallel", "arbitrary"),
        name="moe_gather",
    )(spt, afft, hb)


def _expert_kernel(xe_ref, gt_ref, wg_ref, wu_ref, wd_ref, ye_ref, acc_scr):
    f = pl.program_id(1)

    @pl.when(f == 0)
    def _():
        acc_scr[...] = jnp.zeros_like(acc_scr)

    xe = xe_ref[0]
    gate = _dot(xe, wg_ref[0].astype(BF16))
    up = _dot(xe, wu_ref[0].astype(BF16))
    hid = (gate * jax.nn.sigmoid(gate) * up).astype(BF16)
    acc_scr[...] += _dot(hid, wd_ref[0].astype(BF16))

    @pl.when(f == pl.num_programs(1) - 1)
    def _():
        ye_ref[0] = (acc_scr[...] * gt_ref[0]).astype(ye_ref.dtype)


def moe_experts(xe, gates, e_gate, e_up, e_down, tf=512):
    ne, rows, d = xe.shape
    ff = e_gate.shape[2]
    return pl.pallas_call(
        _expert_kernel,
        grid=(ne, ff // tf),
        in_specs=[
            pl.BlockSpec((1, rows, d), lambda e, f: (e, 0, 0)),
            pl.BlockSpec((1, rows, 1), lambda e, f: (e, 0, 0)),
            pl.BlockSpec((1, d, tf), lambda e, f: (e, 0, f)),
            pl.BlockSpec((1, d, tf), lambda e, f: (e, 0, f)),
            pl.BlockSpec((1, tf, d), lambda e, f: (e, f, 0)),
        ],
        out_specs=pl.BlockSpec((1, rows, d), lambda e, f: (e, 0, 0)),
        out_shape=jax.ShapeDtypeStruct((ne, rows, d), BF16),
        scratch_shapes=[pltpu.VMEM((rows, d), F32)],
        compiler_params=_params("parallel", "arbitrary"),
        name="moe_experts",
    )(xe, gates, e_gate, e_up, e_down)


def _combine_kernel(x_ref, sp_ref, ye_ref, nw_ref, o_ref, acc_scr, *, cap, group):
    eg = pl.program_id(2)

    @pl.when(eg == 0)
    def _():
        acc_scr[...] = x_ref[...]

    tt = x_ref.shape[0]
    lane = lax.broadcasted_iota(jnp.int32, (tt, LANES), 1)
    slot = lax.broadcasted_iota(jnp.int32, (tt, cap), 1).astype(F32)
    sp_all = sp_ref[...]
    hits = []
    for k in range(group):
        sp = jnp.sum(jnp.where(lane == eg * group + k, sp_all, 0.0), axis=-1, keepdims=True)
        hits.append((sp == slot).astype(BF16))
    onehot = jnp.concatenate(hits, axis=1)
    acc_scr[...] += _dot(onehot, ye_ref[...].reshape(group * cap, ye_ref.shape[2]))

    @pl.when(eg == pl.num_programs(2) - 1)
    def _():
        y = acc_scr[...]
        ms = jnp.mean(y * y, axis=-1, keepdims=True)
        o_ref[...] = y * lax.rsqrt(ms + NORM_EPS) * nw_ref[...]


def moe_combine(x1, sp, ye, norm_w, bsz, seq, cap, tt=512, group=8):
    m, d = x1.shape
    nt = seq // tt
    return pl.pallas_call(
        functools.partial(_combine_kernel, cap=cap, group=group),
        grid=(bsz, nt, N_EXPERTS // group),
        in_specs=[
            pl.BlockSpec((tt, d), lambda b, i, e: (b * nt + i, 0)),
            pl.BlockSpec((tt, LANES), lambda b, i, e: (b * nt + i, 0)),
            pl.BlockSpec((group, cap, d), lambda b, i, e: (e, b, 0)),
            pl.BlockSpec((1, d), lambda b, i, e: (0, 0)),
        ],
        out_specs=pl.BlockSpec((tt, d), lambda b, i, e: (b * nt + i, 0)),
        out_shape=jax.ShapeDtypeStruct((m, d), F32),
        scratch_shapes=[pltpu.VMEM((tt, d), F32)],
        compiler_params=_params("parallel", "parallel", "arbitrary"),
        name="moe_combine",
    )(x1, sp, ye, norm_w.reshape(1, d))


def _pad_cols(w, width):
    return jnp.pad(w, [(0, 0)] * (w.ndim - 1) + [(0, width - w.shape[-1])])


def rwkv_mixer(pz, mu_prev, mu_next, w0, decay_up, a0, iclr_up, gate_up, k_k, k_a, r_k, ln_x_w, ln_x_b,
               bsz, seq):
    ab, rb, bt, kt, p_last, vb, bonus, g = rwkv_prep(
        pz, mu_prev, mu_next, w0, decay_up, a0, iclr_up, gate_up, k_k, k_a, r_k, bsz, seq)
    y_fwd, y_bwd = rwkv_scan(ab, rb, bt, kt, vb, p_last, bsz, seq)
    return rwkv_post(y_fwd, y_bwd, bonus, g, ln_x_w, ln_x_b)


def moe_block(x1, norm2_w, w_router, e_gate, e_up, e_down, norm_f_w, bsz, seq):
    cap = CAPACITY_FACTOR * seq // N_EXPERTS
    hb, afft = router(x1, norm2_w, w_router)
    sp, spt = topk_select(afft, bsz, seq, cap)
    xe, gates = moe_gather(spt, afft, hb, bsz, seq, cap)
    ye = moe_experts(xe, gates, e_gate, e_up, e_down)
    return moe_combine(x1, sp, ye, norm_f_w, bsz, seq, cap)


def kernel(x, positions, norm1_w, w_in, mu_prev, mu_next, lambda_q1, lambda_k1, lambda_q2, lambda_k2, subln_w, w0, decay_up, a0, iclr_up, gate_up, k_k, k_a, r_k, ln_x_w, ln_x_b, w_out, norm2_w, w_router, e_gate, e_up, e_down, norm_f_w):
    bsz, seq, d = x.shape
    m = bsz * seq
    xf = x.reshape(m, d)
    pos = positions.reshape(m, 1)
    lambda_init = 0.8 - 0.6 * math.exp(-0.3 * 0)

    h = rms_norm_bf16(xf, norm1_w[0])
    w_in_t = jnp.swapaxes(w_in, 1, 2)
    qkv = matmul_rope(h, w_in_t, pos)
    p_z = matmul(h, w_in_t, ATT_COLS, SHIFT_PAD, F32)

    att = attention(qkv, lambda_q1[0], lambda_k1[0], lambda_q2[0], lambda_k2[0],
                    subln_w[0], bsz, seq, lambda_init)

    rw = rwkv_mixer(p_z, _pad_cols(mu_prev, SHIFT_PAD), _pad_cols(mu_next, SHIFT_PAD),
                    w0[0], decay_up[0], a0[0], iclr_up[0], gate_up[0], k_k[0], k_a[0], r_k[0],
                    ln_x_w[0], ln_x_b[0], bsz, seq)

    wo = w_out[0].astype(BF16)
    x1 = out_proj(xf, att, rw, wo[:ATT_WIDTH], wo[ATT_WIDTH:])

    out = moe_block(x1, norm2_w[0], w_router[0], e_gate[0], e_up[0], e_down[0], norm_f_w, bsz, seq)
    return out.reshape(bsz, seq, d)
```

```python
import functools
import math

import jax
import jax.numpy as jnp
from jax import lax
from jax.experimental import pallas as pl
from jax.experimental.pallas import tpu as pltpu

F32 = jnp.float32
BF16 = jnp.bfloat16

LANES = 128
VMEM_LIMIT_BYTES = 56 * 1024 * 1024

D_MODEL = 2048
NORM_EPS = 1e-6
ATT_HEADS = 8
ATT_QK_DIM = 64
ATT_V_DIM = 128
ATT_WIDTH = ATT_HEADS * ATT_V_DIM
ATT_QK_COLS = ATT_HEADS * 2 * ATT_QK_DIM
ATT_COLS = 2 * ATT_QK_COLS + ATT_WIDTH
ROPE_THETA = 500000.0
ROPE_DIM = ATT_QK_DIM // 4
SUBLN_EPS = 1e-5
RWKV_WIDTH = 1024
RWKV_HEAD = 64
RWKV_HEADS = RWKV_WIDTH // RWKV_HEAD
DECAY_LORA = 64
ICLR_LORA = 64
GATE_LORA = 160
LORA_COLS = DECAY_LORA + ICLR_LORA + GATE_LORA
LORA_PAD = 384
GN_EPS = 64e-5
SHIFT_WIDTH = 3 * RWKV_WIDTH + LORA_COLS
SHIFT_PAD = 3 * RWKV_WIDTH + 512
N_EXPERTS = 16
CAPACITY_FACTOR = 2
EXPERT_FF = 2048
CHUNK = 64


def _params(*sem):
    return pltpu.CompilerParams(dimension_semantics=sem, vmem_limit_bytes=VMEM_LIMIT_BYTES)


def _dot(a, b):
    return jnp.dot(a, b, preferred_element_type=F32)


def _dot_nt(a, b):
    return lax.dot_general(a, b, (((1,), (1,)), ((), ())), preferred_element_type=F32)


def _split2(x):
    hi = x.astype(BF16)
    lo = (x - hi.astype(F32)).astype(BF16)
    return hi, lo


def _dot_lhs2(x, m_bf16):
    hi, lo = _split2(x)
    return _dot(hi, m_bf16) + _dot(lo, m_bf16)


def _rms_norm_kernel(x_ref, nw_ref, o_ref):
    x = x_ref[...]
    ms = jnp.mean(x * x, axis=-1, keepdims=True)
    o_ref[...] = (x * lax.rsqrt(ms + NORM_EPS) * nw_ref[...]).astype(o_ref.dtype)


def rms_norm_bf16(x, nw, tm=512):
    m, k = x.shape
    return pl.pallas_call(
        _rms_norm_kernel,
        grid=(m // tm,),
        in_specs=[pl.BlockSpec((tm, k), lambda i: (i, 0)), pl.BlockSpec((1, k), lambda i: (0, 0))],
        out_specs=pl.BlockSpec((tm, k), lambda i: (i, 0)),
        out_shape=jax.ShapeDtypeStruct((m, k), BF16),
        compiler_params=_params("parallel"),
        name="rms_norm",
    )(x, nw.reshape(1, k))


def _mm_kernel(h_ref, wt_ref, o_ref, *, j0, valid):
    tn = o_ref.shape[1]
    col = (j0 + pl.program_id(1)) * tn + lax.broadcasted_iota(jnp.int32, wt_ref.shape[1:], 0)
    wt = jnp.where(col < valid, wt_ref[0], 0.0).astype(BF16)
    o_ref[...] = _dot_nt(h_ref[...], wt).astype(o_ref.dtype)


def matmul(h, wt, col0, n, out_dtype, tm=2048, tn=512):
    m, k = h.shape
    j0 = col0 // tn
    return pl.pallas_call(
        functools.partial(_mm_kernel, j0=j0, valid=wt.shape[1]),
        grid=(m // tm, n // tn),
        in_specs=[
            pl.BlockSpec((tm, k), lambda i, j: (i, 0)),
            pl.BlockSpec((1, tn, k), lambda i, j: (0, j0 + j, 0)),
        ],
        out_specs=pl.BlockSpec((tm, tn), lambda i, j: (i, j)),
        out_shape=jax.ShapeDtypeStruct((m, n), out_dtype),
        compiler_params=_params("parallel", "arbitrary"),
        name="matmul",
    )(h, wt)


def _mm_rope_kernel(h_ref, wt_ref, pos_ref, freq_ref, slo_ref, shi_ref, o_ref,
                    c_scr, lo_scr, hi_scr, *, rope_tiles, q_tiles):
    j = pl.program_id(1)

    @pl.when(j == 0)
    def _():
        ang = pos_ref[...].astype(F32) * freq_ref[...]
        s = jnp.sin(ang)
        c_scr[...] = jnp.cos(ang)
        lo_scr[...] = s * slo_ref[...]
        hi_scr[...] = s * shi_ref[...]

    @pl.when(j < rope_tiles)
    def _():
        scale = jnp.where(j < q_tiles, ATT_QK_DIM ** -0.5 * math.log2(math.e), 1.0)
        half = ROPE_DIM // 2
        h = h_ref[...]
        part = 2 * LANES
        for p in range(o_ref.shape[1] // part):
            acc = _dot_nt(h, wt_ref[0, p * part:(p + 1) * part, :].astype(BF16))
            for g in range(part // LANES):
                x = acc[:, g * LANES:(g + 1) * LANES]
                x_dn = pltpu.roll(x, half, axis=1)
                x_up = pltpu.roll(x, LANES - half, axis=1)
                y = x * c_scr[...] + x_dn * hi_scr[...] + x_up * lo_scr[...]
                lanes = slice(p * part + g * LANES, p * part + (g + 1) * LANES)
                o_ref[:, lanes] = (y * scale).astype(o_ref.dtype)

    @pl.when(j >= rope_tiles)
    def _():
        o_ref[...] = _dot_nt(h_ref[...], wt_ref[0].astype(BF16)).astype(o_ref.dtype)


def matmul_rope(h, wt, pos, tm=2048, tn=512):
    m, k = h.shape
    n = ATT_COLS
    lane = jnp.arange(LANES) % ATT_QK_DIM
    half = ROPE_DIM // 2
    inv_freq = ROPE_THETA ** (-jnp.arange(0, ROPE_DIM, 2, dtype=F32) / ROPE_DIM)
    freq = jnp.where(lane < ROPE_DIM, inv_freq[lane % half], 0.0).astype(F32).reshape(1, LANES)
    sgn_lo = jnp.where(lane < half, -1.0, 0.0).astype(F32).reshape(1, LANES)
    sgn_hi = jnp.where((lane >= half) & (lane < ROPE_DIM), 1.0, 0.0).astype(F32).reshape(1, LANES)
    vec = pl.BlockSpec((1, LANES), lambda i, j: (0, 0))
    return pl.pallas_call(
        functools.partial(_mm_rope_kernel, rope_tiles=2 * ATT_QK_COLS // tn, q_tiles=ATT_QK_COLS // tn),
        grid=(m // tm, n // tn),
        in_specs=[
            pl.BlockSpec((tm, k), lambda i, j: (i, 0)),
            pl.BlockSpec((1, tn, k), lambda i, j: (0, j, 0)),
            pl.BlockSpec((tm, 1), lambda i, j: (i, 0)),
            vec, vec, vec,
        ],
        out_specs=pl.BlockSpec((tm, tn), lambda i, j: (i, j)),
        out_shape=jax.ShapeDtypeStruct((m, n), BF16),
        scratch_shapes=[pltpu.VMEM((tm, LANES), F32), pltpu.VMEM((tm, LANES), F32),
                        pltpu.VMEM((tm, LANES), F32)],
        compiler_params=_params("parallel", "arbitrary"),
        name="matmul_rope",
    )(h, wt, pos, freq, sgn_lo, sgn_hi)


def _attn_kernel(q_ref, qn_ref, k_ref, v_ref, lq1_ref, lk1_ref, lq2_ref, lk2_ref, sw_ref, o_ref,
                 v_scr, sa_scr, ma_scr, sb_scr, mb_scr, *, lambda_init, heads):
    dv = ATT_V_DIM
    i = pl.program_id(2)

    def scores(q, s_scr, m_scr):
        for hc in range(2 * heads):
            cols = slice(hc * ATT_QK_DIM, (hc + 1) * ATT_QK_DIM)
            s = _dot_nt(q[:, cols], k_ref[:, cols])
            s_scr[hc] = s
            m_scr[hc] = jnp.max(s, axis=-1, keepdims=True)

    @pl.when(i == 0)
    def _():
        lane = lax.broadcasted_iota(jnp.int32, (v_scr.shape[1], dv), 1)
        for h in range(heads):
            v_scr[h, :, :dv] = v_ref[:, h * dv:(h + 1) * dv]
            v_scr[h, :, dv:] = (lane == 0).astype(BF16)
        scores(q_ref[...], sa_scr, ma_scr)

    def step(cur_s, cur_m, nxt_s, nxt_m):
        if nxt_s is not None:
            scores(qn_ref[...], nxt_s, nxt_m)
        lam =(jnp.exp(jnp.sum(lq1_ref[...] * lk1_ref[...], axis=-1, keepdims=True))
               - jnp.exp(jnp.sum(lq2_ref[...] * lk2_ref[...], axis=-1, keepdims=True)) + lambda_init)
        for h in range(heads):
            va = v_scr[h]
            o1 = _dot(jnp.exp2(cur_s[2 * h] - cur_m[2 * h]).astype(BF16), va)
            o2 = _dot(jnp.exp2(cur_s[2 * h + 1] - cur_m[2 * h + 1]).astype(BF16), va)
            o = o1[:, :dv] / o1[:, dv:dv + 1] - o2[:, :dv] * (lam / o2[:, dv:dv + 1])
            ms = jnp.mean(o * o, axis=-1, keepdims=True)
            o = o * lax.rsqrt(ms + SUBLN_EPS) * sw_ref[...] * (1.0 - lambda_init)
            o_ref[:, h * dv:(h + 1) * dv] = o.astype(o_ref.dtype)

    last = i == pl.num_programs(2) - 1
    even = i % 2 == 0

    @pl.when(even & ~last)
    def _():
        step(sa_scr, ma_scr, sb_scr, mb_scr)

    @pl.when(~even & ~last)
    def _():
        step(sb_scr, mb_scr, sa_scr, ma_scr)

    @pl.when(even & last)
    def _():
        step(sa_scr, ma_scr, None, None)

    @pl.when(~even & last)
    def _():
        step(sb_scr, mb_scr, None, None)


def attention(qkv, lq1, lk1, lq2, lk2, subln_w, bsz, seq, lambda_init, tq=256, heads=4):
    nq = seq // tq
    vec = lambda n: pl.BlockSpec((1, n), lambda b, h, i: (0, 0))
    ng = ATT_HEADS // heads
    width = heads * LANES
    s_buf = pltpu.VMEM((2 * heads, tq, seq), F32)
    m_buf = pltpu.VMEM((2 * heads, tq, 1), F32)
    return pl.pallas_call(
        functools.partial(_attn_kernel, lambda_init=lambda_init, heads=heads),
        grid=(bsz, ng, nq),
        in_specs=[
            pl.BlockSpec((tq, width), lambda b, h, i: (b * nq + i, h)),
            pl.BlockSpec((tq, width), lambda b, h, i: (b * nq + jnp.minimum(i + 1, nq - 1), h)),
            pl.BlockSpec((seq, width), lambda b, h, i: (b, ng + h)),
            pl.BlockSpec((seq, width), lambda b, h, i: (b, 2 * ng + h)),
            vec(ATT_QK_DIM), vec(ATT_QK_DIM), vec(ATT_QK_DIM), vec(ATT_QK_DIM),
            vec(ATT_V_DIM),
        ],
        out_specs=pl.BlockSpec((tq, width), lambda b, h, i: (b * nq + i, h)),
        out_shape=jax.ShapeDtypeStruct((bsz * seq, ATT_WIDTH), BF16),
        scratch_shapes=[pltpu.VMEM((heads, seq, 2 * ATT_V_DIM), BF16), s_buf, m_buf, s_buf, m_buf],
        compiler_params=_params("parallel", "parallel", "arbitrary"),
        name="diff_attention",
    )(qkv, qkv, qkv, qkv,
      lq1.reshape(1, -1), lk1.reshape(1, -1), lq2.reshape(1, -1), lk2.reshape(1, -1),
      subln_w.reshape(1, -1))


HALO = 8


def _token_shift(z_ref, prev_ref, next_ref, mu_p, mu_n, first, last):
    z = z_ref[...]
    n = z.shape[0]
    row = lax.broadcasted_iota(jnp.int32, z.shape, 0)
    before = jnp.where(first, 0.0, prev_ref[HALO - 1:HALO, :])
    after = jnp.where(last, 0.0, next_ref[0:1, :])
    zp = jnp.where(row == 0, before, pltpu.roll(z, 1, axis=0))
    zn = jnp.where(row == n - 1, after, pltpu.roll(z, n - 1, axis=0))
    return z + mu_p * (zp - z) + mu_n * (zn - z)


def _seg_sum(x, ones_blk):
    parts = []
    for j in range(x.shape[1] // LANES):
        parts.append(_dot_lhs2(x[:, j * LANES:(j + 1) * LANES], ones_blk))
    return jnp.concatenate(parts, axis=1)


def _prep_kernel(r_ref, k_ref, v_ref, lo_ref, rp_ref, kp_ref, vp_ref, lp_ref, rn_ref, kn_ref, vn_ref,
                 ln_ref, mup_ref, mun_ref, w0_ref, du_ref, a0_ref, iu_ref, gu_ref, kk_ref,
                 ka_ref, rk_ref, ones_ref, tril_ref, triu_ref,
                 ab_ref, rb_ref, bt_ref, kt_ref, pl_ref, vb_ref, bonus_ref, g_ref, *, tt):
    first = pl.program_id(1) == 0
    last = pl.program_id(1) == pl.num_programs(1) - 1
    c = RWKV_WIDTH
    shift = lambda z, p, n, lo_col, hi_col: _token_shift(
        z, p, n, mup_ref[:, lo_col:hi_col], mun_ref[:, lo_col:hi_col], first, last)
    r = shift(r_ref, rp_ref, rn_ref, 0, c)
    k = shift(k_ref, kp_ref, kn_ref, c, 2 * c)
    v = shift(v_ref, vp_ref, vn_ref, 2 * c, 3 * c)
    lo = shift(lo_ref, lp_ref, ln_ref, 3 * c, 3 * c + LORA_PAD)
    ones_blk = ones_ref[...]
    vb_ref[...] = v.astype(BF16)
    g_ref[...] = _dot(jax.nn.sigmoid(lo).astype(BF16), gu_ref[...])
    kk = k * kk_ref[...]
    kk = kk * jnp.minimum(lax.rsqrt(_seg_sum(kk * kk, ones_blk)), 1e12)
    th = jnp.tanh(lo).astype(BF16)
    lob = lo.astype(BF16)
    nchunk = tt // CHUNK
    ksum = jnp.zeros_like(k)
    for d in range(2):
        wl = w0_ref[d] + _dot(th, du_ref[d])
        lw = -math.exp(-0.5) * jax.nn.sigmoid(wl)
        a = jax.nn.sigmoid(a0_ref[d] + _dot(lob, iu_ref[d]))
        kd = k * (1.0 + (a - 1.0) * ka_ref[...])
        ksum = ksum + kd
        tri = tril_ref[...] if d == 0 else triu_ref[...]
        cum = _dot_lhs2_rhs(tri, lw)
        e_pos = jnp.exp(cum)
        e_neg = jnp.exp(-cum)
        ab_ref[d] = (-kk * jnp.exp(cum - lw)).astype(BF16)
        rb_ref[d] = (r * e_pos).astype(BF16)
        bt_ref[d] = (kk * a * e_neg).astype(BF16)
        kt_ref[d] = (kd * e_neg).astype(BF16)
        for c in range(nchunk):
            last = c * CHUNK + (CHUNK - 1 if d == 0 else 0)
            pl_ref[d, c] = e_pos[last:last + 1, :]
    bonus_ref[...] = _seg_sum(r * ksum * rk_ref[...], ones_blk) * v


def _dot_lhs2_rhs(tri_bf16, x):
    hi, lo = _split2(x)
    return _dot(tri_bf16, hi) + _dot(tri_bf16, lo)


def rwkv_prep(pz, mu_prev, mu_next, w0, decay_up, a0, iclr_up, gate_up, k_k, k_a, r_k, bsz, seq, tt=256):
    m = bsz * seq
    c = RWKV_WIDTH
    nt = seq // tt
    ncb = tt // CHUNK
    nc = seq // CHUNK

    def pad_rows(w, start):
        out = jnp.zeros(w.shape[:-2] + (LORA_PAD, c), F32)
        return lax.dynamic_update_slice_in_dim(out, w.astype(F32), start, axis=w.ndim - 2).astype(BF16)

    du = pad_rows(decay_up, 0)
    iu = pad_rows(iclr_up, DECAY_LORA)
    gu = pad_rows(gate_up, DECAY_LORA + ICLR_LORA)
    lane = jnp.arange(LANES)
    ones_blk = (lane[:, None] // RWKV_HEAD == lane[None, :] // RWKV_HEAD).astype(BF16)
    t = jnp.arange(tt)
    same = t[:, None] // CHUNK == t[None, :] // CHUNK
    tril = (same & (t[:, None] >= t[None, :])).astype(BF16)
    triu = (same & (t[:, None] <= t[None, :])).astype(BF16)
    r_k_flat = r_k.reshape(1, c)
    row = lambda: pl.BlockSpec((1, c), lambda b, i: (0, 0))
    full3 = lambda s: pl.BlockSpec(s, lambda b, i: (0, 0, 0))
    tok = lambda j, w: pl.BlockSpec((tt, w), lambda b, i: (b * nt + i, j))
    per = tt // HALO
    prev = lambda j, w: pl.BlockSpec((HALO, w), lambda b, i: (jnp.maximum((b * nt + i) * per - 1, 0), j))
    nxt = lambda j, w: pl.BlockSpec(
        (HALO, w), lambda b, i: (jnp.minimum((b * nt + i + 1) * per, m // HALO - 1), j))
    lora_j = 3 * c // LORA_PAD
    cols = [(0, c), (1, c), (2, c), (lora_j, LORA_PAD)]
    mu_row = pl.BlockSpec((1, pz.shape[1]), lambda b, i: (0, 0))
    dir_tok = pl.BlockSpec((2, tt, c), lambda b, i: (0, b * nt + i, 0))
    outs = pl.pallas_call(
        functools.partial(_prep_kernel, tt=tt),
        grid=(bsz, nt),
        in_specs=[tok(j, w) for j, w in cols] + [prev(j, w) for j, w in cols]
        + [nxt(j, w) for j, w in cols] + [
            mu_row, mu_row,
            full3((2, 1, c)), full3((2, LORA_PAD, c)), full3((2, 1, c)), full3((2, LORA_PAD, c)),
            pl.BlockSpec((LORA_PAD, c), lambda b, i: (0, 0)),
            row(), row(), row(),
            pl.BlockSpec((LANES, LANES), lambda b, i: (0, 0)),
            pl.BlockSpec((tt, tt), lambda b, i: (0, 0)),
            pl.BlockSpec((tt, tt), lambda b, i: (0, 0)),
        ],
        out_specs=[
            dir_tok, dir_tok, dir_tok, dir_tok,
            pl.BlockSpec((2, ncb, 1, c), lambda b, i: (0, b * nt + i, 0, 0)),
            pl.BlockSpec((tt, c), lambda b, i: (b * nt + i, 0)),
            pl.BlockSpec((tt, c), lambda b, i: (b * nt + i, 0)),
            pl.BlockSpec((tt, c), lambda b, i: (b * nt + i, 0)),
        ],
        out_shape=[
            jax.ShapeDtypeStruct((2, m, c), BF16),
            jax.ShapeDtypeStruct((2, m, c), BF16),
            jax.ShapeDtypeStruct((2, m, c), BF16),
            jax.ShapeDtypeStruct((2, m, c), BF16),
            jax.ShapeDtypeStruct((2, bsz * nc, 1, c), F32),
            jax.ShapeDtypeStruct((m, c), BF16),
            jax.ShapeDtypeStruct((m, c), F32),
            jax.ShapeDtypeStruct((m, c), F32),
        ],
        compiler_params=_params("parallel", "parallel"),
        name="rwkv_prep",
    )(*([pz] * 12), mu_prev, mu_next, w0.reshape(2, 1, c), du, a0.reshape(2, 1, c), iu, gu,
      k_k.reshape(1, c), k_a.reshape(1, c), r_k_flat, ones_blk, tril, triu)
    return outs


def _scan_kernel(abf_ref, abb_ref, rbf_ref, rbb_ref, btf_ref, btb_ref, ktf_ref, ktb_ref,
                 vf_ref, vb_ref, plf_ref, plb_ref, yf_ref, yb_ref, m_scr, *, heads, cpb):
    @pl.when(pl.program_id(1) == 0)
    def _():
        m_scr[...] = jnp.zeros_like(m_scr)

    ab_ref = (abf_ref, abb_ref)
    rb_ref = (rbf_ref, rbb_ref)
    bt_ref = (btf_ref, btb_ref)
    kt_ref = (ktf_ref, ktb_ref)
    v_ref = (vf_ref, vb_ref)
    pl_ref = (plf_ref, plb_ref)
    y_ref = (yf_ref, yb_ref)
    n = CHUNK
    row = lax.broadcasted_iota(jnp.int32, (2 * n, 2 * n), 0)
    col = lax.broadcasted_iota(jnp.int32, (2 * n, 2 * n), 1)
    top = row < n
    tr = row % n
    tc = col % n
    eye = (lax.broadcasted_iota(jnp.int32, (n, n), 0)
           == lax.broadcasted_iota(jnp.int32, (n, n), 1)).astype(F32)
    keep = ((tr > tc) | (~top & (tr == tc)), (tr < tc) | (~top & (tr == tc)))
    probs = [(sub, d, hh) for sub in range(cpb) for d in range(2) for hh in range(heads)]
    sl = lambda hh: slice(hh * n, (hh + 1) * n)
    chunk_of = lambda sub, d: sub if d == 0 else cpb - 1 - sub
    rows = lambda sub, d: slice(chunk_of(sub, d) * n, (chunk_of(sub, d) + 1) * n)
    ab = [ab_ref[d][0, rows(sub, d), sl(hh)] for sub, d, hh in probs]
    rb = [rb_ref[d][0, rows(sub, d), sl(hh)] for sub, d, hh in probs]
    bt = [bt_ref[d][0, rows(sub, d), sl(hh)] for sub, d, hh in probs]
    kt = [kt_ref[d][0, rows(sub, d), sl(hh)] for sub, d, hh in probs]
    vv = [v_ref[d][rows(sub, d), sl(hh)] for sub, d, hh in probs]
    p_last = [pl_ref[d][0, chunk_of(sub, d), :, sl(hh)] for sub, d, hh in probs]
    idx = range(len(probs))
    g1 = [_dot_nt(jnp.concatenate([ab[i], rb[i]], axis=0), jnp.concatenate([bt[i], kt[i]], axis=0))
          for i in idx]
    g1 = [jnp.where(keep[probs[i][1]], g1[i], 0.0) for i in idx]
    a_ab = [g[:n, :n] for g in g1]
    low = [g[n:, :].astype(BF16) for g in g1]
    tm = [eye + a for a in a_ab]
    pw = [_dot(a.astype(BF16), a.astype(BF16)) for a in a_ab]
    span = 2
    while span < n:
        last = span * 2 >= n
        nxt = []
        for i in idx:
            pwb = pw[i].astype(BF16)
            if last:
                nxt.append((tm[i] + _dot(tm[i].astype(BF16), pwb), None))
            else:
                both = _dot(jnp.concatenate([tm[i], pw[i]], axis=0).astype(BF16), pwb)
                nxt.append((tm[i] + both[:n], both[n:]))
        tm = [t for t, _ in nxt]
        pw = [p for _, p in nxt]
        span *= 2
    wc = [_dot(tm[i].astype(BF16), jnp.concatenate([ab[i], g1[i][:n, n:].astype(BF16)], axis=1)) for i in idx]
    w = [x[:, :n].astype(BF16) for x in wc]
    cuv = [_dot(wc[i][:, n:].astype(BF16), vv[i]).astype(BF16) for i in idx]
    zero = jnp.zeros((n, n), BF16)
    rhs = [jnp.concatenate([jnp.concatenate([w[i], cuv[i]], axis=1),
                            jnp.concatenate([zero, vv[i]], axis=1)], axis=0) for i in idx]
    bh_t = [(bt[i].astype(F32) * p_last[i]).T for i in idx]
    kh_t = [(kt[i].astype(F32) * p_last[i]).T for i in idx]
    lhs = [jnp.concatenate([low[i], jnp.concatenate([bh_t[i], kh_t[i]], axis=1).astype(BF16)], axis=0)
           for i in idx]
    out = [_dot(lhs[i], rhs[i]) for i in idx]
    q_m = [rb[i].astype(F32) + out[i][:n, :n] for i in idx]
    g_m = [eye * p_last[i] + out[i][n:, :n] for i in idx]
    state = {(d, hh): m_scr[d, hh] for d in range(2) for hh in range(heads)}
    for i, (sub, d, hh) in enumerate(probs):
        upd = _dot(jnp.concatenate([q_m[i], g_m[i]], axis=0).astype(BF16), state[d, hh].astype(BF16))
        y_ref[d][rows(sub, d), sl(hh)] = upd[:n] + out[i][:n, n:]
        state[d, hh] = upd[n:] + out[i][n:, n:]
    for (d, hh), m_new in state.items():
        m_scr[d, hh] = m_new


def rwkv_scan(ab, rb, bt, kt, vb, p_last, bsz, seq, cpb=2):
    n = CHUNK
    nb = seq // (CHUNK * cpb)
    nh = RWKV_HEADS
    c_w = RWKV_WIDTH
    fwd = lambda b, c: b * nb + c
    bwd = lambda b, c: b * nb + nb - 1 - c
    dir_f = pl.BlockSpec((1, cpb * n, c_w), lambda b, c: (0, fwd(b, c), 0))
    dir_b = pl.BlockSpec((1, cpb * n, c_w), lambda b, c: (1, bwd(b, c), 0))
    tok_f = pl.BlockSpec((cpb * n, c_w), lambda b, c: (fwd(b, c), 0))
    tok_b = pl.BlockSpec((cpb * n, c_w), lambda b, c: (bwd(b, c), 0))
    pl_f = pl.BlockSpec((1, cpb, 1, c_w), lambda b, c: (0, fwd(b, c), 0, 0))
    pl_b = pl.BlockSpec((1, cpb, 1, c_w), lambda b, c: (1, bwd(b, c), 0, 0))
    y_shape = jax.ShapeDtypeStruct((bsz * seq, c_w), F32)
    return pl.pallas_call(
        functools.partial(_scan_kernel, heads=nh, cpb=cpb),
        grid=(bsz, nb),
        in_specs=[dir_f, dir_b, dir_f, dir_b, dir_f, dir_b, dir_f, dir_b, tok_f, tok_b, pl_f, pl_b],
        out_specs=[tok_f, tok_b],
        out_shape=[y_shape, y_shape],
        scratch_shapes=[pltpu.VMEM((2, nh, n, n), F32)],
        compiler_params=_params("parallel", "arbitrary"),
        name="rwkv_scan",
    )(ab, ab, rb, rb, bt, bt, kt, kt, vb, vb, p_last, p_last)


def _post_kernel(yf_ref, yb_ref, bonus_ref, g_ref, lw_ref, lb_ref, o_ref):
    y = yf_ref[...] + yb_ref[...]
    n = RWKV_HEAD
    parts = []
    for h in range(RWKV_HEADS):
        yh = y[:, h * n:(h + 1) * n]
        mu = jnp.mean(yh, axis=-1, keepdims=True)
        yc = yh - mu
        var = jnp.mean(yc * yc, axis=-1, keepdims=True)
        parts.append(yc * lax.rsqrt(var + GN_EPS))
    yn = jnp.concatenate(parts, axis=1)
    o_ref[...] = ((yn * lw_ref[...] + lb_ref[...] + bonus_ref[...]) * g_ref[...]).astype(o_ref.dtype)


def rwkv_post(y_fwd, y_bwd, bonus, g, ln_w, ln_b, tt=256):
    m = y_fwd.shape[0]
    c = RWKV_WIDTH
    tok = pl.BlockSpec((tt, c), lambda i: (i, 0))
    row = pl.BlockSpec((1, c), lambda i: (0, 0))
    return pl.pallas_call(
        _post_kernel,
        grid=(m // tt,),
        in_specs=[tok, tok, tok, tok, row, row],
        out_specs=tok,
        out_shape=jax.ShapeDtypeStruct((m, c), BF16),
        compiler_params=_params("parallel"),
        name="rwkv_post",
    )(y_fwd, y_bwd, bonus, g, ln_w.reshape(1, c), ln_b.reshape(1, c))


def _out_proj_kernel(x_ref, a_ref, r_ref, wa_ref, wr_ref, o_ref):
    o_ref[...] = (x_ref[...] + _dot(a_ref[...], wa_ref[0].astype(BF16))
                  + _dot(r_ref[...], wr_ref[0].astype(BF16)))


def out_proj(x, att, rw, w_out, tm=2048, tn=512):
    m, n = x.shape
    assert ATT_WIDTH == RWKV_WIDTH
    return pl.pallas_call(
        _out_proj_kernel,
        grid=(m // tm, n // tn),
        in_specs=[
            pl.BlockSpec((tm, tn), lambda i, j: (i, j)),
            pl.BlockSpec((tm, ATT_WIDTH), lambda i, j: (i, 0)),
            pl.BlockSpec((tm, RWKV_WIDTH), lambda i, j: (i, 0)),
            pl.BlockSpec((1, ATT_WIDTH, tn), lambda i, j: (0, 0, j)),
            pl.BlockSpec((1, RWKV_WIDTH, tn), lambda i, j: (0, 1, j)),
        ],
        out_specs=pl.BlockSpec((tm, tn), lambda i, j: (i, j)),
        out_shape=jax.ShapeDtypeStruct((m, n), F32),
        compiler_params=_params("parallel", "parallel"),
        name="out_proj",
    )(x, att, rw, w_out, w_out)


def _router_kernel(x_ref, nw_ref, wh_ref, wl_ref, hb_ref, afft_ref):
    x = x_ref[...]
    ms = jnp.mean(x * x, axis=-1, keepdims=True)
    h = x * lax.rsqrt(ms + NORM_EPS) * nw_ref[...]
    hb_ref[...] = h.astype(BF16)
    hi, lo = _split2(h)
    logits = _dot(hi, wh_ref[...]) + _dot(lo, wh_ref[...]) + _dot(hi, wl_ref[...])
    lane = lax.broadcasted_iota(jnp.int32, logits.shape, 1)
    valid = lane < N_EXPERTS
    logits = jnp.where(valid, logits, -1e30)
    e = jnp.where(valid, jnp.exp(logits - jnp.max(logits, axis=-1, keepdims=True)), 0.0)
    aff = e / jnp.sum(e, axis=-1, keepdims=True)
    afft_ref[...] = aff.T[:N_EXPERTS, :]


def router(x1, norm_w, w_router, tm=512):
    m, k = x1.shape
    wpad = jnp.zeros((k, LANES), F32).at[:, :N_EXPERTS].set(w_router)
    wh, wl = _split2(wpad)
    return pl.pallas_call(
        _router_kernel,
        grid=(m // tm,),
        in_specs=[
            pl.BlockSpec((tm, k), lambda i: (i, 0)),
            pl.BlockSpec((1, k), lambda i: (0, 0)),
            pl.BlockSpec((k, LANES), lambda i: (0, 0)),
            pl.BlockSpec((k, LANES), lambda i: (0, 0)),
        ],
        out_specs=[
            pl.BlockSpec((tm, k), lambda i: (i, 0)),
            pl.BlockSpec((N_EXPERTS, tm), lambda i: (0, i)),
        ],
        out_shape=[
            jax.ShapeDtypeStruct((m, k), BF16),
            jax.ShapeDtypeStruct((N_EXPERTS, m), F32),
        ],
        compiler_params=_params("parallel"),
        name="router",
    )(x1, norm_w.reshape(1, k), wh, wl)


def _prefix_count(x, upper):
    outs = []
    carry = jnp.zeros((x.shape[0], 1), F32)
    for j in range(x.shape[1] // LANES):
        xt = x[:, j * LANES:(j + 1) * LANES]
        outs.append(_dot(xt.astype(BF16), upper) + carry)
        carry = carry + jnp.sum(xt, axis=-1, keepdims=True)
    return jnp.concatenate(outs, axis=1)


BISECT_STEPS = 152


def _topk_kernel(afft_ref, sp_ref, spt_ref, *, cap, bsz):
    ne = afft_ref.shape[0]
    seq = afft_ref.shape[1] // bsz
    a = [afft_ref[:, b * seq:(b + 1) * seq] for b in range(bsz)]

    def halve(_, bracket):
        out = []
        for b in range(bsz):
            lo, hi = bracket[b]
            mid = (lo + hi) * 0.5
            cnt = jnp.sum((a[b] >= mid).astype(F32), axis=-1, keepdims=True)
            enough = cnt >= cap
            out.append((jnp.where(enough, mid, lo), jnp.where(enough, hi, mid)))
        return tuple(out)

    start = tuple((jnp.zeros((ne, 1), F32), jnp.full((ne, 1), 2.0, F32)) for _ in range(bsz))
    bracket = lax.fori_loop(0, BISECT_STEPS, halve, start)
    r_i = lax.broadcasted_iota(jnp.int32, (LANES, LANES), 0)
    c_i = lax.broadcasted_iota(jnp.int32, (LANES, LANES), 1)
    upper = (r_i < c_i).astype(BF16)
    for b in range(bsz):
        lo, hi = bracket[b]
        above = (a[b] >= hi).astype(F32)
        tied = ((a[b] >= lo) & (a[b] < hi)).astype(F32)
        need = cap - jnp.sum(above, axis=-1, keepdims=True)
        sel = above + tied * (_prefix_count(tied, upper) < need).astype(F32)
        spt = jnp.where(sel > 0.5, _prefix_count(sel, upper), -1.0)
        spt_ref[:, b * seq:(b + 1) * seq] = spt
        full = jnp.concatenate([spt, jnp.full((LANES - ne, seq), -1.0, F32)], axis=0)
        sp_ref[b * seq:(b + 1) * seq, :] = full.T


def topk_select(afft, bsz, seq, cap):
    return pl.pallas_call(
        functools.partial(_topk_kernel, cap=cap, bsz=bsz),
        grid=(1,),
        in_specs=[pl.BlockSpec((N_EXPERTS, bsz * seq), lambda i: (0, 0))],
        out_specs=[
            pl.BlockSpec((bsz * seq, LANES), lambda i: (0, 0)),
            pl.BlockSpec((N_EXPERTS, bsz * seq), lambda i: (0, 0)),
        ],
        out_shape=[
            jax.ShapeDtypeStruct((bsz * seq, LANES), F32),
            jax.ShapeDtypeStruct((N_EXPERTS, bsz * seq), F32),
        ],
        compiler_params=_params("arbitrary"),
        name="topk_select",
    )(afft)


def _gather_kernel(spt_ref, afft_ref, h_ref, xe_ref, gate_ref, *, cap, group):
    eg = pl.program_id(1)
    seq = h_ref.shape[0]
    slot = lax.broadcasted_iota(jnp.int32, (cap, seq), 0).astype(F32)
    hits = []
    for k in range(group):
        e = eg * group + k
        hit = spt_ref[pl.ds(e, 1), :] == slot
        gate_ref[k] = jnp.sum(jnp.where(hit, afft_ref[pl.ds(e, 1), :], 0.0), axis=-1, keepdims=True)
        hits.append(hit.astype(BF16))
    xe = _dot(jnp.concatenate(hits, axis=0), h_ref[...]).astype(BF16)
    xe_ref[...] = xe.reshape(group, cap, xe.shape[1])


def moe_gather(spt, afft, hb, bsz, seq, cap, group=4):
    d = hb.shape[1]
    return pl.pallas_call(
        functools.partial(_gather_kernel, cap=cap, group=group),
        grid=(bsz, N_EXPERTS // group),
        in_specs=[
            pl.BlockSpec((N_EXPERTS, seq), lambda b, e: (0, b)),
            pl.BlockSpec((N_EXPERTS, seq), lambda b, e: (0, b)),
            pl.BlockSpec((seq, d), lambda b, e: (b, 0)),
        ],
        out_specs=[
            pl.BlockSpec((group, cap, d), lambda b, e: (e, b, 0)),
            pl.BlockSpec((group, cap, 1), lambda b, e: (e, b, 0)),
        ],
        out_shape=[
            jax.ShapeDtypeStruct((N_EXPERTS, bsz * cap, d), BF16),
            jax.ShapeDtypeStruct((N_EXPERTS, bsz * cap, 1), F32),
        ],
        compiler_params=_params("parallel", "arbitrary"),
        name="moe_gather",
    )(spt, afft, hb)


def _expert_kernel(xe_ref, gt_ref, wg_ref, wu_ref, wd_ref, ye_ref, acc_scr):
    f = pl.program_id(1)

    @pl.when(f == 0)
    def _():
        acc_scr[...] = jnp.zeros_like(acc_scr)

    xe = xe_ref[0]
    gate = _dot(xe, wg_ref[0].astype(BF16))
    up = _dot(xe, wu_ref[0].astype(BF16))
    hid = (gate * jax.nn.sigmoid(gate) * up).astype(BF16)
    acc_scr[...] += _dot(hid, wd_ref[0].astype(BF16))

    @pl.when(f == pl.num_programs(1) - 1)
    def _():
        ye_ref[0] = (acc_scr[...] * gt_ref[0]).astype(ye_ref.dtype)


def moe_experts(xe, gates, e_gate, e_up, e_down, tf=512):
    ne, rows, d = xe.shape
    ff = e_gate.shape[2]
    return pl.pallas_call(
        _expert_kernel,
        grid=(ne, ff // tf),
        in_specs=[
            pl.BlockSpec((1, rows, d), lambda e, f: (e, 0, 0)),
            pl.BlockSpec((1, rows, 1), lambda e, f: (e, 0, 0)),
            pl.BlockSpec((1, d, tf), lambda e, f: (e, 0, f)),
            pl.BlockSpec((1, d, tf), lambda e, f: (e, 0, f)),
            pl.BlockSpec((1, tf, d), lambda e, f: (e, f, 0)),
        ],
        out_specs=pl.BlockSpec((1, rows, d), lambda e, f: (e, 0, 0)),
        out_shape=jax.ShapeDtypeStruct((ne, rows, d), BF16),
        scratch_shapes=[pltpu.VMEM((rows, d), F32)],
        compiler_params=_params("parallel", "arbitrary"),
        name="moe_experts",
    )(xe, gates, e_gate, e_up, e_down)


def _combine_kernel(x_ref, sp_ref, ye_ref, nw_ref, o_ref, acc_scr, *, cap, group):
    eg = pl.program_id(2)

    @pl.when(eg == 0)
    def _():
        acc_scr[...] = x_ref[...]

    tt = x_ref.shape[0]
    lane = lax.broadcasted_iota(jnp.int32, (tt, LANES), 1)
    slot = lax.broadcasted_iota(jnp.int32, (tt, cap), 1).astype(F32)
    sp_all = sp_ref[...]
    hits = []
    for k in range(group):
        sp = jnp.sum(jnp.where(lane == eg * group + k, sp_all, 0.0), axis=-1, keepdims=True)
        hits.append((sp == slot).astype(BF16))
    onehot = jnp.concatenate(hits, axis=1)
    acc_scr[...] += _dot(onehot, ye_ref[...].reshape(group * cap, ye_ref.shape[2]))

    @pl.when(eg == pl.num_programs(2) - 1)
    def _():
        y = acc_scr[...]
        ms = jnp.mean(y * y, axis=-1, keepdims=True)
        o_ref[...] = y * lax.rsqrt(ms + NORM_EPS) * nw_ref[...]


def moe_combine(x1, sp, ye, norm_w, bsz, seq, cap, tt=512, group=8):
    m, d = x1.shape
    nt = seq // tt
    return pl.pallas_call(
        functools.partial(_combine_kernel, cap=cap, group=group),
        grid=(bsz, nt, N_EXPERTS // group),
        in_specs=[
            pl.BlockSpec((tt, d), lambda b, i, e: (b * nt + i, 0)),
            pl.BlockSpec((tt, LANES), lambda b, i, e: (b * nt + i, 0)),
            pl.BlockSpec((group, cap, d), lambda b, i, e: (e, b, 0)),
            pl.BlockSpec((1, d), lambda b, i, e: (0, 0)),
        ],
        out_specs=pl.BlockSpec((tt, d), lambda b, i, e: (b * nt + i, 0)),
        out_shape=jax.ShapeDtypeStruct((m, d), F32),
        scratch_shapes=[pltpu.VMEM((tt, d), F32)],
        compiler_params=_params("parallel", "parallel", "arbitrary"),
        name="moe_combine",
    )(x1, sp, ye, norm_w.reshape(1, d))


def _pad_cols(w, width):
    return jnp.pad(w, [(0, 0)] * (w.ndim - 1) + [(0, width - w.shape[-1])])


def rwkv_mixer(pz, mu_prev, mu_next, w0, decay_up, a0, iclr_up, gate_up, k_k, k_a, r_k, ln_x_w, ln_x_b,
               bsz, seq):
    ab, rb, bt, kt, p_last, vb, bonus, g = rwkv_prep(
        pz, mu_prev, mu_next, w0, decay_up, a0, iclr_up, gate_up, k_k, k_a, r_k, bsz, seq)
    y_fwd, y_bwd = rwkv_scan(ab, rb, bt, kt, vb, p_last, bsz, seq)
    return rwkv_post(y_fwd, y_bwd, bonus, g, ln_x_w, ln_x_b)


def moe_block(x1, norm2_w, w_router, e_gate, e_up, e_down, norm_f_w, bsz, seq):
    cap = CAPACITY_FACTOR * seq // N_EXPERTS
    hb, afft = router(x1, norm2_w, w_router)
    sp, spt = topk_select(afft, bsz, seq, cap)
    xe, gates = moe_gather(spt, afft, hb, bsz, seq, cap)
    ye = moe_experts(xe, gates, e_gate, e_up, e_down)
    return moe_combine(x1, sp, ye, norm_f_w, bsz, seq, cap)


def kernel(x, positions, norm1_w, w_in, mu_prev, mu_next, lambda_q1, lambda_k1, lambda_q2, lambda_k2, subln_w, w0, decay_up, a0, iclr_up, gate_up, k_k, k_a, r_k, ln_x_w, ln_x_b, w_out, norm2_w, w_router, e_gate, e_up, e_down, norm_f_w):
    bsz, seq, d = x.shape
    m = bsz * seq
    xf = x.reshape(m, d)
    pos = positions.reshape(m, 1)
    lambda_init = 0.8 - 0.6 * math.exp(-0.3 * 0)

    h = rms_norm_bf16(xf, norm1_w[0])
    w_in_t = jnp.swapaxes(w_in, 1, 2)
    qkv = matmul_rope(h, w_in_t, pos)
    p_z = matmul(h, w_in_t, ATT_COLS, SHIFT_PAD, F32)

    att = attention(qkv, lambda_q1[0], lambda_k1[0], lambda_q2[0], lambda_k2[0],
                    subln_w[0], bsz, seq, lambda_init)

    rw = rwkv_mixer(p_z, _pad_cols(mu_prev, SHIFT_PAD), _pad_cols(mu_next, SHIFT_PAD),
                    w0[0], decay_up[0], a0[0], iclr_up[0], gate_up[0], k_k[0], k_a[0], r_k[0],
                    ln_x_w[0], ln_x_b[0], bsz, seq)

    x1 = out_proj(xf, att, rw, w_out)

    out = moe_block(x1, norm2_w[0], w_router[0], e_gate[0], e_up[0], e_down[0], norm_f_w, bsz, seq)
    return out.reshape(bsz, seq, d)
```

```python
import functools
import math

import jax
import jax.numpy as jnp
from jax import lax
from jax.experimental import pallas as pl
from jax.experimental.pallas import tpu as pltpu

F32 = jnp.float32
BF16 = jnp.bfloat16

LANES = 128
VMEM_LIMIT_BYTES = 56 * 1024 * 1024

D_MODEL = 2048
NORM_EPS = 1e-6
ATT_HEADS = 8
ATT_QK_DIM = 64
ATT_V_DIM = 128
ATT_WIDTH = ATT_HEADS * ATT_V_DIM
ATT_QK_COLS = ATT_HEADS * 2 * ATT_QK_DIM
ATT_COLS = 2 * ATT_QK_COLS + ATT_WIDTH
ROPE_THETA = 500000.0
ROPE_DIM = ATT_QK_DIM // 4
SUBLN_EPS = 1e-5
RWKV_WIDTH = 1024
RWKV_HEAD = 64
RWKV_HEADS = RWKV_WIDTH // RWKV_HEAD
DECAY_LORA = 64
ICLR_LORA = 64
GATE_LORA = 160
LORA_COLS = DECAY_LORA + ICLR_LORA + GATE_LORA
LORA_PAD = 384
GN_EPS = 64e-5
SHIFT_WIDTH = 3 * RWKV_WIDTH + LORA_COLS
SHIFT_PAD = 3 * RWKV_WIDTH + 512
N_EXPERTS = 16
CAPACITY_FACTOR = 2
EXPERT_FF = 2048
CHUNK = 64


def _params(*sem):
    return pltpu.CompilerParams(dimension_semantics=sem, vmem_limit_bytes=VMEM_LIMIT_BYTES)


def _dot(a, b):
    return jnp.dot(a, b, preferred_element_type=F32)


def _dot_nt(a, b):
    return lax.dot_general(a, b, (((1,), (1,)), ((), ())), preferred_element_type=F32)


def _split2(x):
    hi = x.astype(BF16)
    lo = (x - hi.astype(F32)).astype(BF16)
    return hi, lo


def _dot_lhs2(x, m_bf16):
    hi, lo = _split2(x)
    return _dot(hi, m_bf16) + _dot(lo, m_bf16)


def _rms_norm_kernel(x_ref, nw_ref, o_ref):
    x = x_ref[...]
    ms = jnp.mean(x * x, axis=-1, keepdims=True)
    o_ref[...] = (x * lax.rsqrt(ms + NORM_EPS) * nw_ref[...]).astype(o_ref.dtype)


def rms_norm_bf16(x, nw, tm=512):
    m, k = x.shape
    return pl.pallas_call(
        _rms_norm_kernel,
        grid=(m // tm,),
        in_specs=[pl.BlockSpec((tm, k), lambda i: (i, 0)), pl.BlockSpec((1, k), lambda i: (0, 0))],
        out_specs=pl.BlockSpec((tm, k), lambda i: (i, 0)),
        out_shape=jax.ShapeDtypeStruct((m, k), BF16),
        compiler_params=_params("parallel"),
        name="rms_norm",
    )(x, nw.reshape(1, k))


def _mm_kernel(h_ref, wt_ref, o_ref, *, j0, valid):
    tn = o_ref.shape[1]
    col = (j0 + pl.program_id(1)) * tn + lax.broadcasted_iota(jnp.int32, wt_ref.shape[1:], 0)
    wt = jnp.where(col < valid, wt_ref[0], 0.0).astype(BF16)
    o_ref[...] = _dot_nt(h_ref[...], wt).astype(o_ref.dtype)


def matmul(h, wt, col0, n, out_dtype, tm=2048, tn=512):
    m, k = h.shape
    j0 = col0 // tn
    return pl.pallas_call(
        functools.partial(_mm_kernel, j0=j0, valid=wt.shape[1]),
        grid=(m // tm, n // tn),
        in_specs=[
            pl.BlockSpec((tm, k), lambda i, j: (i, 0)),
            pl.BlockSpec((1, tn, k), lambda i, j: (0, j0 + j, 0)),
        ],
        out_specs=pl.BlockSpec((tm, tn), lambda i, j: (i, j)),
        out_shape=jax.ShapeDtypeStruct((m, n), out_dtype),
        compiler_params=_params("parallel", "arbitrary"),
        name="matmul",
    )(h, wt)


def _mm_rope_kernel(h_ref, wt_ref, pos_ref, freq_ref, slo_ref, shi_ref, o_ref,
                    c_scr, lo_scr, hi_scr, *, rope_tiles, q_tiles):
    j = pl.program_id(1)

    @pl.when(j == 0)
    def _():
        ang = pos_ref[...].astype(F32) * freq_ref[...]
        s = jnp.sin(ang)
        c_scr[...] = jnp.cos(ang)
        lo_scr[...] = s * slo_ref[...]
        hi_scr[...] = s * shi_ref[...]

    @pl.when(j < rope_tiles)
    def _():
        scale = jnp.where(j < q_tiles, ATT_QK_DIM ** -0.5 * math.log2(math.e), 1.0)
        half = ROPE_DIM // 2
        h = h_ref[...]
        part = 2 * LANES
        for p in range(o_ref.shape[1] // part):
            acc = _dot_nt(h, wt_ref[0, p * part:(p + 1) * part, :].astype(BF16))
            for g in range(part // LANES):
                x = acc[:, g * LANES:(g + 1) * LANES]
                x_dn = pltpu.roll(x, half, axis=1)
                x_up = pltpu.roll(x, LANES - half, axis=1)
                y = x * c_scr[...] + x_dn * hi_scr[...] + x_up * lo_scr[...]
                lanes = slice(p * part + g * LANES, p * part + (g + 1) * LANES)
                o_ref[:, lanes] = (y * scale).astype(o_ref.dtype)

    @pl.when(j >= rope_tiles)
    def _():
        o_ref[...] = _dot_nt(h_ref[...], wt_ref[0].astype(BF16)).astype(o_ref.dtype)


def matmul_rope(h, wt, pos, tm=2048, tn=512):
    m, k = h.shape
    n = ATT_COLS
    lane = jnp.arange(LANES) % ATT_QK_DIM
    half = ROPE_DIM // 2
    inv_freq = ROPE_THETA ** (-jnp.arange(0, ROPE_DIM, 2, dtype=F32) / ROPE_DIM)
    freq = jnp.where(lane < ROPE_DIM, inv_freq[lane % half], 0.0).astype(F32).reshape(1, LANES)
    sgn_lo = jnp.where(lane < half, -1.0, 0.0).astype(F32).reshape(1, LANES)
    sgn_hi = jnp.where((lane >= half) & (lane < ROPE_DIM), 1.0, 0.0).astype(F32).reshape(1, LANES)
    vec = pl.BlockSpec((1, LANES), lambda i, j: (0, 0))
    return pl.pallas_call(
        functools.partial(_mm_rope_kernel, rope_tiles=2 * ATT_QK_COLS // tn, q_tiles=ATT_QK_COLS // tn),
        grid=(m // tm, n // tn),
        in_specs=[
            pl.BlockSpec((tm, k), lambda i, j: (i, 0)),
            pl.BlockSpec((1, tn, k), lambda i, j: (0, j, 0)),
            pl.BlockSpec((tm, 1), lambda i, j: (i, 0)),
            vec, vec, vec,
        ],
        out_specs=pl.BlockSpec((tm, tn), lambda i, j: (i, j)),
        out_shape=jax.ShapeDtypeStruct((m, n), BF16),
        scratch_shapes=[pltpu.VMEM((tm, LANES), F32), pltpu.VMEM((tm, LANES), F32),
                        pltpu.VMEM((tm, LANES), F32)],
        compiler_params=_params("parallel", "arbitrary"),
        name="matmul_rope",
    )(h, wt, pos, freq, sgn_lo, sgn_hi)


def _attn_kernel(q_ref, qn_ref, k_ref, v_ref, lq1_ref, lk1_ref, lq2_ref, lk2_ref, sw_ref, o_ref,
                 v_scr, sa_scr, ma_scr, sb_scr, mb_scr, *, lambda_init, heads):
    dv = ATT_V_DIM
    i = pl.program_id(2)

    def scores(q, s_scr, m_scr):
        for hc in range(2 * heads):
            cols = slice(hc * ATT_QK_DIM, (hc + 1) * ATT_QK_DIM)
            s = _dot_nt(q[:, cols], k_ref[:, cols])
            s_scr[hc] = s
            m_scr[hc] = jnp.max(s, axis=-1, keepdims=True)

    @pl.when(i == 0)
    def _():
        lane = lax.broadcasted_iota(jnp.int32, (v_scr.shape[1], dv), 1)
        for h in range(heads):
            v_scr[h, :, :dv] = v_ref[:, h * dv:(h + 1) * dv]
            v_scr[h, :, dv:] = (lane == 0).astype(BF16)
        scores(q_ref[...], sa_scr, ma_scr)

    def step(cur_s, cur_m, nxt_s, nxt_m):
        if nxt_s is not None:
            scores(qn_ref[...], nxt_s, nxt_m)
        lam =(jnp.exp(jnp.sum(lq1_ref[...] * lk1_ref[...], axis=-1, keepdims=True))
               - jnp.exp(jnp.sum(lq2_ref[...] * lk2_ref[...], axis=-1, keepdims=True)) + lambda_init)
        for h in range(heads):
            va = v_scr[h]
            o1 = _dot(jnp.exp2(cur_s[2 * h] - cur_m[2 * h]).astype(BF16), va)
            o2 = _dot(jnp.exp2(cur_s[2 * h + 1] - cur_m[2 * h + 1]).astype(BF16), va)
            o = o1[:, :dv] / o1[:, dv:dv + 1] - o2[:, :dv] * (lam / o2[:, dv:dv + 1])
            ms = jnp.mean(o * o, axis=-1, keepdims=True)
            o = o * lax.rsqrt(ms + SUBLN_EPS) * sw_ref[...] * (1.0 - lambda_init)
            o_ref[:, h * dv:(h + 1) * dv] = o.astype(o_ref.dtype)

    last = i == pl.num_programs(2) - 1
    even = i % 2 == 0

    @pl.when(even & ~last)
    def _():
        step(sa_scr, ma_scr, sb_scr, mb_scr)

    @pl.when(~even & ~last)
    def _():
        step(sb_scr, mb_scr, sa_scr, ma_scr)

    @pl.when(even & last)
    def _():
        step(sa_scr, ma_scr, None, None)

    @pl.when(~even & last)
    def _():
        step(sb_scr, mb_scr, None, None)


def attention(qkv, lq1, lk1, lq2, lk2, subln_w, bsz, seq, lambda_init, tq=256, heads=4):
    nq = seq // tq
    vec = lambda n: pl.BlockSpec((1, n), lambda b, h, i: (0, 0))
    ng = ATT_HEADS // heads
    width = heads * LANES
    s_buf = pltpu.VMEM((2 * heads, tq, seq), F32)
    m_buf = pltpu.VMEM((2 * heads, tq, 1), F32)
    return pl.pallas_call(
        functools.partial(_attn_kernel, lambda_init=lambda_init, heads=heads),
        grid=(bsz, ng, nq),
        in_specs=[
            pl.BlockSpec((tq, width), lambda b, h, i: (b * nq + i, h)),
            pl.BlockSpec((tq, width), lambda b, h, i: (b * nq + jnp.minimum(i + 1, nq - 1), h)),
            pl.BlockSpec((seq, width), lambda b, h, i: (b, ng + h)),
            pl.BlockSpec((seq, width), lambda b, h, i: (b, 2 * ng + h)),
            vec(ATT_QK_DIM), vec(ATT_QK_DIM), vec(ATT_QK_DIM), vec(ATT_QK_DIM),
            vec(ATT_V_DIM),
        ],
        out_specs=pl.BlockSpec((tq, width), lambda b, h, i: (b * nq + i, h)),
        out_shape=jax.ShapeDtypeStruct((bsz * seq, ATT_WIDTH), BF16),
        scratch_shapes=[pltpu.VMEM((heads, seq, 2 * ATT_V_DIM), BF16), s_buf, m_buf, s_buf, m_buf],
        compiler_params=_params("parallel", "parallel", "arbitrary"),
        name="diff_attention",
    )(qkv, qkv, qkv, qkv,
      lq1.reshape(1, -1), lk1.reshape(1, -1), lq2.reshape(1, -1), lk2.reshape(1, -1),
      subln_w.reshape(1, -1))


HALO = 8


def _token_shift(z_ref, prev_ref, next_ref, mu_p, mu_n, first, last):
    z = z_ref[...]
    n = z.shape[0]
    row = lax.broadcasted_iota(jnp.int32, z.shape, 0)
    before = jnp.where(first, 0.0, prev_ref[HALO - 1:HALO, :])
    after = jnp.where(last, 0.0, next_ref[0:1, :])
    zp = jnp.where(row == 0, before, pltpu.roll(z, 1, axis=0))
    zn = jnp.where(row == n - 1, after, pltpu.roll(z, n - 1, axis=0))
    return z + mu_p * (zp - z) + mu_n * (zn - z)


def _seg_sum(x, ones_blk):
    parts = []
    for j in range(x.shape[1] // LANES):
        parts.append(_dot_lhs2(x[:, j * LANES:(j + 1) * LANES], ones_blk))
    return jnp.concatenate(parts, axis=1)


def _prep_kernel(r_ref, k_ref, v_ref, lo_ref, rp_ref, kp_ref, vp_ref, lp_ref, rn_ref, kn_ref, vn_ref,
                 ln_ref, mup_ref, mun_ref, w0_ref, du_ref, a0_ref, iu_ref, gu_ref, kk_ref,
                 ka_ref, rk_ref, ones_ref, tril_ref, triu_ref,
                 ab_ref, rb_ref, bt_ref, kt_ref, pl_ref, vb_ref, bonus_ref, g_ref, *, tt):
    first = pl.program_id(1) == 0
    last = pl.program_id(1) == pl.num_programs(1) - 1
    c = RWKV_WIDTH
    shift = lambda z, p, n, lo_col, hi_col: _token_shift(
        z, p, n, mup_ref[:, lo_col:hi_col], mun_ref[:, lo_col:hi_col], first, last)
    r = shift(r_ref, rp_ref, rn_ref, 0, c)
    k = shift(k_ref, kp_ref, kn_ref, c, 2 * c)
    v = shift(v_ref, vp_ref, vn_ref, 2 * c, 3 * c)
    lo = shift(lo_ref, lp_ref, ln_ref, 3 * c, 3 * c + LORA_PAD)
    ones_blk = ones_ref[...]
    vb_ref[...] = v.astype(BF16)
    g_ref[...] = _dot(jax.nn.sigmoid(lo).astype(BF16), gu_ref[...])
    kk = k * kk_ref[...]
    kk = kk * jnp.minimum(lax.rsqrt(_seg_sum(kk * kk, ones_blk)), 1e12)
    th = jnp.tanh(lo).astype(BF16)
    lob = lo.astype(BF16)
    nchunk = tt // CHUNK
    ksum = jnp.zeros_like(k)
    for d in range(2):
        wl = w0_ref[d] + _dot(th, du_ref[d])
        lw = -math.exp(-0.5) * jax.nn.sigmoid(wl)
        a = jax.nn.sigmoid(a0_ref[d] + _dot(lob, iu_ref[d]))
        kd = k * (1.0 + (a - 1.0) * ka_ref[...])
        ksum = ksum + kd
        tri = tril_ref[...] if d == 0 else triu_ref[...]
        cum = _dot_lhs2_rhs(tri, lw)
        e_pos = jnp.exp(cum)
        e_neg = jnp.exp(-cum)
        ab_ref[d] = (-kk * jnp.exp(cum - lw)).astype(BF16)
        rb_ref[d] = (r * e_pos).astype(BF16)
        bt_ref[d] = (kk * a * e_neg).astype(BF16)
        kt_ref[d] = (kd * e_neg).astype(BF16)
        for c in range(nchunk):
            last = c * CHUNK + (CHUNK - 1 if d == 0 else 0)
            pl_ref[d, c] = e_pos[last:last + 1, :]
    bonus_ref[...] = _seg_sum(r * ksum * rk_ref[...], ones_blk) * v


def _dot_lhs2_rhs(tri_bf16, x):
    hi, lo = _split2(x)
    return _dot(tri_bf16, hi) + _dot(tri_bf16, lo)


def rwkv_prep(pz, mu_prev, mu_next, w0, decay_up, a0, iclr_up, gate_up, k_k, k_a, r_k, bsz, seq, tt=256):
    m = bsz * seq
    c = RWKV_WIDTH
    nt = seq // tt
    ncb = tt // CHUNK
    nc = seq // CHUNK

    def pad_rows(w, start):
        out = jnp.zeros(w.shape[:-2] + (LORA_PAD, c), F32)
        return lax.dynamic_update_slice_in_dim(out, w.astype(F32), start, axis=w.ndim - 2).astype(BF16)

    du = pad_rows(decay_up, 0)
    iu = pad_rows(iclr_up, DECAY_LORA)
    gu = pad_rows(gate_up, DECAY_LORA + ICLR_LORA)
    lane = jnp.arange(LANES)
    ones_blk = (lane[:, None] // RWKV_HEAD == lane[None, :] // RWKV_HEAD).astype(BF16)
    t = jnp.arange(tt)
    same = t[:, None] // CHUNK == t[None, :] // CHUNK
    tril = (same & (t[:, None] >= t[None, :])).astype(BF16)
    triu = (same & (t[:, None] <= t[None, :])).astype(BF16)
    r_k_flat = r_k.reshape(1, c)
    row = lambda: pl.BlockSpec((1, c), lambda b, i: (0, 0))
    full3 = lambda s: pl.BlockSpec(s, lambda b, i: (0, 0, 0))
    tok = lambda j, w: pl.BlockSpec((tt, w), lambda b, i: (b * nt + i, j))
    per = tt // HALO
    prev = lambda j, w: pl.BlockSpec((HALO, w), lambda b, i: (jnp.maximum((b * nt + i) * per - 1, 0), j))
    nxt = lambda j, w: pl.BlockSpec(
        (HALO, w), lambda b, i: (jnp.minimum((b * nt + i + 1) * per, m // HALO - 1), j))
    lora_j = 3 * c // LORA_PAD
    cols = [(0, c), (1, c), (2, c), (lora_j, LORA_PAD)]
    mu_row = pl.BlockSpec((1, pz.shape[1]), lambda b, i: (0, 0))
    dir_tok = pl.BlockSpec((2, tt, c), lambda b, i: (0, b * nt + i, 0))
    outs = pl.pallas_call(
        functools.partial(_prep_kernel, tt=tt),
        grid=(bsz, nt),
        in_specs=[tok(j, w) for j, w in cols] + [prev(j, w) for j, w in cols]
        + [nxt(j, w) for j, w in cols] + [
            mu_row, mu_row,
            full3((2, 1, c)), full3((2, LORA_PAD, c)), full3((2, 1, c)), full3((2, LORA_PAD, c)),
            pl.BlockSpec((LORA_PAD, c), lambda b, i: (0, 0)),
            row(), row(), row(),
            pl.BlockSpec((LANES, LANES), lambda b, i: (0, 0)),
            pl.BlockSpec((tt, tt), lambda b, i: (0, 0)),
            pl.BlockSpec((tt, tt), lambda b, i: (0, 0)),
        ],
        out_specs=[
            dir_tok, dir_tok, dir_tok, dir_tok,
            pl.BlockSpec((2, ncb, 1, c), lambda b, i: (0, b * nt + i, 0, 0)),
            pl.BlockSpec((tt, c), lambda b, i: (b * nt + i, 0)),
            pl.BlockSpec((tt, c), lambda b, i: (b * nt + i, 0)),
            pl.BlockSpec((tt, c), lambda b, i: (b * nt + i, 0)),
        ],
        out_shape=[
            jax.ShapeDtypeStruct((2, m, c), BF16),
            jax.ShapeDtypeStruct((2, m, c), BF16),
            jax.ShapeDtypeStruct((2, m, c), BF16),
            jax.ShapeDtypeStruct((2, m, c), BF16),
            jax.ShapeDtypeStruct((2, bsz * nc, 1, c), F32),
            jax.ShapeDtypeStruct((m, c), BF16),
            jax.ShapeDtypeStruct((m, c), F32),
            jax.ShapeDtypeStruct((m, c), F32),
        ],
        compiler_params=_params("parallel", "parallel"),
        name="rwkv_prep",
    )(*([pz] * 12), mu_prev, mu_next, w0.reshape(2, 1, c), du, a0.reshape(2, 1, c), iu, gu,
      k_k.reshape(1, c), k_a.reshape(1, c), r_k_flat, ones_blk, tril, triu)
    return outs


def _scan_kernel(abf_ref, abb_ref, rbf_ref, rbb_ref, btf_ref, btb_ref, ktf_ref, ktb_ref,
                 vf_ref, vb_ref, plf_ref, plb_ref, yf_ref, yb_ref, m_scr, *, heads, cpb):
    @pl.when(pl.program_id(1) == 0)
    def _():
        m_scr[...] = jnp.zeros_like(m_scr)

    ab_ref = (abf_ref, abb_ref)
    rb_ref = (rbf_ref, rbb_ref)
    bt_ref = (btf_ref, btb_ref)
    kt_ref = (ktf_ref, ktb_ref)
    v_ref = (vf_ref, vb_ref)
    pl_ref = (plf_ref, plb_ref)
    y_ref = (yf_ref, yb_ref)
    n = CHUNK
    row = lax.broadcasted_iota(jnp.int32, (2 * n, 2 * n), 0)
    col = lax.broadcasted_iota(jnp.int32, (2 * n, 2 * n), 1)
    top = row < n
    tr = row % n
    tc = col % n
    eye = (lax.broadcasted_iota(jnp.int32, (n, n), 0)
           == lax.broadcasted_iota(jnp.int32, (n, n), 1)).astype(F32)
    keep = ((tr > tc) | (~top & (tr == tc)), (tr < tc) | (~top & (tr == tc)))
    probs = [(sub, d, hh) for sub in range(cpb) for d in range(2) for hh in range(heads)]
    sl = lambda hh: slice(hh * n, (hh + 1) * n)
    chunk_of = lambda sub, d: sub if d == 0 else cpb - 1 - sub
    rows = lambda sub, d: slice(chunk_of(sub, d) * n, (chunk_of(sub, d) + 1) * n)
    ab = [ab_ref[d][0, rows(sub, d), sl(hh)] for sub, d, hh in probs]
    rb = [rb_ref[d][0, rows(sub, d), sl(hh)] for sub, d, hh in probs]
    bt = [bt_ref[d][0, rows(sub, d), sl(hh)] for sub, d, hh in probs]
    kt = [kt_ref[d][0, rows(sub, d), sl(hh)] for sub, d, hh in probs]
    vv = [v_ref[d][rows(sub, d), sl(hh)] for sub, d, hh in probs]
    p_last = [pl_ref[d][0, chunk_of(sub, d), :, sl(hh)] for sub, d, hh in probs]
    idx = range(len(probs))
    g1 = [_dot_nt(jnp.concatenate([ab[i], rb[i]], axis=0), jnp.concatenate([bt[i], kt[i]], axis=0))
          for i in idx]
    g1 = [jnp.where(keep[probs[i][1]], g1[i], 0.0) for i in idx]
    a_ab = [g[:n, :n] for g in g1]
    low = [g[n:, :].astype(BF16) for g in g1]
    tm = [eye + a for a in a_ab]
    pw = [_dot(a.astype(BF16), a.astype(BF16)) for a in a_ab]
    span = 2
    while span < n:
        last = span * 2 >= n
        nxt = []
        for i in idx:
            pwb = pw[i].astype(BF16)
            if last:
                nxt.append((tm[i] + _dot(tm[i].astype(BF16), pwb), None))
            else:
                both = _dot(jnp.concatenate([tm[i], pw[i]], axis=0).astype(BF16), pwb)
                nxt.append((tm[i] + both[:n], both[n:]))
        tm = [t for t, _ in nxt]
        pw = [p for _, p in nxt]
        span *= 2
    wc = [_dot(tm[i].astype(BF16), jnp.concatenate([ab[i], g1[i][:n, n:].astype(BF16)], axis=1)) for i in idx]
    w = [x[:, :n].astype(BF16) for x in wc]
    cuv = [_dot(wc[i][:, n:].astype(BF16), vv[i]).astype(BF16) for i in idx]
    zero = jnp.zeros((n, n), BF16)
    rhs = [jnp.concatenate([jnp.concatenate([w[i], cuv[i]], axis=1),
                            jnp.concatenate([zero, vv[i]], axis=1)], axis=0) for i in idx]
    bh_t = [(bt[i].astype(F32) * p_last[i]).T for i in idx]
    kh_t = [(kt[i].astype(F32) * p_last[i]).T for i in idx]
    lhs = [jnp.concatenate([low[i], jnp.concatenate([bh_t[i], kh_t[i]], axis=1).astype(BF16)], axis=0)
           for i in idx]
    out = [_dot(lhs[i], rhs[i]) for i in idx]
    q_m = [rb[i].astype(F32) + out[i][:n, :n] for i in idx]
    g_m = [eye * p_last[i] + out[i][n:, :n] for i in idx]
    state = {(d, hh): m_scr[d, hh] for d in range(2) for hh in range(heads)}
    for i, (sub, d, hh) in enumerate(probs):
        upd = _dot(jnp.concatenate([q_m[i], g_m[i]], axis=0).astype(BF16), state[d, hh].astype(BF16))
        y_ref[d][rows(sub, d), sl(hh)] = upd[:n] + out[i][:n, n:]
        state[d, hh] = upd[n:] + out[i][n:, n:]
    for (d, hh), m_new in state.items():
        m_scr[d, hh] = m_new


def rwkv_scan(ab, rb, bt, kt, vb, p_last, bsz, seq, cpb=2):
    n = CHUNK
    nb = seq // (CHUNK * cpb)
    nh = RWKV_HEADS
    c_w = RWKV_WIDTH
    fwd = lambda b, c: b * nb + c
    bwd = lambda b, c: b * nb + nb - 1 - c
    dir_f = pl.BlockSpec((1, cpb * n, c_w), lambda b, c: (0, fwd(b, c), 0))
    dir_b = pl.BlockSpec((1, cpb * n, c_w), lambda b, c: (1, bwd(b, c), 0))
    tok_f = pl.BlockSpec((cpb * n, c_w), lambda b, c: (fwd(b, c), 0))
    tok_b = pl.BlockSpec((cpb * n, c_w), lambda b, c: (bwd(b, c), 0))
    pl_f = pl.BlockSpec((1, cpb, 1, c_w), lambda b, c: (0, fwd(b, c), 0, 0))
    pl_b = pl.BlockSpec((1, cpb, 1, c_w), lambda b, c: (1, bwd(b, c), 0, 0))
    y_shape = jax.ShapeDtypeStruct((bsz * seq, c_w), F32)
    return pl.pallas_call(
        functools.partial(_scan_kernel, heads=nh, cpb=cpb),
        grid=(bsz, nb),
        in_specs=[dir_f, dir_b, dir_f, dir_b, dir_f, dir_b, dir_f, dir_b, tok_f, tok_b, pl_f, pl_b],
        out_specs=[tok_f, tok_b],
        out_shape=[y_shape, y_shape],
        scratch_shapes=[pltpu.VMEM((2, nh, n, n), F32)],
        compiler_params=_params("parallel", "arbitrary"),
        name="rwkv_scan",
    )(ab, ab, rb, rb, bt, bt, kt, kt, vb, vb, p_last, p_last)


def _post_kernel(yf_ref, yb_ref, bonus_ref, g_ref, lw_ref, lb_ref, ones_ref, o_ref):
    y = yf_ref[...] + yb_ref[...]
    ones_blk = ones_ref[...]
    yc = y - _seg_sum(y, ones_blk) * (1.0 / RWKV_HEAD)
    var = _seg_sum(yc * yc, ones_blk) * (1.0 / RWKV_HEAD)
    yn = yc * lax.rsqrt(var + GN_EPS)
    o_ref[...] = ((yn * lw_ref[...] + lb_ref[...] + bonus_ref[...]) * g_ref[...]).astype(o_ref.dtype)


def rwkv_post(y_fwd, y_bwd, bonus, g, ln_w, ln_b, tt=512):
    m = y_fwd.shape[0]
    c = RWKV_WIDTH
    tok = pl.BlockSpec((tt, c), lambda i: (i, 0))
    row = pl.BlockSpec((1, c), lambda i: (0, 0))
    lane = jnp.arange(LANES)
    ones_blk = (lane[:, None] // RWKV_HEAD == lane[None, :] // RWKV_HEAD).astype(BF16)
    return pl.pallas_call(
        _post_kernel,
        grid=(m // tt,),
        in_specs=[tok, tok, tok, tok, row, row, pl.BlockSpec((LANES, LANES), lambda i: (0, 0))],
        out_specs=tok,
        out_shape=jax.ShapeDtypeStruct((m, c), BF16),
        compiler_params=_params("parallel"),
        name="rwkv_post",
    )(y_fwd, y_bwd, bonus, g, ln_w.reshape(1, c), ln_b.reshape(1, c), ones_blk)


def _out_proj_kernel(x_ref, a_ref, r_ref, wa_ref, wr_ref, o_ref):
    o_ref[...] = (x_ref[...] + _dot(a_ref[...], wa_ref[0].astype(BF16))
                  + _dot(r_ref[...], wr_ref[0].astype(BF16)))


def out_proj(x, att, rw, w_out, tm=2048, tn=512):
    m, n = x.shape
    assert ATT_WIDTH == RWKV_WIDTH
    return pl.pallas_call(
        _out_proj_kernel,
        grid=(m // tm, n // tn),
        in_specs=[
            pl.BlockSpec((tm, tn), lambda i, j: (i, j)),
            pl.BlockSpec((tm, ATT_WIDTH), lambda i, j: (i, 0)),
            pl.BlockSpec((tm, RWKV_WIDTH), lambda i, j: (i, 0)),
            pl.BlockSpec((1, ATT_WIDTH, tn), lambda i, j: (0, 0, j)),
            pl.BlockSpec((1, RWKV_WIDTH, tn), lambda i, j: (0, 1, j)),
        ],
        out_specs=pl.BlockSpec((tm, tn), lambda i, j: (i, j)),
        out_shape=jax.ShapeDtypeStruct((m, n), F32),
        compiler_params=_params("parallel", "parallel"),
        name="out_proj",
    )(x, att, rw, w_out, w_out)


def _router_kernel(x_ref, nw_ref, wh_ref, wl_ref, hb_ref, afft_ref):
    x = x_ref[...]
    ms = jnp.mean(x * x, axis=-1, keepdims=True)
    h = x * lax.rsqrt(ms + NORM_EPS) * nw_ref[...]
    hb_ref[...] = h.astype(BF16)
    hi, lo = _split2(h)
    logits = _dot(hi, wh_ref[...]) + _dot(lo, wh_ref[...]) + _dot(hi, wl_ref[...])
    lane = lax.broadcasted_iota(jnp.int32, logits.shape, 1)
    valid = lane < N_EXPERTS
    logits = jnp.where(valid, logits, -1e30)
    e = jnp.where(valid, jnp.exp(logits - jnp.max(logits, axis=-1, keepdims=True)), 0.0)
    aff = e / jnp.sum(e, axis=-1, keepdims=True)
    afft_ref[...] = aff.T[:N_EXPERTS, :]


def router(x1, norm_w, w_router, tm=512):
    m, k = x1.shape
    wpad = jnp.zeros((k, LANES), F32).at[:, :N_EXPERTS].set(w_router)
    wh, wl = _split2(wpad)
    return pl.pallas_call(
        _router_kernel,
        grid=(m // tm,),
        in_specs=[
            pl.BlockSpec((tm, k), lambda i: (i, 0)),
            pl.BlockSpec((1, k), lambda i: (0, 0)),
            pl.BlockSpec((k, LANES), lambda i: (0, 0)),
            pl.BlockSpec((k, LANES), lambda i: (0, 0)),
        ],
        out_specs=[
            pl.BlockSpec((tm, k), lambda i: (i, 0)),
            pl.BlockSpec((N_EXPERTS, tm), lambda i: (0, i)),
        ],
        out_shape=[
            jax.ShapeDtypeStruct((m, k), BF16),
            jax.ShapeDtypeStruct((N_EXPERTS, m), F32),
        ],
        compiler_params=_params("parallel"),
        name="router",
    )(x1, norm_w.reshape(1, k), wh, wl)


def _prefix_count(x, upper):
    outs = []
    carry = jnp.zeros((x.shape[0], 1), F32)
    for j in range(x.shape[1] // LANES):
        xt = x[:, j * LANES:(j + 1) * LANES]
        outs.append(_dot(xt.astype(BF16), upper) + carry)
        carry = carry + jnp.sum(xt, axis=-1, keepdims=True)
    return jnp.concatenate(outs, axis=1)


BISECT_STEPS = 152


def _topk_kernel(afft_ref, sp_ref, spt_ref, *, cap, bsz):
    ne = afft_ref.shape[0]
    seq = afft_ref.shape[1] // bsz
    a = [afft_ref[:, b * seq:(b + 1) * seq] for b in range(bsz)]

    def halve(_, bracket):
        out = []
        for b in range(bsz):
            lo, hi = bracket[b]
            mid = (lo + hi) * 0.5
            cnt = jnp.sum((a[b] >= mid).astype(F32), axis=-1, keepdims=True)
            enough = cnt >= cap
            out.append((jnp.where(enough, mid, lo), jnp.where(enough, hi, mid)))
        return tuple(out)

    start = tuple((jnp.zeros((ne, 1), F32), jnp.full((ne, 1), 2.0, F32)) for _ in range(bsz))
    bracket = lax.fori_loop(0, BISECT_STEPS, halve, start)
    r_i = lax.broadcasted_iota(jnp.int32, (LANES, LANES), 0)
    c_i = lax.broadcasted_iota(jnp.int32, (LANES, LANES), 1)
    upper = (r_i < c_i).astype(BF16)
    for b in range(bsz):
        lo, hi = bracket[b]
        above = (a[b] >= hi).astype(F32)
        tied = ((a[b] >= lo) & (a[b] < hi)).astype(F32)
        need = cap - jnp.sum(above, axis=-1, keepdims=True)
        sel = above + tied * (_prefix_count(tied, upper) < need).astype(F32)
        spt = jnp.where(sel > 0.5, _prefix_count(sel, upper), -1.0)
        spt_ref[:, b * seq:(b + 1) * seq] = spt
        full = jnp.concatenate([spt, jnp.full((LANES - ne, seq), -1.0, F32)], axis=0)
        sp_ref[b * seq:(b + 1) * seq, :] = full.T


def topk_select(afft, bsz, seq, cap):
    return pl.pallas_call(
        functools.partial(_topk_kernel, cap=cap, bsz=bsz),
        grid=(1,),
        in_specs=[pl.BlockSpec((N_EXPERTS, bsz * seq), lambda i: (0, 0))],
        out_specs=[
            pl.BlockSpec((bsz * seq, LANES), lambda i: (0, 0)),
            pl.BlockSpec((N_EXPERTS, bsz * seq), lambda i: (0, 0)),
        ],
        out_shape=[
            jax.ShapeDtypeStruct((bsz * seq, LANES), F32),
            jax.ShapeDtypeStruct((N_EXPERTS, bsz * seq), F32),
        ],
        compiler_params=_params("arbitrary"),
        name="topk_select",
    )(afft)


def _gather_kernel(spt_ref, afft_ref, h_ref, xe_ref, gate_ref, *, cap, group):
    eg = pl.program_id(1)
    seq = h_ref.shape[0]
    slot = lax.broadcasted_iota(jnp.int32, (cap, seq), 0).astype(F32)
    hits = []
    for k in range(group):
        e = eg * group + k
        hit = spt_ref[pl.ds(e, 1), :] == slot
        gate_ref[k] = jnp.sum(jnp.where(hit, afft_ref[pl.ds(e, 1), :], 0.0), axis=-1, keepdims=True)
        hits.append(hit.astype(BF16))
    xe = _dot(jnp.concatenate(hits, axis=0), h_ref[...]).astype(BF16)
    xe_ref[...] = xe.reshape(group, cap, xe.shape[1])


def moe_gather(spt, afft, hb, bsz, seq, cap, group=4):
    d = hb.shape[1]
    return pl.pallas_call(
        functools.partial(_gather_kernel, cap=cap, group=group),
        grid=(bsz, N_EXPERTS // group),
        in_specs=[
            pl.BlockSpec((N_EXPERTS, seq), lambda b, e: (0, b)),
            pl.BlockSpec((N_EXPERTS, seq), lambda b, e: (0, b)),
            pl.BlockSpec((seq, d), lambda b, e: (b, 0)),
        ],
        out_specs=[
            pl.BlockSpec((group, cap, d), lambda b, e: (e, b, 0)),
            pl.BlockSpec((group, cap, 1), lambda b, e: (e, b, 0)),
        ],
        out_shape=[
            jax.ShapeDtypeStruct((N_EXPERTS, bsz * cap, d), BF16),
            jax.ShapeDtypeStruct((N_EXPERTS, bsz * cap, 1), F32),
        ],
        compiler_params=_params("parallel", "arbitrary"),
        name="moe_gather",
    )(spt, afft, hb)


def _expert_kernel(xe_ref, gt_ref, wg_ref, wu_ref, wd_ref, ye_ref, acc_scr):
    f = pl.program_id(1)

    @pl.when(f == 0)
    def _():
        acc_scr[...] = jnp.zeros_like(acc_scr)

    xe = xe_ref[0]
    gate = _dot(xe, wg_ref[0].astype(BF16))
    up = _dot(xe, wu_ref[0].astype(BF16))
    hid = (gate * jax.nn.sigmoid(gate) * up).astype(BF16)
    acc_scr[...] += _dot(hid, wd_ref[0].astype(BF16))

    @pl.when(f == pl.num_programs(1) - 1)
    def _():
        ye_ref[0] = (acc_scr[...] * gt_ref[0]).astype(ye_ref.dtype)


def moe_experts(xe, gates, e_gate, e_up, e_down, tf=512):
    ne, rows, d = xe.shape
    ff = e_gate.shape[2]
    return pl.pallas_call(
        _expert_kernel,
        grid=(ne, ff // tf),
        in_specs=[
            pl.BlockSpec((1, rows, d), lambda e, f: (e, 0, 0)),
            pl.BlockSpec((1, rows, 1), lambda e, f: (e, 0, 0)),
            pl.BlockSpec((1, d, tf), lambda e, f: (e, 0, f)),
            pl.BlockSpec((1, d, tf), lambda e, f: (e, 0, f)),
            pl.BlockSpec((1, tf, d), lambda e, f: (e, f, 0)),
        ],
        out_specs=pl.BlockSpec((1, rows, d), lambda e, f: (e, 0, 0)),
        out_shape=jax.ShapeDtypeStruct((ne, rows, d), BF16),
        scratch_shapes=[pltpu.VMEM((rows, d), F32)],
        compiler_params=_params("parallel", "arbitrary"),
        name="moe_experts",
    )(xe, gates, e_gate, e_up, e_down)


def _combine_kernel(x_ref, sp_ref, ye_ref, nw_ref, o_ref, acc_scr, *, cap, group):
    eg = pl.program_id(2)

    @pl.when(eg == 0)
    def _():
        acc_scr[...] = x_ref[...]

    tt = x_ref.shape[0]
    lane = lax.broadcasted_iota(jnp.int32, (tt, LANES), 1)
    slot = lax.broadcasted_iota(jnp.int32, (tt, cap), 1).astype(F32)
    sp_all = sp_ref[...]
    hits = []
    for k in range(group):
        sp = jnp.sum(jnp.where(lane == eg * group + k, sp_all, 0.0), axis=-1, keepdims=True)
        hits.append((sp == slot).astype(BF16))
    onehot = jnp.concatenate(hits, axis=1)
    acc_scr[...] += _dot(onehot, ye_ref[...].reshape(group * cap, ye_ref.shape[2]))

    @pl.when(eg == pl.num_programs(2) - 1)
    def _():
        y = acc_scr[...]
        ms = jnp.mean(y * y, axis=-1, keepdims=True)
        o_ref[...] = y * lax.rsqrt(ms + NORM_EPS) * nw_ref[...]


def moe_combine(x1, sp, ye, norm_w, bsz, seq, cap, tt=512, group=8):
    m, d = x1.shape
    nt = seq // tt
    return pl.pallas_call(
        functools.partial(_combine_kernel, cap=cap, group=group),
        grid=(bsz, nt, N_EXPERTS // group),
        in_specs=[
            pl.BlockSpec((tt, d), lambda b, i, e: (b * nt + i, 0)),
            pl.BlockSpec((tt, LANES), lambda b, i, e: (b * nt + i, 0)),
            pl.BlockSpec((group, cap, d), lambda b, i, e: (e, b, 0)),
            pl.BlockSpec((1, d), lambda b, i, e: (0, 0)),
        ],
        out_specs=pl.BlockSpec((tt, d), lambda b, i, e: (b * nt + i, 0)),
        out_shape=jax.ShapeDtypeStruct((m, d), F32),
        scratch_shapes=[pltpu.VMEM((tt, d), F32)],
        compiler_params=_params("parallel", "parallel", "arbitrary"),
        name="moe_combine",
    )(x1, sp, ye, norm_w.reshape(1, d))


def _pad_cols(w, width):
    return jnp.pad(w, [(0, 0)] * (w.ndim - 1) + [(0, width - w.shape[-1])])


def rwkv_mixer(pz, mu_prev, mu_next, w0, decay_up, a0, iclr_up, gate_up, k_k, k_a, r_k, ln_x_w, ln_x_b,
               bsz, seq):
    ab, rb, bt, kt, p_last, vb, bonus, g = rwkv_prep(
        pz, mu_prev, mu_next, w0, decay_up, a0, iclr_up, gate_up, k_k, k_a, r_k, bsz, seq)
    y_fwd, y_bwd = rwkv_scan(ab, rb, bt, kt, vb, p_last, bsz, seq)
    return rwkv_post(y_fwd, y_bwd, bonus, g, ln_x_w, ln_x_b)


def moe_block(x1, norm2_w, w_router, e_gate, e_up, e_down, norm_f_w, bsz, seq):
    cap = CAPACITY_FACTOR * seq // N_EXPERTS
    hb, afft = router(x1, norm2_w, w_router)
    sp, spt = topk_select(afft, bsz, seq, cap)
    xe, gates = moe_gather(spt, afft, hb, bsz, seq, cap)
    ye = moe_experts(xe, gates, e_gate, e_up, e_down)
    return moe_combine(x1, sp, ye, norm_f_w, bsz, seq, cap)


def kernel(x, positions, norm1_w, w_in, mu_prev, mu_next, lambda_q1, lambda_k1, lambda_q2, lambda_k2, subln_w, w0, decay_up, a0, iclr_up, gate_up, k_k, k_a, r_k, ln_x_w, ln_x_b, w_out, norm2_w, w_router, e_gate, e_up, e_down, norm_f_w):
    bsz, seq, d = x.shape
    m = bsz * seq
    xf = x.reshape(m, d)
    pos = positions.reshape(m, 1)
    lambda_init = 0.8 - 0.6 * math.exp(-0.3 * 0)

    h = rms_norm_bf16(xf, norm1_w[0])
    w_in_t = jnp.swapaxes(w_in, 1, 2)
    qkv = matmul_rope(h, w_in_t, pos)
    p_z = matmul(h, w_in_t, ATT_COLS, SHIFT_PAD, F32)

    att = attention(qkv, lambda_q1[0], lambda_k1[0], lambda_q2[0], lambda_k2[0],
                    subln_w[0], bsz, seq, lambda_init)

    rw = rwkv_mixer(p_z, _pad_cols(mu_prev, SHIFT_PAD), _pad_cols(mu_next, SHIFT_PAD),
                    w0[0], decay_up[0], a0[0], iclr_up[0], gate_up[0], k_k[0], k_a[0], r_k[0],
                    ln_x_w[0], ln_x_b[0], bsz, seq)

    x1 = out_proj(xf, att, rw, w_out)

    out = moe_block(x1, norm2_w[0], w_router[0], e_gate[0], e_up[0], e_down[0], norm_f_w, bsz, seq)
    return out.reshape(bsz, seq, d)
```

```python
import functools
import math

import jax
import jax.numpy as jnp
from jax import lax
from jax.experimental import pallas as pl
from jax.experimental.pallas import tpu as pltpu

F32 = jnp.float32
BF16 = jnp.bfloat16

LANES = 128
VMEM_LIMIT_BYTES = 56 * 1024 * 1024

D_MODEL = 2048
NORM_EPS = 1e-6
ATT_HEADS = 8
ATT_QK_DIM = 64
ATT_V_DIM = 128
ATT_WIDTH = ATT_HEADS * ATT_V_DIM
ATT_QK_COLS = ATT_HEADS * 2 * ATT_QK_DIM
ATT_COLS = 2 * ATT_QK_COLS + ATT_WIDTH
ROPE_THETA = 500000.0
ROPE_DIM = ATT_QK_DIM // 4
SUBLN_EPS = 1e-5
RWKV_WIDTH = 1024
RWKV_HEAD = 64
RWKV_HEADS = RWKV_WIDTH // RWKV_HEAD
DECAY_LORA = 64
ICLR_LORA = 64
GATE_LORA = 160
LORA_COLS = DECAY_LORA + ICLR_LORA + GATE_LORA
LORA_PAD = 384
GN_EPS = 64e-5
SHIFT_WIDTH = 3 * RWKV_WIDTH + LORA_COLS
SHIFT_PAD = 3 * RWKV_WIDTH + 512
N_EXPERTS = 16
CAPACITY_FACTOR = 2
EXPERT_FF = 2048
CHUNK = 64


def _params(*sem):
    return pltpu.CompilerParams(dimension_semantics=sem, vmem_limit_bytes=VMEM_LIMIT_BYTES)


def _dot(a, b):
    return jnp.dot(a, b, preferred_element_type=F32)


def _dot_nt(a, b):
    return lax.dot_general(a, b, (((1,), (1,)), ((), ())), preferred_element_type=F32)


def _split2(x):
    hi = x.astype(BF16)
    lo = (x - hi.astype(F32)).astype(BF16)
    return hi, lo


def _sigmoid(x):
    return 0.5 * jnp.tanh(0.5 * x) + 0.5


def _dot_lhs2(x, m_bf16):
    hi, lo = _split2(x)
    return _dot(hi, m_bf16) + _dot(lo, m_bf16)


def _rms_norm_kernel(x_ref, nw_ref, o_ref):
    x = x_ref[...]
    ms = jnp.mean(x * x, axis=-1, keepdims=True)
    o_ref[...] = (x * lax.rsqrt(ms + NORM_EPS) * nw_ref[...]).astype(o_ref.dtype)


def rms_norm_bf16(x, nw, tm=512):
    m, k = x.shape
    return pl.pallas_call(
        _rms_norm_kernel,
        grid=(m // tm,),
        in_specs=[pl.BlockSpec((tm, k), lambda i: (i, 0)), pl.BlockSpec((1, k), lambda i: (0, 0))],
        out_specs=pl.BlockSpec((tm, k), lambda i: (i, 0)),
        out_shape=jax.ShapeDtypeStruct((m, k), BF16),
        compiler_params=_params("parallel"),
        name="rms_norm",
    )(x, nw.reshape(1, k))


def _mm_kernel(h_ref, wt_ref, o_ref, *, j0, valid):
    tn = o_ref.shape[1]
    col = (j0 + pl.program_id(1)) * tn + lax.broadcasted_iota(jnp.int32, wt_ref.shape[1:], 0)
    wt = jnp.where(col < valid, wt_ref[0], 0.0).astype(BF16)
    o_ref[...] = _dot_nt(h_ref[...], wt).astype(o_ref.dtype)


def matmul(h, wt, col0, n, out_dtype, tm=2048, tn=512):
    m, k = h.shape
    j0 = col0 // tn
    return pl.pallas_call(
        functools.partial(_mm_kernel, j0=j0, valid=wt.shape[1]),
        grid=(m // tm, n // tn),
        in_specs=[
            pl.BlockSpec((tm, k), lambda i, j: (i, 0)),
            pl.BlockSpec((1, tn, k), lambda i, j: (0, j0 + j, 0)),
        ],
        out_specs=pl.BlockSpec((tm, tn), lambda i, j: (i, j)),
        out_shape=jax.ShapeDtypeStruct((m, n), out_dtype),
        compiler_params=_params("parallel", "arbitrary"),
        name="matmul",
    )(h, wt)


def _mm_rope_kernel(h_ref, wt_ref, pos_ref, freq_ref, slo_ref, shi_ref, o_ref,
                    c_scr, lo_scr, hi_scr, *, rope_tiles, q_tiles):
    j = pl.program_id(1)

    @pl.when(j == 0)
    def _():
        ang = pos_ref[...].astype(F32) * freq_ref[...]
        s = jnp.sin(ang)
        c_scr[...] = jnp.cos(ang)
        lo_scr[...] = s * slo_ref[...]
        hi_scr[...] = s * shi_ref[...]

    @pl.when(j < rope_tiles)
    def _():
        scale = jnp.where(j < q_tiles, ATT_QK_DIM ** -0.5 * math.log2(math.e), 1.0)
        half = ROPE_DIM // 2
        h = h_ref[...]
        part = 2 * LANES
        for p in range(o_ref.shape[1] // part):
            acc = _dot_nt(h, wt_ref[0, p * part:(p + 1) * part, :].astype(BF16))
            for g in range(part // LANES):
                x = acc[:, g * LANES:(g + 1) * LANES]
                x_dn = pltpu.roll(x, half, axis=1)
                x_up = pltpu.roll(x, LANES - half, axis=1)
                y = x * c_scr[...] + x_dn * hi_scr[...] + x_up * lo_scr[...]
                lanes = slice(p * part + g * LANES, p * part + (g + 1) * LANES)
                o_ref[:, lanes] = (y * scale).astype(o_ref.dtype)

    @pl.when(j >= rope_tiles)
    def _():
        o_ref[...] = _dot_nt(h_ref[...], wt_ref[0].astype(BF16)).astype(o_ref.dtype)


def matmul_rope(h, wt, pos, tm=2048, tn=512):
    m, k = h.shape
    n = ATT_COLS
    lane = jnp.arange(LANES) % ATT_QK_DIM
    half = ROPE_DIM // 2
    inv_freq = ROPE_THETA ** (-jnp.arange(0, ROPE_DIM, 2, dtype=F32) / ROPE_DIM)
    freq = jnp.where(lane < ROPE_DIM, inv_freq[lane % half], 0.0).astype(F32).reshape(1, LANES)
    sgn_lo = jnp.where(lane < half, -1.0, 0.0).astype(F32).reshape(1, LANES)
    sgn_hi = jnp.where((lane >= half) & (lane < ROPE_DIM), 1.0, 0.0).astype(F32).reshape(1, LANES)
    vec = pl.BlockSpec((1, LANES), lambda i, j: (0, 0))
    return pl.pallas_call(
        functools.partial(_mm_rope_kernel, rope_tiles=2 * ATT_QK_COLS // tn, q_tiles=ATT_QK_COLS // tn),
        grid=(m // tm, n // tn),
        in_specs=[
            pl.BlockSpec((tm, k), lambda i, j: (i, 0)),
            pl.BlockSpec((1, tn, k), lambda i, j: (0, j, 0)),
            pl.BlockSpec((tm, 1), lambda i, j: (i, 0)),
            vec, vec, vec,
        ],
        out_specs=pl.BlockSpec((tm, tn), lambda i, j: (i, j)),
        out_shape=jax.ShapeDtypeStruct((m, n), BF16),
        scratch_shapes=[pltpu.VMEM((tm, LANES), F32), pltpu.VMEM((tm, LANES), F32),
                        pltpu.VMEM((tm, LANES), F32)],
        compiler_params=_params("parallel", "arbitrary"),
        name="matmul_rope",
    )(h, wt, pos, freq, sgn_lo, sgn_hi)


def _attn_kernel(q_ref, qn_ref, k_ref, v_ref, lq1_ref, lk1_ref, lq2_ref, lk2_ref, sw_ref, o_ref,
                 v_scr, sa_scr, ma_scr, sb_scr, mb_scr, *, lambda_init, heads):
    dv = ATT_V_DIM
    i = pl.program_id(2)

    def scores(q, s_scr, m_scr):
        for hc in range(2 * heads):
            cols = slice(hc * ATT_QK_DIM, (hc + 1) * ATT_QK_DIM)
            s = _dot_nt(q[:, cols], k_ref[:, cols])
            s_scr[hc] = s
            m_scr[hc] = jnp.max(s, axis=-1, keepdims=True)

    @pl.when(i == 0)
    def _():
        lane = lax.broadcasted_iota(jnp.int32, (v_scr.shape[1], dv), 1)
        for h in range(heads):
            v_scr[h, :, :dv] = v_ref[:, h * dv:(h + 1) * dv]
            v_scr[h, :, dv:] = (lane == 0).astype(BF16)
        scores(q_ref[...], sa_scr, ma_scr)

    def step(cur_s, cur_m, nxt_s, nxt_m):
        if nxt_s is not None:
            scores(qn_ref[...], nxt_s, nxt_m)
        lam =(jnp.exp(jnp.sum(lq1_ref[...] * lk1_ref[...], axis=-1, keepdims=True))
               - jnp.exp(jnp.sum(lq2_ref[...] * lk2_ref[...], axis=-1, keepdims=True)) + lambda_init)
        for h in range(heads):
            va = v_scr[h]
            o1 = _dot(jnp.exp2(cur_s[2 * h] - cur_m[2 * h]).astype(BF16), va)
            o2 = _dot(jnp.exp2(cur_s[2 * h + 1] - cur_m[2 * h + 1]).astype(BF16), va)
            o = o1[:, :dv] / o1[:, dv:dv + 1] - o2[:, :dv] * (lam / o2[:, dv:dv + 1])
            ms = jnp.mean(o * o, axis=-1, keepdims=True)
            o = o * lax.rsqrt(ms + SUBLN_EPS) * sw_ref[...] * (1.0 - lambda_init)
            o_ref[:, h * dv:(h + 1) * dv] = o.astype(o_ref.dtype)

    last = i == pl.num_programs(2) - 1
    even = i % 2 == 0

    @pl.when(even & ~last)
    def _():
        step(sa_scr, ma_scr, sb_scr, mb_scr)

    @pl.when(~even & ~last)
    def _():
        step(sb_scr, mb_scr, sa_scr, ma_scr)

    @pl.when(even & last)
    def _():
        step(sa_scr, ma_scr, None, None)

    @pl.when(~even & last)
    def _():
        step(sb_scr, mb_scr, None, None)


def attention(qkv, lq1, lk1, lq2, lk2, subln_w, bsz, seq, lambda_init, tq=256, heads=4):
    nq = seq // tq
    vec = lambda n: pl.BlockSpec((1, n), lambda b, h, i: (0, 0))
    ng = ATT_HEADS // heads
    width = heads * LANES
    s_buf = pltpu.VMEM((2 * heads, tq, seq), F32)
    m_buf = pltpu.VMEM((2 * heads, tq, 1), F32)
    return pl.pallas_call(
        functools.partial(_attn_kernel, lambda_init=lambda_init, heads=heads),
        grid=(bsz, ng, nq),
        in_specs=[
            pl.BlockSpec((tq, width), lambda b, h, i: (b * nq + i, h)),
            pl.BlockSpec((tq, width), lambda b, h, i: (b * nq + jnp.minimum(i + 1, nq - 1), h)),
            pl.BlockSpec((seq, width), lambda b, h, i: (b, ng + h)),
            pl.BlockSpec((seq, width), lambda b, h, i: (b, 2 * ng + h)),
            vec(ATT_QK_DIM), vec(ATT_QK_DIM), vec(ATT_QK_DIM), vec(ATT_QK_DIM),
            vec(ATT_V_DIM),
        ],
        out_specs=pl.BlockSpec((tq, width), lambda b, h, i: (b * nq + i, h)),
        out_shape=jax.ShapeDtypeStruct((bsz * seq, ATT_WIDTH), BF16),
        scratch_shapes=[pltpu.VMEM((heads, seq, 2 * ATT_V_DIM), BF16), s_buf, m_buf, s_buf, m_buf],
        compiler_params=_params("parallel", "parallel", "arbitrary"),
        name="diff_attention",
    )(qkv, qkv, qkv, qkv,
      lq1.reshape(1, -1), lk1.reshape(1, -1), lq2.reshape(1, -1), lk2.reshape(1, -1),
      subln_w.reshape(1, -1))


HALO = 8


def _token_shift(z_ref, prev_ref, next_ref, mu_p, mu_n, first, last):
    z = z_ref[...]
    n = z.shape[0]
    row = lax.broadcasted_iota(jnp.int32, (HALO, z.shape[1]), 0)
    before = jnp.where(first, 0.0, prev_ref[HALO - 1:HALO, :])
    after = jnp.where(last, 0.0, next_ref[0:1, :])
    zp = pltpu.roll(z, 1, axis=0)
    zp = jnp.concatenate([jnp.where(row == 0, before, zp[:HALO]), zp[HALO:]], axis=0)
    zn = pltpu.roll(z, n - 1, axis=0)
    zn = jnp.concatenate([zn[:n - HALO], jnp.where(row == HALO - 1, after, zn[n - HALO:])], axis=0)
    return (1.0 - mu_p - mu_n) * z + mu_p * zp + mu_n * zn


def _seg_sum(x, ones_blk):
    parts = []
    for j in range(x.shape[1] // LANES):
        parts.append(_dot_lhs2(x[:, j * LANES:(j + 1) * LANES], ones_blk))
    return jnp.concatenate(parts, axis=1)


def _prep_kernel(r_ref, k_ref, v_ref, lo_ref, rp_ref, kp_ref, vp_ref, lp_ref, rn_ref, kn_ref, vn_ref,
                 ln_ref, mup_ref, mun_ref, w0_ref, du_ref, a0_ref, iu_ref, gu_ref, kk_ref,
                 ka_ref, rk_ref, ones_ref, tril_ref, triu_ref,
                 ab_ref, rb_ref, bt_ref, kt_ref, pl_ref, vb_ref, bonus_ref, g_ref, *, tt):
    first = pl.program_id(1) == 0
    last = pl.program_id(1) == pl.num_programs(1) - 1
    c = RWKV_WIDTH
    shift = lambda z, p, n, lo_col, hi_col: _token_shift(
        z, p, n, mup_ref[:, lo_col:hi_col], mun_ref[:, lo_col:hi_col], first, last)
    r = shift(r_ref, rp_ref, rn_ref, 0, c)
    k = shift(k_ref, kp_ref, kn_ref, c, 2 * c)
    v = shift(v_ref, vp_ref, vn_ref, 2 * c, 3 * c)
    lo = shift(lo_ref, lp_ref, ln_ref, 3 * c, 3 * c + LORA_PAD)
    ones_blk = ones_ref[...]
    vb_ref[...] = v.astype(BF16)
    g_ref[...] = _dot(_sigmoid(lo).astype(BF16), gu_ref[...])
    kk = k * kk_ref[...]
    kk = kk * jnp.minimum(lax.rsqrt(_seg_sum(kk * kk, ones_blk)), 1e12)
    th = jnp.tanh(lo).astype(BF16)
    lob = lo.astype(BF16)
    nchunk = tt // CHUNK
    ksum = jnp.zeros_like(k)
    for d in range(2):
        wl = w0_ref[d] + _dot(th, du_ref[d])
        lw = -math.exp(-0.5) * _sigmoid(wl)
        a = _sigmoid(a0_ref[d] + _dot(lob, iu_ref[d]))
        kd = k * (1.0 + (a - 1.0) * ka_ref[...])
        ksum = ksum + kd
        tri = tril_ref[...] if d == 0 else triu_ref[...]
        cum = _dot_lhs2_rhs(tri, lw)
        e_pos = jnp.exp(cum)
        e_neg = jnp.exp(-cum)
        ab_ref[d] = (-kk * jnp.exp(cum - lw)).astype(BF16)
        rb_ref[d] = (r * e_pos).astype(BF16)
        bt_ref[d] = (kk * a * e_neg).astype(BF16)
        kt_ref[d] = (kd * e_neg).astype(BF16)
        for c in range(nchunk):
            last = c * CHUNK + (CHUNK - 1 if d == 0 else 0)
            pl_ref[d, c] = e_pos[last:last + 1, :]
    bonus_ref[...] = _seg_sum(r * ksum * rk_ref[...], ones_blk) * v


def _dot_lhs2_rhs(tri_bf16, x):
    hi, lo = _split2(x)
    return _dot(tri_bf16, hi) + _dot(tri_bf16, lo)


def rwkv_prep(pz, mu_prev, mu_next, w0, decay_up, a0, iclr_up, gate_up, k_k, k_a, r_k, bsz, seq, tt=256):
    m = bsz * seq
    c = RWKV_WIDTH
    nt = seq // tt
    ncb = tt // CHUNK
    nc = seq // CHUNK

    def pad_rows(w, start):
        out = jnp.zeros(w.shape[:-2] + (LORA_PAD, c), F32)
        return lax.dynamic_update_slice_in_dim(out, w.astype(F32), start, axis=w.ndim - 2).astype(BF16)

    du = pad_rows(decay_up, 0)
    iu = pad_rows(iclr_up, DECAY_LORA)
    gu = pad_rows(gate_up, DECAY_LORA + ICLR_LORA)
    lane = jnp.arange(LANES)
    ones_blk = (lane[:, None] // RWKV_HEAD == lane[None, :] // RWKV_HEAD).astype(BF16)
    t = jnp.arange(tt)
    same = t[:, None] // CHUNK == t[None, :] // CHUNK
    tril = (same & (t[:, None] >= t[None, :])).astype(BF16)
    triu = (same & (t[:, None] <= t[None, :])).astype(BF16)
    r_k_flat = r_k.reshape(1, c)
    row = lambda: pl.BlockSpec((1, c), lambda b, i: (0, 0))
    full3 = lambda s: pl.BlockSpec(s, lambda b, i: (0, 0, 0))
    tok = lambda j, w: pl.BlockSpec((tt, w), lambda b, i: (b * nt + i, j))
    per = tt // HALO
    prev = lambda j, w: pl.BlockSpec((HALO, w), lambda b, i: (jnp.maximum((b * nt + i) * per - 1, 0), j))
    nxt = lambda j, w: pl.BlockSpec(
        (HALO, w), lambda b, i: (jnp.minimum((b * nt + i + 1) * per, m // HALO - 1), j))
    lora_j = 3 * c // LORA_PAD
    cols = [(0, c), (1, c), (2, c), (lora_j, LORA_PAD)]
    mu_row = pl.BlockSpec((1, pz.shape[1]), lambda b, i: (0, 0))
    dir_tok = pl.BlockSpec((2, tt, c), lambda b, i: (0, b * nt + i, 0))
    outs = pl.pallas_call(
        functools.partial(_prep_kernel, tt=tt),
        grid=(bsz, nt),
        in_specs=[tok(j, w) for j, w in cols] + [prev(j, w) for j, w in cols]
        + [nxt(j, w) for j, w in cols] + [
            mu_row, mu_row,
            full3((2, 1, c)), full3((2, LORA_PAD, c)), full3((2, 1, c)), full3((2, LORA_PAD, c)),
            pl.BlockSpec((LORA_PAD, c), lambda b, i: (0, 0)),
            row(), row(), row(),
            pl.BlockSpec((LANES, LANES), lambda b, i: (0, 0)),
            pl.BlockSpec((tt, tt), lambda b, i: (0, 0)),
            pl.BlockSpec((tt, tt), lambda b, i: (0, 0)),
        ],
        out_specs=[
            dir_tok, dir_tok, dir_tok, dir_tok,
            pl.BlockSpec((2, ncb, 1, c), lambda b, i: (0, b * nt + i, 0, 0)),
            pl.BlockSpec((tt, c), lambda b, i: (b * nt + i, 0)),
            pl.BlockSpec((tt, c), lambda b, i: (b * nt + i, 0)),
            pl.BlockSpec((tt, c), lambda b, i: (b * nt + i, 0)),
        ],
        out_shape=[
            jax.ShapeDtypeStruct((2, m, c), BF16),
            jax.ShapeDtypeStruct((2, m, c), BF16),
            jax.ShapeDtypeStruct((2, m, c), BF16),
            jax.ShapeDtypeStruct((2, m, c), BF16),
            jax.ShapeDtypeStruct((2, bsz * nc, 1, c), F32),
            jax.ShapeDtypeStruct((m, c), BF16),
            jax.ShapeDtypeStruct((m, c), F32),
            jax.ShapeDtypeStruct((m, c), F32),
        ],
        compiler_params=_params("parallel", "parallel"),
        name="rwkv_prep",
    )(*([pz] * 12), mu_prev, mu_next, w0.reshape(2, 1, c), du, a0.reshape(2, 1, c), iu, gu,
      k_k.reshape(1, c), k_a.reshape(1, c), r_k_flat, ones_blk, tril, triu)
    return outs


def _scan_kernel(abf_ref, abb_ref, rbf_ref, rbb_ref, btf_ref, btb_ref, ktf_ref, ktb_ref,
                 vf_ref, vb_ref, plf_ref, plb_ref, yf_ref, yb_ref, m_scr, *, heads, cpb):
    @pl.when(pl.program_id(1) == 0)
    def _():
        m_scr[...] = jnp.zeros_like(m_scr)

    ab_ref = (abf_ref, abb_ref)
    rb_ref = (rbf_ref, rbb_ref)
    bt_ref = (btf_ref, btb_ref)
    kt_ref = (ktf_ref, ktb_ref)
    v_ref = (vf_ref, vb_ref)
    pl_ref = (plf_ref, plb_ref)
    y_ref = (yf_ref, yb_ref)
    n = CHUNK
    row = lax.broadcasted_iota(jnp.int32, (2 * n, 2 * n), 0)
    col = lax.broadcasted_iota(jnp.int32, (2 * n, 2 * n), 1)
    top = row < n
    tr = row % n
    tc = col % n
    eye = (lax.broadcasted_iota(jnp.int32, (n, n), 0)
           == lax.broadcasted_iota(jnp.int32, (n, n), 1)).astype(F32)
    keep = ((tr > tc) | (~top & (tr == tc)), (tr < tc) | (~top & (tr == tc)))
    probs = [(sub, d, hh) for sub in range(cpb) for d in range(2) for hh in range(heads)]
    sl = lambda hh: slice(hh * n, (hh + 1) * n)
    chunk_of = lambda sub, d: sub if d == 0 else cpb - 1 - sub
    rows = lambda sub, d: slice(chunk_of(sub, d) * n, (chunk_of(sub, d) + 1) * n)
    ab = [ab_ref[d][0, rows(sub, d), sl(hh)] for sub, d, hh in probs]
    rb = [rb_ref[d][0, rows(sub, d), sl(hh)] for sub, d, hh in probs]
    bt = [bt_ref[d][0, rows(sub, d), sl(hh)] for sub, d, hh in probs]
    kt = [kt_ref[d][0, rows(sub, d), sl(hh)] for sub, d, hh in probs]
    vv = [v_ref[d][rows(sub, d), sl(hh)] for sub, d, hh in probs]
    p_last = [pl_ref[d][0, chunk_of(sub, d), :, sl(hh)] for sub, d, hh in probs]
    idx = range(len(probs))
    g1 = [_dot_nt(jnp.concatenate([ab[i], rb[i]], axis=0), jnp.concatenate([bt[i], kt[i]], axis=0))
          for i in idx]
    g1 = [jnp.where(keep[probs[i][1]], g1[i], 0.0) for i in idx]
    a_ab = [g[:n, :n] for g in g1]
    low = [g[n:, :].astype(BF16) for g in g1]
    tm = [eye + a for a in a_ab]
    pw = [_dot(a.astype(BF16), a.astype(BF16)) for a in a_ab]
    span = 2
    while span < n:
        last = span * 2 >= n
        nxt = []
        for i in idx:
            pwb = pw[i].astype(BF16)
            if last:
                nxt.append((tm[i] + _dot(tm[i].astype(BF16), pwb), None))
            else:
                both = _dot(jnp.concatenate([tm[i], pw[i]], axis=0).astype(BF16), pwb)
                nxt.append((tm[i] + both[:n], both[n:]))
        tm = [t for t, _ in nxt]
        pw = [p for _, p in nxt]
        span *= 2
    wc = [_dot(tm[i].astype(BF16), jnp.concatenate([ab[i], g1[i][:n, n:].astype(BF16)], axis=1)) for i in idx]
    w = [x[:, :n].astype(BF16) for x in wc]
    cuv = [_dot(wc[i][:, n:].astype(BF16), vv[i]).astype(BF16) for i in idx]
    zero = jnp.zeros((n, n), BF16)
    rhs = [jnp.concatenate([jnp.concatenate([w[i], cuv[i]], axis=1),
                            jnp.concatenate([zero, vv[i]], axis=1)], axis=0) for i in idx]
    bh_t = [(bt[i].astype(F32) * p_last[i]).T for i in idx]
    kh_t = [(kt[i].astype(F32) * p_last[i]).T for i in idx]
    lhs = [jnp.concatenate([low[i], jnp.concatenate([bh_t[i], kh_t[i]], axis=1).astype(BF16)], axis=0)
           for i in idx]
    out = [_dot(lhs[i], rhs[i]) for i in idx]
    q_m = [rb[i].astype(F32) + out[i][:n, :n] for i in idx]
    g_m = [eye * p_last[i] + out[i][n:, :n] for i in idx]
    state = {(d, hh): m_scr[d, hh] for d in range(2) for hh in range(heads)}
    for i, (sub, d, hh) in enumerate(probs):
        upd = _dot(jnp.concatenate([q_m[i], g_m[i]], axis=0).astype(BF16), state[d, hh].astype(BF16))
        y_ref[d][rows(sub, d), sl(hh)] = upd[:n] + out[i][:n, n:]
        state[d, hh] = upd[n:] + out[i][n:, n:]
    for (d, hh), m_new in state.items():
        m_scr[d, hh] = m_new


def rwkv_scan(ab, rb, bt, kt, vb, p_last, bsz, seq, cpb=2):
    n = CHUNK
    nb = seq // (CHUNK * cpb)
    nh = RWKV_HEADS
    c_w = RWKV_WIDTH
    fwd = lambda b, c: b * nb + c
    bwd = lambda b, c: b * nb + nb - 1 - c
    dir_f = pl.BlockSpec((1, cpb * n, c_w), lambda b, c: (0, fwd(b, c), 0))
    dir_b = pl.BlockSpec((1, cpb * n, c_w), lambda b, c: (1, bwd(b, c), 0))
    tok_f = pl.BlockSpec((cpb * n, c_w), lambda b, c: (fwd(b, c), 0))
    tok_b = pl.BlockSpec((cpb * n, c_w), lambda b, c: (bwd(b, c), 0))
    pl_f = pl.BlockSpec((1, cpb, 1, c_w), lambda b, c: (0, fwd(b, c), 0, 0))
    pl_b = pl.BlockSpec((1, cpb, 1, c_w), lambda b, c: (1, bwd(b, c), 0, 0))
    y_shape = jax.ShapeDtypeStruct((bsz * seq, c_w), F32)
    return pl.pallas_call(
        functools.partial(_scan_kernel, heads=nh, cpb=cpb),
        grid=(bsz, nb),
        in_specs=[dir_f, dir_b, dir_f, dir_b, dir_f, dir_b, dir_f, dir_b, tok_f, tok_b, pl_f, pl_b],
        out_specs=[tok_f, tok_b],
        out_shape=[y_shape, y_shape],
        scratch_shapes=[pltpu.VMEM((2, nh, n, n), F32)],
        compiler_params=_params("parallel", "arbitrary"),
        name="rwkv_scan",
    )(ab, ab, rb, rb, bt, bt, kt, kt, vb, vb, p_last, p_last)


def _post_kernel(yf_ref, yb_ref, bonus_ref, g_ref, lw_ref, lb_ref, ones_ref, o_ref):
    y = yf_ref[...] + yb_ref[...]
    ones_blk = ones_ref[...]
    yc = y - _seg_sum(y, ones_blk) * (1.0 / RWKV_HEAD)
    var = _seg_sum(yc * yc, ones_blk) * (1.0 / RWKV_HEAD)
    yn = yc * lax.rsqrt(var + GN_EPS)
    o_ref[...] = ((yn * lw_ref[...] + lb_ref[...] + bonus_ref[...]) * g_ref[...]).astype(o_ref.dtype)


def rwkv_post(y_fwd, y_bwd, bonus, g, ln_w, ln_b, tt=512):
    m = y_fwd.shape[0]
    c = RWKV_WIDTH
    tok = pl.BlockSpec((tt, c), lambda i: (i, 0))
    row = pl.BlockSpec((1, c), lambda i: (0, 0))
    lane = jnp.arange(LANES)
    ones_blk = (lane[:, None] // RWKV_HEAD == lane[None, :] // RWKV_HEAD).astype(BF16)
    return pl.pallas_call(
        _post_kernel,
        grid=(m // tt,),
        in_specs=[tok, tok, tok, tok, row, row, pl.BlockSpec((LANES, LANES), lambda i: (0, 0))],
        out_specs=tok,
        out_shape=jax.ShapeDtypeStruct((m, c), BF16),
        compiler_params=_params("parallel"),
        name="rwkv_post",
    )(y_fwd, y_bwd, bonus, g, ln_w.reshape(1, c), ln_b.reshape(1, c), ones_blk)


def _out_proj_kernel(x_ref, a_ref, r_ref, wa_ref, wr_ref, o_ref):
    o_ref[...] = (x_ref[...] + _dot(a_ref[...], wa_ref[0].astype(BF16))
                  + _dot(r_ref[...], wr_ref[0].astype(BF16)))


def out_proj(x, att, rw, w_out, tm=2048, tn=512):
    m, n = x.shape
    assert ATT_WIDTH == RWKV_WIDTH
    return pl.pallas_call(
        _out_proj_kernel,
        grid=(m // tm, n // tn),
        in_specs=[
            pl.BlockSpec((tm, tn), lambda i, j: (i, j)),
            pl.BlockSpec((tm, ATT_WIDTH), lambda i, j: (i, 0)),
            pl.BlockSpec((tm, RWKV_WIDTH), lambda i, j: (i, 0)),
            pl.BlockSpec((1, ATT_WIDTH, tn), lambda i, j: (0, 0, j)),
            pl.BlockSpec((1, RWKV_WIDTH, tn), lambda i, j: (0, 1, j)),
        ],
        out_specs=pl.BlockSpec((tm, tn), lambda i, j: (i, j)),
        out_shape=jax.ShapeDtypeStruct((m, n), F32),
        compiler_params=_params("parallel", "parallel"),
        name="out_proj",
    )(x, att, rw, w_out, w_out)


def _router_kernel(x_ref, nw_ref, wh_ref, wl_ref, hb_ref, afft_ref):
    x = x_ref[...]
    ms = jnp.mean(x * x, axis=-1, keepdims=True)
    h = x * lax.rsqrt(ms + NORM_EPS) * nw_ref[...]
    hb_ref[...] = h.astype(BF16)
    hi, lo = _split2(h)
    logits = _dot(hi, wh_ref[...]) + _dot(lo, wh_ref[...]) + _dot(hi, wl_ref[...])
    lane = lax.broadcasted_iota(jnp.int32, logits.shape, 1)
    valid = lane < N_EXPERTS
    logits = jnp.where(valid, logits, -1e30)
    e = jnp.where(valid, jnp.exp(logits - jnp.max(logits, axis=-1, keepdims=True)), 0.0)
    aff = e / jnp.sum(e, axis=-1, keepdims=True)
    afft_ref[...] = aff.T[:N_EXPERTS, :]


def router(x1, norm_w, w_router, tm=512):
    m, k = x1.shape
    wpad = jnp.zeros((k, LANES), F32).at[:, :N_EXPERTS].set(w_router)
    wh, wl = _split2(wpad)
    return pl.pallas_call(
        _router_kernel,
        grid=(m // tm,),
        in_specs=[
            pl.BlockSpec((tm, k), lambda i: (i, 0)),
            pl.BlockSpec((1, k), lambda i: (0, 0)),
            pl.BlockSpec((k, LANES), lambda i: (0, 0)),
            pl.BlockSpec((k, LANES), lambda i: (0, 0)),
        ],
        out_specs=[
            pl.BlockSpec((tm, k), lambda i: (i, 0)),
            pl.BlockSpec((N_EXPERTS, tm), lambda i: (0, i)),
        ],
        out_shape=[
            jax.ShapeDtypeStruct((m, k), BF16),
            jax.ShapeDtypeStruct((N_EXPERTS, m), F32),
        ],
        compiler_params=_params("parallel"),
        name="router",
    )(x1, norm_w.reshape(1, k), wh, wl)


def _prefix_count(x, upper):
    outs = []
    carry = jnp.zeros((x.shape[0], 1), F32)
    for j in range(x.shape[1] // LANES):
        xt = x[:, j * LANES:(j + 1) * LANES]
        outs.append(_dot(xt.astype(BF16), upper) + carry)
        carry = carry + jnp.sum(xt, axis=-1, keepdims=True)
    return jnp.concatenate(outs, axis=1)


BISECT_STEPS = 152


def _topk_kernel(afft_ref, sp_ref, spt_ref, *, cap, bsz):
    ne = afft_ref.shape[0]
    seq = afft_ref.shape[1] // bsz
    a = [afft_ref[:, b * seq:(b + 1) * seq] for b in range(bsz)]

    def halve(_, bracket):
        out = []
        for b in range(bsz):
            lo, hi = bracket[b]
            mid = (lo + hi) * 0.5
            cnt = jnp.sum((a[b] >= mid).astype(F32), axis=-1, keepdims=True)
            enough = cnt >= cap
            out.append((jnp.where(enough, mid, lo), jnp.where(enough, hi, mid)))
        return tuple(out)

    start = tuple((jnp.zeros((ne, 1), F32), jnp.full((ne, 1), 2.0, F32)) for _ in range(bsz))
    bracket = lax.fori_loop(0, BISECT_STEPS, halve, start)
    r_i = lax.broadcasted_iota(jnp.int32, (LANES, LANES), 0)
    c_i = lax.broadcasted_iota(jnp.int32, (LANES, LANES), 1)
    upper = (r_i < c_i).astype(BF16)
    for b in range(bsz):
        lo, hi = bracket[b]
        above = (a[b] >= hi).astype(F32)
        tied = ((a[b] >= lo) & (a[b] < hi)).astype(F32)
        need = cap - jnp.sum(above, axis=-1, keepdims=True)
        sel = above + tied * (_prefix_count(tied, upper) < need).astype(F32)
        spt = jnp.where(sel > 0.5, _prefix_count(sel, upper), -1.0)
        spt_ref[:, b * seq:(b + 1) * seq] = spt
        full = jnp.concatenate([spt, jnp.full((LANES - ne, seq), -1.0, F32)], axis=0)
        sp_ref[b * seq:(b + 1) * seq, :] = full.T


def topk_select(afft, bsz, seq, cap):
    return pl.pallas_call(
        functools.partial(_topk_kernel, cap=cap, bsz=bsz),
        grid=(1,),
        in_specs=[pl.BlockSpec((N_EXPERTS, bsz * seq), lambda i: (0, 0))],
        out_specs=[
            pl.BlockSpec((bsz * seq, LANES), lambda i: (0, 0)),
            pl.BlockSpec((N_EXPERTS, bsz * seq), lambda i: (0, 0)),
        ],
        out_shape=[
            jax.ShapeDtypeStruct((bsz * seq, LANES), F32),
            jax.ShapeDtypeStruct((N_EXPERTS, bsz * seq), F32),
        ],
        compiler_params=_params("arbitrary"),
        name="topk_select",
    )(afft)


def _gather_kernel(spt_ref, afft_ref, h_ref, xe_ref, gate_ref, *, cap, group):
    eg = pl.program_id(1)
    seq = h_ref.shape[0]
    slot = lax.broadcasted_iota(jnp.int32, (cap, seq), 0).astype(F32)
    hits = []
    for k in range(group):
        e = eg * group + k
        hit = spt_ref[pl.ds(e, 1), :] == slot
        gate_ref[k] = jnp.sum(jnp.where(hit, afft_ref[pl.ds(e, 1), :], 0.0), axis=-1, keepdims=True)
        hits.append(hit.astype(BF16))
    xe = _dot(jnp.concatenate(hits, axis=0), h_ref[...]).astype(BF16)
    xe_ref[...] = xe.reshape(group, cap, xe.shape[1])


def moe_gather(spt, afft, hb, bsz, seq, cap, group=4):
    d = hb.shape[1]
    return pl.pallas_call(
        functools.partial(_gather_kernel, cap=cap, group=group),
        grid=(bsz, N_EXPERTS // group),
        in_specs=[
            pl.BlockSpec((N_EXPERTS, seq), lambda b, e: (0, b)),
            pl.BlockSpec((N_EXPERTS, seq), lambda b, e: (0, b)),
            pl.BlockSpec((seq, d), lambda b, e: (b, 0)),
        ],
        out_specs=[
            pl.BlockSpec((group, cap, d), lambda b, e: (e, b, 0)),
            pl.BlockSpec((group, cap, 1), lambda b, e: (e, b, 0)),
        ],
        out_shape=[
            jax.ShapeDtypeStruct((N_EXPERTS, bsz * cap, d), BF16),
            jax.ShapeDtypeStruct((N_EXPERTS, bsz * cap, 1), F32),
        ],
        compiler_params=_params("parallel", "arbitrary"),
        name="moe_gather",
    )(spt, afft, hb)


def _expert_kernel(xe_ref, gt_ref, wg_ref, wu_ref, wd_ref, ye_ref, acc_scr):
    f = pl.program_id(1)

    @pl.when(f == 0)
    def _():
        acc_scr[...] = jnp.zeros_like(acc_scr)

    xe = xe_ref[0]
    gate = _dot(xe, wg_ref[0].astype(BF16))
    up = _dot(xe, wu_ref[0].astype(BF16))
    hid = (gate * _sigmoid(gate) * up).astype(BF16)
    acc_scr[...] += _dot(hid, wd_ref[0].astype(BF16))

    @pl.when(f == pl.num_programs(1) - 1)
    def _():
        ye_ref[0] = (acc_scr[...] * gt_ref[0]).astype(ye_ref.dtype)


def moe_experts(xe, gates, e_gate, e_up, e_down, tf=512):
    ne, rows, d = xe.shape
    ff = e_gate.shape[2]
    return pl.pallas_call(
        _expert_kernel,
        grid=(ne, ff // tf),
        in_specs=[
            pl.BlockSpec((1, rows, d), lambda e, f: (e, 0, 0)),
            pl.BlockSpec((1, rows, 1), lambda e, f: (e, 0, 0)),
            pl.BlockSpec((1, d, tf), lambda e, f: (e, 0, f)),
            pl.BlockSpec((1, d, tf), lambda e, f: (e, 0, f)),
            pl.BlockSpec((1, tf, d), lambda e, f: (e, f, 0)),
        ],
        out_specs=pl.BlockSpec((1, rows, d), lambda e, f: (e, 0, 0)),
        out_shape=jax.ShapeDtypeStruct((ne, rows, d), BF16),
        scratch_shapes=[pltpu.VMEM((rows, d), F32)],
        compiler_params=_params("parallel", "arbitrary"),
        name="moe_experts",
    )(xe, gates, e_gate, e_up, e_down)


def _combine_kernel(x_ref, sp_ref, ye_ref, nw_ref, o_ref, acc_scr, *, cap, group):
    eg = pl.program_id(2)

    @pl.when(eg == 0)
    def _():
        acc_scr[...] = x_ref[...]

    tt = x_ref.shape[0]
    lane = lax.broadcasted_iota(jnp.int32, (tt, LANES), 1)
    slot = lax.broadcasted_iota(jnp.int32, (tt, cap), 1).astype(F32)
    sp_all = sp_ref[...]
    hits = []
    for k in range(group):
        sp = jnp.sum(jnp.where(lane == eg * group + k, sp_all, 0.0), axis=-1, keepdims=True)
        hits.append((sp == slot).astype(BF16))
    onehot = jnp.concatenate(hits, axis=1)
    acc_scr[...] += _dot(onehot, ye_ref[...].reshape(group * cap, ye_ref.shape[2]))

    @pl.when(eg == pl.num_programs(2) - 1)
    def _():
        y = acc_scr[...]
        ms = jnp.mean(y * y, axis=-1, keepdims=True)
        o_ref[...] = y * lax.rsqrt(ms + NORM_EPS) * nw_ref[...]


def moe_combine(x1, sp, ye, norm_w, bsz, seq, cap, tt=512, group=8):
    m, d = x1.shape
    nt = seq // tt
    return pl.pallas_call(
        functools.partial(_combine_kernel, cap=cap, group=group),
        grid=(bsz, nt, N_EXPERTS // group),
        in_specs=[
            pl.BlockSpec((tt, d), lambda b, i, e: (b * nt + i, 0)),
            pl.BlockSpec((tt, LANES), lambda b, i, e: (b * nt + i, 0)),
            pl.BlockSpec((group, cap, d), lambda b, i, e: (e, b, 0)),
            pl.BlockSpec((1, d), lambda b, i, e: (0, 0)),
        ],
        out_specs=pl.BlockSpec((tt, d), lambda b, i, e: (b * nt + i, 0)),
        out_shape=jax.ShapeDtypeStruct((m, d), F32),
        scratch_shapes=[pltpu.VMEM((tt, d), F32)],
        compiler_params=_params("parallel", "parallel", "arbitrary"),
        name="moe_combine",
    )(x1, sp, ye, norm_w.reshape(1, d))


def _pad_cols(w, width):
    return jnp.pad(w, [(0, 0)] * (w.ndim - 1) + [(0, width - w.shape[-1])])


def rwkv_mixer(pz, mu_prev, mu_next, w0, decay_up, a0, iclr_up, gate_up, k_k, k_a, r_k, ln_x_w, ln_x_b,
               bsz, seq):
    ab, rb, bt, kt, p_last, vb, bonus, g = rwkv_prep(
        pz, mu_prev, mu_next, w0, decay_up, a0, iclr_up, gate_up, k_k, k_a, r_k, bsz, seq)
    y_fwd, y_bwd = rwkv_scan(ab, rb, bt, kt, vb, p_last, bsz, seq)
    return rwkv_post(y_fwd, y_bwd, bonus, g, ln_x_w, ln_x_b)


def moe_block(x1, norm2_w, w_router, e_gate, e_up, e_down, norm_f_w, bsz, seq):
    cap = CAPACITY_FACTOR * seq // N_EXPERTS
    hb, afft = router(x1, norm2_w, w_router)
    sp, spt = topk_select(afft, bsz, seq, cap)
    xe, gates = moe_gather(spt, afft, hb, bsz, seq, cap)
    ye = moe_experts(xe, gates, e_gate, e_up, e_down)
    return moe_combine(x1, sp, ye, norm_f_w, bsz, seq, cap)


def kernel(x, positions, norm1_w, w_in, mu_prev, mu_next, lambda_q1, lambda_k1, lambda_q2, lambda_k2, subln_w, w0, decay_up, a0, iclr_up, gate_up, k_k, k_a, r_k, ln_x_w, ln_x_b, w_out, norm2_w, w_router, e_gate, e_up, e_down, norm_f_w):
    bsz, seq, d = x.shape
    m = bsz * seq
    xf = x.reshape(m, d)
    pos = positions.reshape(m, 1)
    lambda_init = 0.8 - 0.6 * math.exp(-0.3 * 0)

    h = rms_norm_bf16(xf, norm1_w[0])
    w_in_t = jnp.swapaxes(w_in, 1, 2)
    qkv = matmul_rope(h, w_in_t, pos)
    p_z = matmul(h, w_in_t, ATT_COLS, SHIFT_PAD, F32)

    att = attention(qkv, lambda_q1[0], lambda_k1[0], lambda_q2[0], lambda_k2[0],
                    subln_w[0], bsz, seq, lambda_init)

    rw = rwkv_mixer(p_z, _pad_cols(mu_prev, SHIFT_PAD), _pad_cols(mu_next, SHIFT_PAD),
                    w0[0], decay_up[0], a0[0], iclr_up[0], gate_up[0], k_k[0], k_a[0], r_k[0],
                    ln_x_w[0], ln_x_b[0], bsz, seq)

    x1 = out_proj(xf, att, rw, w_out)

    out = moe_block(x1, norm2_w[0], w_router[0], e_gate[0], e_up[0], e_down[0], norm_f_w, bsz, seq)
    return out.reshape(bsz, seq, d)
```

```python
import functools
import math

import jax
import jax.numpy as jnp
from jax import lax
from jax.experimental import pallas as pl
from jax.experimental.pallas import tpu as pltpu

F32 = jnp.float32
BF16 = jnp.bfloat16

LANES = 128
VMEM_LIMIT_BYTES = 56 * 1024 * 1024

NORM_EPS = 1e-6
ATT_HEADS = 8
ATT_QK_DIM = 64
ATT_V_DIM = 128
ATT_WIDTH = ATT_HEADS * ATT_V_DIM
ATT_QK_COLS = ATT_HEADS * 2 * ATT_QK_DIM
ATT_COLS = 2 * ATT_QK_COLS + ATT_WIDTH
ROPE_THETA = 500000.0
ROPE_DIM = ATT_QK_DIM // 4
SUBLN_EPS = 1e-5
RWKV_WIDTH = 1024
RWKV_HEAD = 64
RWKV_HEADS = RWKV_WIDTH // RWKV_HEAD
DECAY_LORA = 64
ICLR_LORA = 64
GATE_LORA = 160
LORA_PAD = 384
GN_EPS = 64e-5
SHIFT_PAD = 3 * RWKV_WIDTH + 512
N_EXPERTS = 16
CAPACITY_FACTOR = 2
CHUNK = 64


def _params(*sem):
    return pltpu.CompilerParams(dimension_semantics=sem, vmem_limit_bytes=VMEM_LIMIT_BYTES)


def _dot(a, b):
    return jnp.dot(a, b, preferred_element_type=F32)


def _dot_nt(a, b):
    return lax.dot_general(a, b, (((1,), (1,)), ((), ())), preferred_element_type=F32)


def _split2(x):
    hi = x.astype(BF16)
    lo = (x - hi.astype(F32)).astype(BF16)
    return hi, lo


def _sigmoid(x):
    return 0.5 * jnp.tanh(0.5 * x) + 0.5


def _dot_lhs2(x, m_bf16):
    hi, lo = _split2(x)
    return _dot(hi, m_bf16) + _dot(lo, m_bf16)


def _rms_norm_kernel(x_ref, nw_ref, o_ref):
    x = x_ref[...]
    ms = jnp.mean(x * x, axis=-1, keepdims=True)
    o_ref[...] = (x * lax.rsqrt(ms + NORM_EPS) * nw_ref[...]).astype(o_ref.dtype)


def rms_norm_bf16(x, nw, tm=512):
    m, k = x.shape
    return pl.pallas_call(
        _rms_norm_kernel,
        grid=(m // tm,),
        in_specs=[pl.BlockSpec((tm, k), lambda i: (i, 0)), pl.BlockSpec((1, k), lambda i: (0, 0))],
        out_specs=pl.BlockSpec((tm, k), lambda i: (i, 0)),
        out_shape=jax.ShapeDtypeStruct((m, k), BF16),
        compiler_params=_params("parallel"),
        name="rms_norm",
    )(x, nw.reshape(1, k))


def _mm_kernel(h_ref, wt_ref, o_ref, *, j0, valid):
    tn = o_ref.shape[1]
    col = (j0 + pl.program_id(1)) * tn + lax.broadcasted_iota(jnp.int32, wt_ref.shape[1:], 0)
    wt = jnp.where(col < valid, wt_ref[0], 0.0).astype(BF16)
    o_ref[...] = _dot_nt(h_ref[...], wt).astype(o_ref.dtype)


def matmul(h, wt, col0, n, out_dtype, tm=2048, tn=512):
    m, k = h.shape
    j0 = col0 // tn
    return pl.pallas_call(
        functools.partial(_mm_kernel, j0=j0, valid=wt.shape[1]),
        grid=(m // tm, n // tn),
        in_specs=[
            pl.BlockSpec((tm, k), lambda i, j: (i, 0)),
            pl.BlockSpec((1, tn, k), lambda i, j: (0, j0 + j, 0)),
        ],
        out_specs=pl.BlockSpec((tm, tn), lambda i, j: (i, j)),
        out_shape=jax.ShapeDtypeStruct((m, n), out_dtype),
        compiler_params=_params("parallel", "arbitrary"),
        name="matmul",
    )(h, wt)


def _mm_rope_kernel(h_ref, wt_ref, pos_ref, freq_ref, slo_ref, shi_ref, o_ref,
                    c_scr, lo_scr, hi_scr, *, rope_tiles, q_tiles):
    j = pl.program_id(1)

    @pl.when(j == 0)
    def _():
        ang = pos_ref[...].astype(F32) * freq_ref[...]
        s = jnp.sin(ang)
        c_scr[...] = jnp.cos(ang)
        lo_scr[...] = s * slo_ref[...]
        hi_scr[...] = s * shi_ref[...]

    @pl.when(j < rope_tiles)
    def _():
        scale = jnp.where(j < q_tiles, ATT_QK_DIM ** -0.5 * math.log2(math.e), 1.0)
        half = ROPE_DIM // 2
        h = h_ref[...]
        part = 2 * LANES
        for p in range(o_ref.shape[1] // part):
            acc = _dot_nt(h, wt_ref[0, p * part:(p + 1) * part, :].astype(BF16))
            for g in range(part // LANES):
                x = acc[:, g * LANES:(g + 1) * LANES]
                x_dn = pltpu.roll(x, half, axis=1)
                x_up = pltpu.roll(x, LANES - half, axis=1)
                y = x * c_scr[...] + x_dn * hi_scr[...] + x_up * lo_scr[...]
                lanes = slice(p * part + g * LANES, p * part + (g + 1) * LANES)
                o_ref[:, lanes] = (y * scale).astype(o_ref.dtype)

    @pl.when(j >= rope_tiles)
    def _():
        o_ref[...] = _dot_nt(h_ref[...], wt_ref[0].astype(BF16)).astype(o_ref.dtype)


def matmul_rope(h, wt, pos, tm=2048, tn=512):
    m, k = h.shape
    n = ATT_COLS
    lane = jnp.arange(LANES) % ATT_QK_DIM
    half = ROPE_DIM // 2
    inv_freq = ROPE_THETA ** (-jnp.arange(0, ROPE_DIM, 2, dtype=F32) / ROPE_DIM)
    freq = jnp.where(lane < ROPE_DIM, inv_freq[lane % half], 0.0).astype(F32).reshape(1, LANES)
    sgn_lo = jnp.where(lane < half, -1.0, 0.0).astype(F32).reshape(1, LANES)
    sgn_hi = jnp.where((lane >= half) & (lane < ROPE_DIM), 1.0, 0.0).astype(F32).reshape(1, LANES)
    vec = pl.BlockSpec((1, LANES), lambda i, j: (0, 0))
    return pl.pallas_call(
        functools.partial(_mm_rope_kernel, rope_tiles=2 * ATT_QK_COLS // tn, q_tiles=ATT_QK_COLS // tn),
        grid=(m // tm, n // tn),
        in_specs=[
            pl.BlockSpec((tm, k), lambda i, j: (i, 0)),
            pl.BlockSpec((1, tn, k), lambda i, j: (0, j, 0)),
            pl.BlockSpec((tm, 1), lambda i, j: (i, 0)),
            vec, vec, vec,
        ],
        out_specs=pl.BlockSpec((tm, tn), lambda i, j: (i, j)),
        out_shape=jax.ShapeDtypeStruct((m, n), BF16),
        scratch_shapes=[pltpu.VMEM((tm, LANES), F32), pltpu.VMEM((tm, LANES), F32),
                        pltpu.VMEM((tm, LANES), F32)],
        compiler_params=_params("parallel", "arbitrary"),
        name="matmul_rope",
    )(h, wt, pos, freq, sgn_lo, sgn_hi)


def _attn_kernel(q_ref, qn_ref, k_ref, v_ref, lq1_ref, lk1_ref, lq2_ref, lk2_ref, sw_ref, o_ref,
                 v_scr, sa_scr, ma_scr, sb_scr, mb_scr, *, lambda_init, heads):
    dv = ATT_V_DIM
    i = pl.program_id(2)

    def scores(q, s_scr, m_scr):
        for hc in range(2 * heads):
            cols = slice(hc * ATT_QK_DIM, (hc + 1) * ATT_QK_DIM)
            s = _dot_nt(q[:, cols], k_ref[:, cols])
            s_scr[hc] = s
            m_scr[hc] = jnp.max(s, axis=-1, keepdims=True)

    @pl.when(i == 0)
    def _():
        lane = lax.broadcasted_iota(jnp.int32, (v_scr.shape[1], dv), 1)
        for h in range(heads):
            v_scr[h, :, :dv] = v_ref[:, h * dv:(h + 1) * dv]
            v_scr[h, :, dv:] = (lane == 0).astype(BF16)
        scores(q_ref[...], sa_scr, ma_scr)

    def step(cur_s, cur_m, nxt_s, nxt_m):
        if nxt_s is not None:
            scores(qn_ref[...], nxt_s, nxt_m)
        lam =(jnp.exp(jnp.sum(lq1_ref[...] * lk1_ref[...], axis=-1, keepdims=True))
               - jnp.exp(jnp.sum(lq2_ref[...] * lk2_ref[...], axis=-1, keepdims=True)) + lambda_init)
        for h in range(heads):
            va = v_scr[h]
            o1 = _dot(jnp.exp2(cur_s[2 * h] - cur_m[2 * h]).astype(BF16), va)
            o2 = _dot(jnp.exp2(cur_s[2 * h + 1] - cur_m[2 * h + 1]).astype(BF16), va)
            o = o1[:, :dv] / o1[:, dv:dv + 1] - o2[:, :dv] * (lam / o2[:, dv:dv + 1])
            ms = jnp.mean(o * o, axis=-1, keepdims=True)
            o = o * lax.rsqrt(ms + SUBLN_EPS) * sw_ref[...] * (1.0 - lambda_init)
            o_ref[:, h * dv:(h + 1) * dv] = o.astype(o_ref.dtype)

    last = i == pl.num_programs(2) - 1
    even = i % 2 == 0

    @pl.when(even & ~last)
    def _():
        step(sa_scr, ma_scr, sb_scr, mb_scr)

    @pl.when(~even & ~last)
    def _():
        step(sb_scr, mb_scr, sa_scr, ma_scr)

    @pl.when(even & last)
    def _():
        step(sa_scr, ma_scr, None, None)

    @pl.when(~even & last)
    def _():
        step(sb_scr, mb_scr, None, None)


def attention(qkv, lq1, lk1, lq2, lk2, subln_w, bsz, seq, lambda_init, tq=256, heads=4):
    nq = seq // tq
    vec = lambda n: pl.BlockSpec((1, n), lambda b, h, i: (0, 0))
    ng = ATT_HEADS // heads
    width = heads * LANES
    s_buf = pltpu.VMEM((2 * heads, tq, seq), F32)
    m_buf = pltpu.VMEM((2 * heads, tq, 1), F32)
    return pl.pallas_call(
        functools.partial(_attn_kernel, lambda_init=lambda_init, heads=heads),
        grid=(bsz, ng, nq),
        in_specs=[
            pl.BlockSpec((tq, width), lambda b, h, i: (b * nq + i, h)),
            pl.BlockSpec((tq, width), lambda b, h, i: (b * nq + jnp.minimum(i + 1, nq - 1), h)),
            pl.BlockSpec((seq, width), lambda b, h, i: (b, ng + h)),
            pl.BlockSpec((seq, width), lambda b, h, i: (b, 2 * ng + h)),
            vec(ATT_QK_DIM), vec(ATT_QK_DIM), vec(ATT_QK_DIM), vec(ATT_QK_DIM),
            vec(ATT_V_DIM),
        ],
        out_specs=pl.BlockSpec((tq, width), lambda b, h, i: (b * nq + i, h)),
        out_shape=jax.ShapeDtypeStruct((bsz * seq, ATT_WIDTH), BF16),
        scratch_shapes=[pltpu.VMEM((heads, seq, 2 * ATT_V_DIM), BF16), s_buf, m_buf, s_buf, m_buf],
        compiler_params=_params("parallel", "parallel", "arbitrary"),
        name="diff_attention",
    )(qkv, qkv, qkv, qkv,
      lq1.reshape(1, -1), lk1.reshape(1, -1), lq2.reshape(1, -1), lk2.reshape(1, -1),
      subln_w.reshape(1, -1))


HALO = 8


def _token_shift(z_ref, prev_ref, next_ref, mu_p, mu_n, first, last):
    z = z_ref[...]
    n = z.shape[0]
    row = lax.broadcasted_iota(jnp.int32, (HALO, z.shape[1]), 0)
    before = jnp.where(first, 0.0, prev_ref[HALO - 1:HALO, :])
    after = jnp.where(last, 0.0, next_ref[0:1, :])
    zp = pltpu.roll(z, 1, axis=0)
    zp = jnp.concatenate([jnp.where(row == 0, before, zp[:HALO]), zp[HALO:]], axis=0)
    zn = pltpu.roll(z, n - 1, axis=0)
    zn = jnp.concatenate([zn[:n - HALO], jnp.where(row == HALO - 1, after, zn[n - HALO:])], axis=0)
    return (1.0 - mu_p - mu_n) * z + mu_p * zp + mu_n * zn


def _seg_sum(x, ones_blk):
    parts = []
    for j in range(x.shape[1] // LANES):
        parts.append(_dot_lhs2(x[:, j * LANES:(j + 1) * LANES], ones_blk))
    return jnp.concatenate(parts, axis=1)


def _prep_kernel(r_ref, k_ref, v_ref, lo_ref, rp_ref, kp_ref, vp_ref, lp_ref, rn_ref, kn_ref, vn_ref,
                 ln_ref, mup_ref, mun_ref, w0_ref, du_ref, a0_ref, iu_ref, gu_ref, kk_ref,
                 ka_ref, rk_ref, ones_ref, tril_ref, triu_ref,
                 ab_ref, rb_ref, bt_ref, kt_ref, pl_ref, vb_ref, bonus_ref, g_ref, *, tt):
    first = pl.program_id(1) == 0
    last = pl.program_id(1) == pl.num_programs(1) - 1
    c = RWKV_WIDTH
    shift = lambda z, p, n, lo_col, hi_col: _token_shift(
        z, p, n, mup_ref[:, lo_col:hi_col], mun_ref[:, lo_col:hi_col], first, last)
    r = shift(r_ref, rp_ref, rn_ref, 0, c)
    k = shift(k_ref, kp_ref, kn_ref, c, 2 * c)
    v = shift(v_ref, vp_ref, vn_ref, 2 * c, 3 * c)
    lo = shift(lo_ref, lp_ref, ln_ref, 3 * c, 3 * c + LORA_PAD)
    ones_blk = ones_ref[...]
    vb_ref[...] = v.astype(BF16)
    g_ref[...] = _dot(_sigmoid(lo).astype(BF16), gu_ref[...])
    kk = k * kk_ref[...]
    kk = kk * jnp.minimum(lax.rsqrt(_seg_sum(kk * kk, ones_blk)), 1e12)
    th = jnp.tanh(lo).astype(BF16)
    lob = lo.astype(BF16)
    nchunk = tt // CHUNK
    ksum = jnp.zeros_like(k)
    for d in range(2):
        wl = w0_ref[d] + _dot(th, du_ref[d])
        lw = -math.exp(-0.5) * _sigmoid(wl)
        a = _sigmoid(a0_ref[d] + _dot(lob, iu_ref[d]))
        kd = k * (1.0 + (a - 1.0) * ka_ref[...])
        ksum = ksum + kd
        tri = tril_ref[...] if d == 0 else triu_ref[...]
        cum = _dot_lhs2_rhs(tri, lw)
        e_pos = jnp.exp(cum)
        e_neg = jnp.exp(-cum)
        ab_ref[d] = (-kk * jnp.exp(cum - lw)).astype(BF16)
        rb_ref[d] = (r * e_pos).astype(BF16)
        bt_ref[d] = (kk * a * e_neg).astype(BF16)
        kt_ref[d] = (kd * e_neg).astype(BF16)
        for c in range(nchunk):
            last = c * CHUNK + (CHUNK - 1 if d == 0 else 0)
            pl_ref[d, c] = e_pos[last:last + 1, :]
    bonus_ref[...] = _seg_sum(r * ksum * rk_ref[...], ones_blk) * v


def _dot_lhs2_rhs(tri_bf16, x):
    hi, lo = _split2(x)
    return _dot(tri_bf16, hi) + _dot(tri_bf16, lo)


def rwkv_prep(pz, mu_prev, mu_next, w0, decay_up, a0, iclr_up, gate_up, k_k, k_a, r_k, bsz, seq, tt=256):
    m = bsz * seq
    c = RWKV_WIDTH
    nt = seq // tt
    ncb = tt // CHUNK
    nc = seq // CHUNK

    def pad_rows(w, start):
        out = jnp.zeros(w.shape[:-2] + (LORA_PAD, c), F32)
        return lax.dynamic_update_slice_in_dim(out, w.astype(F32), start, axis=w.ndim - 2).astype(BF16)

    du = pad_rows(decay_up, 0)
    iu = pad_rows(iclr_up, DECAY_LORA)
    gu = pad_rows(gate_up, DECAY_LORA + ICLR_LORA)
    lane = jnp.arange(LANES)
    ones_blk = (lane[:, None] // RWKV_HEAD == lane[None, :] // RWKV_HEAD).astype(BF16)
    t = jnp.arange(tt)
    same = t[:, None] // CHUNK == t[None, :] // CHUNK
    tril = (same & (t[:, None] >= t[None, :])).astype(BF16)
    triu = (same & (t[:, None] <= t[None, :])).astype(BF16)
    r_k_flat = r_k.reshape(1, c)
    row = lambda: pl.BlockSpec((1, c), lambda b, i: (0, 0))
    full3 = lambda s: pl.BlockSpec(s, lambda b, i: (0, 0, 0))
    tok = lambda j, w: pl.BlockSpec((tt, w), lambda b, i: (b * nt + i, j))
    per = tt // HALO
    prev = lambda j, w: pl.BlockSpec((HALO, w), lambda b, i: (jnp.maximum((b * nt + i) * per - 1, 0), j))
    nxt = lambda j, w: pl.BlockSpec(
        (HALO, w), lambda b, i: (jnp.minimum((b * nt + i + 1) * per, m // HALO - 1), j))
    lora_j = 3 * c // LORA_PAD
    cols = [(0, c), (1, c), (2, c), (lora_j, LORA_PAD)]
    mu_row = pl.BlockSpec((1, pz.shape[1]), lambda b, i: (0, 0))
    dir_tok = pl.BlockSpec((2, tt, c), lambda b, i: (0, b * nt + i, 0))
    outs = pl.pallas_call(
        functools.partial(_prep_kernel, tt=tt),
        grid=(bsz, nt),
        in_specs=[tok(j, w) for j, w in cols] + [prev(j, w) for j, w in cols]
        + [nxt(j, w) for j, w in cols] + [
            mu_row, mu_row,
            full3((2, 1, c)), full3((2, LORA_PAD, c)), full3((2, 1, c)), full3((2, LORA_PAD, c)),
            pl.BlockSpec((LORA_PAD, c), lambda b, i: (0, 0)),
            row(), row(), row(),
            pl.BlockSpec((LANES, LANES), lambda b, i: (0, 0)),
            pl.BlockSpec((tt, tt), lambda b, i: (0, 0)),
            pl.BlockSpec((tt, tt), lambda b, i: (0, 0)),
        ],
        out_specs=[
            dir_tok, dir_tok, dir_tok, dir_tok,
            pl.BlockSpec((2, ncb, 1, c), lambda b, i: (0, b * nt + i, 0, 0)),
            pl.BlockSpec((tt, c), lambda b, i: (b * nt + i, 0)),
            pl.BlockSpec((tt, c), lambda b, i: (b * nt + i, 0)),
            pl.BlockSpec((tt, c), lambda b, i: (b * nt + i, 0)),
        ],
        out_shape=[
            jax.ShapeDtypeStruct((2, m, c), BF16),
            jax.ShapeDtypeStruct((2, m, c), BF16),
            jax.ShapeDtypeStruct((2, m, c), BF16),
            jax.ShapeDtypeStruct((2, m, c), BF16),
            jax.ShapeDtypeStruct((2, bsz * nc, 1, c), F32),
            jax.ShapeDtypeStruct((m, c), BF16),
            jax.ShapeDtypeStruct((m, c), F32),
            jax.ShapeDtypeStruct((m, c), F32),
        ],
        compiler_params=_params("parallel", "parallel"),
        name="rwkv_prep",
    )(*([pz] * 12), mu_prev, mu_next, w0.reshape(2, 1, c), du, a0.reshape(2, 1, c), iu, gu,
      k_k.reshape(1, c), k_a.reshape(1, c), r_k_flat, ones_blk, tril, triu)
    return outs


def _scan_kernel(abf_ref, abb_ref, rbf_ref, rbb_ref, btf_ref, btb_ref, ktf_ref, ktb_ref,
                 vf_ref, vb_ref, plf_ref, plb_ref, yf_ref, yb_ref, m_scr, *, heads, cpb):
    @pl.when(pl.program_id(1) == 0)
    def _():
        m_scr[...] = jnp.zeros_like(m_scr)

    ab_ref = (abf_ref, abb_ref)
    rb_ref = (rbf_ref, rbb_ref)
    bt_ref = (btf_ref, btb_ref)
    kt_ref = (ktf_ref, ktb_ref)
    v_ref = (vf_ref, vb_ref)
    pl_ref = (plf_ref, plb_ref)
    y_ref = (yf_ref, yb_ref)
    n = CHUNK
    row = lax.broadcasted_iota(jnp.int32, (2 * n, 2 * n), 0)
    col = lax.broadcasted_iota(jnp.int32, (2 * n, 2 * n), 1)
    top = row < n
    tr = row % n
    tc = col % n
    eye = (lax.broadcasted_iota(jnp.int32, (n, n), 0)
           == lax.broadcasted_iota(jnp.int32, (n, n), 1)).astype(F32)
    keep = ((tr > tc) | (~top & (tr == tc)), (tr < tc) | (~top & (tr == tc)))
    probs = [(sub, d, hh) for sub in range(cpb) for d in range(2) for hh in range(heads)]
    sl = lambda hh: slice(hh * n, (hh + 1) * n)
    chunk_of = lambda sub, d: sub if d == 0 else cpb - 1 - sub
    rows = lambda sub, d: slice(chunk_of(sub, d) * n, (chunk_of(sub, d) + 1) * n)
    ab = [ab_ref[d][0, rows(sub, d), sl(hh)] for sub, d, hh in probs]
    rb = [rb_ref[d][0, rows(sub, d), sl(hh)] for sub, d, hh in probs]
    bt = [bt_ref[d][0, rows(sub, d), sl(hh)] for sub, d, hh in probs]
    kt = [kt_ref[d][0, rows(sub, d), sl(hh)] for sub, d, hh in probs]
    vv = [v_ref[d][rows(sub, d), sl(hh)] for sub, d, hh in probs]
    p_last = [pl_ref[d][0, chunk_of(sub, d), :, sl(hh)] for sub, d, hh in probs]
    idx = range(len(probs))
    g1 = [_dot_nt(jnp.concatenate([ab[i], rb[i]], axis=0), jnp.concatenate([bt[i], kt[i]], axis=0))
          for i in idx]
    g1 = [jnp.where(keep[probs[i][1]], g1[i], 0.0) for i in idx]
    a_ab = [g[:n, :n] for g in g1]
    low = [g[n:, :].astype(BF16) for g in g1]
    tm = [eye + a for a in a_ab]
    pw = [_dot(a.astype(BF16), a.astype(BF16)) for a in a_ab]
    span = 2
    while span < n:
        last = span * 2 >= n
        nxt = []
        for i in idx:
            pwb = pw[i].astype(BF16)
            if last:
                nxt.append((tm[i] + _dot(tm[i].astype(BF16), pwb), None))
            else:
                both = _dot(jnp.concatenate([tm[i], pw[i]], axis=0).astype(BF16), pwb)
                nxt.append((tm[i] + both[:n], both[n:]))
        tm = [t for t, _ in nxt]
        pw = [p for _, p in nxt]
        span *= 2
    wc = [_dot(tm[i].astype(BF16), jnp.concatenate([ab[i], g1[i][:n, n:].astype(BF16)], axis=1)) for i in idx]
    w = [x[:, :n].astype(BF16) for x in wc]
    cuv = [_dot(wc[i][:, n:].astype(BF16), vv[i]).astype(BF16) for i in idx]
    zero = jnp.zeros((n, n), BF16)
    rhs = [jnp.concatenate([jnp.concatenate([w[i], cuv[i]], axis=1),
                            jnp.concatenate([zero, vv[i]], axis=1)], axis=0) for i in idx]
    bh_t = [(bt[i].astype(F32) * p_last[i]).T for i in idx]
    kh_t = [(kt[i].astype(F32) * p_last[i]).T for i in idx]
    lhs = [jnp.concatenate([low[i], jnp.concatenate([bh_t[i], kh_t[i]], axis=1).astype(BF16)], axis=0)
           for i in idx]
    out = [_dot(lhs[i], rhs[i]) for i in idx]
    q_m = [rb[i].astype(F32) + out[i][:n, :n] for i in idx]
    g_m = [eye * p_last[i] + out[i][n:, :n] for i in idx]
    state = {(d, hh): m_scr[d, hh] for d in range(2) for hh in range(heads)}
    for i, (sub, d, hh) in enumerate(probs):
        upd = _dot(jnp.concatenate([q_m[i], g_m[i]], axis=0).astype(BF16), state[d, hh].astype(BF16))
        y_ref[d][rows(sub, d), sl(hh)] = upd[:n] + out[i][:n, n:]
        state[d, hh] = upd[n:] + out[i][n:, n:]
    for (d, hh), m_new in state.items():
        m_scr[d, hh] = m_new


def rwkv_scan(ab, rb, bt, kt, vb, p_last, bsz, seq, cpb=2):
    n = CHUNK
    nb = seq // (CHUNK * cpb)
    nh = RWKV_HEADS
    c_w = RWKV_WIDTH
    fwd = lambda b, c: b * nb + c
    bwd = lambda b, c: b * nb + nb - 1 - c
    dir_f = pl.BlockSpec((1, cpb * n, c_w), lambda b, c: (0, fwd(b, c), 0))
    dir_b = pl.BlockSpec((1, cpb * n, c_w), lambda b, c: (1, bwd(b, c), 0))
    tok_f = pl.BlockSpec((cpb * n, c_w), lambda b, c: (fwd(b, c), 0))
    tok_b = pl.BlockSpec((cpb * n, c_w), lambda b, c: (bwd(b, c), 0))
    pl_f = pl.BlockSpec((1, cpb, 1, c_w), lambda b, c: (0, fwd(b, c), 0, 0))
    pl_b = pl.BlockSpec((1, cpb, 1, c_w), lambda b, c: (1, bwd(b, c), 0, 0))
    y_shape = jax.ShapeDtypeStruct((bsz * seq, c_w), F32)
    return pl.pallas_call(
        functools.partial(_scan_kernel, heads=nh, cpb=cpb),
        grid=(bsz, nb),
        in_specs=[dir_f, dir_b, dir_f, dir_b, dir_f, dir_b, dir_f, dir_b, tok_f, tok_b, pl_f, pl_b],
        out_specs=[tok_f, tok_b],
        out_shape=[y_shape, y_shape],
        scratch_shapes=[pltpu.VMEM((2, nh, n, n), F32)],
        compiler_params=_params("parallel", "arbitrary"),
        name="rwkv_scan",
    )(ab, ab, rb, rb, bt, bt, kt, kt, vb, vb, p_last, p_last)


def _post_kernel(yf_ref, yb_ref, bonus_ref, g_ref, lw_ref, lb_ref, ones_ref, o_ref):
    y = yf_ref[...] + yb_ref[...]
    ones_blk = ones_ref[...]
    yc = y - _seg_sum(y, ones_blk) * (1.0 / RWKV_HEAD)
    var = _seg_sum(yc * yc, ones_blk) * (1.0 / RWKV_HEAD)
    yn = yc * lax.rsqrt(var + GN_EPS)
    o_ref[...] = ((yn * lw_ref[...] + lb_ref[...] + bonus_ref[...]) * g_ref[...]).astype(o_ref.dtype)


def rwkv_post(y_fwd, y_bwd, bonus, g, ln_w, ln_b, tt=512):
    m = y_fwd.shape[0]
    c = RWKV_WIDTH
    tok = pl.BlockSpec((tt, c), lambda i: (i, 0))
    row = pl.BlockSpec((1, c), lambda i: (0, 0))
    lane = jnp.arange(LANES)
    ones_blk = (lane[:, None] // RWKV_HEAD == lane[None, :] // RWKV_HEAD).astype(BF16)
    return pl.pallas_call(
        _post_kernel,
        grid=(m // tt,),
        in_specs=[tok, tok, tok, tok, row, row, pl.BlockSpec((LANES, LANES), lambda i: (0, 0))],
        out_specs=tok,
        out_shape=jax.ShapeDtypeStruct((m, c), BF16),
        compiler_params=_params("parallel"),
        name="rwkv_post",
    )(y_fwd, y_bwd, bonus, g, ln_w.reshape(1, c), ln_b.reshape(1, c), ones_blk)


def _out_proj_kernel(x_ref, a_ref, r_ref, wa_ref, wr_ref, o_ref):
    o_ref[...] = (x_ref[...] + _dot(a_ref[...], wa_ref[0].astype(BF16))
                  + _dot(r_ref[...], wr_ref[0].astype(BF16)))


def out_proj(x, att, rw, w_out, tm=2048, tn=512):
    m, n = x.shape
    assert ATT_WIDTH == RWKV_WIDTH
    return pl.pallas_call(
        _out_proj_kernel,
        grid=(m // tm, n // tn),
        in_specs=[
            pl.BlockSpec((tm, tn), lambda i, j: (i, j)),
            pl.BlockSpec((tm, ATT_WIDTH), lambda i, j: (i, 0)),
            pl.BlockSpec((tm, RWKV_WIDTH), lambda i, j: (i, 0)),
            pl.BlockSpec((1, ATT_WIDTH, tn), lambda i, j: (0, 0, j)),
            pl.BlockSpec((1, RWKV_WIDTH, tn), lambda i, j: (0, 1, j)),
        ],
        out_specs=pl.BlockSpec((tm, tn), lambda i, j: (i, j)),
        out_shape=jax.ShapeDtypeStruct((m, n), F32),
        compiler_params=_params("parallel", "parallel"),
        name="out_proj",
    )(x, att, rw, w_out, w_out)


def _router_kernel(x_ref, nw_ref, wh_ref, wl_ref, hb_ref, afft_ref):
    x = x_ref[...]
    ms = jnp.mean(x * x, axis=-1, keepdims=True)
    h = x * lax.rsqrt(ms + NORM_EPS) * nw_ref[...]
    hb_ref[...] = h.astype(BF16)
    hi, lo = _split2(h)
    logits = _dot(hi, wh_ref[...]) + _dot(lo, wh_ref[...]) + _dot(hi, wl_ref[...])
    lane = lax.broadcasted_iota(jnp.int32, logits.shape, 1)
    valid = lane < N_EXPERTS
    logits = jnp.where(valid, logits, -1e30)
    e = jnp.where(valid, jnp.exp(logits - jnp.max(logits, axis=-1, keepdims=True)), 0.0)
    aff = e / jnp.sum(e, axis=-1, keepdims=True)
    afft_ref[...] = aff.T[:N_EXPERTS, :]


def router(x1, norm_w, w_router, tm=512):
    m, k = x1.shape
    wpad = jnp.zeros((k, LANES), F32).at[:, :N_EXPERTS].set(w_router)
    wh, wl = _split2(wpad)
    return pl.pallas_call(
        _router_kernel,
        grid=(m // tm,),
        in_specs=[
            pl.BlockSpec((tm, k), lambda i: (i, 0)),
            pl.BlockSpec((1, k), lambda i: (0, 0)),
            pl.BlockSpec((k, LANES), lambda i: (0, 0)),
            pl.BlockSpec((k, LANES), lambda i: (0, 0)),
        ],
        out_specs=[
            pl.BlockSpec((tm, k), lambda i: (i, 0)),
            pl.BlockSpec((N_EXPERTS, tm), lambda i: (0, i)),
        ],
        out_shape=[
            jax.ShapeDtypeStruct((m, k), BF16),
            jax.ShapeDtypeStruct((N_EXPERTS, m), F32),
        ],
        compiler_params=_params("parallel"),
        name="router",
    )(x1, norm_w.reshape(1, k), wh, wl)


def _prefix_count(x, upper):
    outs = []
    carry = jnp.zeros((x.shape[0], 1), F32)
    for j in range(x.shape[1] // LANES):
        xt = x[:, j * LANES:(j + 1) * LANES]
        outs.append(_dot(xt.astype(BF16), upper) + carry)
        carry = carry + jnp.sum(xt, axis=-1, keepdims=True)
    return jnp.concatenate(outs, axis=1)


BISECT_STEPS = 152


def _topk_kernel(afft_ref, sp_ref, spt_ref, *, cap, bsz):
    ne = afft_ref.shape[0]
    seq = afft_ref.shape[1] // bsz
    a = [afft_ref[:, b * seq:(b + 1) * seq] for b in range(bsz)]

    def halve(_, bracket):
        out = []
        for b in range(bsz):
            lo, hi = bracket[b]
            mid = (lo + hi) * 0.5
            cnt = jnp.sum((a[b] >= mid).astype(F32), axis=-1, keepdims=True)
            enough = cnt >= cap
            out.append((jnp.where(enough, mid, lo), jnp.where(enough, hi, mid)))
        return tuple(out)

    start = tuple((jnp.zeros((ne, 1), F32), jnp.full((ne, 1), 2.0, F32)) for _ in range(bsz))
    bracket = lax.fori_loop(0, BISECT_STEPS, halve, start)
    r_i = lax.broadcasted_iota(jnp.int32, (LANES, LANES), 0)
    c_i = lax.broadcasted_iota(jnp.int32, (LANES, LANES), 1)
    upper = (r_i < c_i).astype(BF16)
    for b in range(bsz):
        lo, hi = bracket[b]
        above = (a[b] >= hi).astype(F32)
        tied = ((a[b] >= lo) & (a[b] < hi)).astype(F32)
        need = cap - jnp.sum(above, axis=-1, keepdims=True)
        sel = above + tied * (_prefix_count(tied, upper) < need).astype(F32)
        spt = jnp.where(sel > 0.5, _prefix_count(sel, upper), -1.0)
        spt_ref[:, b * seq:(b + 1) * seq] = spt
        full = jnp.concatenate([spt, jnp.full((LANES - ne, seq), -1.0, F32)], axis=0)
        sp_ref[b * seq:(b + 1) * seq, :] = full.T


def topk_select(afft, bsz, seq, cap):
    return pl.pallas_call(
        functools.partial(_topk_kernel, cap=cap, bsz=bsz),
        grid=(1,),
        in_specs=[pl.BlockSpec((N_EXPERTS, bsz * seq), lambda i: (0, 0))],
        out_specs=[
            pl.BlockSpec((bsz * seq, LANES), lambda i: (0, 0)),
            pl.BlockSpec((N_EXPERTS, bsz * seq), lambda i: (0, 0)),
        ],
        out_shape=[
            jax.ShapeDtypeStruct((bsz * seq, LANES), F32),
            jax.ShapeDtypeStruct((N_EXPERTS, bsz * seq), F32),
        ],
        compiler_params=_params("arbitrary"),
        name="topk_select",
    )(afft)


def _gather_kernel(spt_ref, afft_ref, h_ref, xe_ref, gate_ref, *, cap, group):
    eg = pl.program_id(1)
    seq = h_ref.shape[0]
    slot = lax.broadcasted_iota(jnp.int32, (cap, seq), 0).astype(F32)
    hits = []
    for k in range(group):
        e = eg * group + k
        hit = spt_ref[pl.ds(e, 1), :] == slot
        gate_ref[k] = jnp.sum(jnp.where(hit, afft_ref[pl.ds(e, 1), :], 0.0), axis=-1, keepdims=True)
        hits.append(hit.astype(BF16))
    xe = _dot(jnp.concatenate(hits, axis=0), h_ref[...]).astype(BF16)
    xe_ref[...] = xe.reshape(group, cap, xe.shape[1])


def moe_gather(spt, afft, hb, bsz, seq, cap, group=4):
    d = hb.shape[1]
    return pl.pallas_call(
        functools.partial(_gather_kernel, cap=cap, group=group),
        grid=(bsz, N_EXPERTS // group),
        in_specs=[
            pl.BlockSpec((N_EXPERTS, seq), lambda b, e: (0, b)),
            pl.BlockSpec((N_EXPERTS, seq), lambda b, e: (0, b)),
            pl.BlockSpec((seq, d), lambda b, e: (b, 0)),
        ],
        out_specs=[
            pl.BlockSpec((group, cap, d), lambda b, e: (e, b, 0)),
            pl.BlockSpec((group, cap, 1), lambda b, e: (e, b, 0)),
        ],
        out_shape=[
            jax.ShapeDtypeStruct((N_EXPERTS, bsz * cap, d), BF16),
            jax.ShapeDtypeStruct((N_EXPERTS, bsz * cap, 1), F32),
        ],
        compiler_params=_params("parallel", "arbitrary"),
        name="moe_gather",
    )(spt, afft, hb)


def _expert_kernel(xe_ref, gt_ref, wg_ref, wu_ref, wd_ref, ye_ref, acc_scr):
    f = pl.program_id(1)

    @pl.when(f == 0)
    def _():
        acc_scr[...] = jnp.zeros_like(acc_scr)

    xe = xe_ref[0]
    gate = _dot(xe, wg_ref[0].astype(BF16))
    up = _dot(xe, wu_ref[0].astype(BF16))
    hid = (gate * _sigmoid(gate) * up).astype(BF16)
    acc_scr[...] += _dot(hid, wd_ref[0].astype(BF16))

    @pl.when(f == pl.num_programs(1) - 1)
    def _():
        ye_ref[0] = (acc_scr[...] * gt_ref[0]).astype(ye_ref.dtype)


def moe_experts(xe, gates, e_gate, e_up, e_down, tf=512):
    ne, rows, d = xe.shape
    ff = e_gate.shape[2]
    return pl.pallas_call(
        _expert_kernel,
        grid=(ne, ff // tf),
        in_specs=[
            pl.BlockSpec((1, rows, d), lambda e, f: (e, 0, 0)),
            pl.BlockSpec((1, rows, 1), lambda e, f: (e, 0, 0)),
            pl.BlockSpec((1, d, tf), lambda e, f: (e, 0, f)),
            pl.BlockSpec((1, d, tf), lambda e, f: (e, 0, f)),
            pl.BlockSpec((1, tf, d), lambda e, f: (e, f, 0)),
        ],
        out_specs=pl.BlockSpec((1, rows, d), lambda e, f: (e, 0, 0)),
        out_shape=jax.ShapeDtypeStruct((ne, rows, d), BF16),
        scratch_shapes=[pltpu.VMEM((rows, d), F32)],
        compiler_params=_params("parallel", "arbitrary"),
        name="moe_experts",
    )(xe, gates, e_gate, e_up, e_down)


def _combine_kernel(x_ref, sp_ref, ye_ref, nw_ref, o_ref, *, cap, group):
    eg = pl.program_id(2)

    @pl.when(eg == 0)
    def _():
        o_ref[...] = x_ref[...]

    tt = x_ref.shape[0]
    lane = lax.broadcasted_iota(jnp.int32, (tt, LANES), 1)
    slot = lax.broadcasted_iota(jnp.int32, (tt, cap), 1).astype(F32)
    sp_all = sp_ref[...]
    hits = []
    for k in range(group):
        sp = jnp.sum(jnp.where(lane == eg * group + k, sp_all, 0.0), axis=-1, keepdims=True)
        hits.append((sp == slot).astype(BF16))
    onehot = jnp.concatenate(hits, axis=1)
    o_ref[...] += _dot(onehot, ye_ref[...].reshape(group * cap, ye_ref.shape[2]))

    @pl.when(eg == pl.num_programs(2) - 1)
    def _():
        y = o_ref[...]
        ms = jnp.mean(y * y, axis=-1, keepdims=True)
        o_ref[...] = y * lax.rsqrt(ms + NORM_EPS) * nw_ref[...]


def moe_combine(x1, sp, ye, norm_w, bsz, seq, cap, tt=1024, group=8):
    m, d = x1.shape
    tt = min(tt, seq)
    nt = seq // tt
    return pl.pallas_call(
        functools.partial(_combine_kernel, cap=cap, group=group),
        grid=(bsz, nt, N_EXPERTS // group),
        in_specs=[
            pl.BlockSpec((tt, d), lambda b, i, e: (b * nt + i, 0)),
            pl.BlockSpec((tt, LANES), lambda b, i, e: (b * nt + i, 0)),
            pl.BlockSpec((group, cap, d), lambda b, i, e: (e, b, 0)),
            pl.BlockSpec((1, d), lambda b, i, e: (0, 0)),
        ],
        out_specs=pl.BlockSpec((tt, d), lambda b, i, e: (b * nt + i, 0)),
        out_shape=jax.ShapeDtypeStruct((m, d), F32),
        compiler_params=_params("parallel", "parallel", "arbitrary"),
        name="moe_combine",
    )(x1, sp, ye, norm_w.reshape(1, d))


def _pad_cols(w, width):
    return jnp.pad(w, [(0, 0)] * (w.ndim - 1) + [(0, width - w.shape[-1])])


def rwkv_mixer(pz, mu_prev, mu_next, w0, decay_up, a0, iclr_up, gate_up, k_k, k_a, r_k, ln_x_w, ln_x_b,
               bsz, seq):
    ab, rb, bt, kt, p_last, vb, bonus, g = rwkv_prep(
        pz, mu_prev, mu_next, w0, decay_up, a0, iclr_up, gate_up, k_k, k_a, r_k, bsz, seq)
    y_fwd, y_bwd = rwkv_scan(ab, rb, bt, kt, vb, p_last, bsz, seq)
    return rwkv_post(y_fwd, y_bwd, bonus, g, ln_x_w, ln_x_b)


def moe_block(x1, norm2_w, w_router, e_gate, e_up, e_down, norm_f_w, bsz, seq):
    cap = CAPACITY_FACTOR * seq // N_EXPERTS
    hb, afft = router(x1, norm2_w, w_router)
    sp, spt = topk_select(afft, bsz, seq, cap)
    xe, gates = moe_gather(spt, afft, hb, bsz, seq, cap)
    ye = moe_experts(xe, gates, e_gate, e_up, e_down)
    return moe_combine(x1, sp, ye, norm_f_w, bsz, seq, cap)


def kernel(x, positions, norm1_w, w_in, mu_prev, mu_next, lambda_q1, lambda_k1, lambda_q2, lambda_k2, subln_w, w0, decay_up, a0, iclr_up, gate_up, k_k, k_a, r_k, ln_x_w, ln_x_b, w_out, norm2_w, w_router, e_gate, e_up, e_down, norm_f_w):
    bsz, seq, d = x.shape
    m = bsz * seq
    xf = x.reshape(m, d)
    pos = positions.reshape(m, 1)
    lambda_init = 0.8 - 0.6 * math.exp(-0.3 * 0)

    h = rms_norm_bf16(xf, norm1_w[0])
    w_in_t = jnp.swapaxes(w_in, 1, 2)
    qkv = matmul_rope(h, w_in_t, pos)
    p_z = matmul(h, w_in_t, ATT_COLS, SHIFT_PAD, F32)

    att = attention(qkv, lambda_q1[0], lambda_k1[0], lambda_q2[0], lambda_k2[0],
                    subln_w[0], bsz, seq, lambda_init)

    rw = rwkv_mixer(p_z, _pad_cols(mu_prev, SHIFT_PAD), _pad_cols(mu_next, SHIFT_PAD),
                    w0[0], decay_up[0], a0[0], iclr_up[0], gate_up[0], k_k[0], k_a[0], r_k[0],
                    ln_x_w[0], ln_x_b[0], bsz, seq)

    x1 = out_proj(xf, att, rw, w_out)

    out = moe_block(x1, norm2_w[0], w_router[0], e_gate[0], e_up[0], e_down[0], norm_f_w, bsz, seq)
    return out.reshape(bsz, seq, d)
```
